```python
import functools
import jax, jax.numpy as jnp
from jax import lax
import numpy as np

D_MODEL = 1024
BATCH = 4
SEQ = 4096
DEPTH = 2
DEC_BATCH = 128
DEC_SEQ = 8
PAST_LEN = 8192
PAGE_SIZE = 128

N_EVEN = (DEPTH + 1) // 2
N_ODD = DEPTH // 2
POOL_WINDOWS = (2, 4, 8, 16)
POOL_GROUPS = len(POOL_WINDOWS)
POOL_GROUP_DIM = 128
POOL_DIM = POOL_GROUPS * POOL_GROUP_DIM
POOL_HIST = max(POOL_WINDOWS) - 1
N_HEADS = 8
QK_NOPE_DIM = 64
QK_ROPE_DIM = 32
QK_HEAD_DIM = QK_NOPE_DIM + QK_ROPE_DIM
V_HEAD_DIM = 64
KV_RANK = 256
Q_DIM = N_HEADS * QK_HEAD_DIM
ATTN_OUT_DIM = N_HEADS * V_HEAD_DIM
ROPE_BASE = 10000.0
SOFTMAX_SCALE = QK_HEAD_DIM ** -0.5
Q_BLOCK = 128
IN_EVEN_DIM = POOL_DIM + Q_DIM + KV_RANK + QK_ROPE_DIM
MIX_DIM = POOL_DIM + ATTN_OUT_DIM
RNN_DIM = D_MODEL
RNN_BLOCKS = 8
RNN_BLOCK_DIM = RNN_DIM // RNN_BLOCKS
CONV_WIDTH = 4
LRU_C = 8.0
D_FF = 4 * D_MODEL
NORM_EPS = 1e-6

kernel_name = 'hybrid_pool_mla_rglru_decode_step'


def rms_norm(x, g):
    xf = x.astype(jnp.float32)
    y = xf * lax.rsqrt(jnp.mean(xf * xf, axis=-1, keepdims=True) + NORM_EPS)
    return (y * g.astype(jnp.float32)).astype(x.dtype)


def rope(x, pos):
    half = QK_ROPE_DIM // 2
    inv = ROPE_BASE ** (-jnp.arange(half, dtype=jnp.float32) / half)
    ang = pos.astype(jnp.float32)[:, None] * inv[None, :]
    ang = ang.reshape((1, ang.shape[0]) + (1,) * (x.ndim - 3) + (half,))
    cos, sin = jnp.cos(ang), jnp.sin(ang)
    xf = x.astype(jnp.float32)
    x1, x2 = xf[..., :half], xf[..., half:]
    return jnp.concatenate([x1 * cos - x2 * sin, x1 * sin + x2 * cos], axis=-1).astype(x.dtype)


def pool_mix(u_ext, pos, w_pool, pool_scale):
    T = pos.shape[0]
    B = u_ext.shape[0]
    uf = u_ext.astype(jnp.float32)
    cs = jnp.cumsum(uf, axis=1)
    cs = jnp.concatenate([jnp.zeros_like(cs[:, :1]), cs], axis=1)
    end = cs[:, POOL_HIST + 1:]
    u_new = uf[:, POOL_HIST:]
    groups = []
    for g, w in enumerate(POOL_WINDOWS):
        sl = slice(g * POOL_GROUP_DIM, (g + 1) * POOL_GROUP_DIM)
        start = cs[:, POOL_HIST + 1 - w: POOL_HIST + 1 - w + T, sl]
        cnt = jnp.minimum(pos + 1, w).astype(jnp.float32)[None, :, None]
        groups.append((end[..., sl] - start) / cnt - u_new[..., sl])
    d = jnp.stack(groups, axis=2)
    y = jnp.einsum('btgc,gcd->btgd', d, w_pool.astype(jnp.float32)).reshape(B, T, POOL_DIM)
    return (y * pool_scale.astype(jnp.float32)).astype(u_ext.dtype)


def k_nope_from_latent(c, w_uk, g_k_nope):
    return rms_norm(jnp.einsum('bkc,chd->bkhd', c, w_uk), g_k_nope)


def mla_core(qn, qr, k_nope, k_rope, c, mask, w_uv):
    b, tq = qn.shape[:2]
    s = (jnp.einsum('bqhd,bkhd->bhqk', qn, k_nope)
         + jnp.einsum('bqhr,bkr->bhqk', qr, k_rope)).astype(jnp.float32) * SOFTMAX_SCALE
    s = jnp.where(mask[None, None], s, -jnp.inf)
    p = jax.nn.softmax(s, axis=-1)
    lat = jnp.einsum('bhqk,bkc->bqhc', p.astype(c.dtype), c)
    return jnp.einsum('bqhc,chv->bqhv', lat, w_uv).reshape(b, tq, ATTN_OUT_DIM)


def mla_attend_prompt(qn, qr, c, kr, w_uk, g_k_nope, w_uv):
    B, S = qn.shape[:2]
    k_nope = k_nope_from_latent(c, w_uk, g_k_nope)
    key_pos = jnp.arange(S)

    def block(i):
        q0 = i * Q_BLOCK
        qn_b = lax.dynamic_slice_in_dim(qn, q0, Q_BLOCK, axis=1)
        qr_b = lax.dynamic_slice_in_dim(qr, q0, Q_BLOCK, axis=1)
        mask = (q0 + jnp.arange(Q_BLOCK))[:, None] >= key_pos[None, :]
        return mla_core(qn_b, qr_b, k_nope, kr, c, mask, w_uv)

    out = lax.map(block, jnp.arange(S // Q_BLOCK))
    return out.transpose(1, 0, 2, 3).reshape(B, S, ATTN_OUT_DIM)


def mla_attend_sample(qn, qr, c, kr, ckv_pool, krope_pool, layer, page_table, w_uk, g_k_nope, w_uv):
    T = qn.shape[1]
    n_past = page_table.shape[1] * PAGE_SIZE
    key_pos = jnp.arange(n_past + T)
    q_pos = n_past + jnp.arange(T)
    mask = key_pos[None, :] <= q_pos[:, None]

    def one(args):
        qn_b, qr_b, c_b, kr_b, pt_b = args
        c_all = jnp.concatenate([ckv_pool[layer, pt_b].reshape(n_past, KV_RANK).astype(c_b.dtype), c_b], axis=0)[None]
        kr_all = jnp.concatenate([krope_pool[layer, pt_b].reshape(n_past, QK_ROPE_DIM).astype(kr_b.dtype), kr_b], axis=0)[None]
        k_nope = k_nope_from_latent(c_all, w_uk, g_k_nope)
        return mla_core(qn_b[None], qr_b[None], k_nope, kr_all, c_all, mask, w_uv)[0]

    return lax.map(one, (qn, qr, c, kr, page_table))


def even_mixer(x, pos, pool_hist, attend, g_mix, w_in, g_q_nope, g_q_rope, g_ckv, g_k_rope, w_pool, pool_scale, w_out):
    B, T, _ = x.shape
    z = rms_norm(x, g_mix) @ w_in
    u = z[..., :POOL_DIM]
    q = z[..., POOL_DIM:POOL_DIM + Q_DIM].reshape(B, T, N_HEADS, QK_HEAD_DIM)
    c = z[..., POOL_DIM + Q_DIM:POOL_DIM + Q_DIM + KV_RANK]
    kr = z[..., POOL_DIM + Q_DIM + KV_RANK:]
    qn = rms_norm(q[..., :QK_NOPE_DIM], g_q_nope)
    qr = rope(rms_norm(q[..., QK_NOPE_DIM:], g_q_rope), pos)
    c = rms_norm(c, g_ckv)
    kr = rope(rms_norm(kr, g_k_rope), pos)
    u_ext = jnp.concatenate([pool_hist.astype(u.dtype), u], axis=1)
    pool_out = pool_mix(u_ext, pos, w_pool, pool_scale)
    attn_out = attend(qn, qr, c, kr)
    out = jnp.concatenate([pool_out, attn_out], axis=-1) @ w_out
    return out, c, kr, u_ext[:, -POOL_HIST:]


def linear_scan(a, b, h0):
    b = b.at[:, 0].add(a[:, 0] * h0)

    def comb(l, r):
        al, bl = l
        ar, br = r
        return al * ar, ar * bl + br

    _, h = lax.associative_scan(comb, (a, b), axis=1)
    return h


def odd_mixer(x, pos, conv_hist, h0, g_mix, w_in, conv_w, conv_b, w_ga, b_ga, w_gx, b_gx, lam, w_out):
    B, T, _ = x.shape
    z = rms_norm(x, g_mix) @ w_in
    gate, u = z[..., :RNN_DIM], z[..., RNN_DIM:]
    u_ext = jnp.concatenate([conv_hist.astype(u.dtype), u], axis=1)
    v = conv_b
    for k in range(CONV_WIDTH):
        v = v + u_ext[:, k:k + T] * conv_w[k]
    vb = v.reshape(B, T, RNN_BLOCKS, RNN_BLOCK_DIM)
    r = jax.nn.sigmoid((jnp.einsum('btni,nij->btnj', vb, w_ga).reshape(B, T, RNN_DIM) + b_ga).astype(jnp.float32))
    i = jax.nn.sigmoid((jnp.einsum('btni,nij->btnj', vb, w_gx).reshape(B, T, RNN_DIM) + b_gx).astype(jnp.float32))
    log_a = -LRU_C * r * jax.nn.softplus(-lam.astype(jnp.float32))
    a = jnp.exp(log_a)
    mult = jnp.where((pos == 0)[None, :, None], 1.0, jnp.sqrt(-jnp.expm1(2.0 * log_a)))
    h = linear_scan(a, mult * i * v.astype(jnp.float32), h0.astype(jnp.float32))
    y = (jax.nn.gelu(gate, approximate=True).astype(jnp.float32) * h).astype(x.dtype)
    return y @ w_out, u_ext[:, -(CONV_WIDTH - 1):], h[:, -1].astype(x.dtype)


def sqrelu_mlp(x, g, w_up, w_down):
    return jnp.square(jax.nn.relu(rms_norm(x, g) @ w_up)) @ w_down


def setup_inputs(seed: int = 0) -> dict:
    key = jax.random.key(seed)
    keys = iter(jax.random.split(key, 48))
    f32 = jnp.float32

    def normal(shape, scale=1.0):
        return scale * jax.random.normal(next(keys), shape, f32)

    def gain(shape):
        return 1.0 + 0.05 * jax.random.normal(next(keys), shape, f32)

    n_pages = PAST_LEN // PAGE_SIZE
    n_used = DEC_BATCH * n_pages
    n_pool = n_used + n_used // 4
    page_table = jax.random.permutation(next(keys), n_pool)[:n_used].reshape(DEC_BATCH, n_pages).astype(jnp.int32)
    a_c = jax.random.uniform(next(keys), (N_ODD, RNN_DIM), f32, 0.9, 0.999)
    s = a_c ** (1.0 / LRU_C)
    lru_lambda = jnp.log(s) - jnp.log1p(-s)
    return {
        'x_prompt': normal((BATCH, SEQ, D_MODEL)),
        'x_sample': normal((DEC_BATCH, DEC_SEQ, D_MODEL)),
        'cache_ckv': normal((N_EVEN, n_pool, PAGE_SIZE, KV_RANK)),
        'cache_krope': normal((N_EVEN, n_pool, PAGE_SIZE, QK_ROPE_DIM)),
        'state_pool': normal((N_EVEN, DEC_BATCH, POOL_HIST, POOL_DIM)),
        'state_conv': normal((N_ODD, DEC_BATCH, CONV_WIDTH - 1, RNN_DIM)),
        'state_lru': normal((N_ODD, DEC_BATCH, RNN_DIM), 0.5),
        'page_table': page_table,
        'norm_mix': gain((DEPTH, D_MODEL)),
        'w_in_even': normal((N_EVEN, D_MODEL, IN_EVEN_DIM), D_MODEL ** -0.5),
        'g_q_nope': gain((N_EVEN, QK_NOPE_DIM)),
        'g_q_rope': gain((N_EVEN, QK_ROPE_DIM)),
        'g_ckv': gain((N_EVEN, KV_RANK)),
        'g_k_rope': gain((N_EVEN, QK_ROPE_DIM)),
        'g_k_nope': gain((N_EVEN, QK_NOPE_DIM)),
        'w_uk': normal((N_EVEN, KV_RANK, N_HEADS, QK_NOPE_DIM), KV_RANK ** -0.5),
        'w_uv': normal((N_EVEN, KV_RANK, N_HEADS, V_HEAD_DIM), KV_RANK ** -0.5),
        'w_pool': normal((N_EVEN, POOL_GROUPS, POOL_GROUP_DIM, POOL_GROUP_DIM), POOL_GROUP_DIM ** -0.5),
        'pool_scale': gain((N_EVEN, POOL_DIM)),
        'w_out_even': normal((N_EVEN, MIX_DIM, D_MODEL), MIX_DIM ** -0.5),
        'w_in_rnn': normal((N_ODD, D_MODEL, 2 * RNN_DIM), D_MODEL ** -0.5),
        'conv_w': normal((N_ODD, CONV_WIDTH, RNN_DIM), CONV_WIDTH ** -0.5),
        'conv_b': normal((N_ODD, RNN_DIM), 0.01),
        'w_gate_a': normal((N_ODD, RNN_BLOCKS, RNN_BLOCK_DIM, RNN_BLOCK_DIM), RNN_BLOCK_DIM ** -0.5),
        'b_gate_a': normal((N_ODD, RNN_DIM), 0.01),
        'w_gate_x': normal((N_ODD, RNN_BLOCKS, RNN_BLOCK_DIM, RNN_BLOCK_DIM), RNN_BLOCK_DIM ** -0.5),
        'b_gate_x': normal((N_ODD, RNN_DIM), 0.01),
        'lru_lambda': lru_lambda,
        'w_out_rnn': normal((N_ODD, RNN_DIM, D_MODEL), RNN_DIM ** -0.5),
        'norm_ffn': gain((DEPTH, D_MODEL)),
        'w_up': normal((DEPTH, D_MODEL, D_FF), D_MODEL ** -0.5),
        'w_down': normal((DEPTH, D_FF, D_MODEL), D_FF ** -0.5),
    }


def reference(x_prompt, x_sample, cache_ckv, cache_krope, state_pool, state_conv, state_lru, page_table,
              norm_mix, w_in_even, g_q_nope, g_q_rope, g_ckv, g_k_rope, g_k_nope, w_uk, w_uv, w_pool, pool_scale,
              w_out_even, w_in_rnn, conv_w, conv_b, w_gate_a, b_gate_a, w_gate_x, b_gate_x, lru_lambda, w_out_rnn,
              norm_ffn, w_up, w_down):
    B, S = x_prompt.shape[:2]
    DB, T = x_sample.shape[:2]
    n_past = page_table.shape[1] * PAGE_SIZE
    pos_p = jnp.arange(S, dtype=jnp.int32)
    pos_s = n_past + jnp.arange(T, dtype=jnp.int32)
    yp, ys = x_prompt, x_sample
    ckv_p, kr_p, pool_p, conv_p, lru_p = [], [], [], [], []
    ckv_s, kr_s, pool_s, conv_s, lru_s = [], [], [], [], []
    for layer in range(DEPTH):
        j = layer // 2
        if layer % 2 == 0:
            ew = (norm_mix[layer], w_in_even[j], g_q_nope[j], g_q_rope[j], g_ckv[j], g_k_rope[j],
                  w_pool[j], pool_scale[j], w_out_even[j])
            attend_p = functools.partial(mla_attend_prompt, w_uk=w_uk[j], g_k_nope=g_k_nope[j], w_uv=w_uv[j])
            attend_s = functools.partial(mla_attend_sample, ckv_pool=cache_ckv, krope_pool=cache_krope, layer=j,
                                         page_table=page_table, w_uk=w_uk[j], g_k_nope=g_k_nope[j], w_uv=w_uv[j])
            hist0 = jnp.zeros((B, POOL_HIST, POOL_DIM), yp.dtype)
            mp, c, kr, ph = even_mixer(yp, pos_p, hist0, attend_p, *ew)
            ms, cs_, krs, phs = even_mixer(ys, pos_s, state_pool[j], attend_s, *ew)
            ckv_p.append(c); kr_p.append(kr); pool_p.append(ph)
            ckv_s.append(cs_); kr_s.append(krs); pool_s.append(phs)
        else:
            ow = (norm_mix[layer], w_in_rnn[j], conv_w[j], conv_b[j], w_gate_a[j], b_gate_a[j],
                  w_gate_x[j], b_gate_x[j], lru_lambda[j], w_out_rnn[j])
            conv0 = jnp.zeros((B, CONV_WIDTH - 1, RNN_DIM), yp.dtype)
            h0 = jnp.zeros((B, RNN_DIM), yp.dtype)
            mp, cvp, hp = odd_mixer(yp, pos_p, conv0, h0, *ow)
            ms, cvs, hs = odd_mixer(ys, pos_s, state_conv[j], state_lru[j], *ow)
            conv_p.append(cvp); lru_p.append(hp)
            conv_s.append(cvs); lru_s.append(hs)
        yp = yp + mp
        ys = ys + ms
        yp = yp + sqrelu_mlp(yp, norm_ffn[layer], w_up[layer], w_down[layer])
        ys = ys + sqrelu_mlp(ys, norm_ffn[layer], w_up[layer], w_down[layer])
    return (yp, ys,
            jnp.stack(ckv_p), jnp.stack(kr_p), jnp.stack(pool_p), jnp.stack(conv_p), jnp.stack(lru_p),
            jnp.stack(ckv_s), jnp.stack(kr_s), jnp.stack(pool_s), jnp.stack(conv_s), jnp.stack(lru_s))
```

```python
import functools

import numpy as np
import jax
import jax.numpy as jnp
from jax import lax
from jax.experimental import pallas as pl
from jax.experimental.pallas import tpu as pltpu

D_MODEL = 1024
PAGE_SIZE = 128
POOL_WINDOWS = (2, 4, 8, 16)
POOL_GROUP_DIM = 128
POOL_DIM = len(POOL_WINDOWS) * POOL_GROUP_DIM
POOL_HIST = max(POOL_WINDOWS) - 1
N_HEADS = 8
QK_NOPE_DIM = 64
QK_ROPE_DIM = 32
QK_HEAD_DIM = QK_NOPE_DIM + QK_ROPE_DIM
V_HEAD_DIM = 64
KV_RANK = 256
Q_DIM = N_HEADS * QK_HEAD_DIM
ATTN_OUT_DIM = N_HEADS * V_HEAD_DIM
ROPE_BASE = 10000.0
SOFTMAX_SCALE = QK_HEAD_DIM ** -0.5
RNN_DIM = D_MODEL
RNN_BLOCKS = 8
RNN_BLOCK_DIM = RNN_DIM // RNN_BLOCKS
CONV_WIDTH = 4
LRU_C = 8.0
D_FF = 4 * D_MODEL
NORM_EPS = 1e-6

LANES = 128
SUBLANES = 8
VMEM_LIMIT_BYTES = 56 * 2 ** 20

HEAD_PAD = LANES
QP_DIM = N_HEADS * HEAD_PAD
W1_DIM = POOL_DIM + QP_DIM + KV_RANK + HEAD_PAD
ROPE_LANE0 = QK_NOPE_DIM
HALF = QK_ROPE_DIM // 2

F32 = jnp.float32
BF16 = jnp.bfloat16


def _dot(a, b):
    return jnp.dot(a, b, preferred_element_type=F32)


def _dot_nt(a, b):
    return lax.dot_general(a, b, (((1,), (1,)), ((), ())), preferred_element_type=F32)


def _dot_tn(a, b):
    return lax.dot_general(a, b, (((0,), (0,)), ((), ())), preferred_element_type=F32)


def _rms(x, g):
    ms = jnp.mean(x * x, axis=-1, keepdims=True)
    return x * lax.rsqrt(ms + NORM_EPS) * g


def _expand(rs, e):
    hi = rs.astype(BF16)
    lo = (rs - hi.astype(F32)).astype(BF16)
    return _dot(hi, e) + _dot(lo, e)


def _gelu_tanh(x):
    return 0.5 * x * (1.0 + jnp.tanh(0.7978845608028654 * (x + 0.044715 * (x * x * x))))


def _sigmoid(x):
    return 1.0 / (1.0 + jnp.exp(-x))


def _sqrt_one_minus_exp2(x):
    t = jnp.tanh(x)
    return jnp.sqrt(-2.0 * t / (1.0 - t))


def _const_spec(shape, grid_rank):
    zeros = (0,) * len(shape)
    if grid_rank == 1:
        return pl.BlockSpec(shape, lambda i: zeros, pipeline_mode=pl.Buffered(1))
    return pl.BlockSpec(shape, lambda i, j: zeros, pipeline_mode=pl.Buffered(1))


def _params(*sem):
    return pltpu.CompilerParams(dimension_semantics=sem, vmem_limit_bytes=VMEM_LIMIT_BYTES)


def _even_in_kernel(*refs, tm, prompt):
    if prompt:
        (x_ref, tab_ref, gmix_ref, w1_ref, gq_ref, segq_ref, eq_ref, gckv_ref, gkr_ref,
         wuk_ref, segk_ref, ek_ref, gk_ref, wuv_ref, wpool_ref, pscale_ref,
         q_ref, c_ref, krp_ref, k_ref, v_ref, pool_ref, utail_ref, uext_ref) = refs
    else:
        (x_ref, tab_ref, gmix_ref, w1_ref, gq_ref, segq_ref, eq_ref, gckv_ref, gkr_ref,
         q_ref, c_ref, krp_ref, u_ref) = refs

    xn = _rms(x_ref[...], gmix_ref[...]).astype(BF16)
    z = _dot(xn, w1_ref[...])
    u = z[:, 0:POOL_DIM]
    qz = z[:, POOL_DIM:POOL_DIM + QP_DIM]
    cz = z[:, POOL_DIM + QP_DIM:POOL_DIM + QP_DIM + KV_RANK]
    krz = z[:, POOL_DIM + QP_DIM + KV_RANK:]

    ta = tab_ref[:, 0:LANES]
    tb = tab_ref[:, LANES:2 * LANES]
    tc = tab_ref[:, 2 * LANES:3 * LANES]

    def rope(blk):
        return blk * ta + pltpu.roll(blk, HALF, 1) * tb + pltpu.roll(blk, LANES - HALF, 1) * tc

    msq = _dot((qz * qz).astype(BF16), segq_ref[...])
    qn = qz * _expand(lax.rsqrt(msq + NORM_EPS), eq_ref[...]) * gq_ref[...]
    for h in range(N_HEADS):
        lanes = slice(h * HEAD_PAD, (h + 1) * HEAD_PAD)
        q_ref[:, lanes] = rope(qn[:, lanes]).astype(q_ref.dtype)

    c = _rms(cz, gckv_ref[...])
    c_ref[...] = c
    mskr = jnp.sum(krz * krz, axis=-1, keepdims=True) * (1.0 / QK_ROPE_DIM)
    krr = rope(krz * lax.rsqrt(mskr + NORM_EPS) * gkr_ref[...])
    krp_ref[...] = krr

    if not prompt:
        u_ref[...] = u
        return

    cb = c.astype(BF16)
    kn = _dot(cb, wuk_ref[...])
    msk = _dot((kn * kn).astype(BF16), segk_ref[...])
    knn = kn * _expand(lax.rsqrt(msk + NORM_EPS), ek_ref[...]) * gk_ref[...]
    for h in range(N_HEADS):
        lanes = slice(h * HEAD_PAD, (h + 1) * HEAD_PAD)
        k_ref[:, lanes] = (knn[:, lanes] + krr).astype(BF16)
    v_ref[...] = _dot(cb, wuv_ref[...]).astype(BF16)

    i = pl.program_id(1)
    hal = 2 * SUBLANES

    @pl.when(i == 0)
    def _():
        uext_ref[0:hal, :] = jnp.zeros((hal, POOL_DIM), F32)

    uext_ref[hal:hal + tm, :] = u
    pos = lax.broadcasted_iota(jnp.int32, (tm, POOL_GROUP_DIM), 0) + i * tm
    for g, w in enumerate(POOL_WINDOWS):
        cols = slice(g * POOL_GROUP_DIM, (g + 1) * POOL_GROUP_DIM)
        acc = uext_ref[hal:hal + tm, cols]
        for k in range(1, w):
            acc = acc + uext_ref[hal - k:hal - k + tm, cols]
        cnt = jnp.minimum(pos + 1, w).astype(F32)
        d = acc / cnt - u[:, cols]
        y = _dot(d.astype(BF16), wpool_ref[g]) * pscale_ref[:, cols]
        pool_ref[:, cols] = y.astype(BF16)
    uext_ref[0:hal, :] = uext_ref[tm:tm + hal, :]

    @pl.when(i == pl.num_programs(1) - 1)
    def _():
        utail_ref[...] = uext_ref[0:hal, :]


def _seg_mats():
    segq = np.zeros((QP_DIM, LANES), np.float32)
    eq = np.zeros((LANES, QP_DIM), np.float32)
    segk = np.zeros((QP_DIM, LANES), np.float32)
    ek = np.zeros((LANES, QP_DIM), np.float32)
    for h in range(N_HEADS):
        b = h * HEAD_PAD
        segq[b:b + QK_NOPE_DIM, 2 * h] = 1.0 / QK_NOPE_DIM
        segq[b + QK_NOPE_DIM:b + QK_HEAD_DIM, 2 * h + 1] = 1.0 / QK_ROPE_DIM
        eq[2 * h, b:b + QK_NOPE_DIM] = 1.0
        eq[2 * h + 1, b + QK_NOPE_DIM:b + QK_HEAD_DIM] = 1.0
        segk[b:b + QK_NOPE_DIM, h] = 1.0 / QK_NOPE_DIM
        ek[h, b:b + QK_NOPE_DIM] = 1.0
    return [jnp.asarray(m, BF16) for m in (segq, eq, segk, ek)]


def _rope_table(pos):
    inv = ROPE_BASE ** (-jnp.arange(HALF, dtype=F32) / HALF)
    ang = pos.astype(F32)[:, None] * inv[None, :]
    cos, sin = jnp.cos(ang), jnp.sin(ang)
    n = pos.shape[0]
    one = jnp.ones((n, ROPE_LANE0), F32)
    zero = jnp.zeros((n, ROPE_LANE0), F32)
    zh = jnp.zeros((n, HALF), F32)
    tail1 = jnp.ones((n, LANES - ROPE_LANE0 - QK_ROPE_DIM), F32)
    tail0 = jnp.zeros((n, LANES - ROPE_LANE0 - QK_ROPE_DIM), F32)
    ta = jnp.concatenate([one, cos, cos, tail1], axis=1)
    tb = jnp.concatenate([zero, zh, sin, tail0], axis=1)
    tc = jnp.concatenate([zero, -sin, zh, tail0], axis=1)
    return jnp.concatenate([ta, tb, tc], axis=1)


def _head_pad(w, lead):
    d = w.shape[-1]
    return jnp.pad(w, ((0, 0), (0, 0), (0, HEAD_PAD - d))).reshape(lead, QP_DIM)


def _head_row(parts):
    row = jnp.concatenate(parts)
    row = jnp.pad(row, (0, HEAD_PAD - row.shape[0]))
    return jnp.tile(row, N_HEADS)[None, :]


def _even_weights(norm_mix, w_in, g_q_nope, g_q_rope, g_ckv, g_k_rope, g_k_nope, w_uk, w_uv, w_pool, pool_scale):
    wq = _head_pad(w_in[:, POOL_DIM:POOL_DIM + Q_DIM].reshape(D_MODEL, N_HEADS, QK_HEAD_DIM), D_MODEL)
    wc = w_in[:, POOL_DIM + Q_DIM:POOL_DIM + Q_DIM + KV_RANK]
    wkr = jnp.pad(w_in[:, POOL_DIM + Q_DIM + KV_RANK:], ((0, 0), (ROPE_LANE0, HEAD_PAD - QK_HEAD_DIM)))
    w1 = jnp.concatenate([w_in[:, :POOL_DIM], wq, wc, wkr], axis=1).astype(BF16)
    gq = _head_row([g_q_nope, g_q_rope]) * SOFTMAX_SCALE
    gkr = _head_row([jnp.zeros((ROPE_LANE0,), F32), g_k_rope])[:, :HEAD_PAD]
    gk = _head_row([g_k_nope])
    wuk = _head_pad(w_uk, KV_RANK).astype(BF16)
    v_even = jnp.pad(w_uv, ((0, 0), (0, 0), (0, HEAD_PAD - V_HEAD_DIM)))
    v_odd = jnp.pad(w_uv, ((0, 0), (0, 0), (HEAD_PAD - V_HEAD_DIM, 0)))
    odd = (jnp.arange(N_HEADS) % 2 == 1)[None, :, None]
    wuv = jnp.where(odd, v_odd, v_even).reshape(KV_RANK, QP_DIM).astype(BF16)
    return dict(gmix=norm_mix[None, :], w1=w1, gq=gq, gckv=g_ckv[None, :], gkr=gkr, gk=gk, wuk=wuk, wuv=wuv,
                wpool=w_pool.astype(BF16), pscale=pool_scale[None, :])


def _even_in_prompt(x, tab, ew, tm):
    nb, seq, _ = x.shape
    nt = seq // tm
    segq, eq, segk, ek = _seg_mats()
    row = lambda width: pl.BlockSpec((None, tm, width), lambda b, i: (b, i, 0))
    cs = lambda a: _const_spec(a.shape, 2)
    consts = [ew['gmix'], ew['w1'], ew['gq'], segq, eq, ew['gckv'], ew['gkr'],
              ew['wuk'], segk, ek, ew['gk'], ew['wuv'], ew['wpool'], ew['pscale']]
    out_shape = (
        jax.ShapeDtypeStruct((nb, seq, QP_DIM), BF16),
        jax.ShapeDtypeStruct((nb, seq, KV_RANK), F32),
        jax.ShapeDtypeStruct((nb, seq, HEAD_PAD), F32),
        jax.ShapeDtypeStruct((nb, seq, QP_DIM), BF16),
        jax.ShapeDtypeStruct((nb, seq, QP_DIM), BF16),
        jax.ShapeDtypeStruct((nb, seq, POOL_DIM), BF16),
        jax.ShapeDtypeStruct((nb, 2 * SUBLANES, POOL_DIM), F32),
    )
    out_specs = (row(QP_DIM), row(KV_RANK), row(HEAD_PAD), row(QP_DIM), row(QP_DIM), row(POOL_DIM),
                 pl.BlockSpec((None, 2 * SUBLANES, POOL_DIM), lambda b, i: (b, 0, 0)))
    return pl.pallas_call(
        functools.partial(_even_in_kernel, tm=tm, prompt=True),
        grid=(nb, nt),
        in_specs=[row(D_MODEL), pl.BlockSpec((tm, 3 * LANES), lambda b, i: (i, 0))] + [cs(a) for a in consts],
        out_specs=out_specs,
        out_shape=out_shape,
        scratch_shapes=[pltpu.VMEM((2 * SUBLANES + tm, POOL_DIM), F32)],
        compiler_params=_params("arbitrary", "arbitrary"),
        name="even_in_prompt",
    )(x, tab, *consts)


def _even_in_sample(x, tab, ew, tm):
    rows = x.shape[0]
    segq, eq, _, _ = _seg_mats()
    row = lambda width: pl.BlockSpec((tm, width), lambda i: (i, 0))
    cs = lambda a: _const_spec(a.shape, 1)
    consts = [ew['gmix'], ew['w1'], ew['gq'], segq, eq, ew['gckv'], ew['gkr']]
    out_shape = (
        jax.ShapeDtypeStruct((rows, QP_DIM), F32),
        jax.ShapeDtypeStruct((rows, KV_RANK), F32),
        jax.ShapeDtypeStruct((rows, HEAD_PAD), F32),
        jax.ShapeDtypeStruct((rows, POOL_DIM), F32),
    )
    return pl.pallas_call(
        functools.partial(_even_in_kernel, tm=tm, prompt=False),
        grid=(rows // tm,),
        in_specs=[row(D_MODEL), row(3 * LANES)] + [cs(a) for a in consts],
        out_specs=(row(QP_DIM), row(KV_RANK), row(HEAD_PAD), row(POOL_DIM)),
        out_shape=out_shape,
        compiler_params=_params("arbitrary"),
        name="even_in_sample",
    )(x, tab, *consts)


def _prompt_attn_kernel(q_ref, k_ref, v_ref, o_ref, *, tq):
    qi = pl.program_id(1)
    causal = (lax.broadcasted_iota(jnp.int32, (tq, tq), 0) >= lax.broadcasted_iota(jnp.int32, (tq, tq), 1))
    outs = []
    for h in range(N_HEADS):
        lanes = slice(h * HEAD_PAD, (h + 1) * HEAD_PAD)
        qh = q_ref[:, lanes]

        def tile(kj, carry, masked, lanes=lanes, qh=qh):
            m, l, acc = carry
            off = pl.multiple_of(kj * tq, tq)
            s = _dot_nt(qh, k_ref[pl.ds(off, tq), lanes])
            if masked:
                s = jnp.where(causal, s, -jnp.inf)
            m_new = jnp.maximum(m, jnp.max(s, axis=-1, keepdims=True))
            alpha = jnp.exp(m - m_new)
            p = jnp.exp(s - m_new)
            l = alpha * l + jnp.sum(p, axis=-1, keepdims=True)
            acc = alpha * acc + _dot(p.astype(BF16), v_ref[pl.ds(off, tq), lanes])
            return m_new, l, acc

        init = (jnp.full((tq, 1), -jnp.inf, F32), jnp.zeros((tq, 1), F32), jnp.zeros((tq, HEAD_PAD), F32))
        carry = lax.fori_loop(0, qi, functools.partial(tile, masked=False), init)
        _, l, acc = tile(qi, carry, True)
        outs.append(acc / l)
    for j in range(N_HEADS // 2):
        o_ref[:, j * LANES:(j + 1) * LANES] = (outs[2 * j] + outs[2 * j + 1]).astype(BF16)


def _prompt_attn(q, k, v, tq):
    nb, seq, _ = q.shape
    full = pl.BlockSpec((None, seq, QP_DIM), lambda b, i: (b, 0, 0))
    return pl.pallas_call(
        functools.partial(_prompt_attn_kernel, tq=tq),
        grid=(nb, seq // tq),
        in_specs=[pl.BlockSpec((None, tq, QP_DIM), lambda b, i: (b, i, 0)), full, full],
        out_specs=pl.BlockSpec((None, tq, ATTN_OUT_DIM), lambda b, i: (b, i, 0)),
        out_shape=jax.ShapeDtypeStruct((nb, seq, ATTN_OUT_DIM), BF16),
        compiler_params=_params("arbitrary", "arbitrary"),
        name="prompt_attn",
    )(q, k, v)


def _sample_attn_kernel(pt_ref, q_ref, cn_ref, krn_ref, gk_ref, wukp_ref, wukd_ref, wuv_ref, segt_ref, *rest,
                        ppc, n_tok):
    del pt_ref
    pages = rest[:ppc]
    krpages = rest[ppc:2 * ppc]
    o_ref, qabs_ref, w2t_ref, m_ref, l_ref, acc_ref, u_ref = rest[2 * ppc:]
    j = pl.program_id(1)
    kc = ppc * PAGE_SIZE
    ncol = N_HEADS * n_tok

    @pl.when(j == 0)
    def _():
        q = q_ref[...]
        qt = jnp.concatenate([q] * N_HEADS + [jnp.zeros((LANES - ncol, QP_DIM), F32)], axis=0)
        r = lax.broadcasted_iota(jnp.int32, (LANES, QP_DIM), 0)
        ln = lax.broadcasted_iota(jnp.int32, (LANES, QP_DIM), 1)
        keep = jnp.where((ln & (HEAD_PAD - 1)) < QK_NOPE_DIM, r // n_tok, -1) == ln // HEAD_PAD
        qg = jnp.where(keep, qt * gk_ref[...], 0.0).astype(BF16)
        qabs_ref[...] = _dot_nt(qg, wukp_ref[...]).astype(BF16)
        lane = lax.broadcasted_iota(jnp.int32, (n_tok, LANES), 1)
        blocks = [jnp.where(lane < QK_ROPE_DIM,
                            pltpu.roll(q[:, h * HEAD_PAD:(h + 1) * HEAD_PAD], LANES - ROPE_LANE0, 1), 0.0)
                  for h in range(N_HEADS)]
        qr = jnp.concatenate([jnp.zeros((LANES - ncol, LANES), F32)] + blocks, axis=0)
        w2t_ref[:, 0:LANES] = segt_ref[...]
        w2t_ref[:, LANES:2 * LANES] = qr.astype(BF16)
        m_ref[...] = jnp.full(m_ref.shape, -jnp.inf, F32)
        l_ref[...] = jnp.zeros(l_ref.shape, F32)
        acc_ref[...] = jnp.zeros(acc_ref.shape, F32)
        u_ref[...] = jnp.zeros(u_ref.shape, F32)

    def scores(cb, ub):
        g = _dot_nt(ub, w2t_ref[...])
        sp = _dot_nt(cb, qabs_ref[...])
        lane = lax.broadcasted_iota(jnp.int32, g.shape, 1)
        r = lax.rsqrt(jnp.where(lane < ncol, g, 1.0) + NORM_EPS)
        return sp * r + pltpu.roll(g, LANES - ncol, 1)

    def online(s, cb):
        m_old = m_ref[...]
        m_new = jnp.maximum(m_old, jnp.max(s, axis=0, keepdims=True))
        alpha = jnp.exp(m_old - m_new)
        p = jnp.exp(s - m_new)
        l_ref[...] = alpha * l_ref[...] + jnp.sum(p, axis=0, keepdims=True)
        acc_ref[...] = acc_ref[...] * alpha + _dot_tn(cb, p.astype(BF16))
        m_ref[...] = m_new

    def sumsq(cb):
        k2 = _dot(cb, wukd_ref[...])
        k2 = k2 * k2
        return (k2[:, 0:LANES] + k2[:, LANES:2 * LANES]) + (k2[:, 2 * LANES:3 * LANES] + k2[:, 3 * LANES:])

    cb = jnp.concatenate([p[...] for p in pages], axis=0).astype(BF16)
    u_ref[:, 0:LANES] = sumsq(cb)
    for i, kp in enumerate(krpages):
        u_ref[i * PAGE_SIZE:(i + 1) * PAGE_SIZE, LANES:LANES + QK_ROPE_DIM] = kp[...]
    online(scores(cb, u_ref[...].astype(BF16)), cb)

    @pl.when(j == pl.num_programs(1) - 1)
    def _():
        cn = jnp.concatenate([cn_ref[...], jnp.zeros((PAGE_SIZE - n_tok, KV_RANK), F32)], axis=0).astype(BF16)
        krn = jnp.concatenate([pltpu.roll(krn_ref[...], LANES - ROPE_LANE0, 1),
                               jnp.zeros((PAGE_SIZE - n_tok, LANES), F32)], axis=0)
        un = jnp.concatenate([sumsq(cn), krn], axis=1).astype(BF16)
        s = scores(cn, un)
        key = lax.broadcasted_iota(jnp.int32, s.shape, 0)
        tq = lax.broadcasted_iota(jnp.int32, s.shape, 1) % n_tok
        online(jnp.where(key <= tq, s, -jnp.inf), cn)
        lat = (acc_ref[...] * (1.0 / l_ref[...])).T
        zz = _dot(lat.astype(BF16), wuv_ref[...])
        lane_h = lax.broadcasted_iota(jnp.int32, (n_tok, ATTN_OUT_DIM), 1) // V_HEAD_DIM
        out = jnp.zeros((n_tok, ATTN_OUT_DIM), F32)
        for h in range(N_HEADS):
            out = out + jnp.where(lane_h == h, zz[h * n_tok:(h + 1) * n_tok, :], 0.0)
        o_ref[...] = out


def _sample_attn(q, c, krp, cache_ckv, cache_krope, page_table, g_k_nope, w_uk, w_uv, n_tok, ppc):
    nb, n_pages = page_table.shape
    nch = n_pages // ppc
    ncol = N_HEADS * n_tok
    gk = _head_row([g_k_nope])
    wukp = _head_pad(w_uk, KV_RANK).astype(BF16)
    wukd = w_uk.transpose(0, 2, 1).reshape(KV_RANK, N_HEADS * QK_NOPE_DIM).astype(BF16)
    wuv = w_uv.reshape(KV_RANK, ATTN_OUT_DIM).astype(BF16)
    segt = np.zeros((LANES, LANES), np.float32)
    for col in range(ncol):
        segt[col, (np.arange(LANES) % N_HEADS) == col // n_tok] = 1.0 / QK_NOPE_DIM
    segt = jnp.asarray(segt, BF16)
    pt = page_table.reshape(-1)

    def page_map(i):
        return lambda b, j, pt_ref: (0, pt_ref[b * n_pages + j * ppc + i], 0, 0)

    rows = lambda width: pl.BlockSpec((n_tok, width), lambda b, j, pt_ref: (b, 0))
    const = lambda a: pl.BlockSpec(a.shape, lambda b, j, pt_ref: (0,) * a.ndim)
    in_specs = ([rows(QP_DIM), rows(KV_RANK), rows(HEAD_PAD), const(gk), const(wukp), const(wukd), const(wuv),
                 const(segt)]
                + [pl.BlockSpec((None, None, PAGE_SIZE, KV_RANK), page_map(i)) for i in range(ppc)]
                + [pl.BlockSpec((None, None, PAGE_SIZE, QK_ROPE_DIM), page_map(i)) for i in range(ppc)])
    grid_spec = pltpu.PrefetchScalarGridSpec(
        num_scalar_prefetch=1,
        grid=(nb, nch),
        in_specs=in_specs,
        out_specs=rows(ATTN_OUT_DIM),
        scratch_shapes=[pltpu.VMEM((LANES, KV_RANK), BF16), pltpu.VMEM((LANES, 2 * LANES), BF16),
                        pltpu.VMEM((1, LANES), F32), pltpu.VMEM((1, LANES), F32),
                        pltpu.VMEM((KV_RANK, LANES), F32), pltpu.VMEM((ppc * PAGE_SIZE, 2 * LANES), F32)],
    )
    return pl.pallas_call(
        functools.partial(_sample_attn_kernel, ppc=ppc, n_tok=n_tok),
        grid_spec=grid_spec,
        out_shape=jax.ShapeDtypeStruct((nb * n_tok, ATTN_OUT_DIM), F32),
        compiler_params=_params("arbitrary", "arbitrary"),
        name="sample_attn",
    )(pt, q, c, krp, gk, wukp, wukd, wuv, segt, *([cache_ckv] * ppc), *([cache_krope] * ppc))


def _pool_sample_kernel(u_ref, hist_ref, wpool_ref, pscale_ref, o_ref, *, n_tok, n_past):
    ext = [hist_ref[k] for k in range(POOL_HIST)] + [u_ref[t] for t in range(n_tok)]
    nb = u_ref.shape[1]
    for g, w in enumerate(POOL_WINDOWS):
        cols = slice(g * POOL_GROUP_DIM, (g + 1) * POOL_GROUP_DIM)
        ds = []
        for t in range(n_tok):
            acc = ext[POOL_HIST + t][:, cols]
            for k in range(1, w):
                acc = acc + ext[POOL_HIST + t - k][:, cols]
            ds.append(acc / float(min(n_past + t + 1, w)) - ext[POOL_HIST + t][:, cols])
        d = jnp.concatenate(ds, axis=0).astype(BF16)
        y = (_dot(d, wpool_ref[g]) * pscale_ref[:, cols]).astype(BF16)
        for t in range(n_tok):
            o_ref[t, :, cols] = y[t * nb:(t + 1) * nb, :]


def _pool_sample(u_tm, hist_tm, w_pool, pscale, n_past):
    n_tok, nb, _ = u_tm.shape
    return pl.pallas_call(
        functools.partial(_pool_sample_kernel, n_tok=n_tok, n_past=n_past),
        out_shape=jax.ShapeDtypeStruct((n_tok, nb, POOL_DIM), BF16),
        compiler_params=pltpu.CompilerParams(vmem_limit_bytes=VMEM_LIMIT_BYTES),
        name="pool_sample",
    )(u_tm, hist_tm, w_pool, pscale)


def _mix_mlp_kernel(x_ref, a1_ref, a2_ref, wo1_ref, wo2_ref, gffn_ref, wup_ref, wdn_ref, y_ref, *, ck):
    y_ref[...] = (x_ref[...] + _dot(a1_ref[...].astype(BF16), wo1_ref[...])
                  + _dot(a2_ref[...].astype(BF16), wo2_ref[...]))
    xn = _rms(y_ref[...], gffn_ref[...]).astype(BF16)
    for c in range(D_FF // ck):
        h = jnp.maximum(_dot(xn, wup_ref[:, c * ck:(c + 1) * ck]), 0.0)
        y_ref[...] += _dot((h * h).astype(BF16), wdn_ref[c * ck:(c + 1) * ck, :])


def _mix_mlp(x, a1, a1_col, a2, a2_col, w_out, g_ffn, w_up, w_down, tm, ck=1024):
    rows = x.shape[0]
    half = D_MODEL // 2
    wo = w_out.astype(BF16)
    consts = [wo[:half], wo[half:], g_ffn[None, :], w_up.astype(BF16), w_down.astype(BF16)]
    row = pl.BlockSpec((tm, D_MODEL), lambda i: (i, 0))
    return pl.pallas_call(
        functools.partial(_mix_mlp_kernel, ck=ck),
        grid=(rows // tm,),
        in_specs=[row, pl.BlockSpec((tm, half), lambda i: (i, a1_col)), pl.BlockSpec((tm, half), lambda i: (i, a2_col))]
                 + [_const_spec(a.shape, 1) for a in consts],
        out_specs=row,
        out_shape=jax.ShapeDtypeStruct((rows, D_MODEL), F32),
        compiler_params=_params("arbitrary"),
        name="mix_mlp",
    )(x, a1, a2, *consts)


def _gates(v, wg_ref, bga_ref, bgx_ref, lam_ref):
    vb = v.astype(BF16)
    pair = 2 * RNN_BLOCK_DIM
    ga, gx = [], []
    for p in range(RNN_BLOCKS // 2):
        g = _dot(vb[:, p * pair:(p + 1) * pair], wg_ref[p])
        ga.append(g[:, :pair])
        gx.append(g[:, pair:])
    r = _sigmoid(jnp.concatenate(ga, axis=1) + bga_ref[...])
    ig = _sigmoid(jnp.concatenate(gx, axis=1) + bgx_ref[...])
    nl = -lam_ref[...]
    softplus = jnp.maximum(nl, 0.0) + jnp.log1p(jnp.exp(-jnp.abs(nl)))
    log_a = (-LRU_C) * r * softplus
    return log_a, ig


def _odd_prompt_kernel(x_ref, gmix_ref, win_ref, cw_ref, cb_ref, wg_ref, bga_ref, bgx_ref, lam_ref,
                       y_ref, ctail_ref, hlast_ref, uext_ref, a_ref, b_ref, hcar_ref, *, tm):
    i = pl.program_id(1)
    sub = SUBLANES

    @pl.when(i == 0)
    def _():
        uext_ref[0:sub, :] = jnp.zeros((sub, RNN_DIM), F32)
        a_ref[0:sub, :] = jnp.zeros((sub, RNN_DIM), F32)
        b_ref[0:sub, :] = jnp.zeros((sub, RNN_DIM), F32)
        hcar_ref[...] = jnp.zeros((sub, RNN_DIM), F32)

    xn = _rms(x_ref[...], gmix_ref[...]).astype(BF16)
    z = _dot(xn, win_ref[...])
    gate = z[:, :RNN_DIM]
    uext_ref[sub:sub + tm, :] = z[:, RNN_DIM:]
    v = cb_ref[...]
    for k in range(CONV_WIDTH):
        r0 = sub - (CONV_WIDTH - 1) + k
        v = v + uext_ref[r0:r0 + tm, :] * cw_ref[k:k + 1, :]
    uext_ref[0:sub, :] = uext_ref[tm:tm + sub, :]

    log_a, ig = _gates(v, wg_ref, bga_ref, bgx_ref, lam_ref)
    a = jnp.exp(log_a)
    row = lax.broadcasted_iota(jnp.int32, (tm, RNN_DIM), 0)
    mult = jnp.where(row + i * tm == 0, 1.0, _sqrt_one_minus_exp2(log_a))
    b = mult * ig * v

    rowmod = row & (sub - 1)
    for s in (1, 2, 4):
        a_ref[sub:sub + tm, :] = a
        b_ref[sub:sub + tm, :] = b
        ok = rowmod >= s
        b = jnp.where(ok, a * b_ref[sub - s:sub - s + tm, :], 0.0) + b
        a = jnp.where(ok, a * a_ref[sub - s:sub - s + tm, :], a)
    a_ref[sub:sub + tm, :] = a
    b_ref[sub:sub + tm, :] = b

    def group(g, hb):
        off = pl.multiple_of(sub + g * sub, sub)
        hg = a_ref[pl.ds(off, sub), :] * hb + b_ref[pl.ds(off, sub), :]
        b_ref[pl.ds(off, sub), :] = hg
        return jnp.broadcast_to(hg[sub - 1:sub, :], (sub, RNN_DIM))

    hb = lax.fori_loop(0, tm // sub, group, hcar_ref[...])
    hcar_ref[...] = hb
    y_ref[...] = (_gelu_tanh(gate) * b_ref[sub:sub + tm, :]).astype(BF16)

    @pl.when(i == pl.num_programs(1) - 1)
    def _():
        ctail_ref[...] = uext_ref[0:sub, :]
        hlast_ref[...] = hb


def _odd_weights(norm_mix, w_in, conv_w, conv_b, w_ga, b_ga, w_gx, b_gx, lam):
    def pairs(w):
        z = jnp.zeros((RNN_BLOCKS // 2, RNN_BLOCK_DIM, RNN_BLOCK_DIM), F32)
        top = jnp.concatenate([w[0::2], z], axis=2)
        bot = jnp.concatenate([z, w[1::2]], axis=2)
        return jnp.concatenate([top, bot], axis=1)
    wg = jnp.concatenate([pairs(w_ga), pairs(w_gx)], axis=2).astype(BF16)
    return [norm_mix[None, :], w_in.astype(BF16), conv_w, conv_b[None, :], wg, b_ga[None, :], b_gx[None, :],
            lam[None, :]]


def _odd_prompt(x, ow, tm):
    nb, seq, _ = x.shape
    row = pl.BlockSpec((None, tm, D_MODEL), lambda b, i: (b, i, 0))
    tail = pl.BlockSpec((None, SUBLANES, RNN_DIM), lambda b, i: (b, 0, 0))
    return pl.pallas_call(
        functools.partial(_odd_prompt_kernel, tm=tm),
        grid=(nb, seq // tm),
        in_specs=[row] + [_const_spec(a.shape, 2) for a in ow],
        out_specs=(row, tail, tail),
        out_shape=(jax.ShapeDtypeStruct((nb, seq, RNN_DIM), BF16),
                   jax.ShapeDtypeStruct((nb, SUBLANES, RNN_DIM), F32),
                   jax.ShapeDtypeStruct((nb, SUBLANES, RNN_DIM), F32)),
        scratch_shapes=[pltpu.VMEM((SUBLANES + tm, RNN_DIM), F32), pltpu.VMEM((SUBLANES + tm, RNN_DIM), F32),
                        pltpu.VMEM((SUBLANES + tm, RNN_DIM), F32), pltpu.VMEM((SUBLANES, RNN_DIM), F32)],
        compiler_params=_params("arbitrary", "arbitrary"),
        name="odd_prompt",
    )(x, *ow)


def _odd_sample_kernel(x_ref, ch_ref, h0_ref, gmix_ref, win_ref, cw_ref, cb_ref, wg_ref, bga_ref, bgx_ref, lam_ref,
                       y_ref, ctail_ref, hlast_ref, *, n_tok, n_past):
    nb = x_ref.shape[1]
    x = x_ref[...].reshape(n_tok * nb, D_MODEL)
    z = _dot(_rms(x, gmix_ref[...]).astype(BF16), win_ref[...])
    gate = z[:, :RNN_DIM]
    ext = [ch_ref[k] for k in range(CONV_WIDTH - 1)] + [z[t * nb:(t + 1) * nb, RNN_DIM:] for t in range(n_tok)]
    vs = []
    for t in range(n_tok):
        v = cb_ref[...]
        for k in range(CONV_WIDTH):
            v = v + ext[t + k] * cw_ref[k:k + 1, :]
        vs.append(v)
    v = jnp.concatenate(vs, axis=0)
    log_a, ig = _gates(v, wg_ref, bga_ref, bgx_ref, lam_ref)
    a = jnp.exp(log_a)
    mult = _sqrt_one_minus_exp2(log_a)
    if n_past == 0:
        first = lax.broadcasted_iota(jnp.int32, mult.shape, 0) < nb
        mult = jnp.where(first, 1.0, mult)
    b = mult * ig * v
    h = h0_ref[...]
    hs = []
    for t in range(n_tok):
        h = a[t * nb:(t + 1) * nb, :] * h + b[t * nb:(t + 1) * nb, :]
        hs.append(h)
    y = (_gelu_tanh(gate) * jnp.concatenate(hs, axis=0)).astype(BF16)
    y_ref[...] = y.reshape(n_tok, nb, RNN_DIM)
    for k in range(CONV_WIDTH - 1):
        ctail_ref[k] = ext[n_tok + k]
    hlast_ref[...] = h


def _odd_sample(x_tm, ch_tm, h0, ow, n_past, bb=32):
    n_tok, nb, _ = x_tm.shape
    blk = lambda t, w: pl.BlockSpec((t, bb, w), lambda i: (0, i, 0))
    return pl.pallas_call(
        functools.partial(_odd_sample_kernel, n_tok=n_tok, n_past=n_past),
        grid=(nb // bb,),
        in_specs=[blk(n_tok, D_MODEL), blk(CONV_WIDTH - 1, RNN_DIM), pl.BlockSpec((bb, RNN_DIM), lambda i: (i, 0))]
                 + [_const_spec(a.shape, 1) for a in ow],
        out_specs=(blk(n_tok, RNN_DIM), blk(CONV_WIDTH - 1, RNN_DIM), pl.BlockSpec((bb, RNN_DIM), lambda i: (i, 0))),
        out_shape=(jax.ShapeDtypeStruct((n_tok, nb, RNN_DIM), BF16),
                   jax.ShapeDtypeStruct((CONV_WIDTH - 1, nb, RNN_DIM), F32),
                   jax.ShapeDtypeStruct((nb, RNN_DIM), F32)),
        compiler_params=_params("arbitrary"),
        name="odd_sample",
    )(x_tm, ch_tm, h0, *ow)


def kernel(x_prompt, x_sample, cache_ckv, cache_krope, state_pool, state_conv, state_lru, page_table, norm_mix,
           w_in_even, g_q_nope, g_q_rope, g_ckv, g_k_rope, g_k_nope, w_uk, w_uv, w_pool, pool_scale, w_out_even,
           w_in_rnn, conv_w, conv_b, w_gate_a, b_gate_a, w_gate_x, b_gate_x, lru_lambda, w_out_rnn, norm_ffn,
           w_up, w_down):
    nb, seq, _ = x_prompt.shape
    db, n_tok, _ = x_sample.shape
    n_past = page_table.shape[1] * PAGE_SIZE
    depth = norm_mix.shape[0]
    assert depth == 2 and cache_ckv.shape[0] == 1, "one even (pool + MLA) layer followed by one odd (RG-LRU) layer"
    rope_sl = slice(ROPE_LANE0, ROPE_LANE0 + QK_ROPE_DIM)

    ew = _even_weights(norm_mix[0], w_in_even[0], g_q_nope[0], g_q_rope[0], g_ckv[0], g_k_rope[0], g_k_nope[0],
                       w_uk[0], w_uv[0], w_pool[0], pool_scale[0])
    tab_p = _rope_table(jnp.arange(seq, dtype=jnp.int32))
    tab_s = jnp.tile(_rope_table(n_past + jnp.arange(n_tok, dtype=jnp.int32)), (db, 1))

    q_p, c_p, krp_p, k_p, v_p, pool_p, utail_p = _even_in_prompt(x_prompt, tab_p, ew, tm=512)
    attn_p = _prompt_attn(q_p, k_p, v_p, tq=512)
    xs = x_sample.reshape(db * n_tok, D_MODEL)
    q_s, c_s, krp_s, u_s = _even_in_sample(xs, tab_s, ew, tm=256)
    attn_s = _sample_attn(q_s, c_s, krp_s, cache_ckv, cache_krope, page_table, g_k_nope[0], w_uk[0], w_uv[0],
                          n_tok=n_tok, ppc=4)
    u_s3 = u_s.reshape(db, n_tok, POOL_DIM)
    pool_s = _pool_sample(u_s3.transpose(1, 0, 2), state_pool[0].transpose(1, 0, 2), ew['wpool'], ew['pscale'],
                          n_past)
    pool_s = pool_s.transpose(1, 0, 2).reshape(db * n_tok, POOL_DIM)

    mlp0 = (w_out_even[0], norm_ffn[0], w_up[0], w_down[0])
    yp = _mix_mlp(x_prompt.reshape(nb * seq, D_MODEL), pool_p.reshape(nb * seq, POOL_DIM), 0,
                  attn_p.reshape(nb * seq, ATTN_OUT_DIM), 0, *mlp0, tm=512)
    ys = _mix_mlp(xs, pool_s, 0, attn_s, 0, *mlp0, tm=256)

    ow = _odd_weights(norm_mix[1], w_in_rnn[0], conv_w[0], conv_b[0], w_gate_a[0], b_gate_a[0], w_gate_x[0],
                      b_gate_x[0], lru_lambda[0])
    rnn_p, ctail_p, hlast_p = _odd_prompt(yp.reshape(nb, seq, D_MODEL), ow, tm=256)
    ys_tm = ys.reshape(db, n_tok, D_MODEL).transpose(1, 0, 2)
    rnn_s, conv_s_tm, lru_s = _odd_sample(ys_tm, state_conv[0].transpose(1, 0, 2), state_lru[0], ow, n_past)

    mlp1 = (w_out_rnn[0], norm_ffn[1], w_up[1], w_down[1])
    rnn_p2 = rnn_p.reshape(nb * seq, RNN_DIM)
    yp = _mix_mlp(yp, rnn_p2, 0, rnn_p2, 1, *mlp1, tm=512)
    rnn_s2 = rnn_s.reshape(n_tok * db, RNN_DIM)
    ys_out = _mix_mlp(ys_tm.reshape(n_tok * db, D_MODEL), rnn_s2, 0, rnn_s2, 1, *mlp1, tm=256)
    ys_out = ys_out.reshape(n_tok, db, D_MODEL).transpose(1, 0, 2)

    pool_state_s = jnp.concatenate([state_pool[0], u_s3], axis=1)[:, -POOL_HIST:]
    return (yp.reshape(nb, seq, D_MODEL), ys_out,
            c_p[None], krp_p[None, :, :, rope_sl], utail_p[None, :, 1:], ctail_p[None, :, SUBLANES - CONV_WIDTH + 1:],
            hlast_p[None, :, 0],
            c_s.reshape(1, db, n_tok, KV_RANK), krp_s[:, rope_sl].reshape(1, db, n_tok, QK_ROPE_DIM),
            pool_state_s[None], conv_s_tm.transpose(1, 0, 2)[None], lru_s[None])
```

```python
import functools

import numpy as np
import jax
import jax.numpy as jnp
from jax import lax
from jax.experimental import pallas as pl
from jax.experimental.pallas import tpu as pltpu

D_MODEL = 1024
PAGE_SIZE = 128
POOL_WINDOWS = (2, 4, 8, 16)
POOL_GROUP_DIM = 128
POOL_DIM = len(POOL_WINDOWS) * POOL_GROUP_DIM
POOL_HIST = max(POOL_WINDOWS) - 1
N_HEADS = 8
QK_NOPE_DIM = 64
QK_ROPE_DIM = 32
QK_HEAD_DIM = QK_NOPE_DIM + QK_ROPE_DIM
V_HEAD_DIM = 64
KV_RANK = 256
Q_DIM = N_HEADS * QK_HEAD_DIM
ATTN_OUT_DIM = N_HEADS * V_HEAD_DIM
ROPE_BASE = 10000.0
SOFTMAX_SCALE = QK_HEAD_DIM ** -0.5
RNN_DIM = D_MODEL
RNN_BLOCKS = 8
RNN_BLOCK_DIM = RNN_DIM // RNN_BLOCKS
CONV_WIDTH = 4
LRU_C = 8.0
D_FF = 4 * D_MODEL
NORM_EPS = 1e-6

LANES = 128
SUBLANES = 8
VMEM_LIMIT_BYTES = 56 * 2 ** 20

HEAD_PAD = LANES
QP_DIM = N_HEADS * HEAD_PAD
W1_DIM = POOL_DIM + QP_DIM + KV_RANK + HEAD_PAD
ROPE_LANE0 = QK_NOPE_DIM
HALF = QK_ROPE_DIM // 2

F32 = jnp.float32
BF16 = jnp.bfloat16


def _dot(a, b):
    return jnp.dot(a, b, preferred_element_type=F32)


def _dot_nt(a, b):
    return lax.dot_general(a, b, (((1,), (1,)), ((), ())), preferred_element_type=F32)


def _dot_tn(a, b):
    return lax.dot_general(a, b, (((0,), (0,)), ((), ())), preferred_element_type=F32)


def _rms(x, g):
    ms = jnp.mean(x * x, axis=-1, keepdims=True)
    return x * lax.rsqrt(ms + NORM_EPS) * g


def _expand(rs, e):
    hi = rs.astype(BF16)
    lo = (rs - hi.astype(F32)).astype(BF16)
    return _dot(hi, e) + _dot(lo, e)


def _gelu_tanh(x):
    return 0.5 * x * (1.0 + jnp.tanh(0.7978845608028654 * (x + 0.044715 * (x * x * x))))


def _sigmoid(x):
    return 1.0 / (1.0 + jnp.exp(-x))


def _sqrt_one_minus_exp2(x):
    t = jnp.tanh(x)
    return jnp.sqrt(-2.0 * t / (1.0 - t))


def _const_spec(shape, grid_rank):
    zeros = (0,) * len(shape)
    if grid_rank == 1:
        return pl.BlockSpec(shape, lambda i: zeros, pipeline_mode=pl.Buffered(1))
    return pl.BlockSpec(shape, lambda i, j: zeros, pipeline_mode=pl.Buffered(1))


def _params(*sem):
    return pltpu.CompilerParams(dimension_semantics=sem, vmem_limit_bytes=VMEM_LIMIT_BYTES)


def _even_in_kernel(*refs, tm, prompt):
    if prompt:
        (x_ref, tab_ref, gmix_ref, w1_ref, gq_ref, segq_ref, eq_ref, gckv_ref, gkr_ref,
         wuk_ref, segk_ref, ek_ref, gk_ref, wuv_ref, wpool_ref, pscale_ref,
         q_ref, c_ref, krp_ref, k_ref, v_ref, pool_ref, utail_ref, uext_ref) = refs
    else:
        (x_ref, tab_ref, gmix_ref, w1_ref, gq_ref, segq_ref, eq_ref, gckv_ref, gkr_ref,
         q_ref, c_ref, krp_ref, u_ref) = refs

    xn = _rms(x_ref[...], gmix_ref[...]).astype(BF16)
    z = _dot(xn, w1_ref[...])
    u = z[:, 0:POOL_DIM]
    qz = z[:, POOL_DIM:POOL_DIM + QP_DIM]
    cz = z[:, POOL_DIM + QP_DIM:POOL_DIM + QP_DIM + KV_RANK]
    krz = z[:, POOL_DIM + QP_DIM + KV_RANK:]

    ta = tab_ref[:, 0:LANES]
    tb = tab_ref[:, LANES:2 * LANES]
    tc = tab_ref[:, 2 * LANES:3 * LANES]

    def rope(blk):
        return blk * ta + pltpu.roll(blk, HALF, 1) * tb + pltpu.roll(blk, LANES - HALF, 1) * tc

    msq = _dot((qz * qz).astype(BF16), segq_ref[...])
    qn = qz * _expand(lax.rsqrt(msq + NORM_EPS), eq_ref[...]) * gq_ref[...]
    for h in range(N_HEADS):
        lanes = slice(h * HEAD_PAD, (h + 1) * HEAD_PAD)
        q_ref[:, lanes] = rope(qn[:, lanes]).astype(q_ref.dtype)

    c = _rms(cz, gckv_ref[...])
    c_ref[...] = c
    mskr = jnp.sum(krz * krz, axis=-1, keepdims=True) * (1.0 / QK_ROPE_DIM)
    krr = rope(krz * lax.rsqrt(mskr + NORM_EPS) * gkr_ref[...])
    krp_ref[...] = krr

    if not prompt:
        u_ref[...] = u
        return

    cb = c.astype(BF16)
    kn = _dot(cb, wuk_ref[...])
    msk = _dot((kn * kn).astype(BF16), segk_ref[...])
    knn = kn * _expand(lax.rsqrt(msk + NORM_EPS), ek_ref[...]) * gk_ref[...]
    for h in range(N_HEADS):
        lanes = slice(h * HEAD_PAD, (h + 1) * HEAD_PAD)
        k_ref[:, lanes] = (knn[:, lanes] + krr).astype(BF16)
    v_ref[...] = _dot(cb, wuv_ref[...]).astype(BF16)

    i = pl.program_id(1)
    hal = 2 * SUBLANES

    @pl.when(i == 0)
    def _():
        uext_ref[0:hal, :] = jnp.zeros((hal, POOL_DIM), F32)

    uext_ref[hal:hal + tm, :] = u
    pos = lax.broadcasted_iota(jnp.int32, (tm, POOL_GROUP_DIM), 0) + i * tm
    for g, w in enumerate(POOL_WINDOWS):
        cols = slice(g * POOL_GROUP_DIM, (g + 1) * POOL_GROUP_DIM)
        acc = uext_ref[hal:hal + tm, cols]
        for k in range(1, w):
            acc = acc + uext_ref[hal - k:hal - k + tm, cols]
        cnt = jnp.minimum(pos + 1, w).astype(F32)
        d = acc / cnt - u[:, cols]
        y = _dot(d.astype(BF16), wpool_ref[g]) * pscale_ref[:, cols]
        pool_ref[:, cols] = y.astype(BF16)
    uext_ref[0:hal, :] = uext_ref[tm:tm + hal, :]

    @pl.when(i == pl.num_programs(1) - 1)
    def _():
        utail_ref[...] = uext_ref[0:hal, :]


def _seg_mats():
    segq = np.zeros((QP_DIM, LANES), np.float32)
    eq = np.zeros((LANES, QP_DIM), np.float32)
    segk = np.zeros((QP_DIM, LANES), np.float32)
    ek = np.zeros((LANES, QP_DIM), np.float32)
    for h in range(N_HEADS):
        b = h * HEAD_PAD
        segq[b:b + QK_NOPE_DIM, 2 * h] = 1.0 / QK_NOPE_DIM
        segq[b + QK_NOPE_DIM:b + QK_HEAD_DIM, 2 * h + 1] = 1.0 / QK_ROPE_DIM
        eq[2 * h, b:b + QK_NOPE_DIM] = 1.0
        eq[2 * h + 1, b + QK_NOPE_DIM:b + QK_HEAD_DIM] = 1.0
        segk[b:b + QK_NOPE_DIM, h] = 1.0 / QK_NOPE_DIM
        ek[h, b:b + QK_NOPE_DIM] = 1.0
    return [jnp.asarray(m, BF16) for m in (segq, eq, segk, ek)]


def _rope_table(pos):
    inv = ROPE_BASE ** (-jnp.arange(HALF, dtype=F32) / HALF)
    ang = pos.astype(F32)[:, None] * inv[None, :]
    cos, sin = jnp.cos(ang), jnp.sin(ang)
    n = pos.shape[0]
    one = jnp.ones((n, ROPE_LANE0), F32)
    zero = jnp.zeros((n, ROPE_LANE0), F32)
    zh = jnp.zeros((n, HALF), F32)
    tail1 = jnp.ones((n, LANES - ROPE_LANE0 - QK_ROPE_DIM), F32)
    tail0 = jnp.zeros((n, LANES - ROPE_LANE0 - QK_ROPE_DIM), F32)
    ta = jnp.concatenate([one, cos, cos, tail1], axis=1)
    tb = jnp.concatenate([zero, zh, sin, tail0], axis=1)
    tc = jnp.concatenate([zero, -sin, zh, tail0], axis=1)
    return jnp.concatenate([ta, tb, tc], axis=1)


def _head_pad(w, lead):
    d = w.shape[-1]
    return jnp.pad(w, ((0, 0), (0, 0), (0, HEAD_PAD - d))).reshape(lead, QP_DIM)


def _head_row(parts):
    row = jnp.concatenate(parts)
    row = jnp.pad(row, (0, HEAD_PAD - row.shape[0]))
    return jnp.tile(row, N_HEADS)[None, :]


def _even_weights(norm_mix, w_in, g_q_nope, g_q_rope, g_ckv, g_k_rope, g_k_nope, w_uk, w_uv, w_pool, pool_scale):
    wq = _head_pad(w_in[:, POOL_DIM:POOL_DIM + Q_DIM].reshape(D_MODEL, N_HEADS, QK_HEAD_DIM), D_MODEL)
    wc = w_in[:, POOL_DIM + Q_DIM:POOL_DIM + Q_DIM + KV_RANK]
    wkr = jnp.pad(w_in[:, POOL_DIM + Q_DIM + KV_RANK:], ((0, 0), (ROPE_LANE0, HEAD_PAD - QK_HEAD_DIM)))
    w1 = jnp.concatenate([w_in[:, :POOL_DIM], wq, wc, wkr], axis=1).astype(BF16)
    gq = _head_row([g_q_nope, g_q_rope]) * SOFTMAX_SCALE
    gkr = _head_row([jnp.zeros((ROPE_LANE0,), F32), g_k_rope])[:, :HEAD_PAD]
    gk = _head_row([g_k_nope])
    wuk = _head_pad(w_uk, KV_RANK).astype(BF16)
    v_even = jnp.pad(w_uv, ((0, 0), (0, 0), (0, HEAD_PAD - V_HEAD_DIM)))
    v_odd = jnp.pad(w_uv, ((0, 0), (0, 0), (HEAD_PAD - V_HEAD_DIM, 0)))
    odd = (jnp.arange(N_HEADS) % 2 == 1)[None, :, None]
    wuv = jnp.where(odd, v_odd, v_even).reshape(KV_RANK, QP_DIM).astype(BF16)
    return dict(gmix=norm_mix[None, :], w1=w1, gq=gq, gckv=g_ckv[None, :], gkr=gkr, gk=gk, wuk=wuk, wuv=wuv,
                wpool=w_pool.astype(BF16), pscale=pool_scale[None, :])


def _even_in_prompt(x, tab, ew, tm):
    nb, seq, _ = x.shape
    nt = seq // tm
    segq, eq, segk, ek = _seg_mats()
    row = lambda width: pl.BlockSpec((None, tm, width), lambda b, i: (b, i, 0))
    cs = lambda a: _const_spec(a.shape, 2)
    consts = [ew['gmix'], ew['w1'], ew['gq'], segq, eq, ew['gckv'], ew['gkr'],
              ew['wuk'], segk, ek, ew['gk'], ew['wuv'], ew['wpool'], ew['pscale']]
    out_shape = (
        jax.ShapeDtypeStruct((nb, seq, QP_DIM), BF16),
        jax.ShapeDtypeStruct((nb, seq, KV_RANK), F32),
        jax.ShapeDtypeStruct((nb, seq, HEAD_PAD), F32),
        jax.ShapeDtypeStruct((nb, seq, QP_DIM), BF16),
        jax.ShapeDtypeStruct((nb, seq, QP_DIM), BF16),
        jax.ShapeDtypeStruct((nb, seq, POOL_DIM), BF16),
        jax.ShapeDtypeStruct((nb, 2 * SUBLANES, POOL_DIM), F32),
    )
    out_specs = (row(QP_DIM), row(KV_RANK), row(HEAD_PAD), row(QP_DIM), row(QP_DIM), row(POOL_DIM),
                 pl.BlockSpec((None, 2 * SUBLANES, POOL_DIM), lambda b, i: (b, 0, 0)))
    return pl.pallas_call(
        functools.partial(_even_in_kernel, tm=tm, prompt=True),
        grid=(nb, nt),
        in_specs=[row(D_MODEL), pl.BlockSpec((tm, 3 * LANES), lambda b, i: (i, 0))] + [cs(a) for a in consts],
        out_specs=out_specs,
        out_shape=out_shape,
        scratch_shapes=[pltpu.VMEM((2 * SUBLANES + tm, POOL_DIM), F32)],
        compiler_params=_params("arbitrary", "arbitrary"),
        name="even_in_prompt",
    )(x, tab, *consts)


def _even_in_sample(x, tab, ew, tm):
    rows = x.shape[0]
    segq, eq, _, _ = _seg_mats()
    row = lambda width: pl.BlockSpec((tm, width), lambda i: (i, 0))
    cs = lambda a: _const_spec(a.shape, 1)
    consts = [ew['gmix'], ew['w1'], ew['gq'], segq, eq, ew['gckv'], ew['gkr']]
    out_shape = (
        jax.ShapeDtypeStruct((rows, QP_DIM), F32),
        jax.ShapeDtypeStruct((rows, KV_RANK), F32),
        jax.ShapeDtypeStruct((rows, HEAD_PAD), F32),
        jax.ShapeDtypeStruct((rows, POOL_DIM), F32),
    )
    return pl.pallas_call(
        functools.partial(_even_in_kernel, tm=tm, prompt=False),
        grid=(rows // tm,),
        in_specs=[row(D_MODEL), row(3 * LANES)] + [cs(a) for a in consts],
        out_specs=(row(QP_DIM), row(KV_RANK), row(HEAD_PAD), row(POOL_DIM)),
        out_shape=out_shape,
        compiler_params=_params("arbitrary"),
        name="even_in_sample",
    )(x, tab, *consts)


def _prompt_attn_kernel(q_ref, k_ref, v_ref, o_ref, *, tq):
    qi = pl.program_id(1)
    causal = (lax.broadcasted_iota(jnp.int32, (tq, tq), 0) >= lax.broadcasted_iota(jnp.int32, (tq, tq), 1))
    outs = []
    for h in range(N_HEADS):
        lanes = slice(h * HEAD_PAD, (h + 1) * HEAD_PAD)
        qh = q_ref[:, lanes]

        def tile(kj, carry, masked, lanes=lanes, qh=qh):
            m, l, acc = carry
            off = pl.multiple_of(kj * tq, tq)
            s = _dot_nt(qh, k_ref[pl.ds(off, tq), lanes])
            if masked:
                s = jnp.where(causal, s, -jnp.inf)
            m_new = jnp.maximum(m, jnp.max(s, axis=-1, keepdims=True))
            alpha = jnp.exp(m - m_new)
            p = jnp.exp(s - m_new)
            l = alpha * l + jnp.sum(p, axis=-1, keepdims=True)
            acc = alpha * acc + _dot(p.astype(BF16), v_ref[pl.ds(off, tq), lanes])
            return m_new, l, acc

        init = (jnp.full((tq, 1), -jnp.inf, F32), jnp.zeros((tq, 1), F32), jnp.zeros((tq, HEAD_PAD), F32))
        carry = lax.fori_loop(0, qi, functools.partial(tile, masked=False), init)
        _, l, acc = tile(qi, carry, True)
        outs.append(acc / l)
    for j in range(N_HEADS // 2):
        o_ref[:, j * LANES:(j + 1) * LANES] = (outs[2 * j] + outs[2 * j + 1]).astype(BF16)


def _prompt_attn(q, k, v, tq):
    nb, seq, _ = q.shape
    full = pl.BlockSpec((None, seq, QP_DIM), lambda b, i: (b, 0, 0))
    return pl.pallas_call(
        functools.partial(_prompt_attn_kernel, tq=tq),
        grid=(nb, seq // tq),
        in_specs=[pl.BlockSpec((None, tq, QP_DIM), lambda b, i: (b, i, 0)), full, full],
        out_specs=pl.BlockSpec((None, tq, ATTN_OUT_DIM), lambda b, i: (b, i, 0)),
        out_shape=jax.ShapeDtypeStruct((nb, seq, ATTN_OUT_DIM), BF16),
        compiler_params=_params("arbitrary", "arbitrary"),
        name="prompt_attn",
    )(q, k, v)


def _sample_attn_kernel(pt_ref, q_ref, cn_ref, krn_ref, gk_ref, wukp_ref, wukd_ref, wuv_ref, segt_ref,
                        ckv_hbm, kr_hbm, o_ref, cbuf, krbuf, sem, qabs_ref, w2t_ref, s_ref, cb_ref, acc_ref,
                        *, n_tok, n_pages, sbp):
    b = pl.program_id(0)
    slot = lax.rem(b, 2)
    sb = sbp * PAGE_SIZE
    n_sb = n_pages // sbp
    ncol = N_HEADS * n_tok

    def start_pages(row, i0, sl):
        for k in range(sbp):
            page = pt_ref[row * n_pages + i0 + k]
            pltpu.make_async_copy(ckv_hbm.at[0, page], cbuf.at[sl, i0 + k], sem.at[0, sl]).start()
            pltpu.make_async_copy(kr_hbm.at[0, page], krbuf.at[sl, i0 + k], sem.at[1, sl]).start()

    @pl.when(b == 0)
    def _():
        def first(i, carry):
            start_pages(0, i * sbp, 0)
            return carry
        lax.fori_loop(0, n_sb, first, 0)

    q = q_ref[...]
    qt = jnp.concatenate([q] * N_HEADS + [jnp.zeros((LANES - ncol, QP_DIM), F32)], axis=0)
    r = lax.broadcasted_iota(jnp.int32, (LANES, QP_DIM), 0)
    ln = lax.broadcasted_iota(jnp.int32, (LANES, QP_DIM), 1)
    keep = jnp.where((ln & (HEAD_PAD - 1)) < QK_NOPE_DIM, r // n_tok, -1) == ln // HEAD_PAD
    qg = jnp.where(keep, qt * gk_ref[...], 0.0).astype(BF16)
    qabs_ref[...] = _dot_nt(qg, wukp_ref[...]).astype(BF16)
    lane = lax.broadcasted_iota(jnp.int32, (n_tok, LANES), 1)
    blocks = [jnp.where(lane < QK_ROPE_DIM,
                        pltpu.roll(q[:, h * HEAD_PAD:(h + 1) * HEAD_PAD], LANES - ROPE_LANE0, 1), 0.0)
              for h in range(N_HEADS)]
    qr = jnp.concatenate([jnp.zeros((LANES - ncol, LANES), F32)] + blocks, axis=0)
    w2t_ref[:, 0:LANES] = segt_ref[...]
    w2t_ref[:, LANES:2 * LANES] = qr.astype(BF16)

    def scores(cb, ub):
        g = _dot_nt(ub, w2t_ref[...])
        sp = _dot_nt(cb, qabs_ref[...])
        lane = lax.broadcasted_iota(jnp.int32, g.shape, 1)
        rs = lax.rsqrt(jnp.where(lane < ncol, g, 1.0) + NORM_EPS)
        return sp * rs + pltpu.roll(g, LANES - ncol, 1)

    def sumsq(cb):
        k2 = _dot(cb, wukd_ref[...])
        k2 = k2 * k2
        return (k2[:, 0:LANES] + k2[:, LANES:2 * LANES]) + (k2[:, 2 * LANES:3 * LANES] + k2[:, 3 * LANES:])

    pltpu.make_async_copy(ckv_hbm.at[0, pl.ds(0, n_pages)], cbuf.at[slot], sem.at[0, slot]).wait()
    pltpu.make_async_copy(kr_hbm.at[0, pl.ds(0, n_pages)], krbuf.at[slot], sem.at[1, slot]).wait()

    zpad = jnp.zeros((LANES - QK_ROPE_DIM, PAGE_SIZE), F32)

    def score_block(i, m):
        @pl.when(b + 1 < pl.num_programs(0))
        def _():
            start_pages(b + 1, i * sbp, 1 - slot)

        off = pl.multiple_of(i * sb, sb)
        cb = cbuf[slot, pl.ds(i * sbp, sbp)].reshape(sb, KV_RANK).astype(BF16)
        cb_ref[pl.ds(off, sb), :] = cb
        kr = jnp.concatenate([jnp.concatenate([krbuf[slot, i * sbp + k], zpad], axis=0).T for k in range(sbp)],
                             axis=0)
        s = scores(cb, jnp.concatenate([sumsq(cb), kr], axis=1).astype(BF16))
        s_ref[pl.ds(off, sb), :] = s
        return jnp.maximum(m, jnp.max(s, axis=0, keepdims=True))

    m = lax.fori_loop(0, n_sb, score_block, jnp.full((1, LANES), -jnp.inf, F32))

    cn = jnp.concatenate([cn_ref[...], jnp.zeros((PAGE_SIZE - n_tok, KV_RANK), F32)], axis=0).astype(BF16)
    krn = jnp.concatenate([pltpu.roll(krn_ref[...], LANES - ROPE_LANE0, 1),
                           jnp.zeros((PAGE_SIZE - n_tok, LANES), F32)], axis=0)
    s = scores(cn, jnp.concatenate([sumsq(cn), krn], axis=1).astype(BF16))
    key = lax.broadcasted_iota(jnp.int32, s.shape, 0)
    tq = lax.broadcasted_iota(jnp.int32, s.shape, 1) % n_tok
    s = jnp.where(key <= tq, s, -jnp.inf)
    m = jnp.maximum(m, jnp.max(s, axis=0, keepdims=True))

    pt = jnp.exp(s - m).T
    acc_ref[...] = _dot(pt.astype(BF16), cn)
    l0 = jnp.sum(pt, axis=1, keepdims=True)

    def value_block(i, l):
        off = pl.multiple_of(i * sb, sb)
        pt = jnp.exp(s_ref[pl.ds(off, sb), :] - m).T
        acc_ref[...] += _dot(pt.astype(BF16), cb_ref[pl.ds(off, sb), :])
        return l + jnp.sum(pt, axis=1, keepdims=True)

    l = lax.fori_loop(0, n_sb, value_block, l0)

    lat = acc_ref[...] * (1.0 / l)
    zz = _dot(lat.astype(BF16), wuv_ref[...])
    lane_h = lax.broadcasted_iota(jnp.int32, (n_tok, ATTN_OUT_DIM), 1) // V_HEAD_DIM
    out = jnp.zeros((n_tok, ATTN_OUT_DIM), F32)
    for h in range(N_HEADS):
        out = out + jnp.where(lane_h == h, zz[h * n_tok:(h + 1) * n_tok, :], 0.0)
    o_ref[...] = out


def _sample_attn(q, c, krp, cache_ckv, cache_krope, page_table, g_k_nope, w_uk, w_uv, n_tok, sbp):
    nb, n_pages = page_table.shape
    ncol = N_HEADS * n_tok
    gk = _head_row([g_k_nope])
    wukp = _head_pad(w_uk, KV_RANK).astype(BF16)
    wukd = w_uk.transpose(0, 2, 1).reshape(KV_RANK, N_HEADS * QK_NOPE_DIM).astype(BF16)
    wuv = w_uv.reshape(KV_RANK, ATTN_OUT_DIM).astype(BF16)
    segt = np.zeros((LANES, LANES), np.float32)
    for col in range(ncol):
        segt[col, (np.arange(LANES) % N_HEADS) == col // n_tok] = 1.0 / QK_NOPE_DIM
    segt = jnp.asarray(segt, BF16)
    pt = page_table.reshape(-1)
    kr_t = jnp.swapaxes(cache_krope, 2, 3)

    rows = lambda width: pl.BlockSpec((n_tok, width), lambda b, pt_ref: (b, 0))
    const = lambda a: pl.BlockSpec(a.shape, lambda b, pt_ref: (0,) * a.ndim)
    hbm = pl.BlockSpec(memory_space=pl.ANY)
    n_keys = n_pages * PAGE_SIZE
    grid_spec = pltpu.PrefetchScalarGridSpec(
        num_scalar_prefetch=1,
        grid=(nb,),
        in_specs=[rows(QP_DIM), rows(KV_RANK), rows(HEAD_PAD), const(gk), const(wukp), const(wukd), const(wuv),
                  const(segt), hbm, hbm],
        out_specs=rows(ATTN_OUT_DIM),
        scratch_shapes=[pltpu.VMEM((2, n_pages, PAGE_SIZE, KV_RANK), F32),
                        pltpu.VMEM((2, n_pages, QK_ROPE_DIM, PAGE_SIZE), F32),
                        pltpu.SemaphoreType.DMA((2, 2)),
                        pltpu.VMEM((LANES, KV_RANK), BF16), pltpu.VMEM((LANES, 2 * LANES), BF16),
                        pltpu.VMEM((n_keys, LANES), F32), pltpu.VMEM((n_keys, KV_RANK), BF16),
                        pltpu.VMEM((LANES, KV_RANK), F32)],
    )
    return pl.pallas_call(
        functools.partial(_sample_attn_kernel, n_tok=n_tok, n_pages=n_pages, sbp=sbp),
        grid_spec=grid_spec,
        out_shape=jax.ShapeDtypeStruct((nb * n_tok, ATTN_OUT_DIM), F32),
        compiler_params=_params("arbitrary"),
        name="sample_attn",
    )(pt, q, c, krp, gk, wukp, wukd, wuv, segt, cache_ckv, kr_t)


def _pool_sample_kernel(u_ref, hist_ref, wpool_ref, pscale_ref, o_ref, *, n_tok, n_past):
    ext = [hist_ref[k] for k in range(POOL_HIST)] + [u_ref[t] for t in range(n_tok)]
    nb = u_ref.shape[1]
    for g, w in enumerate(POOL_WINDOWS):
        cols = slice(g * POOL_GROUP_DIM, (g + 1) * POOL_GROUP_DIM)
        ds = []
        for t in range(n_tok):
            acc = ext[POOL_HIST + t][:, cols]
            for k in range(1, w):
                acc = acc + ext[POOL_HIST + t - k][:, cols]
            ds.append(acc / float(min(n_past + t + 1, w)) - ext[POOL_HIST + t][:, cols])
        d = jnp.concatenate(ds, axis=0).astype(BF16)
        y = (_dot(d, wpool_ref[g]) * pscale_ref[:, cols]).astype(BF16)
        for t in range(n_tok):
            o_ref[t, :, cols] = y[t * nb:(t + 1) * nb, :]


def _pool_sample(u_tm, hist_tm, w_pool, pscale, n_past):
    n_tok, nb, _ = u_tm.shape
    return pl.pallas_call(
        functools.partial(_pool_sample_kernel, n_tok=n_tok, n_past=n_past),
        out_shape=jax.ShapeDtypeStruct((n_tok, nb, POOL_DIM), BF16),
        compiler_params=pltpu.CompilerParams(vmem_limit_bytes=VMEM_LIMIT_BYTES),
        name="pool_sample",
    )(u_tm, hist_tm, w_pool, pscale)


def _mix_mlp_kernel(x_ref, a1_ref, a2_ref, wo1_ref, wo2_ref, gffn_ref, wup_ref, wdn_ref, y_ref, *, ck):
    y_ref[...] = (x_ref[...] + _dot(a1_ref[...].astype(BF16), wo1_ref[...])
                  + _dot(a2_ref[...].astype(BF16), wo2_ref[...]))
    xn = _rms(y_ref[...], gffn_ref[...]).astype(BF16)
    for c in range(D_FF // ck):
        h = jnp.maximum(_dot(xn, wup_ref[:, c * ck:(c + 1) * ck]), 0.0)
        y_ref[...] += _dot((h * h).astype(BF16), wdn_ref[c * ck:(c + 1) * ck, :])


def _mix_mlp(x, a1, a1_col, a2, a2_col, w_out, g_ffn, w_up, w_down, tm, ck=1024):
    rows = x.shape[0]
    half = D_MODEL // 2
    wo = w_out.astype(BF16)
    consts = [wo[:half], wo[half:], g_ffn[None, :], w_up.astype(BF16), w_down.astype(BF16)]
    row = pl.BlockSpec((tm, D_MODEL), lambda i: (i, 0))
    return pl.pallas_call(
        functools.partial(_mix_mlp_kernel, ck=ck),
        grid=(rows // tm,),
        in_specs=[row, pl.BlockSpec((tm, half), lambda i: (i, a1_col)), pl.BlockSpec((tm, half), lambda i: (i, a2_col))]
                 + [_const_spec(a.shape, 1) for a in consts],
        out_specs=row,
        out_shape=jax.ShapeDtypeStruct((rows, D_MODEL), F32),
        compiler_params=_params("arbitrary"),
        name="mix_mlp",
    )(x, a1, a2, *consts)


def _gates(v, wg_ref, bga_ref, bgx_ref, lam_ref):
    vb = v.astype(BF16)
    pair = 2 * RNN_BLOCK_DIM
    ga, gx = [], []
    for p in range(RNN_BLOCKS // 2):
        g = _dot(vb[:, p * pair:(p + 1) * pair], wg_ref[p])
        ga.append(g[:, :pair])
        gx.append(g[:, pair:])
    r = _sigmoid(jnp.concatenate(ga, axis=1) + bga_ref[...])
    ig = _sigmoid(jnp.concatenate(gx, axis=1) + bgx_ref[...])
    nl = -lam_ref[...]
    softplus = jnp.maximum(nl, 0.0) + jnp.log1p(jnp.exp(-jnp.abs(nl)))
    log_a = (-LRU_C) * r * softplus
    return log_a, ig


def _odd_prompt_kernel(x_ref, gmix_ref, win_ref, cw_ref, cb_ref, wg_ref, bga_ref, bgx_ref, lam_ref,
                       y_ref, ctail_ref, hlast_ref, uext_ref, a_ref, b_ref, hcar_ref, *, tm):
    i = pl.program_id(1)
    sub = SUBLANES

    @pl.when(i == 0)
    def _():
        uext_ref[0:sub, :] = jnp.zeros((sub, RNN_DIM), F32)
        a_ref[0:sub, :] = jnp.zeros((sub, RNN_DIM), F32)
        b_ref[0:sub, :] = jnp.zeros((sub, RNN_DIM), F32)
        hcar_ref[...] = jnp.zeros((sub, RNN_DIM), F32)

    xn = _rms(x_ref[...], gmix_ref[...]).astype(BF16)
    z = _dot(xn, win_ref[...])
    gate = z[:, :RNN_DIM]
    uext_ref[sub:sub + tm, :] = z[:, RNN_DIM:]
    v = cb_ref[...]
    for k in range(CONV_WIDTH):
        r0 = sub - (CONV_WIDTH - 1) + k
        v = v + uext_ref[r0:r0 + tm, :] * cw_ref[k:k + 1, :]
    uext_ref[0:sub, :] = uext_ref[tm:tm + sub, :]

    log_a, ig = _gates(v, wg_ref, bga_ref, bgx_ref, lam_ref)
    a = jnp.exp(log_a)
    row = lax.broadcasted_iota(jnp.int32, (tm, RNN_DIM), 0)
    mult = jnp.where(row + i * tm == 0, 1.0, _sqrt_one_minus_exp2(log_a))
    b = mult * ig * v

    rowmod = row & (sub - 1)
    for s in (1, 2, 4):
        a_ref[sub:sub + tm, :] = a
        b_ref[sub:sub + tm, :] = b
        ok = rowmod >= s
        b = jnp.where(ok, a * b_ref[sub - s:sub - s + tm, :], 0.0) + b
        a = jnp.where(ok, a * a_ref[sub - s:sub - s + tm, :], a)
    a_ref[sub:sub + tm, :] = a
    b_ref[sub:sub + tm, :] = b

    def group(g, hb):
        off = pl.multiple_of(sub + g * sub, sub)
        hg = a_ref[pl.ds(off, sub), :] * hb + b_ref[pl.ds(off, sub), :]
        b_ref[pl.ds(off, sub), :] = hg
        return jnp.broadcast_to(hg[sub - 1:sub, :], (sub, RNN_DIM))

    hb = lax.fori_loop(0, tm // sub, group, hcar_ref[...])
    hcar_ref[...] = hb
    y_ref[...] = (_gelu_tanh(gate) * b_ref[sub:sub + tm, :]).astype(BF16)

    @pl.when(i == pl.num_programs(1) - 1)
    def _():
        ctail_ref[...] = uext_ref[0:sub, :]
        hlast_ref[...] = hb


def _odd_weights(norm_mix, w_in, conv_w, conv_b, w_ga, b_ga, w_gx, b_gx, lam):
    def pairs(w):
        z = jnp.zeros((RNN_BLOCKS // 2, RNN_BLOCK_DIM, RNN_BLOCK_DIM), F32)
        top = jnp.concatenate([w[0::2], z], axis=2)
        bot = jnp.concatenate([z, w[1::2]], axis=2)
        return jnp.concatenate([top, bot], axis=1)
    wg = jnp.concatenate([pairs(w_ga), pairs(w_gx)], axis=2).astype(BF16)
    return [norm_mix[None, :], w_in.astype(BF16), conv_w, conv_b[None, :], wg, b_ga[None, :], b_gx[None, :],
            lam[None, :]]


def _odd_prompt(x, ow, tm):
    nb, seq, _ = x.shape
    row = pl.BlockSpec((None, tm, D_MODEL), lambda b, i: (b, i, 0))
    tail = pl.BlockSpec((None, SUBLANES, RNN_DIM), lambda b, i: (b, 0, 0))
    return pl.pallas_call(
        functools.partial(_odd_prompt_kernel, tm=tm),
        grid=(nb, seq // tm),
        in_specs=[row] + [_const_spec(a.shape, 2) for a in ow],
        out_specs=(row, tail, tail),
        out_shape=(jax.ShapeDtypeStruct((nb, seq, RNN_DIM), BF16),
                   jax.ShapeDtypeStruct((nb, SUBLANES, RNN_DIM), F32),
                   jax.ShapeDtypeStruct((nb, SUBLANES, RNN_DIM), F32)),
        scratch_shapes=[pltpu.VMEM((SUBLANES + tm, RNN_DIM), F32), pltpu.VMEM((SUBLANES + tm, RNN_DIM), F32),
                        pltpu.VMEM((SUBLANES + tm, RNN_DIM), F32), pltpu.VMEM((SUBLANES, RNN_DIM), F32)],
        compiler_params=_params("arbitrary", "arbitrary"),
        name="odd_prompt",
    )(x, *ow)


def _odd_sample_kernel(x_ref, ch_ref, h0_ref, gmix_ref, win_ref, cw_ref, cb_ref, wg_ref, bga_ref, bgx_ref, lam_ref,
                       y_ref, ctail_ref, hlast_ref, *, n_tok, n_past):
    nb = x_ref.shape[1]
    x = x_ref[...].reshape(n_tok * nb, D_MODEL)
    z = _dot(_rms(x, gmix_ref[...]).astype(BF16), win_ref[...])
    gate = z[:, :RNN_DIM]
    ext = [ch_ref[k] for k in range(CONV_WIDTH - 1)] + [z[t * nb:(t + 1) * nb, RNN_DIM:] for t in range(n_tok)]
    vs = []
    for t in range(n_tok):
        v = cb_ref[...]
        for k in range(CONV_WIDTH):
            v = v + ext[t + k] * cw_ref[k:k + 1, :]
        vs.append(v)
    v = jnp.concatenate(vs, axis=0)
    log_a, ig = _gates(v, wg_ref, bga_ref, bgx_ref, lam_ref)
    a = jnp.exp(log_a)
    mult = _sqrt_one_minus_exp2(log_a)
    if n_past == 0:
        first = lax.broadcasted_iota(jnp.int32, mult.shape, 0) < nb
        mult = jnp.where(first, 1.0, mult)
    b = mult * ig * v
    h = h0_ref[...]
    hs = []
    for t in range(n_tok):
        h = a[t * nb:(t + 1) * nb, :] * h + b[t * nb:(t + 1) * nb, :]
        hs.append(h)
    y = (_gelu_tanh(gate) * jnp.concatenate(hs, axis=0)).astype(BF16)
    y_ref[...] = y.reshape(n_tok, nb, RNN_DIM)
    for k in range(CONV_WIDTH - 1):
        ctail_ref[k] = ext[n_tok + k]
    hlast_ref[...] = h


def _odd_sample(x_tm, ch_tm, h0, ow, n_past, bb=32):
    n_tok, nb, _ = x_tm.shape
    blk = lambda t, w: pl.BlockSpec((t, bb, w), lambda i: (0, i, 0))
    return pl.pallas_call(
        functools.partial(_odd_sample_kernel, n_tok=n_tok, n_past=n_past),
        grid=(nb // bb,),
        in_specs=[blk(n_tok, D_MODEL), blk(CONV_WIDTH - 1, RNN_DIM), pl.BlockSpec((bb, RNN_DIM), lambda i: (i, 0))]
                 + [_const_spec(a.shape, 1) for a in ow],
        out_specs=(blk(n_tok, RNN_DIM), blk(CONV_WIDTH - 1, RNN_DIM), pl.BlockSpec((bb, RNN_DIM), lambda i: (i, 0))),
        out_shape=(jax.ShapeDtypeStruct((n_tok, nb, RNN_DIM), BF16),
                   jax.ShapeDtypeStruct((CONV_WIDTH - 1, nb, RNN_DIM), F32),
                   jax.ShapeDtypeStruct((nb, RNN_DIM), F32)),
        compiler_params=_params("arbitrary"),
        name="odd_sample",
    )(x_tm, ch_tm, h0, *ow)


def kernel(x_prompt, x_sample, cache_ckv, cache_krope, state_pool, state_conv, state_lru, page_table, norm_mix,
           w_in_even, g_q_nope, g_q_rope, g_ckv, g_k_rope, g_k_nope, w_uk, w_uv, w_pool, pool_scale, w_out_even,
           w_in_rnn, conv_w, conv_b, w_gate_a, b_gate_a, w_gate_x, b_gate_x, lru_lambda, w_out_rnn, norm_ffn,
           w_up, w_down):
    nb, seq, _ = x_prompt.shape
    db, n_tok, _ = x_sample.shape
    n_past = page_table.shape[1] * PAGE_SIZE
    depth = norm_mix.shape[0]
    assert depth == 2 and cache_ckv.shape[0] == 1, "one even (pool + MLA) layer followed by one odd (RG-LRU) layer"
    rope_sl = slice(ROPE_LANE0, ROPE_LANE0 + QK_ROPE_DIM)

    ew = _even_weights(norm_mix[0], w_in_even[0], g_q_nope[0], g_q_rope[0], g_ckv[0], g_k_rope[0], g_k_nope[0],
                       w_uk[0], w_uv[0], w_pool[0], pool_scale[0])
    tab_p = _rope_table(jnp.arange(seq, dtype=jnp.int32))
    tab_s = jnp.tile(_rope_table(n_past + jnp.arange(n_tok, dtype=jnp.int32)), (db, 1))

    q_p, c_p, krp_p, k_p, v_p, pool_p, utail_p = _even_in_prompt(x_prompt, tab_p, ew, tm=512)
    attn_p = _prompt_attn(q_p, k_p, v_p, tq=512)
    xs = x_sample.reshape(db * n_tok, D_MODEL)
    q_s, c_s, krp_s, u_s = _even_in_sample(xs, tab_s, ew, tm=256)
    attn_s = _sample_attn(q_s, c_s, krp_s, cache_ckv, cache_krope, page_table, g_k_nope[0], w_uk[0], w_uv[0],
                          n_tok=n_tok, sbp=8)
    u_s3 = u_s.reshape(db, n_tok, POOL_DIM)
    pool_s = _pool_sample(u_s3.transpose(1, 0, 2), state_pool[0].transpose(1, 0, 2), ew['wpool'], ew['pscale'],
                          n_past)
    pool_s = pool_s.transpose(1, 0, 2).reshape(db * n_tok, POOL_DIM)

    mlp0 = (w_out_even[0], norm_ffn[0], w_up[0], w_down[0])
    yp = _mix_mlp(x_prompt.reshape(nb * seq, D_MODEL), pool_p.reshape(nb * seq, POOL_DIM), 0,
                  attn_p.reshape(nb * seq, ATTN_OUT_DIM), 0, *mlp0, tm=512)
    ys = _mix_mlp(xs, pool_s, 0, attn_s, 0, *mlp0, tm=256)

    ow = _odd_weights(norm_mix[1], w_in_rnn[0], conv_w[0], conv_b[0], w_gate_a[0], b_gate_a[0], w_gate_x[0],
                      b_gate_x[0], lru_lambda[0])
    rnn_p, ctail_p, hlast_p = _odd_prompt(yp.reshape(nb, seq, D_MODEL), ow, tm=256)
    ys_tm = ys.reshape(db, n_tok, D_MODEL).transpose(1, 0, 2)
    rnn_s, conv_s_tm, lru_s = _odd_sample(ys_tm, state_conv[0].transpose(1, 0, 2), state_lru[0], ow, n_past)

    mlp1 = (w_out_rnn[0], norm_ffn[1], w_up[1], w_down[1])
    rnn_p2 = rnn_p.reshape(nb * seq, RNN_DIM)
    yp = _mix_mlp(yp, rnn_p2, 0, rnn_p2, 1, *mlp1, tm=512)
    rnn_s2 = rnn_s.reshape(n_tok * db, RNN_DIM)
    ys_out = _mix_mlp(ys_tm.reshape(n_tok * db, D_MODEL), rnn_s2, 0, rnn_s2, 1, *mlp1, tm=256)
    ys_out = ys_out.reshape(n_tok, db, D_MODEL).transpose(1, 0, 2)

    pool_state_s = jnp.concatenate([state_pool[0], u_s3], axis=1)[:, -POOL_HIST:]
    return (yp.reshape(nb, seq, D_MODEL), ys_out,
            c_p[None], krp_p[None, :, :, rope_sl], utail_p[None, :, 1:], ctail_p[None, :, SUBLANES - CONV_WIDTH + 1:],
            hlast_p[None, :, 0],
            c_s.reshape(1, db, n_tok, KV_RANK), krp_s[:, rope_sl].reshape(1, db, n_tok, QK_ROPE_DIM),
            pool_state_s[None], conv_s_tm.transpose(1, 0, 2)[None], lru_s[None])
```

```python
import functools

import numpy as np
import jax
import jax.numpy as jnp
from jax import lax
from jax.experimental import pallas as pl
from jax.experimental.pallas import tpu as pltpu

D_MODEL = 1024
PAGE_SIZE = 128
POOL_WINDOWS = (2, 4, 8, 16)
POOL_GROUP_DIM = 128
POOL_DIM = len(POOL_WINDOWS) * POOL_GROUP_DIM
POOL_HIST = max(POOL_WINDOWS) - 1
N_HEADS = 8
QK_NOPE_DIM = 64
QK_ROPE_DIM = 32
QK_HEAD_DIM = QK_NOPE_DIM + QK_ROPE_DIM
V_HEAD_DIM = 64
KV_RANK = 256
Q_DIM = N_HEADS * QK_HEAD_DIM
ATTN_OUT_DIM = N_HEADS * V_HEAD_DIM
ROPE_BASE = 10000.0
SOFTMAX_SCALE = QK_HEAD_DIM ** -0.5
RNN_DIM = D_MODEL
RNN_BLOCKS = 8
RNN_BLOCK_DIM = RNN_DIM // RNN_BLOCKS
CONV_WIDTH = 4
LRU_C = 8.0
D_FF = 4 * D_MODEL
NORM_EPS = 1e-6

LANES = 128
SUBLANES = 8
VMEM_LIMIT_BYTES = 56 * 2 ** 20

HEAD_PAD = LANES
QP_DIM = N_HEADS * HEAD_PAD
W1_DIM = POOL_DIM + QP_DIM + KV_RANK + HEAD_PAD
ROPE_LANE0 = QK_NOPE_DIM
HALF = QK_ROPE_DIM // 2

F32 = jnp.float32
BF16 = jnp.bfloat16


def _dot(a, b):
    return jnp.dot(a, b, preferred_element_type=F32)


def _dot_nt(a, b):
    return lax.dot_general(a, b, (((1,), (1,)), ((), ())), preferred_element_type=F32)


def _dot_tn(a, b):
    return lax.dot_general(a, b, (((0,), (0,)), ((), ())), preferred_element_type=F32)


def _rms(x, g):
    ms = jnp.mean(x * x, axis=-1, keepdims=True)
    return x * lax.rsqrt(ms + NORM_EPS) * g


def _expand(rs, e):
    hi = rs.astype(BF16)
    lo = (rs - hi.astype(F32)).astype(BF16)
    return _dot(hi, e) + _dot(lo, e)


def _gelu_tanh(x):
    return 0.5 * x * (1.0 + jnp.tanh(0.7978845608028654 * (x + 0.044715 * (x * x * x))))


def _sigmoid(x):
    return 1.0 / (1.0 + jnp.exp(-x))


def _sqrt_one_minus_exp2(x):
    t = jnp.tanh(x)
    return jnp.sqrt(-2.0 * t / (1.0 - t))


def _const_spec(shape, grid_rank):
    zeros = (0,) * len(shape)
    if grid_rank == 1:
        return pl.BlockSpec(shape, lambda i: zeros, pipeline_mode=pl.Buffered(1))
    return pl.BlockSpec(shape, lambda i, j: zeros, pipeline_mode=pl.Buffered(1))


def _params(*sem):
    return pltpu.CompilerParams(dimension_semantics=sem, vmem_limit_bytes=VMEM_LIMIT_BYTES)


def _even_in_kernel(*refs, tm, prompt):
    if prompt:
        (x_ref, tab_ref, gmix_ref, w1_ref, gq_ref, segq_ref, eq_ref, gckv_ref, gkr_ref,
         wuk_ref, segk_ref, ek_ref, gk_ref, wuv_ref, wpool_ref, pscale_ref,
         q_ref, c_ref, krp_ref, k_ref, v_ref, pool_ref, utail_ref, uext_ref) = refs
    else:
        (x_ref, tab_ref, gmix_ref, w1_ref, gq_ref, segq_ref, eq_ref, gckv_ref, gkr_ref,
         q_ref, c_ref, krp_ref, u_ref) = refs

    xn = _rms(x_ref[...], gmix_ref[...]).astype(BF16)
    z = _dot(xn, w1_ref[...])
    u = z[:, 0:POOL_DIM]
    qz = z[:, POOL_DIM:POOL_DIM + QP_DIM]
    cz = z[:, POOL_DIM + QP_DIM:POOL_DIM + QP_DIM + KV_RANK]
    krz = z[:, POOL_DIM + QP_DIM + KV_RANK:]

    ta = tab_ref[:, 0:LANES]
    tb = tab_ref[:, LANES:2 * LANES]
    tc = tab_ref[:, 2 * LANES:3 * LANES]

    def rope(blk):
        return blk * ta + pltpu.roll(blk, HALF, 1) * tb + pltpu.roll(blk, LANES - HALF, 1) * tc

    msq = _dot((qz * qz).astype(BF16), segq_ref[...])
    qn = qz * _expand(lax.rsqrt(msq + NORM_EPS), eq_ref[...]) * gq_ref[...]
    for h in range(N_HEADS):
        lanes = slice(h * HEAD_PAD, (h + 1) * HEAD_PAD)
        q_ref[:, lanes] = rope(qn[:, lanes]).astype(q_ref.dtype)

    c = _rms(cz, gckv_ref[...])
    c_ref[...] = c
    mskr = jnp.sum(krz * krz, axis=-1, keepdims=True) * (1.0 / QK_ROPE_DIM)
    krr = rope(krz * lax.rsqrt(mskr + NORM_EPS) * gkr_ref[...])
    krp_ref[...] = krr

    if not prompt:
        u_ref[...] = u
        return

    cb = c.astype(BF16)
    kn = _dot(cb, wuk_ref[...])
    msk = _dot((kn * kn).astype(BF16), segk_ref[...])
    knn = kn * _expand(lax.rsqrt(msk + NORM_EPS), ek_ref[...]) * gk_ref[...]
    for h in range(N_HEADS):
        lanes = slice(h * HEAD_PAD, (h + 1) * HEAD_PAD)
        k_ref[:, lanes] = (knn[:, lanes] + krr).astype(BF16)
    v_ref[...] = _dot(cb, wuv_ref[...]).astype(BF16)

    i = pl.program_id(1)
    hal = 2 * SUBLANES

    @pl.when(i == 0)
    def _():
        uext_ref[0:hal, :] = jnp.zeros((hal, POOL_DIM), F32)

    uext_ref[hal:hal + tm, :] = u
    pos = lax.broadcasted_iota(jnp.int32, (tm, POOL_GROUP_DIM), 0) + i * tm
    for g, w in enumerate(POOL_WINDOWS):
        cols = slice(g * POOL_GROUP_DIM, (g + 1) * POOL_GROUP_DIM)
        acc = uext_ref[hal:hal + tm, cols]
        for k in range(1, w):
            acc = acc + uext_ref[hal - k:hal - k + tm, cols]
        cnt = jnp.minimum(pos + 1, w).astype(F32)
        d = acc / cnt - u[:, cols]
        y = _dot(d.astype(BF16), wpool_ref[g]) * pscale_ref[:, cols]
        pool_ref[:, cols] = y.astype(BF16)
    uext_ref[0:hal, :] = uext_ref[tm:tm + hal, :]

    @pl.when(i == pl.num_programs(1) - 1)
    def _():
        utail_ref[...] = uext_ref[0:hal, :]


def _seg_mats():
    segq = np.zeros((QP_DIM, LANES), np.float32)
    eq = np.zeros((LANES, QP_DIM), np.float32)
    segk = np.zeros((QP_DIM, LANES), np.float32)
    ek = np.zeros((LANES, QP_DIM), np.float32)
    for h in range(N_HEADS):
        b = h * HEAD_PAD
        segq[b:b + QK_NOPE_DIM, 2 * h] = 1.0 / QK_NOPE_DIM
        segq[b + QK_NOPE_DIM:b + QK_HEAD_DIM, 2 * h + 1] = 1.0 / QK_ROPE_DIM
        eq[2 * h, b:b + QK_NOPE_DIM] = 1.0
        eq[2 * h + 1, b + QK_NOPE_DIM:b + QK_HEAD_DIM] = 1.0
        segk[b:b + QK_NOPE_DIM, h] = 1.0 / QK_NOPE_DIM
        ek[h, b:b + QK_NOPE_DIM] = 1.0
    return [jnp.asarray(m, BF16) for m in (segq, eq, segk, ek)]


def _rope_table(pos):
    inv = ROPE_BASE ** (-jnp.arange(HALF, dtype=F32) / HALF)
    ang = pos.astype(F32)[:, None] * inv[None, :]
    cos, sin = jnp.cos(ang), jnp.sin(ang)
    n = pos.shape[0]
    one = jnp.ones((n, ROPE_LANE0), F32)
    zero = jnp.zeros((n, ROPE_LANE0), F32)
    zh = jnp.zeros((n, HALF), F32)
    tail1 = jnp.ones((n, LANES - ROPE_LANE0 - QK_ROPE_DIM), F32)
    tail0 = jnp.zeros((n, LANES - ROPE_LANE0 - QK_ROPE_DIM), F32)
    ta = jnp.concatenate([one, cos, cos, tail1], axis=1)
    tb = jnp.concatenate([zero, zh, sin, tail0], axis=1)
    tc = jnp.concatenate([zero, -sin, zh, tail0], axis=1)
    return jnp.concatenate([ta, tb, tc], axis=1)


def _head_pad(w, lead):
    d = w.shape[-1]
    return jnp.pad(w, ((0, 0), (0, 0), (0, HEAD_PAD - d))).reshape(lead, QP_DIM)


def _head_row(parts):
    row = jnp.concatenate(parts)
    row = jnp.pad(row, (0, HEAD_PAD - row.shape[0]))
    return jnp.tile(row, N_HEADS)[None, :]


def _even_weights(norm_mix, w_in, g_q_nope, g_q_rope, g_ckv, g_k_rope, g_k_nope, w_uk, w_uv, w_pool, pool_scale):
    wq = _head_pad(w_in[:, POOL_DIM:POOL_DIM + Q_DIM].reshape(D_MODEL, N_HEADS, QK_HEAD_DIM), D_MODEL)
    wc = w_in[:, POOL_DIM + Q_DIM:POOL_DIM + Q_DIM + KV_RANK]
    wkr = jnp.pad(w_in[:, POOL_DIM + Q_DIM + KV_RANK:], ((0, 0), (ROPE_LANE0, HEAD_PAD - QK_HEAD_DIM)))
    w1 = jnp.concatenate([w_in[:, :POOL_DIM], wq, wc, wkr], axis=1).astype(BF16)
    gq = _head_row([g_q_nope, g_q_rope]) * SOFTMAX_SCALE
    gkr = _head_row([jnp.zeros((ROPE_LANE0,), F32), g_k_rope])[:, :HEAD_PAD]
    gk = _head_row([g_k_nope])
    wuk = _head_pad(w_uk, KV_RANK).astype(BF16)
    v_even = jnp.pad(w_uv, ((0, 0), (0, 0), (0, HEAD_PAD - V_HEAD_DIM)))
    v_odd = jnp.pad(w_uv, ((0, 0), (0, 0), (HEAD_PAD - V_HEAD_DIM, 0)))
    odd = (jnp.arange(N_HEADS) % 2 == 1)[None, :, None]
    wuv = jnp.where(odd, v_odd, v_even).reshape(KV_RANK, QP_DIM).astype(BF16)
    return dict(gmix=norm_mix[None, :], w1=w1, gq=gq, gckv=g_ckv[None, :], gkr=gkr, gk=gk, wuk=wuk, wuv=wuv,
                wpool=w_pool.astype(BF16), pscale=pool_scale[None, :])


def _even_in_prompt(x, tab, ew, tm):
    nb, seq, _ = x.shape
    nt = seq // tm
    segq, eq, segk, ek = _seg_mats()
    row = lambda width: pl.BlockSpec((None, tm, width), lambda b, i: (b, i, 0))
    cs = lambda a: _const_spec(a.shape, 2)
    consts = [ew['gmix'], ew['w1'], ew['gq'], segq, eq, ew['gckv'], ew['gkr'],
              ew['wuk'], segk, ek, ew['gk'], ew['wuv'], ew['wpool'], ew['pscale']]
    out_shape = (
        jax.ShapeDtypeStruct((nb, seq, QP_DIM), BF16),
        jax.ShapeDtypeStruct((nb, seq, KV_RANK), F32),
        jax.ShapeDtypeStruct((nb, seq, HEAD_PAD), F32),
        jax.ShapeDtypeStruct((nb, seq, QP_DIM), BF16),
        jax.ShapeDtypeStruct((nb, seq, QP_DIM), BF16),
        jax.ShapeDtypeStruct((nb, seq, POOL_DIM), BF16),
        jax.ShapeDtypeStruct((nb, 2 * SUBLANES, POOL_DIM), F32),
    )
    out_specs = (row(QP_DIM), row(KV_RANK), row(HEAD_PAD), row(QP_DIM), row(QP_DIM), row(POOL_DIM),
                 pl.BlockSpec((None, 2 * SUBLANES, POOL_DIM), lambda b, i: (b, 0, 0)))
    return pl.pallas_call(
        functools.partial(_even_in_kernel, tm=tm, prompt=True),
        grid=(nb, nt),
        in_specs=[row(D_MODEL), pl.BlockSpec((tm, 3 * LANES), lambda b, i: (i, 0))] + [cs(a) for a in consts],
        out_specs=out_specs,
        out_shape=out_shape,
        scratch_shapes=[pltpu.VMEM((2 * SUBLANES + tm, POOL_DIM), F32)],
        compiler_params=_params("arbitrary", "arbitrary"),
        name="even_in_prompt",
    )(x, tab, *consts)


def _even_in_sample(x, tab, ew, tm):
    rows = x.shape[0]
    segq, eq, _, _ = _seg_mats()
    row = lambda width: pl.BlockSpec((tm, width), lambda i: (i, 0))
    cs = lambda a: _const_spec(a.shape, 1)
    consts = [ew['gmix'], ew['w1'], ew['gq'], segq, eq, ew['gckv'], ew['gkr']]
    out_shape = (
        jax.ShapeDtypeStruct((rows, QP_DIM), F32),
        jax.ShapeDtypeStruct((rows, KV_RANK), F32),
        jax.ShapeDtypeStruct((rows, HEAD_PAD), F32),
        jax.ShapeDtypeStruct((rows, POOL_DIM), F32),
    )
    return pl.pallas_call(
        functools.partial(_even_in_kernel, tm=tm, prompt=False),
        grid=(rows // tm,),
        in_specs=[row(D_MODEL), row(3 * LANES)] + [cs(a) for a in consts],
        out_specs=(row(QP_DIM), row(KV_RANK), row(HEAD_PAD), row(POOL_DIM)),
        out_shape=out_shape,
        compiler_params=_params("arbitrary"),
        name="even_in_sample",
    )(x, tab, *consts)


def _prompt_attn_kernel(q_ref, k_ref, v_ref, o_ref, *, tq):
    qi = pl.program_id(1)
    causal = (lax.broadcasted_iota(jnp.int32, (tq, tq), 0) >= lax.broadcasted_iota(jnp.int32, (tq, tq), 1))
    outs = []
    for h in range(N_HEADS):
        lanes = slice(h * HEAD_PAD, (h + 1) * HEAD_PAD)
        qh = q_ref[:, lanes]

        def tile(kj, carry, masked, lanes=lanes, qh=qh):
            m, l, acc = carry
            off = pl.multiple_of(kj * tq, tq)
            s = _dot_nt(qh, k_ref[pl.ds(off, tq), lanes])
            if masked:
                s = jnp.where(causal, s, -jnp.inf)
            m_new = jnp.maximum(m, jnp.max(s, axis=-1, keepdims=True))
            alpha = jnp.exp(m - m_new)
            p = jnp.exp(s - m_new)
            l = alpha * l + jnp.sum(p, axis=-1, keepdims=True)
            acc = alpha * acc + _dot(p.astype(BF16), v_ref[pl.ds(off, tq), lanes])
            return m_new, l, acc

        init = (jnp.full((tq, 1), -jnp.inf, F32), jnp.zeros((tq, 1), F32), jnp.zeros((tq, HEAD_PAD), F32))
        carry = lax.fori_loop(0, qi, functools.partial(tile, masked=False), init)
        _, l, acc = tile(qi, carry, True)
        outs.append(acc / l)
    for j in range(N_HEADS // 2):
        o_ref[:, j * LANES:(j + 1) * LANES] = (outs[2 * j] + outs[2 * j + 1]).astype(BF16)


def _prompt_attn(q, k, v, tq):
    nb, seq, _ = q.shape
    full = pl.BlockSpec((None, seq, QP_DIM), lambda b, i: (b, 0, 0))
    return pl.pallas_call(
        functools.partial(_prompt_attn_kernel, tq=tq),
        grid=(nb, seq // tq),
        in_specs=[pl.BlockSpec((None, tq, QP_DIM), lambda b, i: (b, i, 0)), full, full],
        out_specs=pl.BlockSpec((None, tq, ATTN_OUT_DIM), lambda b, i: (b, i, 0)),
        out_shape=jax.ShapeDtypeStruct((nb, seq, ATTN_OUT_DIM), BF16),
        compiler_params=_params("arbitrary", "arbitrary"),
        name="prompt_attn",
    )(q, k, v)


def _sample_attn_kernel(pt_ref, q_ref, cn_ref, krn_ref, gk_ref, wukp_ref, wukd_ref, wuv_ref, segt_ref,
                        ckv_hbm, kr_hbm, o_ref, cbuf, krbuf, sem, qabs_ref, qr_ref, s_ref, cb_ref, acc_ref,
                        *, n_tok, n_pages, sbp):
    b = pl.program_id(0)
    slot = lax.rem(b, 2)
    sb = sbp * PAGE_SIZE
    n_sb = n_pages // sbp
    ncol = N_HEADS * n_tok

    def start_pages(row, i0, sl):
        for k in range(sbp):
            page = pt_ref[row * n_pages + i0 + k]
            pltpu.make_async_copy(ckv_hbm.at[0, page], cbuf.at[sl, i0 + k], sem.at[0, sl]).start()
            pltpu.make_async_copy(kr_hbm.at[0, page], krbuf.at[sl, i0 + k], sem.at[1, sl]).start()

    @pl.when(b == 0)
    def _():
        def first(i, carry):
            start_pages(0, i * sbp, 0)
            return carry
        lax.fori_loop(0, n_sb, first, 0)

    q = q_ref[...]
    qt = jnp.concatenate([q] * N_HEADS + [jnp.zeros((LANES - ncol, QP_DIM), F32)], axis=0)
    r = lax.broadcasted_iota(jnp.int32, (LANES, QP_DIM), 0)
    ln = lax.broadcasted_iota(jnp.int32, (LANES, QP_DIM), 1)
    keep = jnp.where((ln & (HEAD_PAD - 1)) < QK_NOPE_DIM, r // n_tok, -1) == ln // HEAD_PAD
    qg = jnp.where(keep, qt * gk_ref[...], 0.0).astype(BF16)
    qabs_ref[...] = _dot_nt(qg, wukp_ref[...]).astype(BF16)
    lane = lax.broadcasted_iota(jnp.int32, (n_tok, LANES), 1)
    blocks = [jnp.where(lane < QK_ROPE_DIM,
                        pltpu.roll(q[:, h * HEAD_PAD:(h + 1) * HEAD_PAD], LANES - ROPE_LANE0, 1), 0.0)
              for h in range(N_HEADS)]
    qr_ref[...] = jnp.concatenate(blocks + [jnp.zeros((LANES - ncol, LANES), F32)], axis=0).astype(BF16)

    def sumsq(cb):
        k2 = _dot(cb, wukd_ref[...])
        k2 = k2 * k2
        return (k2[:, 0:LANES] + k2[:, LANES:2 * LANES]) + (k2[:, 2 * LANES:3 * LANES] + k2[:, 3 * LANES:])

    def scores(cb, krt):
        ss = _dot_nt(segt_ref[...], sumsq(cb).astype(BF16))
        sp = _dot_nt(qabs_ref[...], cb)
        rope = _dot(qr_ref[:, 0:QK_ROPE_DIM], krt.astype(BF16))
        return sp * lax.rsqrt(ss + NORM_EPS) + rope

    def lane_tiles(x, op):
        out = x[:, 0:LANES]
        for j in range(1, x.shape[1] // LANES):
            out = op(out, x[:, j * LANES:(j + 1) * LANES])
        return out

    pltpu.make_async_copy(ckv_hbm.at[0, pl.ds(0, n_pages)], cbuf.at[slot], sem.at[0, slot]).wait()
    pltpu.make_async_copy(kr_hbm.at[0, pl.ds(0, n_pages)], krbuf.at[slot], sem.at[1, slot]).wait()

    def score_block(i, m):
        @pl.when(b + 1 < pl.num_programs(0))
        def _():
            start_pages(b + 1, i * sbp, 1 - slot)

        off = pl.multiple_of(i * sb, sb)
        cb = cbuf[slot, pl.ds(i * sbp, sbp)].reshape(sb, KV_RANK).astype(BF16)
        cb_ref[pl.ds(off, sb), :] = cb
        krt = jnp.concatenate([krbuf[slot, i * sbp + k] for k in range(sbp)], axis=1)
        s = scores(cb, krt)
        s_ref[i] = s
        return jnp.maximum(m, lane_tiles(s, jnp.maximum))

    m = lax.fori_loop(0, n_sb, score_block, jnp.full((LANES, LANES), -jnp.inf, F32))

    cn = jnp.concatenate([cn_ref[...], jnp.zeros((PAGE_SIZE - n_tok, KV_RANK), F32)], axis=0).astype(BF16)
    krn = jnp.concatenate([pltpu.roll(krn_ref[...], LANES - ROPE_LANE0, 1),
                           jnp.zeros((PAGE_SIZE - n_tok, LANES), F32)], axis=0)
    s = scores(cn, krn.T[0:QK_ROPE_DIM, :])
    key = lax.broadcasted_iota(jnp.int32, s.shape, 1)
    tq = lax.broadcasted_iota(jnp.int32, s.shape, 0) % n_tok
    s = jnp.where(key <= tq, s, -jnp.inf)
    m = jnp.max(jnp.maximum(m, s), axis=1, keepdims=True)

    p = jnp.exp(s - m)
    acc_ref[...] = _dot(p.astype(BF16), cn)

    def value_block(i, lp):
        off = pl.multiple_of(i * sb, sb)
        p = jnp.exp(s_ref[i] - m)
        acc_ref[...] += _dot(p.astype(BF16), cb_ref[pl.ds(off, sb), :])
        return lp + lane_tiles(p, jnp.add)

    l = jnp.sum(lax.fori_loop(0, n_sb, value_block, p), axis=1, keepdims=True)

    lat = acc_ref[...] * (1.0 / l)
    zz = _dot(lat.astype(BF16), wuv_ref[...])
    lane_h = lax.broadcasted_iota(jnp.int32, (n_tok, ATTN_OUT_DIM), 1) // V_HEAD_DIM
    out = jnp.zeros((n_tok, ATTN_OUT_DIM), F32)
    for h in range(N_HEADS):
        out = out + jnp.where(lane_h == h, zz[h * n_tok:(h + 1) * n_tok, :], 0.0)
    o_ref[...] = out


def _sample_attn(q, c, krp, cache_ckv, cache_krope, page_table, g_k_nope, w_uk, w_uv, n_tok, sbp):
    nb, n_pages = page_table.shape
    ncol = N_HEADS * n_tok
    gk = _head_row([g_k_nope])
    wukp = _head_pad(w_uk, KV_RANK).astype(BF16)
    wukd = w_uk.transpose(0, 2, 1).reshape(KV_RANK, N_HEADS * QK_NOPE_DIM).astype(BF16)
    wuv = w_uv.reshape(KV_RANK, ATTN_OUT_DIM).astype(BF16)
    segt = np.zeros((LANES, LANES), np.float32)
    for col in range(ncol):
        segt[col, (np.arange(LANES) % N_HEADS) == col // n_tok] = 1.0 / QK_NOPE_DIM
    segt = jnp.asarray(segt, BF16)
    pt = page_table.reshape(-1)
    kr_t = jnp.swapaxes(cache_krope, 2, 3)

    rows = lambda width: pl.BlockSpec((n_tok, width), lambda b, pt_ref: (b, 0))
    const = lambda a: pl.BlockSpec(a.shape, lambda b, pt_ref: (0,) * a.ndim)
    hbm = pl.BlockSpec(memory_space=pl.ANY)
    n_keys = n_pages * PAGE_SIZE
    grid_spec = pltpu.PrefetchScalarGridSpec(
        num_scalar_prefetch=1,
        grid=(nb,),
        in_specs=[rows(QP_DIM), rows(KV_RANK), rows(HEAD_PAD), const(gk), const(wukp), const(wukd), const(wuv),
                  const(segt), hbm, hbm],
        out_specs=rows(ATTN_OUT_DIM),
        scratch_shapes=[pltpu.VMEM((2, n_pages, PAGE_SIZE, KV_RANK), F32),
                        pltpu.VMEM((2, n_pages, QK_ROPE_DIM, PAGE_SIZE), F32),
                        pltpu.SemaphoreType.DMA((2, 2)),
                        pltpu.VMEM((LANES, KV_RANK), BF16), pltpu.VMEM((LANES, LANES), BF16),
                        pltpu.VMEM((n_pages // sbp, LANES, sbp * PAGE_SIZE), F32),
                        pltpu.VMEM((n_keys, KV_RANK), BF16),
                        pltpu.VMEM((LANES, KV_RANK), F32)],
    )
    return pl.pallas_call(
        functools.partial(_sample_attn_kernel, n_tok=n_tok, n_pages=n_pages, sbp=sbp),
        grid_spec=grid_spec,
        out_shape=jax.ShapeDtypeStruct((nb * n_tok, ATTN_OUT_DIM), F32),
        compiler_params=_params("arbitrary"),
        name="sample_attn",
    )(pt, q, c, krp, gk, wukp, wukd, wuv, segt, cache_ckv, kr_t)


def _pool_sample_kernel(u_ref, hist_ref, wpool_ref, pscale_ref, o_ref, *, n_tok, n_past):
    ext = [hist_ref[k] for k in range(POOL_HIST)] + [u_ref[t] for t in range(n_tok)]
    nb = u_ref.shape[1]
    for g, w in enumerate(POOL_WINDOWS):
        cols = slice(g * POOL_GROUP_DIM, (g + 1) * POOL_GROUP_DIM)
        ds = []
        for t in range(n_tok):
            acc = ext[POOL_HIST + t][:, cols]
            for k in range(1, w):
                acc = acc + ext[POOL_HIST + t - k][:, cols]
            ds.append(acc / float(min(n_past + t + 1, w)) - ext[POOL_HIST + t][:, cols])
        d = jnp.concatenate(ds, axis=0).astype(BF16)
        y = (_dot(d, wpool_ref[g]) * pscale_ref[:, cols]).astype(BF16)
        for t in range(n_tok):
            o_ref[t, :, cols] = y[t * nb:(t + 1) * nb, :]


def _pool_sample(u_tm, hist_tm, w_pool, pscale, n_past):
    n_tok, nb, _ = u_tm.shape
    return pl.pallas_call(
        functools.partial(_pool_sample_kernel, n_tok=n_tok, n_past=n_past),
        out_shape=jax.ShapeDtypeStruct((n_tok, nb, POOL_DIM), BF16),
        compiler_params=pltpu.CompilerParams(vmem_limit_bytes=VMEM_LIMIT_BYTES),
        name="pool_sample",
    )(u_tm, hist_tm, w_pool, pscale)


def _mix_mlp_kernel(x_ref, a1_ref, a2_ref, wo1_ref, wo2_ref, gffn_ref, wup_ref, wdn_ref, y_ref, *, ck):
    y_ref[...] = (x_ref[...] + _dot(a1_ref[...].astype(BF16), wo1_ref[...])
                  + _dot(a2_ref[...].astype(BF16), wo2_ref[...]))
    xn = _rms(y_ref[...], gffn_ref[...]).astype(BF16)
    for c in range(D_FF // ck):
        h = jnp.maximum(_dot(xn, wup_ref[:, c * ck:(c + 1) * ck]), 0.0)
        y_ref[...] += _dot((h * h).astype(BF16), wdn_ref[c * ck:(c + 1) * ck, :])


def _mix_mlp(x, a1, a1_col, a2, a2_col, w_out, g_ffn, w_up, w_down, tm, ck=1024):
    rows = x.shape[0]
    half = D_MODEL // 2
    wo = w_out.astype(BF16)
    consts = [wo[:half], wo[half:], g_ffn[None, :], w_up.astype(BF16), w_down.astype(BF16)]
    row = pl.BlockSpec((tm, D_MODEL), lambda i: (i, 0))
    return pl.pallas_call(
        functools.partial(_mix_mlp_kernel, ck=ck),
        grid=(rows // tm,),
        in_specs=[row, pl.BlockSpec((tm, half), lambda i: (i, a1_col)), pl.BlockSpec((tm, half), lambda i: (i, a2_col))]
                 + [_const_spec(a.shape, 1) for a in consts],
        out_specs=row,
        out_shape=jax.ShapeDtypeStruct((rows, D_MODEL), F32),
        compiler_params=_params("arbitrary"),
        name="mix_mlp",
    )(x, a1, a2, *consts)


def _gates(v, wg_ref, bga_ref, bgx_ref, lam_ref):
    vb = v.astype(BF16)
    pair = 2 * RNN_BLOCK_DIM
    ga, gx = [], []
    for p in range(RNN_BLOCKS // 2):
        g = _dot(vb[:, p * pair:(p + 1) * pair], wg_ref[p])
        ga.append(g[:, :pair])
        gx.append(g[:, pair:])
    r = _sigmoid(jnp.concatenate(ga, axis=1) + bga_ref[...])
    ig = _sigmoid(jnp.concatenate(gx, axis=1) + bgx_ref[...])
    nl = -lam_ref[...]
    softplus = jnp.maximum(nl, 0.0) + jnp.log1p(jnp.exp(-jnp.abs(nl)))
    log_a = (-LRU_C) * r * softplus
    return log_a, ig


def _odd_prompt_kernel(x_ref, gmix_ref, win_ref, cw_ref, cb_ref, wg_ref, bga_ref, bgx_ref, lam_ref,
                       y_ref, ctail_ref, hlast_ref, uext_ref, a_ref, b_ref, hcar_ref, *, tm):
    i = pl.program_id(1)
    sub = SUBLANES

    @pl.when(i == 0)
    def _():
        uext_ref[0:sub, :] = jnp.zeros((sub, RNN_DIM), F32)
        a_ref[0:sub, :] = jnp.zeros((sub, RNN_DIM), F32)
        b_ref[0:sub, :] = jnp.zeros((sub, RNN_DIM), F32)
        hcar_ref[...] = jnp.zeros((sub, RNN_DIM), F32)

    xn = _rms(x_ref[...], gmix_ref[...]).astype(BF16)
    z = _dot(xn, win_ref[...])
    gate = z[:, :RNN_DIM]
    uext_ref[sub:sub + tm, :] = z[:, RNN_DIM:]
    v = cb_ref[...]
    for k in range(CONV_WIDTH):
        r0 = sub - (CONV_WIDTH - 1) + k
        v = v + uext_ref[r0:r0 + tm, :] * cw_ref[k:k + 1, :]
    uext_ref[0:sub, :] = uext_ref[tm:tm + sub, :]

    log_a, ig = _gates(v, wg_ref, bga_ref, bgx_ref, lam_ref)
    a = jnp.exp(log_a)
    row = lax.broadcasted_iota(jnp.int32, (tm, RNN_DIM), 0)
    mult = jnp.where(row + i * tm == 0, 1.0, _sqrt_one_minus_exp2(log_a))
    b = mult * ig * v

    rowmod = row & (sub - 1)
    for s in (1, 2, 4):
        a_ref[sub:sub + tm, :] = a
        b_ref[sub:sub + tm, :] = b
        ok = rowmod >= s
        b = jnp.where(ok, a * b_ref[sub - s:sub - s + tm, :], 0.0) + b
        a = jnp.where(ok, a * a_ref[sub - s:sub - s + tm, :], a)
    a_ref[sub:sub + tm, :] = a
    b_ref[sub:sub + tm, :] = b

    def group(g, hb):
        off = pl.multiple_of(sub + g * sub, sub)
        hg = a_ref[pl.ds(off, sub), :] * hb + b_ref[pl.ds(off, sub), :]
        b_ref[pl.ds(off, sub), :] = hg
        return jnp.broadcast_to(hg[sub - 1:sub, :], (sub, RNN_DIM))

    hb = lax.fori_loop(0, tm // sub, group, hcar_ref[...])
    hcar_ref[...] = hb
    y_ref[...] = (_gelu_tanh(gate) * b_ref[sub:sub + tm, :]).astype(BF16)

    @pl.when(i == pl.num_programs(1) - 1)
    def _():
        ctail_ref[...] = uext_ref[0:sub, :]
        hlast_ref[...] = hb


def _odd_weights(norm_mix, w_in, conv_w, conv_b, w_ga, b_ga, w_gx, b_gx, lam):
    def pairs(w):
        z = jnp.zeros((RNN_BLOCKS // 2, RNN_BLOCK_DIM, RNN_BLOCK_DIM), F32)
        top = jnp.concatenate([w[0::2], z], axis=2)
        bot = jnp.concatenate([z, w[1::2]], axis=2)
        return jnp.concatenate([top, bot], axis=1)
    wg = jnp.concatenate([pairs(w_ga), pairs(w_gx)], axis=2).astype(BF16)
    return [norm_mix[None, :], w_in.astype(BF16), conv_w, conv_b[None, :], wg, b_ga[None, :], b_gx[None, :],
            lam[None, :]]


def _odd_prompt(x, ow, tm):
    nb, seq, _ = x.shape
    row = pl.BlockSpec((None, tm, D_MODEL), lambda b, i: (b, i, 0))
    tail = pl.BlockSpec((None, SUBLANES, RNN_DIM), lambda b, i: (b, 0, 0))
    return pl.pallas_call(
        functools.partial(_odd_prompt_kernel, tm=tm),
        grid=(nb, seq // tm),
        in_specs=[row] + [_const_spec(a.shape, 2) for a in ow],
        out_specs=(row, tail, tail),
        out_shape=(jax.ShapeDtypeStruct((nb, seq, RNN_DIM), BF16),
                   jax.ShapeDtypeStruct((nb, SUBLANES, RNN_DIM), F32),
                   jax.ShapeDtypeStruct((nb, SUBLANES, RNN_DIM), F32)),
        scratch_shapes=[pltpu.VMEM((SUBLANES + tm, RNN_DIM), F32), pltpu.VMEM((SUBLANES + tm, RNN_DIM), F32),
                        pltpu.VMEM((SUBLANES + tm, RNN_DIM), F32), pltpu.VMEM((SUBLANES, RNN_DIM), F32)],
        compiler_params=_params("arbitrary", "arbitrary"),
        name="odd_prompt",
    )(x, *ow)


def _odd_sample_kernel(x_ref, ch_ref, h0_ref, gmix_ref, win_ref, cw_ref, cb_ref, wg_ref, bga_ref, bgx_ref, lam_ref,
                       y_ref, ctail_ref, hlast_ref, *, n_tok, n_past):
    nb = x_ref.shape[1]
    x = x_ref[...].reshape(n_tok * nb, D_MODEL)
    z = _dot(_rms(x, gmix_ref[...]).astype(BF16), win_ref[...])
    gate = z[:, :RNN_DIM]
    ext = [ch_ref[k] for k in range(CONV_WIDTH - 1)] + [z[t * nb:(t + 1) * nb, RNN_DIM:] for t in range(n_tok)]
    vs = []
    for t in range(n_tok):
        v = cb_ref[...]
        for k in range(CONV_WIDTH):
            v = v + ext[t + k] * cw_ref[k:k + 1, :]
        vs.append(v)
    v = jnp.concatenate(vs, axis=0)
    log_a, ig = _gates(v, wg_ref, bga_ref, bgx_ref, lam_ref)
    a = jnp.exp(log_a)
    mult = _sqrt_one_minus_exp2(log_a)
    if n_past == 0:
        first = lax.broadcasted_iota(jnp.int32, mult.shape, 0) < nb
        mult = jnp.where(first, 1.0, mult)
    b = mult * ig * v
    h = h0_ref[...]
    hs = []
    for t in range(n_tok):
        h = a[t * nb:(t + 1) * nb, :] * h + b[t * nb:(t + 1) * nb, :]
        hs.append(h)
    y = (_gelu_tanh(gate) * jnp.concatenate(hs, axis=0)).astype(BF16)
    y_ref[...] = y.reshape(n_tok, nb, RNN_DIM)
    for k in range(CONV_WIDTH - 1):
        ctail_ref[k] = ext[n_tok + k]
    hlast_ref[...] = h


def _odd_sample(x_tm, ch_tm, h0, ow, n_past, bb=32):
    n_tok, nb, _ = x_tm.shape
    blk = lambda t, w: pl.BlockSpec((t, bb, w), lambda i: (0, i, 0))
    return pl.pallas_call(
        functools.partial(_odd_sample_kernel, n_tok=n_tok, n_past=n_past),
        grid=(nb // bb,),
        in_specs=[blk(n_tok, D_MODEL), blk(CONV_WIDTH - 1, RNN_DIM), pl.BlockSpec((bb, RNN_DIM), lambda i: (i, 0))]
                 + [_const_spec(a.shape, 1) for a in ow],
        out_specs=(blk(n_tok, RNN_DIM), blk(CONV_WIDTH - 1, RNN_DIM), pl.BlockSpec((bb, RNN_DIM), lambda i: (i, 0))),
        out_shape=(jax.ShapeDtypeStruct((n_tok, nb, RNN_DIM), BF16),
                   jax.ShapeDtypeStruct((CONV_WIDTH - 1, nb, RNN_DIM), F32),
                   jax.ShapeDtypeStruct((nb, RNN_DIM), F32)),
        compiler_params=_params("arbitrary"),
        name="odd_sample",
    )(x_tm, ch_tm, h0, *ow)


def kernel(x_prompt, x_sample, cache_ckv, cache_krope, state_pool, state_conv, state_lru, page_table, norm_mix,
           w_in_even, g_q_nope, g_q_rope, g_ckv, g_k_rope, g_k_nope, w_uk, w_uv, w_pool, pool_scale, w_out_even,
           w_in_rnn, conv_w, conv_b, w_gate_a, b_gate_a, w_gate_x, b_gate_x, lru_lambda, w_out_rnn, norm_ffn,
           w_up, w_down):
    nb, seq, _ = x_prompt.shape
    db, n_tok, _ = x_sample.shape
    n_past = page_table.shape[1] * PAGE_SIZE
    depth = norm_mix.shape[0]
    assert depth == 2 and cache_ckv.shape[0] == 1, "one even (pool + MLA) layer followed by one odd (RG-LRU) layer"
    rope_sl = slice(ROPE_LANE0, ROPE_LANE0 + QK_ROPE_DIM)

    ew = _even_weights(norm_mix[0], w_in_even[0], g_q_nope[0], g_q_rope[0], g_ckv[0], g_k_rope[0], g_k_nope[0],
                       w_uk[0], w_uv[0], w_pool[0], pool_scale[0])
    tab_p = _rope_table(jnp.arange(seq, dtype=jnp.int32))
    tab_s = jnp.tile(_rope_table(n_past + jnp.arange(n_tok, dtype=jnp.int32)), (db, 1))

    q_p, c_p, krp_p, k_p, v_p, pool_p, utail_p = _even_in_prompt(x_prompt, tab_p, ew, tm=512)
    attn_p = _prompt_attn(q_p, k_p, v_p, tq=512)
    xs = x_sample.reshape(db * n_tok, D_MODEL)
    q_s, c_s, krp_s, u_s = _even_in_sample(xs, tab_s, ew, tm=256)
    attn_s = _sample_attn(q_s, c_s, krp_s, cache_ckv, cache_krope, page_table, g_k_nope[0], w_uk[0], w_uv[0],
                          n_tok=n_tok, sbp=8)
    u_s3 = u_s.reshape(db, n_tok, POOL_DIM)
    pool_s = _pool_sample(u_s3.transpose(1, 0, 2), state_pool[0].transpose(1, 0, 2), ew['wpool'], ew['pscale'],
                          n_past)
    pool_s = pool_s.transpose(1, 0, 2).reshape(db * n_tok, POOL_DIM)

    mlp0 = (w_out_even[0], norm_ffn[0], w_up[0], w_down[0])
    yp = _mix_mlp(x_prompt.reshape(nb * seq, D_MODEL), pool_p.reshape(nb * seq, POOL_DIM), 0,
                  attn_p.reshape(nb * seq, ATTN_OUT_DIM), 0, *mlp0, tm=512)
    ys = _mix_mlp(xs, pool_s, 0, attn_s, 0, *mlp0, tm=256)

    ow = _odd_weights(norm_mix[1], w_in_rnn[0], conv_w[0], conv_b[0], w_gate_a[0], b_gate_a[0], w_gate_x[0],
                      b_gate_x[0], lru_lambda[0])
    rnn_p, ctail_p, hlast_p = _odd_prompt(yp.reshape(nb, seq, D_MODEL), ow, tm=256)
    ys_tm = ys.reshape(db, n_tok, D_MODEL).transpose(1, 0, 2)
    rnn_s, conv_s_tm, lru_s = _odd_sample(ys_tm, state_conv[0].transpose(1, 0, 2), state_lru[0], ow, n_past)

    mlp1 = (w_out_rnn[0], norm_ffn[1], w_up[1], w_down[1])
    rnn_p2 = rnn_p.reshape(nb * seq, RNN_DIM)
    yp = _mix_mlp(yp, rnn_p2, 0, rnn_p2, 1, *mlp1, tm=512)
    rnn_s2 = rnn_s.reshape(n_tok * db, RNN_DIM)
    ys_out = _mix_mlp(ys_tm.reshape(n_tok * db, D_MODEL), rnn_s2, 0, rnn_s2, 1, *mlp1, tm=256)
    ys_out = ys_out.reshape(n_tok, db, D_MODEL).transpose(1, 0, 2)

    pool_state_s = jnp.concatenate([state_pool[0], u_s3], axis=1)[:, -POOL_HIST:]
    return (yp.reshape(nb, seq, D_MODEL), ys_out,
            c_p[None], krp_p[None, :, :, rope_sl], utail_p[None, :, 1:], ctail_p[None, :, SUBLANES - CONV_WIDTH + 1:],
            hlast_p[None, :, 0],
            c_s.reshape(1, db, n_tok, KV_RANK), krp_s[:, rope_sl].reshape(1, db, n_tok, QK_ROPE_DIM),
            pool_state_s[None], conv_s_tm.transpose(1, 0, 2)[None], lru_s[None])
```

```python
import functools

import numpy as np
import jax
import jax.numpy as jnp
from jax import lax
from jax.experimental import pallas as pl
from jax.experimental.pallas import tpu as pltpu

D_MODEL = 1024
PAGE_SIZE = 128
POOL_WINDOWS = (2, 4, 8, 16)
POOL_GROUP_DIM = 128
POOL_DIM = len(POOL_WINDOWS) * POOL_GROUP_DIM
POOL_HIST = max(POOL_WINDOWS) - 1
N_HEADS = 8
QK_NOPE_DIM = 64
QK_ROPE_DIM = 32
QK_HEAD_DIM = QK_NOPE_DIM + QK_ROPE_DIM
V_HEAD_DIM = 64
KV_RANK = 256
Q_DIM = N_HEADS * QK_HEAD_DIM
ATTN_OUT_DIM = N_HEADS * V_HEAD_DIM
ROPE_BASE = 10000.0
SOFTMAX_SCALE = QK_HEAD_DIM ** -0.5
LOG2_E = 1.4426950408889634
RNN_DIM = D_MODEL
RNN_BLOCKS = 8
RNN_BLOCK_DIM = RNN_DIM // RNN_BLOCKS
CONV_WIDTH = 4
LRU_C = 8.0
D_FF = 4 * D_MODEL
NORM_EPS = 1e-6

LANES = 128
SUBLANES = 8
VMEM_LIMIT_BYTES = 56 * 2 ** 20

HEAD_PAD = LANES
QP_DIM = N_HEADS * HEAD_PAD
W1_DIM = POOL_DIM + QP_DIM + KV_RANK + HEAD_PAD
ROPE_LANE0 = QK_NOPE_DIM
HALF = QK_ROPE_DIM // 2

F32 = jnp.float32
BF16 = jnp.bfloat16


def _dot(a, b):
    return jnp.dot(a, b, preferred_element_type=F32)


def _dot_nt(a, b):
    return lax.dot_general(a, b, (((1,), (1,)), ((), ())), preferred_element_type=F32)


def _dot_tn(a, b):
    return lax.dot_general(a, b, (((0,), (0,)), ((), ())), preferred_element_type=F32)


def _rms(x, g):
    ms = jnp.mean(x * x, axis=-1, keepdims=True)
    return x * lax.rsqrt(ms + NORM_EPS) * g


def _expand(rs, e):
    hi = rs.astype(BF16)
    lo = (rs - hi.astype(F32)).astype(BF16)
    return _dot(hi, e) + _dot(lo, e)


def _gelu_tanh(x):
    return 0.5 * x * (1.0 + jnp.tanh(0.7978845608028654 * (x + 0.044715 * (x * x * x))))


def _sigmoid(x):
    return 0.5 * jnp.tanh(0.5 * x) + 0.5


def _sqrt_one_minus_exp2(x):
    t = jnp.tanh(x)
    return jnp.sqrt(-2.0 * t / (1.0 - t))


def _const_spec(shape, grid_rank):
    zeros = (0,) * len(shape)
    if grid_rank == 1:
        return pl.BlockSpec(shape, lambda i: zeros, pipeline_mode=pl.Buffered(1))
    return pl.BlockSpec(shape, lambda i, j: zeros, pipeline_mode=pl.Buffered(1))


def _params(*sem):
    return pltpu.CompilerParams(dimension_semantics=sem, vmem_limit_bytes=VMEM_LIMIT_BYTES)


def _even_in_kernel(*refs, tm, prompt):
    if prompt:
        (x_ref, tab_ref, gmix_ref, w1_ref, gq_ref, segq_ref, eq_ref, gckv_ref, gkr_ref,
         wuk_ref, segk_ref, ek_ref, gk_ref, wuv_ref, wpool_ref, pscale_ref,
         q_ref, c_ref, krp_ref, k_ref, v_ref, pool_ref, utail_ref, uext_ref) = refs
    else:
        (x_ref, tab_ref, gmix_ref, w1_ref, gq_ref, segq_ref, eq_ref, gckv_ref, gkr_ref,
         q_ref, c_ref, krp_ref, u_ref) = refs

    xn = _rms(x_ref[...], gmix_ref[...]).astype(BF16)
    z = _dot(xn, w1_ref[...])
    u = z[:, 0:POOL_DIM]
    qz = z[:, POOL_DIM:POOL_DIM + QP_DIM]
    cz = z[:, POOL_DIM + QP_DIM:POOL_DIM + QP_DIM + KV_RANK]
    krz = z[:, POOL_DIM + QP_DIM + KV_RANK:]

    ta = tab_ref[:, 0:LANES]
    tb = tab_ref[:, LANES:2 * LANES]
    tc = tab_ref[:, 2 * LANES:3 * LANES]

    def rope(blk):
        return blk * ta + pltpu.roll(blk, HALF, 1) * tb + pltpu.roll(blk, LANES - HALF, 1) * tc

    msq = _dot((qz * qz).astype(BF16), segq_ref[...])
    qn = qz * _expand(lax.rsqrt(msq + NORM_EPS), eq_ref[...]) * gq_ref[...]
    for h in range(N_HEADS):
        lanes = slice(h * HEAD_PAD, (h + 1) * HEAD_PAD)
        q_ref[:, lanes] = rope(qn[:, lanes]).astype(q_ref.dtype)

    c = _rms(cz, gckv_ref[...])
    c_ref[...] = c
    mskr = jnp.sum(krz * krz, axis=-1, keepdims=True) * (1.0 / QK_ROPE_DIM)
    krr = rope(krz * lax.rsqrt(mskr + NORM_EPS) * gkr_ref[...])
    krp_ref[...] = krr

    if not prompt:
        u_ref[...] = u
        return

    cb = c.astype(BF16)
    kn = _dot(cb, wuk_ref[...])
    msk = _dot((kn * kn).astype(BF16), segk_ref[...])
    knn = kn * _expand(lax.rsqrt(msk + NORM_EPS), ek_ref[...]) * gk_ref[...]
    for h in range(N_HEADS):
        lanes = slice(h * HEAD_PAD, (h + 1) * HEAD_PAD)
        k_ref[:, lanes] = (knn[:, lanes] + krr).astype(BF16)
    ln = lax.broadcasted_iota(jnp.int32, (1, QP_DIM), 1)
    one_lane = (ln & (HEAD_PAD - 1)) + ((ln // HEAD_PAD) & 1) * V_HEAD_DIM == V_HEAD_DIM
    v_ref[...] = (_dot(cb, wuv_ref[...]) + jnp.where(one_lane, 1.0, 0.0)).astype(BF16)

    i = pl.program_id(1)
    hal = 2 * SUBLANES

    @pl.when(i == 0)
    def _():
        uext_ref[0:hal, :] = jnp.zeros((hal, POOL_DIM), F32)

    uext_ref[hal:hal + tm, :] = u
    pos = lax.broadcasted_iota(jnp.int32, (tm, POOL_GROUP_DIM), 0) + i * tm
    for g, w in enumerate(POOL_WINDOWS):
        cols = slice(g * POOL_GROUP_DIM, (g + 1) * POOL_GROUP_DIM)
        acc = uext_ref[hal:hal + tm, cols]
        for k in range(1, w):
            acc = acc + uext_ref[hal - k:hal - k + tm, cols]
        cnt = jnp.minimum(pos + 1, w).astype(F32)
        d = acc / cnt - u[:, cols]
        y = _dot(d.astype(BF16), wpool_ref[g]) * pscale_ref[:, cols]
        pool_ref[:, cols] = y.astype(BF16)
    uext_ref[0:hal, :] = uext_ref[tm:tm + hal, :]

    @pl.when(i == pl.num_programs(1) - 1)
    def _():
        utail_ref[...] = uext_ref[0:hal, :]


def _seg_mats():
    segq = np.zeros((QP_DIM, LANES), np.float32)
    eq = np.zeros((LANES, QP_DIM), np.float32)
    segk = np.zeros((QP_DIM, LANES), np.float32)
    ek = np.zeros((LANES, QP_DIM), np.float32)
    for h in range(N_HEADS):
        b = h * HEAD_PAD
        segq[b:b + QK_NOPE_DIM, 2 * h] = 1.0 / QK_NOPE_DIM
        segq[b + QK_NOPE_DIM:b + QK_HEAD_DIM, 2 * h + 1] = 1.0 / QK_ROPE_DIM
        eq[2 * h, b:b + QK_NOPE_DIM] = 1.0
        eq[2 * h + 1, b + QK_NOPE_DIM:b + QK_HEAD_DIM] = 1.0
        segk[b:b + QK_NOPE_DIM, h] = 1.0 / QK_NOPE_DIM
        ek[h, b:b + QK_NOPE_DIM] = 1.0
    return [jnp.asarray(m, BF16) for m in (segq, eq, segk, ek)]


def _rope_table(pos):
    inv = ROPE_BASE ** (-jnp.arange(HALF, dtype=F32) / HALF)
    ang = pos.astype(F32)[:, None] * inv[None, :]
    cos, sin = jnp.cos(ang), jnp.sin(ang)
    n = pos.shape[0]
    one = jnp.ones((n, ROPE_LANE0), F32)
    zero = jnp.zeros((n, ROPE_LANE0), F32)
    zh = jnp.zeros((n, HALF), F32)
    tail1 = jnp.ones((n, LANES - ROPE_LANE0 - QK_ROPE_DIM), F32)
    tail0 = jnp.zeros((n, LANES - ROPE_LANE0 - QK_ROPE_DIM), F32)
    ta = jnp.concatenate([one, cos, cos, tail1], axis=1)
    tb = jnp.concatenate([zero, zh, sin, tail0], axis=1)
    tc = jnp.concatenate([zero, -sin, zh, tail0], axis=1)
    return jnp.concatenate([ta, tb, tc], axis=1)


def _head_pad(w, lead):
    d = w.shape[-1]
    return jnp.pad(w, ((0, 0), (0, 0), (0, HEAD_PAD - d))).reshape(lead, QP_DIM)


def _head_row(parts):
    row = jnp.concatenate(parts)
    row = jnp.pad(row, (0, HEAD_PAD - row.shape[0]))
    return jnp.tile(row, N_HEADS)[None, :]


def _even_weights(norm_mix, w_in, g_q_nope, g_q_rope, g_ckv, g_k_rope, g_k_nope, w_uk, w_uv, w_pool, pool_scale):
    wq = _head_pad(w_in[:, POOL_DIM:POOL_DIM + Q_DIM].reshape(D_MODEL, N_HEADS, QK_HEAD_DIM), D_MODEL)
    wc = w_in[:, POOL_DIM + Q_DIM:POOL_DIM + Q_DIM + KV_RANK]
    wkr = jnp.pad(w_in[:, POOL_DIM + Q_DIM + KV_RANK:], ((0, 0), (ROPE_LANE0, HEAD_PAD - QK_HEAD_DIM)))
    w1 = jnp.concatenate([w_in[:, :POOL_DIM], wq, wc, wkr], axis=1).astype(BF16)
    gq = _head_row([g_q_nope, g_q_rope]) * (SOFTMAX_SCALE * LOG2_E)
    gkr = _head_row([jnp.zeros((ROPE_LANE0,), F32), g_k_rope])[:, :HEAD_PAD]
    gk = _head_row([g_k_nope])
    wuk = _head_pad(w_uk, KV_RANK).astype(BF16)
    v_even = jnp.pad(w_uv, ((0, 0), (0, 0), (0, HEAD_PAD - V_HEAD_DIM)))
    v_odd = jnp.pad(w_uv, ((0, 0), (0, 0), (HEAD_PAD - V_HEAD_DIM, 0)))
    odd = (jnp.arange(N_HEADS) % 2 == 1)[None, :, None]
    wuv = jnp.where(odd, v_odd, v_even).reshape(KV_RANK, QP_DIM).astype(BF16)
    return dict(gmix=norm_mix[None, :], w1=w1, gq=gq, gckv=g_ckv[None, :], gkr=gkr, gk=gk, wuk=wuk, wuv=wuv,
                wpool=w_pool.astype(BF16), pscale=pool_scale[None, :])


def _even_in_prompt(x, tab, ew, tm):
    nb, seq, _ = x.shape
    nt = seq // tm
    segq, eq, segk, ek = _seg_mats()
    row = lambda width: pl.BlockSpec((None, tm, width), lambda b, i: (b, i, 0))
    cs = lambda a: _const_spec(a.shape, 2)
    consts = [ew['gmix'], ew['w1'], ew['gq'], segq, eq, ew['gckv'], ew['gkr'],
              ew['wuk'], segk, ek, ew['gk'], ew['wuv'], ew['wpool'], ew['pscale']]
    out_shape = (
        jax.ShapeDtypeStruct((nb, seq, QP_DIM), BF16),
        jax.ShapeDtypeStruct((nb, seq, KV_RANK), F32),
        jax.ShapeDtypeStruct((nb, seq, HEAD_PAD), F32),
        jax.ShapeDtypeStruct((nb, seq, QP_DIM), BF16),
        jax.ShapeDtypeStruct((nb, seq, QP_DIM), BF16),
        jax.ShapeDtypeStruct((nb, seq, POOL_DIM), BF16),
        jax.ShapeDtypeStruct((nb, 2 * SUBLANES, POOL_DIM), F32),
    )
    out_specs = (row(QP_DIM), row(KV_RANK), row(HEAD_PAD), row(QP_DIM), row(QP_DIM), row(POOL_DIM),
                 pl.BlockSpec((None, 2 * SUBLANES, POOL_DIM), lambda b, i: (b, 0, 0)))
    return pl.pallas_call(
        functools.partial(_even_in_kernel, tm=tm, prompt=True),
        grid=(nb, nt),
        in_specs=[row(D_MODEL), pl.BlockSpec((tm, 3 * LANES), lambda b, i: (i, 0))] + [cs(a) for a in consts],
        out_specs=out_specs,
        out_shape=out_shape,
        scratch_shapes=[pltpu.VMEM((2 * SUBLANES + tm, POOL_DIM), F32)],
        compiler_params=_params("arbitrary", "arbitrary"),
        name="even_in_prompt",
    )(x, tab, *consts)


def _even_in_sample(x, tab, ew, tm):
    rows = x.shape[0]
    segq, eq, _, _ = _seg_mats()
    row = lambda width: pl.BlockSpec((tm, width), lambda i: (i, 0))
    cs = lambda a: _const_spec(a.shape, 1)
    consts = [ew['gmix'], ew['w1'], ew['gq'], segq, eq, ew['gckv'], ew['gkr']]
    out_shape = (
        jax.ShapeDtypeStruct((rows, QP_DIM), F32),
        jax.ShapeDtypeStruct((rows, KV_RANK), F32),
        jax.ShapeDtypeStruct((rows, HEAD_PAD), F32),
        jax.ShapeDtypeStruct((rows, POOL_DIM), F32),
    )
    return pl.pallas_call(
        functools.partial(_even_in_kernel, tm=tm, prompt=False),
        grid=(rows // tm,),
        in_specs=[row(D_MODEL), row(3 * LANES)] + [cs(a) for a in consts],
        out_specs=(row(QP_DIM), row(KV_RANK), row(HEAD_PAD), row(POOL_DIM)),
        out_shape=out_shape,
        compiler_params=_params("arbitrary"),
        name="even_in_sample",
    )(x, tab, *consts)


def _prompt_attn_kernel(q_ref, k_ref, v_ref, o_ref, *, tq):
    qi = pl.program_id(1)
    causal = (lax.broadcasted_iota(jnp.int32, (tq, tq), 0) >= lax.broadcasted_iota(jnp.int32, (tq, tq), 1))
    lane = lax.broadcasted_iota(jnp.int32, (tq, HEAD_PAD), 1)
    for j in range(N_HEADS // 2):
        heads = (2 * j, 2 * j + 1)
        lanes = [slice(h * HEAD_PAD, (h + 1) * HEAD_PAD) for h in heads]
        qs = [q_ref[:, ln] for ln in lanes]

        def tile(kj, carry, masked, lanes=lanes, qs=qs):
            off = pl.multiple_of(kj * tq, tq)
            new = []
            for (m, acc), ln, qh in zip(carry, lanes, qs):
                s = _dot_nt(qh, k_ref[pl.ds(off, tq), ln])
                if masked:
                    s = jnp.where(causal, s, -jnp.inf)
                m_new = jnp.maximum(m, jnp.max(s, axis=-1, keepdims=True))
                p = jnp.exp2(s - m_new)
                acc = jnp.exp2(m - m_new) * acc + _dot(p.astype(BF16), v_ref[pl.ds(off, tq), ln])
                new.append((m_new, acc))
            return tuple(new)

        init = tuple((jnp.full((tq, 1), -jnp.inf, F32), jnp.zeros((tq, HEAD_PAD), F32)) for _ in heads)
        carry = lax.fori_loop(0, qi, functools.partial(tile, masked=False), init)
        (_, acc_e), (_, acc_o) = tile(qi, carry, True)
        out_e = jnp.where(lane < V_HEAD_DIM, acc_e / acc_e[:, V_HEAD_DIM:V_HEAD_DIM + 1], 0.0)
        out_o = jnp.where(lane >= V_HEAD_DIM, acc_o / acc_o[:, 0:1], 0.0)
        o_ref[:, j * LANES:(j + 1) * LANES] = (out_e + out_o).astype(BF16)


def _prompt_attn(q, k, v, tq):
    nb, seq, _ = q.shape
    full = pl.BlockSpec((None, seq, QP_DIM), lambda b, i: (b, 0, 0))
    return pl.pallas_call(
        functools.partial(_prompt_attn_kernel, tq=tq),
        grid=(nb, seq // tq),
        in_specs=[pl.BlockSpec((None, tq, QP_DIM), lambda b, i: (b, i, 0)), full, full],
        out_specs=pl.BlockSpec((None, tq, ATTN_OUT_DIM), lambda b, i: (b, i, 0)),
        out_shape=jax.ShapeDtypeStruct((nb, seq, ATTN_OUT_DIM), BF16),
        compiler_params=_params("arbitrary", "arbitrary"),
        name="prompt_attn",
    )(q, k, v)


def _sample_attn_kernel(pt_ref, q_ref, cn_ref, krn_ref, gk_ref, wukp_ref, wukd_ref, wuv_ref, segt_ref,
                        ckv_hbm, kr_hbm, o_ref, cbuf, krbuf, sem, qabs_ref, qr_ref, s_ref, cb_ref, acc_ref,
                        *, n_tok, n_pages, sbp):
    b = pl.program_id(0)
    slot = lax.rem(b, 2)
    sb = sbp * PAGE_SIZE
    n_sb = n_pages // sbp
    ncol = N_HEADS * n_tok

    def start_pages(row, i0, sl):
        for k in range(sbp):
            page = pt_ref[row * n_pages + i0 + k]
            pltpu.make_async_copy(ckv_hbm.at[0, page], cbuf.at[sl, i0 + k], sem.at[0, sl]).start()
            pltpu.make_async_copy(kr_hbm.at[0, page], krbuf.at[sl, i0 + k], sem.at[1, sl]).start()

    @pl.when(b == 0)
    def _():
        def first(i, carry):
            start_pages(0, i * sbp, 0)
            return carry
        lax.fori_loop(0, n_sb, first, 0)

    q = q_ref[...]
    qt = jnp.concatenate([q] * N_HEADS + [jnp.zeros((LANES - ncol, QP_DIM), F32)], axis=0)
    r = lax.broadcasted_iota(jnp.int32, (LANES, QP_DIM), 0)
    ln = lax.broadcasted_iota(jnp.int32, (LANES, QP_DIM), 1)
    keep = jnp.where((ln & (HEAD_PAD - 1)) < QK_NOPE_DIM, r // n_tok, -1) == ln // HEAD_PAD
    qg = jnp.where(keep, qt * gk_ref[...], 0.0).astype(BF16)
    qabs_ref[...] = _dot_nt(qg, wukp_ref[...]).astype(BF16)
    lane = lax.broadcasted_iota(jnp.int32, (n_tok, LANES), 1)
    blocks = [jnp.where(lane < QK_ROPE_DIM,
                        pltpu.roll(q[:, h * HEAD_PAD:(h + 1) * HEAD_PAD], LANES - ROPE_LANE0, 1), 0.0)
              for h in range(N_HEADS)]
    qr_ref[...] = jnp.concatenate(blocks + [jnp.zeros((LANES - ncol, LANES), F32)], axis=0).astype(BF16)

    def sumsq(cb):
        k2 = _dot(cb, wukd_ref[...])
        k2 = k2 * k2
        return (k2[:, 0:LANES] + k2[:, LANES:2 * LANES]) + (k2[:, 2 * LANES:3 * LANES] + k2[:, 3 * LANES:])

    def scores(cb, krt):
        ss = _dot_nt(segt_ref[...], sumsq(cb).astype(BF16))
        sp = _dot_nt(qabs_ref[...], cb)
        rope = _dot(qr_ref[:, 0:QK_ROPE_DIM], krt.astype(BF16))
        return sp * lax.rsqrt(ss + NORM_EPS) + rope

    def lane_tiles(x, op):
        out = x[:, 0:LANES]
        for j in range(1, x.shape[1] // LANES):
            out = op(out, x[:, j * LANES:(j + 1) * LANES])
        return out

    pltpu.make_async_copy(ckv_hbm.at[0, pl.ds(0, n_pages)], cbuf.at[slot], sem.at[0, slot]).wait()
    pltpu.make_async_copy(kr_hbm.at[0, pl.ds(0, n_pages)], krbuf.at[slot], sem.at[1, slot]).wait()

    def score_block(i, m):
        @pl.when(b + 1 < pl.num_programs(0))
        def _():
            start_pages(b + 1, i * sbp, 1 - slot)

        off = pl.multiple_of(i * sb, sb)
        cb = cbuf[slot, pl.ds(i * sbp, sbp)].reshape(sb, KV_RANK).astype(BF16)
        cb_ref[pl.ds(off, sb), :] = cb
        krt = jnp.concatenate([krbuf[slot, i * sbp + k] for k in range(sbp)], axis=1)
        s = scores(cb, krt)
        s_ref[i] = s
        return jnp.maximum(m, lane_tiles(s, jnp.maximum))

    m = lax.fori_loop(0, n_sb, score_block, jnp.full((LANES, LANES), -jnp.inf, F32))

    cn = jnp.concatenate([cn_ref[...], jnp.zeros((PAGE_SIZE - n_tok, KV_RANK), F32)], axis=0).astype(BF16)
    krn = jnp.concatenate([pltpu.roll(krn_ref[...], LANES - ROPE_LANE0, 1),
                           jnp.zeros((PAGE_SIZE - n_tok, LANES), F32)], axis=0)
    s = scores(cn, krn.T[0:QK_ROPE_DIM, :])
    key = lax.broadcasted_iota(jnp.int32, s.shape, 1)
    tq = lax.broadcasted_iota(jnp.int32, s.shape, 0) % n_tok
    s = jnp.where(key <= tq, s, -jnp.inf)
    m = jnp.max(jnp.maximum(m, s), axis=1, keepdims=True)

    p = jnp.exp2(s - m)
    acc_ref[...] = _dot(p.astype(BF16), cn)

    def value_block(i, lp):
        off = pl.multiple_of(i * sb, sb)
        p = jnp.exp2(s_ref[i] - m)
        acc_ref[...] += _dot(p.astype(BF16), cb_ref[pl.ds(off, sb), :])
        return lp + lane_tiles(p, jnp.add)

    l = jnp.sum(lax.fori_loop(0, n_sb, value_block, p), axis=1, keepdims=True)

    lat = acc_ref[...] * (1.0 / l)
    zz = _dot(lat.astype(BF16), wuv_ref[...])
    lane_h = lax.broadcasted_iota(jnp.int32, (n_tok, ATTN_OUT_DIM), 1) // V_HEAD_DIM
    out = jnp.zeros((n_tok, ATTN_OUT_DIM), F32)
    for h in range(N_HEADS):
        out = out + jnp.where(lane_h == h, zz[h * n_tok:(h + 1) * n_tok, :], 0.0)
    o_ref[...] = out


def _sample_attn(q, c, krp, cache_ckv, cache_krope, page_table, g_k_nope, w_uk, w_uv, n_tok, sbp):
    nb, n_pages = page_table.shape
    ncol = N_HEADS * n_tok
    gk = _head_row([g_k_nope])
    wukp = _head_pad(w_uk, KV_RANK).astype(BF16)
    wukd = w_uk.transpose(0, 2, 1).reshape(KV_RANK, N_HEADS * QK_NOPE_DIM).astype(BF16)
    wuv = w_uv.reshape(KV_RANK, ATTN_OUT_DIM).astype(BF16)
    segt = np.zeros((LANES, LANES), np.float32)
    for col in range(ncol):
        segt[col, (np.arange(LANES) % N_HEADS) == col // n_tok] = 1.0 / QK_NOPE_DIM
    segt = jnp.asarray(segt, BF16)
    pt = page_table.reshape(-1)
    kr_t = jnp.swapaxes(cache_krope, 2, 3)

    rows = lambda width: pl.BlockSpec((n_tok, width), lambda b, pt_ref: (b, 0))
    const = lambda a: pl.BlockSpec(a.shape, lambda b, pt_ref: (0,) * a.ndim)
    hbm = pl.BlockSpec(memory_space=pl.ANY)
    n_keys = n_pages * PAGE_SIZE
    grid_spec = pltpu.PrefetchScalarGridSpec(
        num_scalar_prefetch=1,
        grid=(nb,),
        in_specs=[rows(QP_DIM), rows(KV_RANK), rows(HEAD_PAD), const(gk), const(wukp), const(wukd), const(wuv),
                  const(segt), hbm, hbm],
        out_specs=rows(ATTN_OUT_DIM),
        scratch_shapes=[pltpu.VMEM((2, n_pages, PAGE_SIZE, KV_RANK), F32),
                        pltpu.VMEM((2, n_pages, QK_ROPE_DIM, PAGE_SIZE), F32),
                        pltpu.SemaphoreType.DMA((2, 2)),
                        pltpu.VMEM((LANES, KV_RANK), BF16), pltpu.VMEM((LANES, LANES), BF16),
                        pltpu.VMEM((n_pages // sbp, LANES, sbp * PAGE_SIZE), F32),
                        pltpu.VMEM((n_keys, KV_RANK), BF16),
                        pltpu.VMEM((LANES, KV_RANK), F32)],
    )
    return pl.pallas_call(
        functools.partial(_sample_attn_kernel, n_tok=n_tok, n_pages=n_pages, sbp=sbp),
        grid_spec=grid_spec,
        out_shape=jax.ShapeDtypeStruct((nb * n_tok, ATTN_OUT_DIM), F32),
        compiler_params=_params("arbitrary"),
        name="sample_attn",
    )(pt, q, c, krp, gk, wukp, wukd, wuv, segt, cache_ckv, kr_t)


def _pool_sample_kernel(u_ref, hist_ref, wpool_ref, pscale_ref, o_ref, *, n_tok, n_past):
    ext = [hist_ref[k] for k in range(POOL_HIST)] + [u_ref[t] for t in range(n_tok)]
    nb = u_ref.shape[1]
    for g, w in enumerate(POOL_WINDOWS):
        cols = slice(g * POOL_GROUP_DIM, (g + 1) * POOL_GROUP_DIM)
        ds = []
        for t in range(n_tok):
            acc = ext[POOL_HIST + t][:, cols]
            for k in range(1, w):
                acc = acc + ext[POOL_HIST + t - k][:, cols]
            ds.append(acc / float(min(n_past + t + 1, w)) - ext[POOL_HIST + t][:, cols])
        d = jnp.concatenate(ds, axis=0).astype(BF16)
        y = (_dot(d, wpool_ref[g]) * pscale_ref[:, cols]).astype(BF16)
        for t in range(n_tok):
            o_ref[t, :, cols] = y[t * nb:(t + 1) * nb, :]


def _pool_sample(u_tm, hist_tm, w_pool, pscale, n_past):
    n_tok, nb, _ = u_tm.shape
    return pl.pallas_call(
        functools.partial(_pool_sample_kernel, n_tok=n_tok, n_past=n_past),
        out_shape=jax.ShapeDtypeStruct((n_tok, nb, POOL_DIM), BF16),
        compiler_params=pltpu.CompilerParams(vmem_limit_bytes=VMEM_LIMIT_BYTES),
        name="pool_sample",
    )(u_tm, hist_tm, w_pool, pscale)


def _mix_mlp_kernel(x_ref, a1_ref, a2_ref, wo1_ref, wo2_ref, gffn_ref, wup_ref, wdn_ref, y_ref, *, ck):
    y_ref[...] = (x_ref[...] + _dot(a1_ref[...].astype(BF16), wo1_ref[...])
                  + _dot(a2_ref[...].astype(BF16), wo2_ref[...]))
    xn = _rms(y_ref[...], gffn_ref[...]).astype(BF16)
    for c in range(D_FF // ck):
        h = jnp.maximum(_dot(xn, wup_ref[:, c * ck:(c + 1) * ck]), 0.0)
        y_ref[...] += _dot((h * h).astype(BF16), wdn_ref[c * ck:(c + 1) * ck, :])


def _mix_mlp(x, a1, a1_col, a2, a2_col, w_out, g_ffn, w_up, w_down, tm, ck=1024):
    rows = x.shape[0]
    half = D_MODEL // 2
    wo = w_out.astype(BF16)
    consts = [wo[:half], wo[half:], g_ffn[None, :], w_up.astype(BF16), w_down.astype(BF16)]
    row = pl.BlockSpec((tm, D_MODEL), lambda i: (i, 0))
    return pl.pallas_call(
        functools.partial(_mix_mlp_kernel, ck=ck),
        grid=(rows // tm,),
        in_specs=[row, pl.BlockSpec((tm, half), lambda i: (i, a1_col)), pl.BlockSpec((tm, half), lambda i: (i, a2_col))]
                 + [_const_spec(a.shape, 1) for a in consts],
        out_specs=row,
        out_shape=jax.ShapeDtypeStruct((rows, D_MODEL), F32),
        compiler_params=_params("arbitrary"),
        name="mix_mlp",
    )(x, a1, a2, *consts)


def _gates(v, wg_ref, bga_ref, bgx_ref, lam_ref):
    vb = v.astype(BF16)
    pair = 2 * RNN_BLOCK_DIM
    ga, gx = [], []
    for p in range(RNN_BLOCKS // 2):
        g = _dot(vb[:, p * pair:(p + 1) * pair], wg_ref[p])
        ga.append(g[:, :pair])
        gx.append(g[:, pair:])
    r = _sigmoid(jnp.concatenate(ga, axis=1) + bga_ref[...])
    ig = _sigmoid(jnp.concatenate(gx, axis=1) + bgx_ref[...])
    nl = -lam_ref[...]
    softplus = jnp.maximum(nl, 0.0) + jnp.log1p(jnp.exp(-jnp.abs(nl)))
    log_a = (-LRU_C) * r * softplus
    return log_a, ig


def _odd_prompt_kernel(x_ref, gmix_ref, win_ref, cw_ref, cb_ref, wg_ref, bga_ref, bgx_ref, lam_ref,
                       y_ref, ctail_ref, hlast_ref, uext_ref, a_ref, b_ref, hcar_ref, *, tm):
    i = pl.program_id(1)
    sub = SUBLANES

    @pl.when(i == 0)
    def _():
        uext_ref[0:sub, :] = jnp.zeros((sub, RNN_DIM), F32)
        a_ref[0:sub, :] = jnp.zeros((sub, RNN_DIM), F32)
        b_ref[0:sub, :] = jnp.zeros((sub, RNN_DIM), F32)
        hcar_ref[...] = jnp.zeros((sub, RNN_DIM), F32)

    xn = _rms(x_ref[...], gmix_ref[...]).astype(BF16)
    z = _dot(xn, win_ref[...])
    gate = z[:, :RNN_DIM]
    uext_ref[sub:sub + tm, :] = z[:, RNN_DIM:]
    v = cb_ref[...]
    for k in range(CONV_WIDTH):
        r0 = sub - (CONV_WIDTH - 1) + k
        v = v + uext_ref[r0:r0 + tm, :] * cw_ref[k:k + 1, :]
    uext_ref[0:sub, :] = uext_ref[tm:tm + sub, :]

    log_a, ig = _gates(v, wg_ref, bga_ref, bgx_ref, lam_ref)
    a = jnp.exp(log_a)
    row = lax.broadcasted_iota(jnp.int32, (tm, RNN_DIM), 0)
    mult = jnp.where(row + i * tm == 0, 1.0, _sqrt_one_minus_exp2(log_a))
    b = mult * ig * v

    rowmod = row & (sub - 1)
    for s in (1, 2, 4):
        ok = rowmod >= s
        b = jnp.where(ok, a * pltpu.roll(b, s, 0), 0.0) + b
        a = jnp.where(ok, a * pltpu.roll(a, s, 0), a)
    a_ref[sub:sub + tm, :] = a
    b_ref[sub:sub + tm, :] = b

    def group(g, hb):
        off = pl.multiple_of(sub + g * sub, sub)
        hg = a_ref[pl.ds(off, sub), :] * hb + b_ref[pl.ds(off, sub), :]
        b_ref[pl.ds(off, sub), :] = hg
        return jnp.broadcast_to(hg[sub - 1:sub, :], (sub, RNN_DIM))

    hb = lax.fori_loop(0, tm // sub, group, hcar_ref[...])
    hcar_ref[...] = hb
    y_ref[...] = (_gelu_tanh(gate) * b_ref[sub:sub + tm, :]).astype(BF16)

    @pl.when(i == pl.num_programs(1) - 1)
    def _():
        ctail_ref[...] = uext_ref[0:sub, :]
        hlast_ref[...] = hb


def _odd_weights(norm_mix, w_in, conv_w, conv_b, w_ga, b_ga, w_gx, b_gx, lam):
    def pairs(w):
        z = jnp.zeros((RNN_BLOCKS // 2, RNN_BLOCK_DIM, RNN_BLOCK_DIM), F32)
        top = jnp.concatenate([w[0::2], z], axis=2)
        bot = jnp.concatenate([z, w[1::2]], axis=2)
        return jnp.concatenate([top, bot], axis=1)
    wg = jnp.concatenate([pairs(w_ga), pairs(w_gx)], axis=2).astype(BF16)
    return [norm_mix[None, :], w_in.astype(BF16), conv_w, conv_b[None, :], wg, b_ga[None, :], b_gx[None, :],
            lam[None, :]]


def _odd_prompt(x, ow, tm):
    nb, seq, _ = x.shape
    row = pl.BlockSpec((None, tm, D_MODEL), lambda b, i: (b, i, 0))
    tail = pl.BlockSpec((None, SUBLANES, RNN_DIM), lambda b, i: (b, 0, 0))
    return pl.pallas_call(
        functools.partial(_odd_prompt_kernel, tm=tm),
        grid=(nb, seq // tm),
        in_specs=[row] + [_const_spec(a.shape, 2) for a in ow],
        out_specs=(row, tail, tail),
        out_shape=(jax.ShapeDtypeStruct((nb, seq, RNN_DIM), BF16),
                   jax.ShapeDtypeStruct((nb, SUBLANES, RNN_DIM), F32),
                   jax.ShapeDtypeStruct((nb, SUBLANES, RNN_DIM), F32)),
        scratch_shapes=[pltpu.VMEM((SUBLANES + tm, RNN_DIM), F32), pltpu.VMEM((SUBLANES + tm, RNN_DIM), F32),
                        pltpu.VMEM((SUBLANES + tm, RNN_DIM), F32), pltpu.VMEM((SUBLANES, RNN_DIM), F32)],
        compiler_params=_params("arbitrary", "arbitrary"),
        name="odd_prompt",
    )(x, *ow)


def _odd_sample_kernel(x_ref, ch_ref, h0_ref, gmix_ref, win_ref, cw_ref, cb_ref, wg_ref, bga_ref, bgx_ref, lam_ref,
                       y_ref, ctail_ref, hlast_ref, *, n_tok, n_past):
    nb = x_ref.shape[1]
    x = x_ref[...].reshape(n_tok * nb, D_MODEL)
    z = _dot(_rms(x, gmix_ref[...]).astype(BF16), win_ref[...])
    gate = z[:, :RNN_DIM]
    ext = [ch_ref[k] for k in range(CONV_WIDTH - 1)] + [z[t * nb:(t + 1) * nb, RNN_DIM:] for t in range(n_tok)]
    vs = []
    for t in range(n_tok):
        v = cb_ref[...]
        for k in range(CONV_WIDTH):
            v = v + ext[t + k] * cw_ref[k:k + 1, :]
        vs.append(v)
    v = jnp.concatenate(vs, axis=0)
    log_a, ig = _gates(v, wg_ref, bga_ref, bgx_ref, lam_ref)
    a = jnp.exp(log_a)
    mult = _sqrt_one_minus_exp2(log_a)
    if n_past == 0:
        first = lax.broadcasted_iota(jnp.int32, mult.shape, 0) < nb
        mult = jnp.where(first, 1.0, mult)
    b = mult * ig * v
    h = h0_ref[...]
    hs = []
    for t in range(n_tok):
        h = a[t * nb:(t + 1) * nb, :] * h + b[t * nb:(t + 1) * nb, :]
        hs.append(h)
    y = (_gelu_tanh(gate) * jnp.concatenate(hs, axis=0)).astype(BF16)
    y_ref[...] = y.reshape(n_tok, nb, RNN_DIM)
    for k in range(CONV_WIDTH - 1):
        ctail_ref[k] = ext[n_tok + k]
    hlast_ref[...] = h


def _odd_sample(x_tm, ch_tm, h0, ow, n_past, bb=32):
    n_tok, nb, _ = x_tm.shape
    blk = lambda t, w: pl.BlockSpec((t, bb, w), lambda i: (0, i, 0))
    return pl.pallas_call(
        functools.partial(_odd_sample_kernel, n_tok=n_tok, n_past=n_past),
        grid=(nb // bb,),
        in_specs=[blk(n_tok, D_MODEL), blk(CONV_WIDTH - 1, RNN_DIM), pl.BlockSpec((bb, RNN_DIM), lambda i: (i, 0))]
                 + [_const_spec(a.shape, 1) for a in ow],
        out_specs=(blk(n_tok, RNN_DIM), blk(CONV_WIDTH - 1, RNN_DIM), pl.BlockSpec((bb, RNN_DIM), lambda i: (i, 0))),
        out_shape=(jax.ShapeDtypeStruct((n_tok, nb, RNN_DIM), BF16),
                   jax.ShapeDtypeStruct((CONV_WIDTH - 1, nb, RNN_DIM), F32),
                   jax.ShapeDtypeStruct((nb, RNN_DIM), F32)),
        compiler_params=_params("arbitrary"),
        name="odd_sample",
    )(x_tm, ch_tm, h0, *ow)


def kernel(x_prompt, x_sample, cache_ckv, cache_krope, state_pool, state_conv, state_lru, page_table, norm_mix,
           w_in_even, g_q_nope, g_q_rope, g_ckv, g_k_rope, g_k_nope, w_uk, w_uv, w_pool, pool_scale, w_out_even,
           w_in_rnn, conv_w, conv_b, w_gate_a, b_gate_a, w_gate_x, b_gate_x, lru_lambda, w_out_rnn, norm_ffn,
           w_up, w_down):
    nb, seq, _ = x_prompt.shape
    db, n_tok, _ = x_sample.shape
    n_past = page_table.shape[1] * PAGE_SIZE
    depth = norm_mix.shape[0]
    assert depth == 2 and cache_ckv.shape[0] == 1, "one even (pool + MLA) layer followed by one odd (RG-LRU) layer"
    rope_sl = slice(ROPE_LANE0, ROPE_LANE0 + QK_ROPE_DIM)

    ew = _even_weights(norm_mix[0], w_in_even[0], g_q_nope[0], g_q_rope[0], g_ckv[0], g_k_rope[0], g_k_nope[0],
                       w_uk[0], w_uv[0], w_pool[0], pool_scale[0])
    tab_p = _rope_table(jnp.arange(seq, dtype=jnp.int32))
    tab_s = jnp.tile(_rope_table(n_past + jnp.arange(n_tok, dtype=jnp.int32)), (db, 1))

    q_p, c_p, krp_p, k_p, v_p, pool_p, utail_p = _even_in_prompt(x_prompt, tab_p, ew, tm=512)
    attn_p = _prompt_attn(q_p, k_p, v_p, tq=512)
    xs = x_sample.reshape(db * n_tok, D_MODEL)
    q_s, c_s, krp_s, u_s = _even_in_sample(xs, tab_s, ew, tm=256)
    attn_s = _sample_attn(q_s, c_s, krp_s, cache_ckv, cache_krope, page_table, g_k_nope[0], w_uk[0], w_uv[0],
                          n_tok=n_tok, sbp=8)
    u_s3 = u_s.reshape(db, n_tok, POOL_DIM)
    pool_s = _pool_sample(u_s3.transpose(1, 0, 2), state_pool[0].transpose(1, 0, 2), ew['wpool'], ew['pscale'],
                          n_past)
    pool_s = pool_s.transpose(1, 0, 2).reshape(db * n_tok, POOL_DIM)

    mlp0 = (w_out_even[0], norm_ffn[0], w_up[0], w_down[0])
    yp = _mix_mlp(x_prompt.reshape(nb * seq, D_MODEL), pool_p.reshape(nb * seq, POOL_DIM), 0,
                  attn_p.reshape(nb * seq, ATTN_OUT_DIM), 0, *mlp0, tm=512)
    ys = _mix_mlp(xs, pool_s, 0, attn_s, 0, *mlp0, tm=256)

    ow = _odd_weights(norm_mix[1], w_in_rnn[0], conv_w[0], conv_b[0], w_gate_a[0], b_gate_a[0], w_gate_x[0],
                      b_gate_x[0], lru_lambda[0])
    rnn_p, ctail_p, hlast_p = _odd_prompt(yp.reshape(nb, seq, D_MODEL), ow, tm=256)
    ys_tm = ys.reshape(db, n_tok, D_MODEL).transpose(1, 0, 2)
    rnn_s, conv_s_tm, lru_s = _odd_sample(ys_tm, state_conv[0].transpose(1, 0, 2), state_lru[0], ow, n_past)

    mlp1 = (w_out_rnn[0], norm_ffn[1], w_up[1], w_down[1])
    rnn_p2 = rnn_p.reshape(nb * seq, RNN_DIM)
    yp = _mix_mlp(yp, rnn_p2, 0, rnn_p2, 1, *mlp1, tm=512)
    rnn_s2 = rnn_s.reshape(n_tok * db, RNN_DIM)
    ys_out = _mix_mlp(ys_tm.reshape(n_tok * db, D_MODEL), rnn_s2, 0, rnn_s2, 1, *mlp1, tm=256)
    ys_out = ys_out.reshape(n_tok, db, D_MODEL).transpose(1, 0, 2)

    pool_state_s = jnp.concatenate([state_pool[0], u_s3], axis=1)[:, -POOL_HIST:]
    return (yp.reshape(nb, seq, D_MODEL), ys_out,
            c_p[None], krp_p[None, :, :, rope_sl], utail_p[None, :, 1:], ctail_p[None, :, SUBLANES - CONV_WIDTH + 1:],
            hlast_p[None, :, 0],
            c_s.reshape(1, db, n_tok, KV_RANK), krp_s[:, rope_sl].reshape(1, db, n_tok, QK_ROPE_DIM),
            pool_state_s[None], conv_s_tm.transpose(1, 0, 2)[None], lru_s[None])
```

```python
import functools

import numpy as np
import jax
import jax.numpy as jnp
from jax import lax
from jax.experimental import pallas as pl
from jax.experimental.pallas import tpu as pltpu

D_MODEL = 1024
PAGE_SIZE = 128
POOL_WINDOWS = (2, 4, 8, 16)
POOL_GROUP_DIM = 128
POOL_DIM = len(POOL_WINDOWS) * POOL_GROUP_DIM
POOL_HIST = max(POOL_WINDOWS) - 1
N_HEADS = 8
QK_NOPE_DIM = 64
QK_ROPE_DIM = 32
QK_HEAD_DIM = QK_NOPE_DIM + QK_ROPE_DIM
V_HEAD_DIM = 64
KV_RANK = 256
Q_DIM = N_HEADS * QK_HEAD_DIM
ATTN_OUT_DIM = N_HEADS * V_HEAD_DIM
ROPE_BASE = 10000.0
SOFTMAX_SCALE = QK_HEAD_DIM ** -0.5
LOG2_E = 1.4426950408889634
RNN_DIM = D_MODEL
RNN_BLOCKS = 8
RNN_BLOCK_DIM = RNN_DIM // RNN_BLOCKS
CONV_WIDTH = 4
LRU_C = 8.0
D_FF = 4 * D_MODEL
NORM_EPS = 1e-6

LANES = 128
SUBLANES = 8
VMEM_LIMIT_BYTES = 56 * 2 ** 20

HEAD_PAD = LANES
QP_DIM = N_HEADS * HEAD_PAD
W1_DIM = POOL_DIM + QP_DIM + KV_RANK + HEAD_PAD
ROPE_LANE0 = QK_NOPE_DIM
HALF = QK_ROPE_DIM // 2

F32 = jnp.float32
BF16 = jnp.bfloat16


def _dot(a, b):
    return jnp.dot(a, b, preferred_element_type=F32)


def _dot_nt(a, b):
    return lax.dot_general(a, b, (((1,), (1,)), ((), ())), preferred_element_type=F32)


def _dot_tn(a, b):
    return lax.dot_general(a, b, (((0,), (0,)), ((), ())), preferred_element_type=F32)


def _rms(x, g):
    ms = jnp.mean(x * x, axis=-1, keepdims=True)
    return x * lax.rsqrt(ms + NORM_EPS) * g


def _expand(rs, e):
    hi = rs.astype(BF16)
    lo = (rs - hi.astype(F32)).astype(BF16)
    return _dot(hi, e) + _dot(lo, e)


def _gelu_tanh(x):
    return 0.5 * x * (1.0 + jnp.tanh(0.7978845608028654 * (x + 0.044715 * (x * x * x))))


def _sigmoid(x):
    return 0.5 * jnp.tanh(0.5 * x) + 0.5


def _sqrt_one_minus_exp2(x):
    t = jnp.tanh(x)
    return jnp.sqrt(-2.0 * t / (1.0 - t))


def _const_spec(shape, grid_rank):
    zeros = (0,) * len(shape)
    if grid_rank == 1:
        return pl.BlockSpec(shape, lambda i: zeros, pipeline_mode=pl.Buffered(1))
    return pl.BlockSpec(shape, lambda i, j: zeros, pipeline_mode=pl.Buffered(1))


def _params(*sem):
    return pltpu.CompilerParams(dimension_semantics=sem, vmem_limit_bytes=VMEM_LIMIT_BYTES)


def _even_in_kernel(*refs, tm, prompt):
    if prompt:
        (x_ref, tab_ref, gmix_ref, w1_ref, gq_ref, segq_ref, eq_ref, gckv_ref, gkr_ref,
         wuk_ref, segk_ref, ek_ref, gk_ref, wuv_ref, wpool_ref, pscale_ref,
         q_ref, c_ref, krp_ref, k_ref, v_ref, pool_ref, utail_ref, uext_ref) = refs
    else:
        (x_ref, tab_ref, gmix_ref, w1_ref, gq_ref, segq_ref, eq_ref, gckv_ref, gkr_ref,
         q_ref, c_ref, krp_ref, u_ref) = refs

    xn = _rms(x_ref[...], gmix_ref[...]).astype(BF16)
    z = _dot(xn, w1_ref[...])
    u = z[:, 0:POOL_DIM]
    qz = z[:, POOL_DIM:POOL_DIM + QP_DIM]
    cz = z[:, POOL_DIM + QP_DIM:POOL_DIM + QP_DIM + KV_RANK]
    krz = z[:, POOL_DIM + QP_DIM + KV_RANK:]

    ta = tab_ref[:, 0:LANES]
    tb = tab_ref[:, LANES:2 * LANES]
    tc = tab_ref[:, 2 * LANES:3 * LANES]

    def rope(blk):
        return blk * ta + pltpu.roll(blk, HALF, 1) * tb + pltpu.roll(blk, LANES - HALF, 1) * tc

    msq = _dot((qz * qz).astype(BF16), segq_ref[...])
    qn = qz * _expand(lax.rsqrt(msq + NORM_EPS), eq_ref[...]) * gq_ref[...]
    for h in range(N_HEADS):
        lanes = slice(h * HEAD_PAD, (h + 1) * HEAD_PAD)
        q_ref[:, lanes] = rope(qn[:, lanes]).astype(q_ref.dtype)

    c = _rms(cz, gckv_ref[...])
    c_ref[...] = c
    mskr = jnp.sum(krz * krz, axis=-1, keepdims=True) * (1.0 / QK_ROPE_DIM)
    krr = rope(krz * lax.rsqrt(mskr + NORM_EPS) * gkr_ref[...])
    krp_ref[...] = krr

    if not prompt:
        u_ref[...] = u
        return

    cb = c.astype(BF16)
    kn = _dot(cb, wuk_ref[...])
    msk = _dot((kn * kn).astype(BF16), segk_ref[...])
    knn = kn * _expand(lax.rsqrt(msk + NORM_EPS), ek_ref[...]) * gk_ref[...]
    for h in range(N_HEADS):
        lanes = slice(h * HEAD_PAD, (h + 1) * HEAD_PAD)
        k_ref[:, lanes] = (knn[:, lanes] + krr).astype(BF16)
    ln = lax.broadcasted_iota(jnp.int32, (1, QP_DIM), 1)
    one_lane = (ln & (HEAD_PAD - 1)) + ((ln // HEAD_PAD) & 1) * V_HEAD_DIM == V_HEAD_DIM
    v_ref[...] = (_dot(cb, wuv_ref[...]) + jnp.where(one_lane, 1.0, 0.0)).astype(BF16)

    i = pl.program_id(1)
    hal = 2 * SUBLANES

    @pl.when(i == 0)
    def _():
        uext_ref[0:hal, :] = jnp.zeros((hal, POOL_DIM), F32)

    uext_ref[hal:hal + tm, :] = u
    pos = lax.broadcasted_iota(jnp.int32, (tm, POOL_GROUP_DIM), 0) + i * tm
    for g, w in enumerate(POOL_WINDOWS):
        cols = slice(g * POOL_GROUP_DIM, (g + 1) * POOL_GROUP_DIM)
        acc = uext_ref[hal:hal + tm, cols]
        for k in range(1, w):
            acc = acc + uext_ref[hal - k:hal - k + tm, cols]
        cnt = jnp.minimum(pos + 1, w).astype(F32)
        d = acc / cnt - u[:, cols]
        y = _dot(d.astype(BF16), wpool_ref[g]) * pscale_ref[:, cols]
        pool_ref[:, cols] = y.astype(BF16)
    uext_ref[0:hal, :] = uext_ref[tm:tm + hal, :]

    @pl.when(i == pl.num_programs(1) - 1)
    def _():
        utail_ref[...] = uext_ref[0:hal, :]


def _seg_mats():
    segq = np.zeros((QP_DIM, LANES), np.float32)
    eq = np.zeros((LANES, QP_DIM), np.float32)
    segk = np.zeros((QP_DIM, LANES), np.float32)
    ek = np.zeros((LANES, QP_DIM), np.float32)
    for h in range(N_HEADS):
        b = h * HEAD_PAD
        segq[b:b + QK_NOPE_DIM, 2 * h] = 1.0 / QK_NOPE_DIM
        segq[b + QK_NOPE_DIM:b + QK_HEAD_DIM, 2 * h + 1] = 1.0 / QK_ROPE_DIM
        eq[2 * h, b:b + QK_NOPE_DIM] = 1.0
        eq[2 * h + 1, b + QK_NOPE_DIM:b + QK_HEAD_DIM] = 1.0
        segk[b:b + QK_NOPE_DIM, h] = 1.0 / QK_NOPE_DIM
        ek[h, b:b + QK_NOPE_DIM] = 1.0
    return [jnp.asarray(m, BF16) for m in (segq, eq, segk, ek)]


def _rope_table(pos):
    inv = ROPE_BASE ** (-jnp.arange(HALF, dtype=F32) / HALF)
    ang = pos.astype(F32)[:, None] * inv[None, :]
    cos, sin = jnp.cos(ang), jnp.sin(ang)
    n = pos.shape[0]
    one = jnp.ones((n, ROPE_LANE0), F32)
    zero = jnp.zeros((n, ROPE_LANE0), F32)
    zh = jnp.zeros((n, HALF), F32)
    tail1 = jnp.ones((n, LANES - ROPE_LANE0 - QK_ROPE_DIM), F32)
    tail0 = jnp.zeros((n, LANES - ROPE_LANE0 - QK_ROPE_DIM), F32)
    ta = jnp.concatenate([one, cos, cos, tail1], axis=1)
    tb = jnp.concatenate([zero, zh, sin, tail0], axis=1)
    tc = jnp.concatenate([zero, -sin, zh, tail0], axis=1)
    return jnp.concatenate([ta, tb, tc], axis=1)


def _head_pad(w, lead):
    d = w.shape[-1]
    return jnp.pad(w, ((0, 0), (0, 0), (0, HEAD_PAD - d))).reshape(lead, QP_DIM)


def _head_row(parts):
    row = jnp.concatenate(parts)
    row = jnp.pad(row, (0, HEAD_PAD - row.shape[0]))
    return jnp.tile(row, N_HEADS)[None, :]


def _even_weights(norm_mix, w_in, g_q_nope, g_q_rope, g_ckv, g_k_rope, g_k_nope, w_uk, w_uv, w_pool, pool_scale):
    wq = _head_pad(w_in[:, POOL_DIM:POOL_DIM + Q_DIM].reshape(D_MODEL, N_HEADS, QK_HEAD_DIM), D_MODEL)
    wc = w_in[:, POOL_DIM + Q_DIM:POOL_DIM + Q_DIM + KV_RANK]
    wkr = jnp.pad(w_in[:, POOL_DIM + Q_DIM + KV_RANK:], ((0, 0), (ROPE_LANE0, HEAD_PAD - QK_HEAD_DIM)))
    w1 = jnp.concatenate([w_in[:, :POOL_DIM], wq, wc, wkr], axis=1).astype(BF16)
    gq = _head_row([g_q_nope, g_q_rope]) * (SOFTMAX_SCALE * LOG2_E)
    gkr = _head_row([jnp.zeros((ROPE_LANE0,), F32), g_k_rope])[:, :HEAD_PAD]
    gk = _head_row([g_k_nope])
    wuk = _head_pad(w_uk, KV_RANK).astype(BF16)
    v_even = jnp.pad(w_uv, ((0, 0), (0, 0), (0, HEAD_PAD - V_HEAD_DIM)))
    v_odd = jnp.pad(w_uv, ((0, 0), (0, 0), (HEAD_PAD - V_HEAD_DIM, 0)))
    odd = (jnp.arange(N_HEADS) % 2 == 1)[None, :, None]
    wuv = jnp.where(odd, v_odd, v_even).reshape(KV_RANK, QP_DIM).astype(BF16)
    return dict(gmix=norm_mix[None, :], w1=w1, gq=gq, gckv=g_ckv[None, :], gkr=gkr, gk=gk, wuk=wuk, wuv=wuv,
                wpool=w_pool.astype(BF16), pscale=pool_scale[None, :])


def _even_in_prompt(x, tab, ew, tm):
    nb, seq, _ = x.shape
    nt = seq // tm
    segq, eq, segk, ek = _seg_mats()
    row = lambda width: pl.BlockSpec((None, tm, width), lambda b, i: (b, i, 0))
    cs = lambda a: _const_spec(a.shape, 2)
    consts = [ew['gmix'], ew['w1'], ew['gq'], segq, eq, ew['gckv'], ew['gkr'],
              ew['wuk'], segk, ek, ew['gk'], ew['wuv'], ew['wpool'], ew['pscale']]
    out_shape = (
        jax.ShapeDtypeStruct((nb, seq, QP_DIM), BF16),
        jax.ShapeDtypeStruct((nb, seq, KV_RANK), F32),
        jax.ShapeDtypeStruct((nb, seq, HEAD_PAD), F32),
        jax.ShapeDtypeStruct((nb, seq, QP_DIM), BF16),
        jax.ShapeDtypeStruct((nb, seq, QP_DIM), BF16),
        jax.ShapeDtypeStruct((nb, seq, POOL_DIM), BF16),
        jax.ShapeDtypeStruct((nb, 2 * SUBLANES, POOL_DIM), F32),
    )
    out_specs = (row(QP_DIM), row(KV_RANK), row(HEAD_PAD), row(QP_DIM), row(QP_DIM), row(POOL_DIM),
                 pl.BlockSpec((None, 2 * SUBLANES, POOL_DIM), lambda b, i: (b, 0, 0)))
    return pl.pallas_call(
        functools.partial(_even_in_kernel, tm=tm, prompt=True),
        grid=(nb, nt),
        in_specs=[row(D_MODEL), pl.BlockSpec((tm, 3 * LANES), lambda b, i: (i, 0))] + [cs(a) for a in consts],
        out_specs=out_specs,
        out_shape=out_shape,
        scratch_shapes=[pltpu.VMEM((2 * SUBLANES + tm, POOL_DIM), F32)],
        compiler_params=_params("arbitrary", "arbitrary"),
        name="even_in_prompt",
    )(x, tab, *consts)


def _even_in_sample(x, tab, ew, tm):
    rows = x.shape[0]
    segq, eq, _, _ = _seg_mats()
    row = lambda width: pl.BlockSpec((tm, width), lambda i: (i, 0))
    cs = lambda a: _const_spec(a.shape, 1)
    consts = [ew['gmix'], ew['w1'], ew['gq'], segq, eq, ew['gckv'], ew['gkr']]
    out_shape = (
        jax.ShapeDtypeStruct((rows, QP_DIM), F32),
        jax.ShapeDtypeStruct((rows, KV_RANK), F32),
        jax.ShapeDtypeStruct((rows, HEAD_PAD), F32),
        jax.ShapeDtypeStruct((rows, POOL_DIM), F32),
    )
    return pl.pallas_call(
        functools.partial(_even_in_kernel, tm=tm, prompt=False),
        grid=(rows // tm,),
        in_specs=[row(D_MODEL), row(3 * LANES)] + [cs(a) for a in consts],
        out_specs=(row(QP_DIM), row(KV_RANK), row(HEAD_PAD), row(POOL_DIM)),
        out_shape=out_shape,
        compiler_params=_params("arbitrary"),
        name="even_in_sample",
    )(x, tab, *consts)


def _prompt_attn_kernel(q_ref, k_ref, v_ref, o_ref, *, tq):
    qi = pl.program_id(1)
    causal = (lax.broadcasted_iota(jnp.int32, (tq, tq), 0) >= lax.broadcasted_iota(jnp.int32, (tq, tq), 1))
    lane = lax.broadcasted_iota(jnp.int32, (tq, HEAD_PAD), 1)
    for j in range(N_HEADS // 2):
        heads = (2 * j, 2 * j + 1)
        lanes = [slice(h * HEAD_PAD, (h + 1) * HEAD_PAD) for h in heads]
        qs = [q_ref[:, ln] for ln in lanes]

        def tile(kj, carry, masked, lanes=lanes, qs=qs):
            off = pl.multiple_of(kj * tq, tq)
            new = []
            for (m, acc), ln, qh in zip(carry, lanes, qs):
                s = _dot_nt(qh, k_ref[pl.ds(off, tq), ln])
                if masked:
                    s = jnp.where(causal, s, -jnp.inf)
                m_new = jnp.maximum(m, jnp.max(s, axis=-1, keepdims=True))
                p = jnp.exp2(s - m_new)
                acc = jnp.exp2(m - m_new) * acc + _dot(p.astype(BF16), v_ref[pl.ds(off, tq), ln])
                new.append((m_new, acc))
            return tuple(new)

        init = tuple((jnp.full((tq, 1), -jnp.inf, F32), jnp.zeros((tq, HEAD_PAD), F32)) for _ in heads)
        carry = lax.fori_loop(0, qi, functools.partial(tile, masked=False), init)
        (_, acc_e), (_, acc_o) = tile(qi, carry, True)
        out_e = jnp.where(lane < V_HEAD_DIM, acc_e / acc_e[:, V_HEAD_DIM:V_HEAD_DIM + 1], 0.0)
        out_o = jnp.where(lane >= V_HEAD_DIM, acc_o / acc_o[:, 0:1], 0.0)
        o_ref[:, j * LANES:(j + 1) * LANES] = (out_e + out_o).astype(BF16)


def _prompt_attn(q, k, v, tq):
    nb, seq, _ = q.shape
    full = pl.BlockSpec((None, seq, QP_DIM), lambda b, i: (b, 0, 0))
    return pl.pallas_call(
        functools.partial(_prompt_attn_kernel, tq=tq),
        grid=(nb, seq // tq),
        in_specs=[pl.BlockSpec((None, tq, QP_DIM), lambda b, i: (b, i, 0)), full, full],
        out_specs=pl.BlockSpec((None, tq, ATTN_OUT_DIM), lambda b, i: (b, i, 0)),
        out_shape=jax.ShapeDtypeStruct((nb, seq, ATTN_OUT_DIM), BF16),
        compiler_params=_params("arbitrary", "arbitrary"),
        name="prompt_attn",
    )(q, k, v)


def _sample_attn_kernel(pt_ref, q_ref, cn_ref, krn_ref, gk_ref, wukp_ref, wukd_ref, wuv_ref, segt_ref,
                        ckv_hbm, kr_hbm, o_ref, cbuf, krbuf, sem, qabs_ref, qr_ref, s_ref, cb_ref, acc_ref,
                        *, n_tok, n_pages, sbp):
    b = pl.program_id(0)
    slot = lax.rem(b, 2)
    sb = sbp * PAGE_SIZE
    n_sb = n_pages // sbp
    ncol = N_HEADS * n_tok

    def start_pages(row, i0, sl):
        for k in range(sbp):
            page = pt_ref[row * n_pages + i0 + k]
            pltpu.make_async_copy(ckv_hbm.at[0, page], cbuf.at[sl, i0 + k], sem.at[0, sl]).start()
            pltpu.make_async_copy(kr_hbm.at[0, page], krbuf.at[sl, i0 + k], sem.at[1, sl]).start()

    @pl.when(b == 0)
    def _():
        def first(i, carry):
            start_pages(0, i * sbp, 0)
            return carry
        lax.fori_loop(0, n_sb, first, 0)

    q = q_ref[...]
    qt = jnp.concatenate([q] * N_HEADS, axis=0)
    r = lax.broadcasted_iota(jnp.int32, (ncol, QP_DIM), 0)
    ln = lax.broadcasted_iota(jnp.int32, (ncol, QP_DIM), 1)
    keep = jnp.where((ln & (HEAD_PAD - 1)) < QK_NOPE_DIM, r // n_tok, -1) == ln // HEAD_PAD
    qg = jnp.where(keep, qt * gk_ref[...], 0.0).astype(BF16)
    qabs_ref[...] = _dot_nt(qg, wukp_ref[...]).astype(BF16)
    lane = lax.broadcasted_iota(jnp.int32, (n_tok, LANES), 1)
    blocks = [jnp.where(lane < QK_ROPE_DIM,
                        pltpu.roll(q[:, h * HEAD_PAD:(h + 1) * HEAD_PAD], LANES - ROPE_LANE0, 1), 0.0)
              for h in range(N_HEADS)]
    qr_ref[...] = jnp.concatenate(blocks, axis=0).astype(BF16)

    def sumsq(cb):
        k2 = _dot(cb, wukd_ref[...])
        k2 = k2 * k2
        return (k2[:, 0:LANES] + k2[:, LANES:2 * LANES]) + (k2[:, 2 * LANES:3 * LANES] + k2[:, 3 * LANES:])

    def scores(cb, krt):
        ss = _dot_nt(segt_ref[...], sumsq(cb).astype(BF16))
        sp = _dot_nt(qabs_ref[...], cb)
        rope = _dot(qr_ref[:, 0:QK_ROPE_DIM], krt.astype(BF16))
        return sp * lax.rsqrt(ss + NORM_EPS) + rope

    def lane_tiles(x, op):
        out = x[:, 0:LANES]
        for j in range(1, x.shape[1] // LANES):
            out = op(out, x[:, j * LANES:(j + 1) * LANES])
        return out

    pltpu.make_async_copy(ckv_hbm.at[0, pl.ds(0, n_pages)], cbuf.at[slot], sem.at[0, slot]).wait()
    pltpu.make_async_copy(kr_hbm.at[0, pl.ds(0, n_pages)], krbuf.at[slot], sem.at[1, slot]).wait()

    nxt = jnp.minimum(b + 1, pl.num_programs(0) - 1)

    def score_block(i, m):
        start_pages(nxt, i * sbp, 1 - slot)
        off = pl.multiple_of(i * sb, sb)
        cb = cbuf[slot, pl.ds(i * sbp, sbp)].reshape(sb, KV_RANK).astype(BF16)
        cb_ref[pl.ds(off, sb), :] = cb
        krt = jnp.concatenate([krbuf[slot, i * sbp + k] for k in range(sbp)], axis=1)
        s = scores(cb, krt)
        s_ref[i] = s
        return jnp.maximum(m, lane_tiles(s, jnp.maximum))

    m = lax.fori_loop(0, n_sb, score_block, jnp.full((ncol, LANES), -jnp.inf, F32), unroll=4)

    cn = jnp.concatenate([cn_ref[...], jnp.zeros((PAGE_SIZE - n_tok, KV_RANK), F32)], axis=0).astype(BF16)
    krn = jnp.concatenate([pltpu.roll(krn_ref[...], LANES - ROPE_LANE0, 1),
                           jnp.zeros((PAGE_SIZE - n_tok, LANES), F32)], axis=0)
    s = scores(cn, krn.T[0:QK_ROPE_DIM, :])
    key = lax.broadcasted_iota(jnp.int32, s.shape, 1)
    tq = lax.broadcasted_iota(jnp.int32, s.shape, 0) % n_tok
    s = jnp.where(key <= tq, s, -jnp.inf)
    m = jnp.max(jnp.maximum(m, s), axis=1, keepdims=True)

    p = jnp.exp2(s - m)
    acc_ref[...] = _dot(p.astype(BF16), cn)

    def value_block(i, lp):
        off = pl.multiple_of(i * sb, sb)
        p = jnp.exp2(s_ref[i] - m)
        acc_ref[...] += _dot(p.astype(BF16), cb_ref[pl.ds(off, sb), :])
        return lp + lane_tiles(p, jnp.add)

    l = jnp.sum(lax.fori_loop(0, n_sb, value_block, p, unroll=4), axis=1, keepdims=True)

    lat = acc_ref[...] * (1.0 / l)
    zz = _dot(lat.astype(BF16), wuv_ref[...])
    lane_h = lax.broadcasted_iota(jnp.int32, (n_tok, ATTN_OUT_DIM), 1) // V_HEAD_DIM
    out = jnp.zeros((n_tok, ATTN_OUT_DIM), F32)
    for h in range(N_HEADS):
        out = out + jnp.where(lane_h == h, zz[h * n_tok:(h + 1) * n_tok, :], 0.0)
    o_ref[...] = out

    @pl.when(b == pl.num_programs(0) - 1)
    def _():
        pltpu.make_async_copy(ckv_hbm.at[0, pl.ds(0, n_pages)], cbuf.at[1 - slot], sem.at[0, 1 - slot]).wait()
        pltpu.make_async_copy(kr_hbm.at[0, pl.ds(0, n_pages)], krbuf.at[1 - slot], sem.at[1, 1 - slot]).wait()


def _sample_attn(q, c, krp, cache_ckv, cache_krope, page_table, g_k_nope, w_uk, w_uv, n_tok, sbp):
    nb, n_pages = page_table.shape
    ncol = N_HEADS * n_tok
    gk = _head_row([g_k_nope])
    wukp = _head_pad(w_uk, KV_RANK).astype(BF16)
    wukd = w_uk.transpose(0, 2, 1).reshape(KV_RANK, N_HEADS * QK_NOPE_DIM).astype(BF16)
    wuv = w_uv.reshape(KV_RANK, ATTN_OUT_DIM).astype(BF16)
    segt = np.zeros((ncol, LANES), np.float32)
    for col in range(ncol):
        segt[col, (np.arange(LANES) % N_HEADS) == col // n_tok] = 1.0 / QK_NOPE_DIM
    segt = jnp.asarray(segt, BF16)
    pt = page_table.reshape(-1)
    kr_t = jnp.swapaxes(cache_krope, 2, 3)

    rows = lambda width: pl.BlockSpec((n_tok, width), lambda b, pt_ref: (b, 0))
    const = lambda a: pl.BlockSpec(a.shape, lambda b, pt_ref: (0,) * a.ndim)
    hbm = pl.BlockSpec(memory_space=pl.ANY)
    n_keys = n_pages * PAGE_SIZE
    grid_spec = pltpu.PrefetchScalarGridSpec(
        num_scalar_prefetch=1,
        grid=(nb,),
        in_specs=[rows(QP_DIM), rows(KV_RANK), rows(HEAD_PAD), const(gk), const(wukp), const(wukd), const(wuv),
                  const(segt), hbm, hbm],
        out_specs=rows(ATTN_OUT_DIM),
        scratch_shapes=[pltpu.VMEM((2, n_pages, PAGE_SIZE, KV_RANK), F32),
                        pltpu.VMEM((2, n_pages, QK_ROPE_DIM, PAGE_SIZE), F32),
                        pltpu.SemaphoreType.DMA((2, 2)),
                        pltpu.VMEM((ncol, KV_RANK), BF16), pltpu.VMEM((ncol, LANES), BF16),
                        pltpu.VMEM((n_pages // sbp, ncol, sbp * PAGE_SIZE), F32),
                        pltpu.VMEM((n_keys, KV_RANK), BF16),
                        pltpu.VMEM((ncol, KV_RANK), F32)],
    )
    return pl.pallas_call(
        functools.partial(_sample_attn_kernel, n_tok=n_tok, n_pages=n_pages, sbp=sbp),
        grid_spec=grid_spec,
        out_shape=jax.ShapeDtypeStruct((nb * n_tok, ATTN_OUT_DIM), F32),
        compiler_params=_params("arbitrary"),
        name="sample_attn",
    )(pt, q, c, krp, gk, wukp, wukd, wuv, segt, cache_ckv, kr_t)


def _pool_sample_kernel(u_ref, hist_ref, wpool_ref, pscale_ref, o_ref, *, n_tok, n_past):
    ext = [hist_ref[k] for k in range(POOL_HIST)] + [u_ref[t] for t in range(n_tok)]
    nb = u_ref.shape[1]
    for g, w in enumerate(POOL_WINDOWS):
        cols = slice(g * POOL_GROUP_DIM, (g + 1) * POOL_GROUP_DIM)
        ds = []
        for t in range(n_tok):
            acc = ext[POOL_HIST + t][:, cols]
            for k in range(1, w):
                acc = acc + ext[POOL_HIST + t - k][:, cols]
            ds.append(acc / float(min(n_past + t + 1, w)) - ext[POOL_HIST + t][:, cols])
        d = jnp.concatenate(ds, axis=0).astype(BF16)
        y = (_dot(d, wpool_ref[g]) * pscale_ref[:, cols]).astype(BF16)
        for t in range(n_tok):
            o_ref[t, :, cols] = y[t * nb:(t + 1) * nb, :]


def _pool_sample(u_tm, hist_tm, w_pool, pscale, n_past):
    n_tok, nb, _ = u_tm.shape
    return pl.pallas_call(
        functools.partial(_pool_sample_kernel, n_tok=n_tok, n_past=n_past),
        out_shape=jax.ShapeDtypeStruct((n_tok, nb, POOL_DIM), BF16),
        compiler_params=pltpu.CompilerParams(vmem_limit_bytes=VMEM_LIMIT_BYTES),
        name="pool_sample",
    )(u_tm, hist_tm, w_pool, pscale)


def _mix_mlp_kernel(x_ref, a1_ref, a2_ref, wo1_ref, wo2_ref, gffn_ref, wup_ref, wdn_ref, y_ref, *, ck):
    y_ref[...] = (x_ref[...] + _dot(a1_ref[...].astype(BF16), wo1_ref[...])
                  + _dot(a2_ref[...].astype(BF16), wo2_ref[...]))
    xn = _rms(y_ref[...], gffn_ref[...]).astype(BF16)
    for c in range(D_FF // ck):
        h = jnp.maximum(_dot(xn, wup_ref[:, c * ck:(c + 1) * ck]), 0.0)
        y_ref[...] += _dot((h * h).astype(BF16), wdn_ref[c * ck:(c + 1) * ck, :])


def _mix_mlp(x, a1, a1_col, a2, a2_col, w_out, g_ffn, w_up, w_down, tm, ck=1024):
    rows = x.shape[0]
    half = D_MODEL // 2
    wo = w_out.astype(BF16)
    consts = [wo[:half], wo[half:], g_ffn[None, :], w_up.astype(BF16), w_down.astype(BF16)]
    row = pl.BlockSpec((tm, D_MODEL), lambda i: (i, 0))
    return pl.pallas_call(
        functools.partial(_mix_mlp_kernel, ck=ck),
        grid=(rows // tm,),
        in_specs=[row, pl.BlockSpec((tm, half), lambda i: (i, a1_col)), pl.BlockSpec((tm, half), lambda i: (i, a2_col))]
                 + [_const_spec(a.shape, 1) for a in consts],
        out_specs=row,
        out_shape=jax.ShapeDtypeStruct((rows, D_MODEL), F32),
        compiler_params=_params("arbitrary"),
        name="mix_mlp",
    )(x, a1, a2, *consts)


def _gates(v, wg_ref, bga_ref, bgx_ref, lam_ref):
    vb = v.astype(BF16)
    pair = 2 * RNN_BLOCK_DIM
    ga, gx = [], []
    for p in range(RNN_BLOCKS // 2):
        g = _dot(vb[:, p * pair:(p + 1) * pair], wg_ref[p])
        ga.append(g[:, :pair])
        gx.append(g[:, pair:])
    r = _sigmoid(jnp.concatenate(ga, axis=1) + bga_ref[...])
    ig = _sigmoid(jnp.concatenate(gx, axis=1) + bgx_ref[...])
    nl = -lam_ref[...]
    softplus = jnp.maximum(nl, 0.0) + jnp.log1p(jnp.exp(-jnp.abs(nl)))
    log_a = (-LRU_C) * r * softplus
    return log_a, ig


def _odd_prompt_kernel(x_ref, gmix_ref, win_ref, cw_ref, cb_ref, wg_ref, bga_ref, bgx_ref, lam_ref,
                       y_ref, ctail_ref, hlast_ref, uext_ref, a_ref, b_ref, hcar_ref, *, tm):
    i = pl.program_id(1)
    sub = SUBLANES

    @pl.when(i == 0)
    def _():
        uext_ref[0:sub, :] = jnp.zeros((sub, RNN_DIM), F32)
        a_ref[0:sub, :] = jnp.zeros((sub, RNN_DIM), F32)
        b_ref[0:sub, :] = jnp.zeros((sub, RNN_DIM), F32)
        hcar_ref[...] = jnp.zeros((sub, RNN_DIM), F32)

    xn = _rms(x_ref[...], gmix_ref[...]).astype(BF16)
    z = _dot(xn, win_ref[...])
    gate = z[:, :RNN_DIM]
    uext_ref[sub:sub + tm, :] = z[:, RNN_DIM:]
    v = cb_ref[...]
    for k in range(CONV_WIDTH):
        r0 = sub - (CONV_WIDTH - 1) + k
        v = v + uext_ref[r0:r0 + tm, :] * cw_ref[k:k + 1, :]
    uext_ref[0:sub, :] = uext_ref[tm:tm + sub, :]

    log_a, ig = _gates(v, wg_ref, bga_ref, bgx_ref, lam_ref)
    a = jnp.exp(log_a)
    row = lax.broadcasted_iota(jnp.int32, (tm, RNN_DIM), 0)
    mult = jnp.where(row + i * tm == 0, 1.0, _sqrt_one_minus_exp2(log_a))
    b = mult * ig * v

    rowmod = row & (sub - 1)
    for s in (1, 2, 4):
        ok = rowmod >= s
        b = jnp.where(ok, a * pltpu.roll(b, s, 0), 0.0) + b
        a = jnp.where(ok, a * pltpu.roll(a, s, 0), a)
    a_ref[sub:sub + tm, :] = a
    b_ref[sub:sub + tm, :] = b

    def group(g, hb):
        off = pl.multiple_of(sub + g * sub, sub)
        hg = a_ref[pl.ds(off, sub), :] * hb + b_ref[pl.ds(off, sub), :]
        b_ref[pl.ds(off, sub), :] = hg
        return jnp.broadcast_to(hg[sub - 1:sub, :], (sub, RNN_DIM))

    hb = lax.fori_loop(0, tm // sub, group, hcar_ref[...])
    hcar_ref[...] = hb
    y_ref[...] = (_gelu_tanh(gate) * b_ref[sub:sub + tm, :]).astype(BF16)

    @pl.when(i == pl.num_programs(1) - 1)
    def _():
        ctail_ref[...] = uext_ref[0:sub, :]
        hlast_ref[...] = hb


def _odd_weights(norm_mix, w_in, conv_w, conv_b, w_ga, b_ga, w_gx, b_gx, lam):
    def pairs(w):
        z = jnp.zeros((RNN_BLOCKS // 2, RNN_BLOCK_DIM, RNN_BLOCK_DIM), F32)
        top = jnp.concatenate([w[0::2], z], axis=2)
        bot = jnp.concatenate([z, w[1::2]], axis=2)
        return jnp.concatenate([top, bot], axis=1)
    wg = jnp.concatenate([pairs(w_ga), pairs(w_gx)], axis=2).astype(BF16)
    return [norm_mix[None, :], w_in.astype(BF16), conv_w, conv_b[None, :], wg, b_ga[None, :], b_gx[None, :],
            lam[None, :]]


def _odd_prompt(x, ow, tm):
    nb, seq, _ = x.shape
    row = pl.BlockSpec((None, tm, D_MODEL), lambda b, i: (b, i, 0))
    tail = pl.BlockSpec((None, SUBLANES, RNN_DIM), lambda b, i: (b, 0, 0))
    return pl.pallas_call(
        functools.partial(_odd_prompt_kernel, tm=tm),
        grid=(nb, seq // tm),
        in_specs=[row] + [_const_spec(a.shape, 2) for a in ow],
        out_specs=(row, tail, tail),
        out_shape=(jax.ShapeDtypeStruct((nb, seq, RNN_DIM), BF16),
                   jax.ShapeDtypeStruct((nb, SUBLANES, RNN_DIM), F32),
                   jax.ShapeDtypeStruct((nb, SUBLANES, RNN_DIM), F32)),
        scratch_shapes=[pltpu.VMEM((SUBLANES + tm, RNN_DIM), F32), pltpu.VMEM((SUBLANES + tm, RNN_DIM), F32),
                        pltpu.VMEM((SUBLANES + tm, RNN_DIM), F32), pltpu.VMEM((SUBLANES, RNN_DIM), F32)],
        compiler_params=_params("arbitrary", "arbitrary"),
        name="odd_prompt",
    )(x, *ow)


def _odd_sample_kernel(x_ref, ch_ref, h0_ref, gmix_ref, win_ref, cw_ref, cb_ref, wg_ref, bga_ref, bgx_ref, lam_ref,
                       y_ref, ctail_ref, hlast_ref, *, n_tok, n_past):
    nb = x_ref.shape[1]
    x = x_ref[...].reshape(n_tok * nb, D_MODEL)
    z = _dot(_rms(x, gmix_ref[...]).astype(BF16), win_ref[...])
    gate = z[:, :RNN_DIM]
    ext = [ch_ref[k] for k in range(CONV_WIDTH - 1)] + [z[t * nb:(t + 1) * nb, RNN_DIM:] for t in range(n_tok)]
    vs = []
    for t in range(n_tok):
        v = cb_ref[...]
        for k in range(CONV_WIDTH):
            v = v + ext[t + k] * cw_ref[k:k + 1, :]
        vs.append(v)
    v = jnp.concatenate(vs, axis=0)
    log_a, ig = _gates(v, wg_ref, bga_ref, bgx_ref, lam_ref)
    a = jnp.exp(log_a)
    mult = _sqrt_one_minus_exp2(log_a)
    if n_past == 0:
        first = lax.broadcasted_iota(jnp.int32, mult.shape, 0) < nb
        mult = jnp.where(first, 1.0, mult)
    b = mult * ig * v
    h = h0_ref[...]
    hs = []
    for t in range(n_tok):
        h = a[t * nb:(t + 1) * nb, :] * h + b[t * nb:(t + 1) * nb, :]
        hs.append(h)
    y = (_gelu_tanh(gate) * jnp.concatenate(hs, axis=0)).astype(BF16)
    y_ref[...] = y.reshape(n_tok, nb, RNN_DIM)
    for k in range(CONV_WIDTH - 1):
        ctail_ref[k] = ext[n_tok + k]
    hlast_ref[...] = h


def _odd_sample(x_tm, ch_tm, h0, ow, n_past, bb=32):
    n_tok, nb, _ = x_tm.shape
    blk = lambda t, w: pl.BlockSpec((t, bb, w), lambda i: (0, i, 0))
    return pl.pallas_call(
        functools.partial(_odd_sample_kernel, n_tok=n_tok, n_past=n_past),
        grid=(nb // bb,),
        in_specs=[blk(n_tok, D_MODEL), blk(CONV_WIDTH - 1, RNN_DIM), pl.BlockSpec((bb, RNN_DIM), lambda i: (i, 0))]
                 + [_const_spec(a.shape, 1) for a in ow],
        out_specs=(blk(n_tok, RNN_DIM), blk(CONV_WIDTH - 1, RNN_DIM), pl.BlockSpec((bb, RNN_DIM), lambda i: (i, 0))),
        out_shape=(jax.ShapeDtypeStruct((n_tok, nb, RNN_DIM), BF16),
                   jax.ShapeDtypeStruct((CONV_WIDTH - 1, nb, RNN_DIM), F32),
                   jax.ShapeDtypeStruct((nb, RNN_DIM), F32)),
        compiler_params=_params("arbitrary"),
        name="odd_sample",
    )(x_tm, ch_tm, h0, *ow)


def kernel(x_prompt, x_sample, cache_ckv, cache_krope, state_pool, state_conv, state_lru, page_table, norm_mix,
           w_in_even, g_q_nope, g_q_rope, g_ckv, g_k_rope, g_k_nope, w_uk, w_uv, w_pool, pool_scale, w_out_even,
           w_in_rnn, conv_w, conv_b, w_gate_a, b_gate_a, w_gate_x, b_gate_x, lru_lambda, w_out_rnn, norm_ffn,
           w_up, w_down):
    nb, seq, _ = x_prompt.shape
    db, n_tok, _ = x_sample.shape
    n_past = page_table.shape[1] * PAGE_SIZE
    depth = norm_mix.shape[0]
    assert depth == 2 and cache_ckv.shape[0] == 1, "one even (pool + MLA) layer followed by one odd (RG-LRU) layer"
    rope_sl = slice(ROPE_LANE0, ROPE_LANE0 + QK_ROPE_DIM)

    ew = _even_weights(norm_mix[0], w_in_even[0], g_q_nope[0], g_q_rope[0], g_ckv[0], g_k_rope[0], g_k_nope[0],
                       w_uk[0], w_uv[0], w_pool[0], pool_scale[0])
    tab_p = _rope_table(jnp.arange(seq, dtype=jnp.int32))
    tab_s = jnp.tile(_rope_table(n_past + jnp.arange(n_tok, dtype=jnp.int32)), (db, 1))

    q_p, c_p, krp_p, k_p, v_p, pool_p, utail_p = _even_in_prompt(x_prompt, tab_p, ew, tm=512)
    attn_p = _prompt_attn(q_p, k_p, v_p, tq=512)
    xs = x_sample.reshape(db * n_tok, D_MODEL)
    q_s, c_s, krp_s, u_s = _even_in_sample(xs, tab_s, ew, tm=256)
    attn_s = _sample_attn(q_s, c_s, krp_s, cache_ckv, cache_krope, page_table, g_k_nope[0], w_uk[0], w_uv[0],
                          n_tok=n_tok, sbp=8)
    u_s3 = u_s.reshape(db, n_tok, POOL_DIM)
    pool_s = _pool_sample(u_s3.transpose(1, 0, 2), state_pool[0].transpose(1, 0, 2), ew['wpool'], ew['pscale'],
                          n_past)
    pool_s = pool_s.transpose(1, 0, 2).reshape(db * n_tok, POOL_DIM)

    mlp0 = (w_out_even[0], norm_ffn[0], w_up[0], w_down[0])
    yp = _mix_mlp(x_prompt.reshape(nb * seq, D_MODEL), pool_p.reshape(nb * seq, POOL_DIM), 0,
                  attn_p.reshape(nb * seq, ATTN_OUT_DIM), 0, *mlp0, tm=512)
    ys = _mix_mlp(xs, pool_s, 0, attn_s, 0, *mlp0, tm=256)

    ow = _odd_weights(norm_mix[1], w_in_rnn[0], conv_w[0], conv_b[0], w_gate_a[0], b_gate_a[0], w_gate_x[0],
                      b_gate_x[0], lru_lambda[0])
    rnn_p, ctail_p, hlast_p = _odd_prompt(yp.reshape(nb, seq, D_MODEL), ow, tm=256)
    ys_tm = ys.reshape(db, n_tok, D_MODEL).transpose(1, 0, 2)
    rnn_s, conv_s_tm, lru_s = _odd_sample(ys_tm, state_conv[0].transpose(1, 0, 2), state_lru[0], ow, n_past)

    mlp1 = (w_out_rnn[0], norm_ffn[1], w_up[1], w_down[1])
    rnn_p2 = rnn_p.reshape(nb * seq, RNN_DIM)
    yp = _mix_mlp(yp, rnn_p2, 0, rnn_p2, 1, *mlp1, tm=512)
    rnn_s2 = rnn_s.reshape(n_tok * db, RNN_DIM)
    ys_out = _mix_mlp(ys_tm.reshape(n_tok * db, D_MODEL), rnn_s2, 0, rnn_s2, 1, *mlp1, tm=256)
    ys_out = ys_out.reshape(n_tok, db, D_MODEL).transpose(1, 0, 2)

    pool_state_s = jnp.concatenate([state_pool[0], u_s3], axis=1)[:, -POOL_HIST:]
    return (yp.reshape(nb, seq, D_MODEL), ys_out,
            c_p[None], krp_p[None, :, :, rope_sl], utail_p[None, :, 1:], ctail_p[None, :, SUBLANES - CONV_WIDTH + 1:],
            hlast_p[None, :, 0],
            c_s.reshape(1, db, n_tok, KV_RANK), krp_s[:, rope_sl].reshape(1, db, n_tok, QK_ROPE_DIM),
            pool_state_s[None], conv_s_tm.transpose(1, 0, 2)[None], lru_s[None])
```

```python
import functools

import numpy as np
import jax
import jax.numpy as jnp
from jax import lax
from jax.experimental import pallas as pl
from jax.experimental.pallas import tpu as pltpu

D_MODEL = 1024
PAGE_SIZE = 128
POOL_WINDOWS = (2, 4, 8, 16)
POOL_GROUP_DIM = 128
POOL_DIM = len(POOL_WINDOWS) * POOL_GROUP_DIM
POOL_HIST = max(POOL_WINDOWS) - 1
N_HEADS = 8
QK_NOPE_DIM = 64
QK_ROPE_DIM = 32
QK_HEAD_DIM = QK_NOPE_DIM + QK_ROPE_DIM
V_HEAD_DIM = 64
KV_RANK = 256
Q_DIM = N_HEADS * QK_HEAD_DIM
ATTN_OUT_DIM = N_HEADS * V_HEAD_DIM
ROPE_BASE = 10000.0
SOFTMAX_SCALE = QK_HEAD_DIM ** -0.5
LOG2_E = 1.4426950408889634
RNN_DIM = D_MODEL
RNN_BLOCKS = 8
RNN_BLOCK_DIM = RNN_DIM // RNN_BLOCKS
CONV_WIDTH = 4
LRU_C = 8.0
D_FF = 4 * D_MODEL
NORM_EPS = 1e-6

LANES = 128
SUBLANES = 8
VMEM_LIMIT_BYTES = 56 * 2 ** 20

HEAD_PAD = LANES
QP_DIM = N_HEADS * HEAD_PAD
W1_DIM = POOL_DIM + QP_DIM + KV_RANK + HEAD_PAD
ROPE_LANE0 = QK_NOPE_DIM
HALF = QK_ROPE_DIM // 2

F32 = jnp.float32
BF16 = jnp.bfloat16


def _dot(a, b):
    return jnp.dot(a, b, preferred_element_type=F32)


def _dot_nt(a, b):
    return lax.dot_general(a, b, (((1,), (1,)), ((), ())), preferred_element_type=F32)


def _dot_tn(a, b):
    return lax.dot_general(a, b, (((0,), (0,)), ((), ())), preferred_element_type=F32)


def _rms(x, g):
    ms = jnp.mean(x * x, axis=-1, keepdims=True)
    return x * lax.rsqrt(ms + NORM_EPS) * g


def _expand(rs, e):
    hi = rs.astype(BF16)
    lo = (rs - hi.astype(F32)).astype(BF16)
    return _dot(hi, e) + _dot(lo, e)


def _gelu_tanh(x):
    return 0.5 * x * (1.0 + jnp.tanh(0.7978845608028654 * (x + 0.044715 * (x * x * x))))


def _sigmoid(x):
    return 0.5 * jnp.tanh(0.5 * x) + 0.5


def _sqrt_one_minus_exp2(x):
    t = jnp.tanh(x)
    return jnp.sqrt(-2.0 * t / (1.0 - t))


def _const_spec(shape, grid_rank):
    zeros = (0,) * len(shape)
    if grid_rank == 1:
        return pl.BlockSpec(shape, lambda i: zeros, pipeline_mode=pl.Buffered(1))
    return pl.BlockSpec(shape, lambda i, j: zeros, pipeline_mode=pl.Buffered(1))


def _params(*sem):
    return pltpu.CompilerParams(dimension_semantics=sem, vmem_limit_bytes=VMEM_LIMIT_BYTES)


def _even_in_kernel(*refs, tm, prompt):
    if prompt:
        (x_ref, tab_ref, gmix_ref, w1_ref, gq_ref, segq_ref, eq_ref, gckv_ref, gkr_ref,
         wuk_ref, segk_ref, ek_ref, gk_ref, wuv_ref, wpool_ref, pscale_ref,
         q_ref, c_ref, krp_ref, k_ref, v_ref, pool_ref, utail_ref, uext_ref) = refs
    else:
        (x_ref, tab_ref, gmix_ref, w1_ref, gq_ref, segq_ref, eq_ref, gckv_ref, gkr_ref,
         q_ref, c_ref, krp_ref, u_ref) = refs

    xn = _rms(x_ref[...], gmix_ref[...]).astype(BF16)
    z = _dot(xn, w1_ref[...])
    u = z[:, 0:POOL_DIM]
    qz = z[:, POOL_DIM:POOL_DIM + QP_DIM]
    cz = z[:, POOL_DIM + QP_DIM:POOL_DIM + QP_DIM + KV_RANK]
    krz = z[:, POOL_DIM + QP_DIM + KV_RANK:]

    ta = tab_ref[:, 0:LANES]
    tb = tab_ref[:, LANES:2 * LANES]
    tc = tab_ref[:, 2 * LANES:3 * LANES]

    def rope(blk):
        return blk * ta + pltpu.roll(blk, HALF, 1) * tb + pltpu.roll(blk, LANES - HALF, 1) * tc

    msq = _dot((qz * qz).astype(BF16), segq_ref[...])
    qn = qz * _expand(lax.rsqrt(msq + NORM_EPS), eq_ref[...]) * gq_ref[...]
    for h in range(N_HEADS):
        lanes = slice(h * HEAD_PAD, (h + 1) * HEAD_PAD)
        q_ref[:, lanes] = rope(qn[:, lanes]).astype(q_ref.dtype)

    c = _rms(cz, gckv_ref[...])
    c_ref[...] = c
    mskr = jnp.sum(krz * krz, axis=-1, keepdims=True) * (1.0 / QK_ROPE_DIM)
    krr = rope(krz * lax.rsqrt(mskr + NORM_EPS) * gkr_ref[...])
    krp_ref[...] = krr

    if not prompt:
        u_ref[...] = u
        return

    cb = c.astype(BF16)
    kn = _dot(cb, wuk_ref[...])
    msk = _dot((kn * kn).astype(BF16), segk_ref[...])
    knn = kn * _expand(lax.rsqrt(msk + NORM_EPS), ek_ref[...]) * gk_ref[...]
    for h in range(N_HEADS):
        lanes = slice(h * HEAD_PAD, (h + 1) * HEAD_PAD)
        k_ref[:, lanes] = (knn[:, lanes] + krr).astype(BF16)
    ln = lax.broadcasted_iota(jnp.int32, (1, QP_DIM), 1)
    one_lane = (ln & (HEAD_PAD - 1)) + ((ln // HEAD_PAD) & 1) * V_HEAD_DIM == V_HEAD_DIM
    v_ref[...] = (_dot(cb, wuv_ref[...]) + jnp.where(one_lane, 1.0, 0.0)).astype(BF16)

    i = pl.program_id(1)
    hal = 2 * SUBLANES

    @pl.when(i == 0)
    def _():
        uext_ref[0:hal, :] = jnp.zeros((hal, POOL_DIM), F32)

    uext_ref[hal:hal + tm, :] = u
    pos = lax.broadcasted_iota(jnp.int32, (tm, POOL_GROUP_DIM), 0) + i * tm
    for g, w in enumerate(POOL_WINDOWS):
        cols = slice(g * POOL_GROUP_DIM, (g + 1) * POOL_GROUP_DIM)
        acc = uext_ref[hal:hal + tm, cols]
        for k in range(1, w):
            acc = acc + uext_ref[hal - k:hal - k + tm, cols]
        cnt = jnp.minimum(pos + 1, w).astype(F32)
        d = acc / cnt - u[:, cols]
        y = _dot(d.astype(BF16), wpool_ref[g]) * pscale_ref[:, cols]
        pool_ref[:, cols] = y.astype(BF16)
    uext_ref[0:hal, :] = uext_ref[tm:tm + hal, :]

    @pl.when(i == pl.num_programs(1) - 1)
    def _():
        utail_ref[...] = uext_ref[0:hal, :]


def _seg_mats():
    segq = np.zeros((QP_DIM, LANES), np.float32)
    eq = np.zeros((LANES, QP_DIM), np.float32)
    segk = np.zeros((QP_DIM, LANES), np.float32)
    ek = np.zeros((LANES, QP_DIM), np.float32)
    for h in range(N_HEADS):
        b = h * HEAD_PAD
        segq[b:b + QK_NOPE_DIM, 2 * h] = 1.0 / QK_NOPE_DIM
        segq[b + QK_NOPE_DIM:b + QK_HEAD_DIM, 2 * h + 1] = 1.0 / QK_ROPE_DIM
        eq[2 * h, b:b + QK_NOPE_DIM] = 1.0
        eq[2 * h + 1, b + QK_NOPE_DIM:b + QK_HEAD_DIM] = 1.0
        segk[b:b + QK_NOPE_DIM, h] = 1.0 / QK_NOPE_DIM
        ek[h, b:b + QK_NOPE_DIM] = 1.0
    return [jnp.asarray(m, BF16) for m in (segq, eq, segk, ek)]


def _rope_table(pos):
    inv = ROPE_BASE ** (-jnp.arange(HALF, dtype=F32) / HALF)
    ang = pos.astype(F32)[:, None] * inv[None, :]
    cos, sin = jnp.cos(ang), jnp.sin(ang)
    n = pos.shape[0]
    one = jnp.ones((n, ROPE_LANE0), F32)
    zero = jnp.zeros((n, ROPE_LANE0), F32)
    zh = jnp.zeros((n, HALF), F32)
    tail1 = jnp.ones((n, LANES - ROPE_LANE0 - QK_ROPE_DIM), F32)
    tail0 = jnp.zeros((n, LANES - ROPE_LANE0 - QK_ROPE_DIM), F32)
    ta = jnp.concatenate([one, cos, cos, tail1], axis=1)
    tb = jnp.concatenate([zero, zh, sin, tail0], axis=1)
    tc = jnp.concatenate([zero, -sin, zh, tail0], axis=1)
    return jnp.concatenate([ta, tb, tc], axis=1)


def _head_pad(w, lead):
    d = w.shape[-1]
    return jnp.pad(w, ((0, 0), (0, 0), (0, HEAD_PAD - d))).reshape(lead, QP_DIM)


def _head_row(parts):
    row = jnp.concatenate(parts)
    row = jnp.pad(row, (0, HEAD_PAD - row.shape[0]))
    return jnp.tile(row, N_HEADS)[None, :]


def _even_weights(norm_mix, w_in, g_q_nope, g_q_rope, g_ckv, g_k_rope, g_k_nope, w_uk, w_uv, w_pool, pool_scale):
    wq = _head_pad(w_in[:, POOL_DIM:POOL_DIM + Q_DIM].reshape(D_MODEL, N_HEADS, QK_HEAD_DIM), D_MODEL)
    wc = w_in[:, POOL_DIM + Q_DIM:POOL_DIM + Q_DIM + KV_RANK]
    wkr = jnp.pad(w_in[:, POOL_DIM + Q_DIM + KV_RANK:], ((0, 0), (ROPE_LANE0, HEAD_PAD - QK_HEAD_DIM)))
    w1 = jnp.concatenate([w_in[:, :POOL_DIM], wq, wc, wkr], axis=1).astype(BF16)
    gq = _head_row([g_q_nope, g_q_rope]) * (SOFTMAX_SCALE * LOG2_E)
    gkr = _head_row([jnp.zeros((ROPE_LANE0,), F32), g_k_rope])[:, :HEAD_PAD]
    gk = _head_row([g_k_nope])
    wuk = _head_pad(w_uk, KV_RANK).astype(BF16)
    v_even = jnp.pad(w_uv, ((0, 0), (0, 0), (0, HEAD_PAD - V_HEAD_DIM)))
    v_odd = jnp.pad(w_uv, ((0, 0), (0, 0), (HEAD_PAD - V_HEAD_DIM, 0)))
    odd = (jnp.arange(N_HEADS) % 2 == 1)[None, :, None]
    wuv = jnp.where(odd, v_odd, v_even).reshape(KV_RANK, QP_DIM).astype(BF16)
    return dict(gmix=norm_mix[None, :], w1=w1, gq=gq, gckv=g_ckv[None, :], gkr=gkr, gk=gk, wuk=wuk, wuv=wuv,
                wpool=w_pool.astype(BF16), pscale=pool_scale[None, :])


def _even_in_prompt(x, tab, ew, tm):
    nb, seq, _ = x.shape
    nt = seq // tm
    segq, eq, segk, ek = _seg_mats()
    row = lambda width: pl.BlockSpec((None, tm, width), lambda b, i: (b, i, 0))
    cs = lambda a: _const_spec(a.shape, 2)
    consts = [ew['gmix'], ew['w1'], ew['gq'], segq, eq, ew['gckv'], ew['gkr'],
              ew['wuk'], segk, ek, ew['gk'], ew['wuv'], ew['wpool'], ew['pscale']]
    out_shape = (
        jax.ShapeDtypeStruct((nb, seq, QP_DIM), BF16),
        jax.ShapeDtypeStruct((nb, seq, KV_RANK), F32),
        jax.ShapeDtypeStruct((nb, seq, HEAD_PAD), F32),
        jax.ShapeDtypeStruct((nb, seq, QP_DIM), BF16),
        jax.ShapeDtypeStruct((nb, seq, QP_DIM), BF16),
        jax.ShapeDtypeStruct((nb, seq, POOL_DIM), BF16),
        jax.ShapeDtypeStruct((nb, 2 * SUBLANES, POOL_DIM), F32),
    )
    out_specs = (row(QP_DIM), row(KV_RANK), row(HEAD_PAD), row(QP_DIM), row(QP_DIM), row(POOL_DIM),
                 pl.BlockSpec((None, 2 * SUBLANES, POOL_DIM), lambda b, i: (b, 0, 0)))
    return pl.pallas_call(
        functools.partial(_even_in_kernel, tm=tm, prompt=True),
        grid=(nb, nt),
        in_specs=[row(D_MODEL), pl.BlockSpec((tm, 3 * LANES), lambda b, i: (i, 0))] + [cs(a) for a in consts],
        out_specs=out_specs,
        out_shape=out_shape,
        scratch_shapes=[pltpu.VMEM((2 * SUBLANES + tm, POOL_DIM), F32)],
        compiler_params=_params("arbitrary", "arbitrary"),
        name="even_in_prompt",
    )(x, tab, *consts)


def _even_in_sample(x, tab, ew, tm):
    rows = x.shape[0]
    segq, eq, _, _ = _seg_mats()
    row = lambda width: pl.BlockSpec((tm, width), lambda i: (i, 0))
    cs = lambda a: _const_spec(a.shape, 1)
    consts = [ew['gmix'], ew['w1'], ew['gq'], segq, eq, ew['gckv'], ew['gkr']]
    out_shape = (
        jax.ShapeDtypeStruct((rows, QP_DIM), F32),
        jax.ShapeDtypeStruct((rows, KV_RANK), F32),
        jax.ShapeDtypeStruct((rows, HEAD_PAD), F32),
        jax.ShapeDtypeStruct((rows, POOL_DIM), F32),
    )
    return pl.pallas_call(
        functools.partial(_even_in_kernel, tm=tm, prompt=False),
        grid=(rows // tm,),
        in_specs=[row(D_MODEL), row(3 * LANES)] + [cs(a) for a in consts],
        out_specs=(row(QP_DIM), row(KV_RANK), row(HEAD_PAD), row(POOL_DIM)),
        out_shape=out_shape,
        compiler_params=_params("arbitrary"),
        name="even_in_sample",
    )(x, tab, *consts)


def _prompt_attn_kernel(q_ref, k_ref, v_ref, o_ref, *, tq, tk):
    qi = pl.program_id(1)
    nsub = tq // tk
    row = lax.broadcasted_iota(jnp.int32, (tq, tk), 0)
    col = lax.broadcasted_iota(jnp.int32, (tq, tk), 1)
    lane = lax.broadcasted_iota(jnp.int32, (tq, HEAD_PAD), 1)
    for j in range(N_HEADS // 2):
        heads = (2 * j, 2 * j + 1)
        lanes = [slice(h * HEAD_PAD, (h + 1) * HEAD_PAD) for h in heads]
        qs = [q_ref[:, ln] for ln in lanes]

        def tile(kt, carry, shift, lanes=lanes, qs=qs):
            off = pl.multiple_of(kt * tk, tk)
            new = []
            for (m, acc), ln, qh in zip(carry, lanes, qs):
                s = _dot_nt(qh, k_ref[pl.ds(off, tk), ln])
                if shift is not None:
                    s = jnp.where(row >= col + shift, s, -jnp.inf)
                m_new = jnp.maximum(m, jnp.max(s, axis=-1, keepdims=True))
                p = jnp.exp2(s - m_new)
                acc = jnp.exp2(m - m_new) * acc + _dot(p.astype(BF16), v_ref[pl.ds(off, tk), ln])
                new.append((m_new, acc))
            return tuple(new)

        def below(jj, carry):
            for u in range(nsub):
                carry = tile(jj * nsub + u, carry, None)
            return carry

        carry = tuple((jnp.full((tq, 1), -jnp.inf, F32), jnp.zeros((tq, HEAD_PAD), F32)) for _ in heads)
        carry = lax.fori_loop(0, qi, below, carry)
        for u in range(nsub):
            carry = tile(qi * nsub + u, carry, u * tk)
        (_, acc_e), (_, acc_o) = carry
        out_e = jnp.where(lane < V_HEAD_DIM, acc_e / acc_e[:, V_HEAD_DIM:V_HEAD_DIM + 1], 0.0)
        out_o = jnp.where(lane >= V_HEAD_DIM, acc_o / acc_o[:, 0:1], 0.0)
        o_ref[:, j * LANES:(j + 1) * LANES] = (out_e + out_o).astype(BF16)


def _prompt_attn(q, k, v, tq, tk):
    nb, seq, _ = q.shape
    assert tq % tk == 0 and seq % tq == 0
    full = pl.BlockSpec((None, seq, QP_DIM), lambda b, i: (b, 0, 0), pipeline_mode=pl.Buffered(1))
    return pl.pallas_call(
        functools.partial(_prompt_attn_kernel, tq=tq, tk=tk),
        grid=(nb, seq // tq),
        in_specs=[pl.BlockSpec((None, tq, QP_DIM), lambda b, i: (b, i, 0)), full, full],
        out_specs=pl.BlockSpec((None, tq, ATTN_OUT_DIM), lambda b, i: (b, i, 0)),
        out_shape=jax.ShapeDtypeStruct((nb, seq, ATTN_OUT_DIM), BF16),
        compiler_params=_params("arbitrary", "arbitrary"),
        name="prompt_attn",
    )(q, k, v)


def _sample_attn_kernel(pt_ref, q_ref, cn_ref, krn_ref, gk_ref, wukp_ref, wukd_ref, wuv_ref, segt_ref,
                        ckv_hbm, kr_hbm, o_ref, cbuf, krbuf, sem, qabs_ref, qr_ref, s_ref, cb_ref, acc_ref,
                        *, n_tok, n_pages, sbp):
    b = pl.program_id(0)
    slot = lax.rem(b, 2)
    sb = sbp * PAGE_SIZE
    n_sb = n_pages // sbp
    ncol = N_HEADS * n_tok

    def start_pages(row, i0, sl):
        for k in range(sbp):
            page = pt_ref[row * n_pages + i0 + k]
            pltpu.make_async_copy(ckv_hbm.at[0, page], cbuf.at[sl, i0 + k], sem.at[0, sl]).start()
            pltpu.make_async_copy(kr_hbm.at[0, page], krbuf.at[sl, i0 + k], sem.at[1, sl]).start()

    @pl.when(b == 0)
    def _():
        def first(i, carry):
            start_pages(0, i * sbp, 0)
            return carry
        lax.fori_loop(0, n_sb, first, 0)

    q = q_ref[...]
    qt = jnp.concatenate([q] * N_HEADS, axis=0)
    r = lax.broadcasted_iota(jnp.int32, (ncol, QP_DIM), 0)
    ln = lax.broadcasted_iota(jnp.int32, (ncol, QP_DIM), 1)
    keep = jnp.where((ln & (HEAD_PAD - 1)) < QK_NOPE_DIM, r // n_tok, -1) == ln // HEAD_PAD
    qg = jnp.where(keep, qt * gk_ref[...], 0.0).astype(BF16)
    qabs_ref[...] = _dot_nt(qg, wukp_ref[...]).astype(BF16)
    lane = lax.broadcasted_iota(jnp.int32, (n_tok, LANES), 1)
    blocks = [jnp.where(lane < QK_ROPE_DIM,
                        pltpu.roll(q[:, h * HEAD_PAD:(h + 1) * HEAD_PAD], LANES - ROPE_LANE0, 1), 0.0)
              for h in range(N_HEADS)]
    qr_ref[...] = jnp.concatenate(blocks, axis=0).astype(BF16)

    def sumsq(cb):
        k2 = _dot(cb, wukd_ref[...])
        k2 = k2 * k2
        return (k2[:, 0:LANES] + k2[:, LANES:2 * LANES]) + (k2[:, 2 * LANES:3 * LANES] + k2[:, 3 * LANES:])

    def scores(cb, krt):
        ss = _dot_nt(segt_ref[...], sumsq(cb).astype(BF16))
        sp = _dot_nt(qabs_ref[...], cb)
        rope = _dot(qr_ref[:, 0:QK_ROPE_DIM], krt.astype(BF16))
        return sp * lax.rsqrt(ss + NORM_EPS) + rope

    def lane_tiles(x, op):
        out = x[:, 0:LANES]
        for j in range(1, x.shape[1] // LANES):
            out = op(out, x[:, j * LANES:(j + 1) * LANES])
        return out

    pltpu.make_async_copy(ckv_hbm.at[0, pl.ds(0, n_pages)], cbuf.at[slot], sem.at[0, slot]).wait()
    pltpu.make_async_copy(kr_hbm.at[0, pl.ds(0, n_pages)], krbuf.at[slot], sem.at[1, slot]).wait()

    nxt = jnp.minimum(b + 1, pl.num_programs(0) - 1)

    def score_block(i, m):
        start_pages(nxt, i * sbp, 1 - slot)
        off = pl.multiple_of(i * sb, sb)
        cb = cbuf[slot, pl.ds(i * sbp, sbp)].reshape(sb, KV_RANK).astype(BF16)
        cb_ref[pl.ds(off, sb), :] = cb
        krt = jnp.concatenate([krbuf[slot, i * sbp + k] for k in range(sbp)], axis=1)
        s = scores(cb, krt)
        s_ref[i] = s
        return jnp.maximum(m, lane_tiles(s, jnp.maximum))

    m = lax.fori_loop(0, n_sb, score_block, jnp.full((ncol, LANES), -jnp.inf, F32), unroll=4)

    cn = jnp.concatenate([cn_ref[...], jnp.zeros((PAGE_SIZE - n_tok, KV_RANK), F32)], axis=0).astype(BF16)
    krn = jnp.concatenate([pltpu.roll(krn_ref[...], LANES - ROPE_LANE0, 1),
                           jnp.zeros((PAGE_SIZE - n_tok, LANES), F32)], axis=0)
    s = scores(cn, krn.T[0:QK_ROPE_DIM, :])
    key = lax.broadcasted_iota(jnp.int32, s.shape, 1)
    tq = lax.broadcasted_iota(jnp.int32, s.shape, 0) % n_tok
    s = jnp.where(key <= tq, s, -jnp.inf)
    m = jnp.max(jnp.maximum(m, s), axis=1, keepdims=True)

    p = jnp.exp2(s - m)
    acc_ref[...] = _dot(p.astype(BF16), cn)

    def value_block(i, lp):
        off = pl.multiple_of(i * sb, sb)
        p = jnp.exp2(s_ref[i] - m)
        acc_ref[...] += _dot(p.astype(BF16), cb_ref[pl.ds(off, sb), :])
        return lp + lane_tiles(p, jnp.add)

    l = jnp.sum(lax.fori_loop(0, n_sb, value_block, p, unroll=4), axis=1, keepdims=True)

    lat = acc_ref[...] * (1.0 / l)
    zz = _dot(lat.astype(BF16), wuv_ref[...])
    lane_h = lax.broadcasted_iota(jnp.int32, (n_tok, ATTN_OUT_DIM), 1) // V_HEAD_DIM
    out = jnp.zeros((n_tok, ATTN_OUT_DIM), F32)
    for h in range(N_HEADS):
        out = out + jnp.where(lane_h == h, zz[h * n_tok:(h + 1) * n_tok, :], 0.0)
    o_ref[...] = out

    @pl.when(b == pl.num_programs(0) - 1)
    def _():
        pltpu.make_async_copy(ckv_hbm.at[0, pl.ds(0, n_pages)], cbuf.at[1 - slot], sem.at[0, 1 - slot]).wait()
        pltpu.make_async_copy(kr_hbm.at[0, pl.ds(0, n_pages)], krbuf.at[1 - slot], sem.at[1, 1 - slot]).wait()


def _sample_attn(q, c, krp, cache_ckv, cache_krope, page_table, g_k_nope, w_uk, w_uv, n_tok, sbp):
    nb, n_pages = page_table.shape
    ncol = N_HEADS * n_tok
    gk = _head_row([g_k_nope])
    wukp = _head_pad(w_uk, KV_RANK).astype(BF16)
    wukd = w_uk.transpose(0, 2, 1).reshape(KV_RANK, N_HEADS * QK_NOPE_DIM).astype(BF16)
    wuv = w_uv.reshape(KV_RANK, ATTN_OUT_DIM).astype(BF16)
    segt = np.zeros((ncol, LANES), np.float32)
    for col in range(ncol):
        segt[col, (np.arange(LANES) % N_HEADS) == col // n_tok] = 1.0 / QK_NOPE_DIM
    segt = jnp.asarray(segt, BF16)
    pt = page_table.reshape(-1)
    kr_t = jnp.swapaxes(cache_krope, 2, 3)

    rows = lambda width: pl.BlockSpec((n_tok, width), lambda b, pt_ref: (b, 0))
    const = lambda a: pl.BlockSpec(a.shape, lambda b, pt_ref: (0,) * a.ndim)
    hbm = pl.BlockSpec(memory_space=pl.ANY)
    n_keys = n_pages * PAGE_SIZE
    grid_spec = pltpu.PrefetchScalarGridSpec(
        num_scalar_prefetch=1,
        grid=(nb,),
        in_specs=[rows(QP_DIM), rows(KV_RANK), rows(HEAD_PAD), const(gk), const(wukp), const(wukd), const(wuv),
                  const(segt), hbm, hbm],
        out_specs=rows(ATTN_OUT_DIM),
        scratch_shapes=[pltpu.VMEM((2, n_pages, PAGE_SIZE, KV_RANK), F32),
                        pltpu.VMEM((2, n_pages, QK_ROPE_DIM, PAGE_SIZE), F32),
                        pltpu.SemaphoreType.DMA((2, 2)),
                        pltpu.VMEM((ncol, KV_RANK), BF16), pltpu.VMEM((ncol, LANES), BF16),
                        pltpu.VMEM((n_pages // sbp, ncol, sbp * PAGE_SIZE), F32),
                        pltpu.VMEM((n_keys, KV_RANK), BF16),
                        pltpu.VMEM((ncol, KV_RANK), F32)],
    )
    return pl.pallas_call(
        functools.partial(_sample_attn_kernel, n_tok=n_tok, n_pages=n_pages, sbp=sbp),
        grid_spec=grid_spec,
        out_shape=jax.ShapeDtypeStruct((nb * n_tok, ATTN_OUT_DIM), F32),
        compiler_params=_params("arbitrary"),
        name="sample_attn",
    )(pt, q, c, krp, gk, wukp, wukd, wuv, segt, cache_ckv, kr_t)


def _pool_sample_kernel(u_ref, hist_ref, wpool_ref, pscale_ref, o_ref, *, n_tok, n_past):
    ext = [hist_ref[k] for k in range(POOL_HIST)] + [u_ref[t] for t in range(n_tok)]
    nb = u_ref.shape[1]
    for g, w in enumerate(POOL_WINDOWS):
        cols = slice(g * POOL_GROUP_DIM, (g + 1) * POOL_GROUP_DIM)
        ds = []
        for t in range(n_tok):
            acc = ext[POOL_HIST + t][:, cols]
            for k in range(1, w):
                acc = acc + ext[POOL_HIST + t - k][:, cols]
            ds.append(acc / float(min(n_past + t + 1, w)) - ext[POOL_HIST + t][:, cols])
        d = jnp.concatenate(ds, axis=0).astype(BF16)
        y = (_dot(d, wpool_ref[g]) * pscale_ref[:, cols]).astype(BF16)
        for t in range(n_tok):
            o_ref[t, :, cols] = y[t * nb:(t + 1) * nb, :]


def _pool_sample(u_tm, hist_tm, w_pool, pscale, n_past):
    n_tok, nb, _ = u_tm.shape
    return pl.pallas_call(
        functools.partial(_pool_sample_kernel, n_tok=n_tok, n_past=n_past),
        out_shape=jax.ShapeDtypeStruct((n_tok, nb, POOL_DIM), BF16),
        compiler_params=pltpu.CompilerParams(vmem_limit_bytes=VMEM_LIMIT_BYTES),
        name="pool_sample",
    )(u_tm, hist_tm, w_pool, pscale)


def _mix_mlp_kernel(x_ref, a1_ref, a2_ref, wo1_ref, wo2_ref, gffn_ref, wup_ref, wdn_ref, y_ref, *, ck):
    y_ref[...] = (x_ref[...] + _dot(a1_ref[...].astype(BF16), wo1_ref[...])
                  + _dot(a2_ref[...].astype(BF16), wo2_ref[...]))
    xn = _rms(y_ref[...], gffn_ref[...]).astype(BF16)
    for c in range(D_FF // ck):
        h = jnp.maximum(_dot(xn, wup_ref[:, c * ck:(c + 1) * ck]), 0.0)
        y_ref[...] += _dot((h * h).astype(BF16), wdn_ref[c * ck:(c + 1) * ck, :])


def _mix_mlp(x, a1, a1_col, a2, a2_col, w_out, g_ffn, w_up, w_down, tm, ck=1024):
    rows = x.shape[0]
    half = D_MODEL // 2
    wo = w_out.astype(BF16)
    consts = [wo[:half], wo[half:], g_ffn[None, :], w_up.astype(BF16), w_down.astype(BF16)]
    row = pl.BlockSpec((tm, D_MODEL), lambda i: (i, 0))
    return pl.pallas_call(
        functools.partial(_mix_mlp_kernel, ck=ck),
        grid=(rows // tm,),
        in_specs=[row, pl.BlockSpec((tm, half), lambda i: (i, a1_col)), pl.BlockSpec((tm, half), lambda i: (i, a2_col))]
                 + [_const_spec(a.shape, 1) for a in consts],
        out_specs=row,
        out_shape=jax.ShapeDtypeStruct((rows, D_MODEL), F32),
        compiler_params=_params("arbitrary"),
        name="mix_mlp",
    )(x, a1, a2, *consts)


def _gates(v, wg_ref, bga_ref, bgx_ref, lam_ref):
    vb = v.astype(BF16)
    pair = 2 * RNN_BLOCK_DIM
    ga, gx = [], []
    for p in range(RNN_BLOCKS // 2):
        g = _dot(vb[:, p * pair:(p + 1) * pair], wg_ref[p])
        ga.append(g[:, :pair])
        gx.append(g[:, pair:])
    r = _sigmoid(jnp.concatenate(ga, axis=1) + bga_ref[...])
    ig = _sigmoid(jnp.concatenate(gx, axis=1) + bgx_ref[...])
    nl = -lam_ref[...]
    softplus = jnp.maximum(nl, 0.0) + jnp.log1p(jnp.exp(-jnp.abs(nl)))
    log_a = (-LRU_C) * r * softplus
    return log_a, ig


def _odd_prompt_kernel(x_ref, gmix_ref, win_ref, cw_ref, cb_ref, wg_ref, bga_ref, bgx_ref, lam_ref,
                       y_ref, ctail_ref, hlast_ref, uext_ref, a_ref, b_ref, hcar_ref, *, tm):
    i = pl.program_id(1)
    sub = SUBLANES

    @pl.when(i == 0)
    def _():
        uext_ref[0:sub, :] = jnp.zeros((sub, RNN_DIM), F32)
        a_ref[0:sub, :] = jnp.zeros((sub, RNN_DIM), F32)
        b_ref[0:sub, :] = jnp.zeros((sub, RNN_DIM), F32)
        hcar_ref[...] = jnp.zeros((sub, RNN_DIM), F32)

    xn = _rms(x_ref[...], gmix_ref[...]).astype(BF16)
    z = _dot(xn, win_ref[...])
    gate = z[:, :RNN_DIM]
    uext_ref[sub:sub + tm, :] = z[:, RNN_DIM:]
    v = cb_ref[...]
    for k in range(CONV_WIDTH):
        r0 = sub - (CONV_WIDTH - 1) + k
        v = v + uext_ref[r0:r0 + tm, :] * cw_ref[k:k + 1, :]
    uext_ref[0:sub, :] = uext_ref[tm:tm + sub, :]

    log_a, ig = _gates(v, wg_ref, bga_ref, bgx_ref, lam_ref)
    a = jnp.exp(log_a)
    row = lax.broadcasted_iota(jnp.int32, (tm, RNN_DIM), 0)
    mult = jnp.where(row + i * tm == 0, 1.0, _sqrt_one_minus_exp2(log_a))
    b = mult * ig * v

    rowmod = row & (sub - 1)
    for s in (1, 2, 4):
        ok = rowmod >= s
        b = jnp.where(ok, a * pltpu.roll(b, s, 0), 0.0) + b
        a = jnp.where(ok, a * pltpu.roll(a, s, 0), a)
    a_ref[sub:sub + tm, :] = a
    b_ref[sub:sub + tm, :] = b

    def group(g, hb):
        off = pl.multiple_of(sub + g * sub, sub)
        hg = a_ref[pl.ds(off, sub), :] * hb + b_ref[pl.ds(off, sub), :]
        b_ref[pl.ds(off, sub), :] = hg
        return jnp.broadcast_to(hg[sub - 1:sub, :], (sub, RNN_DIM))

    hb = lax.fori_loop(0, tm // sub, group, hcar_ref[...])
    hcar_ref[...] = hb
    y_ref[...] = (_gelu_tanh(gate) * b_ref[sub:sub + tm, :]).astype(BF16)

    @pl.when(i == pl.num_programs(1) - 1)
    def _():
        ctail_ref[...] = uext_ref[0:sub, :]
        hlast_ref[...] = hb


def _odd_weights(norm_mix, w_in, conv_w, conv_b, w_ga, b_ga, w_gx, b_gx, lam):
    def pairs(w):
        z = jnp.zeros((RNN_BLOCKS // 2, RNN_BLOCK_DIM, RNN_BLOCK_DIM), F32)
        top = jnp.concatenate([w[0::2], z], axis=2)
        bot = jnp.concatenate([z, w[1::2]], axis=2)
        return jnp.concatenate([top, bot], axis=1)
    wg = jnp.concatenate([pairs(w_ga), pairs(w_gx)], axis=2).astype(BF16)
    return [norm_mix[None, :], w_in.astype(BF16), conv_w, conv_b[None, :], wg, b_ga[None, :], b_gx[None, :],
            lam[None, :]]


def _odd_prompt(x, ow, tm):
    nb, seq, _ = x.shape
    row = pl.BlockSpec((None, tm, D_MODEL), lambda b, i: (b, i, 0))
    tail = pl.BlockSpec((None, SUBLANES, RNN_DIM), lambda b, i: (b, 0, 0))
    return pl.pallas_call(
        functools.partial(_odd_prompt_kernel, tm=tm),
        grid=(nb, seq // tm),
        in_specs=[row] + [_const_spec(a.shape, 2) for a in ow],
        out_specs=(row, tail, tail),
        out_shape=(jax.ShapeDtypeStruct((nb, seq, RNN_DIM), BF16),
                   jax.ShapeDtypeStruct((nb, SUBLANES, RNN_DIM), F32),
                   jax.ShapeDtypeStruct((nb, SUBLANES, RNN_DIM), F32)),
        scratch_shapes=[pltpu.VMEM((SUBLANES + tm, RNN_DIM), F32), pltpu.VMEM((SUBLANES + tm, RNN_DIM), F32),
                        pltpu.VMEM((SUBLANES + tm, RNN_DIM), F32), pltpu.VMEM((SUBLANES, RNN_DIM), F32)],
        compiler_params=_params("arbitrary", "arbitrary"),
        name="odd_prompt",
    )(x, *ow)


def _odd_sample_kernel(x_ref, ch_ref, h0_ref, gmix_ref, win_ref, cw_ref, cb_ref, wg_ref, bga_ref, bgx_ref, lam_ref,
                       y_ref, ctail_ref, hlast_ref, *, n_tok, n_past):
    nb = x_ref.shape[1]
    x = x_ref[...].reshape(n_tok * nb, D_MODEL)
    z = _dot(_rms(x, gmix_ref[...]).astype(BF16), win_ref[...])
    gate = z[:, :RNN_DIM]
    ext = [ch_ref[k] for k in range(CONV_WIDTH - 1)] + [z[t * nb:(t + 1) * nb, RNN_DIM:] for t in range(n_tok)]
    vs = []
    for t in range(n_tok):
        v = cb_ref[...]
        for k in range(CONV_WIDTH):
            v = v + ext[t + k] * cw_ref[k:k + 1, :]
        vs.append(v)
    v = jnp.concatenate(vs, axis=0)
    log_a, ig = _gates(v, wg_ref, bga_ref, bgx_ref, lam_ref)
    a = jnp.exp(log_a)
    mult = _sqrt_one_minus_exp2(log_a)
    if n_past == 0:
        first = lax.broadcasted_iota(jnp.int32, mult.shape, 0) < nb
        mult = jnp.where(first, 1.0, mult)
    b = mult * ig * v
    h = h0_ref[...]
    hs = []
    for t in range(n_tok):
        h = a[t * nb:(t + 1) * nb, :] * h + b[t * nb:(t + 1) * nb, :]
        hs.append(h)
    y = (_gelu_tanh(gate) * jnp.concatenate(hs, axis=0)).astype(BF16)
    y_ref[...] = y.reshape(n_tok, nb, RNN_DIM)
    for k in range(CONV_WIDTH - 1):
        ctail_ref[k] = ext[n_tok + k]
    hlast_ref[...] = h


def _odd_sample(x_tm, ch_tm, h0, ow, n_past, bb=32):
    n_tok, nb, _ = x_tm.shape
    blk = lambda t, w: pl.BlockSpec((t, bb, w), lambda i: (0, i, 0))
    return pl.pallas_call(
        functools.partial(_odd_sample_kernel, n_tok=n_tok, n_past=n_past),
        grid=(nb // bb,),
        in_specs=[blk(n_tok, D_MODEL), blk(CONV_WIDTH - 1, RNN_DIM), pl.BlockSpec((bb, RNN_DIM), lambda i: (i, 0))]
                 + [_const_spec(a.shape, 1) for a in ow],
        out_specs=(blk(n_tok, RNN_DIM), blk(CONV_WIDTH - 1, RNN_DIM), pl.BlockSpec((bb, RNN_DIM), lambda i: (i, 0))),
        out_shape=(jax.ShapeDtypeStruct((n_tok, nb, RNN_DIM), BF16),
                   jax.ShapeDtypeStruct((CONV_WIDTH - 1, nb, RNN_DIM), F32),
                   jax.ShapeDtypeStruct((nb, RNN_DIM), F32)),
        compiler_params=_params("arbitrary"),
        name="odd_sample",
    )(x_tm, ch_tm, h0, *ow)


def kernel(x_prompt, x_sample, cache_ckv, cache_krope, state_pool, state_conv, state_lru, page_table, norm_mix,
           w_in_even, g_q_nope, g_q_rope, g_ckv, g_k_rope, g_k_nope, w_uk, w_uv, w_pool, pool_scale, w_out_even,
           w_in_rnn, conv_w, conv_b, w_gate_a, b_gate_a, w_gate_x, b_gate_x, lru_lambda, w_out_rnn, norm_ffn,
           w_up, w_down):
    nb, seq, _ = x_prompt.shape
    db, n_tok, _ = x_sample.shape
    n_past = page_table.shape[1] * PAGE_SIZE
    depth = norm_mix.shape[0]
    assert depth == 2 and cache_ckv.shape[0] == 1, "one even (pool + MLA) layer followed by one odd (RG-LRU) layer"
    rope_sl = slice(ROPE_LANE0, ROPE_LANE0 + QK_ROPE_DIM)

    ew = _even_weights(norm_mix[0], w_in_even[0], g_q_nope[0], g_q_rope[0], g_ckv[0], g_k_rope[0], g_k_nope[0],
                       w_uk[0], w_uv[0], w_pool[0], pool_scale[0])
    tab_p = _rope_table(jnp.arange(seq, dtype=jnp.int32))
    tab_s = jnp.tile(_rope_table(n_past + jnp.arange(n_tok, dtype=jnp.int32)), (db, 1))

    q_p, c_p, krp_p, k_p, v_p, pool_p, utail_p = _even_in_prompt(x_prompt, tab_p, ew, tm=512)
    attn_p = _prompt_attn(q_p, k_p, v_p, tq=1024, tk=512)
    xs = x_sample.reshape(db * n_tok, D_MODEL)
    q_s, c_s, krp_s, u_s = _even_in_sample(xs, tab_s, ew, tm=256)
    attn_s = _sample_attn(q_s, c_s, krp_s, cache_ckv, cache_krope, page_table, g_k_nope[0], w_uk[0], w_uv[0],
                          n_tok=n_tok, sbp=8)
    u_s3 = u_s.reshape(db, n_tok, POOL_DIM)
    pool_s = _pool_sample(u_s3.transpose(1, 0, 2), state_pool[0].transpose(1, 0, 2), ew['wpool'], ew['pscale'],
                          n_past)
    pool_s = pool_s.transpose(1, 0, 2).reshape(db * n_tok, POOL_DIM)

    mlp0 = (w_out_even[0], norm_ffn[0], w_up[0], w_down[0])
    yp = _mix_mlp(x_prompt.reshape(nb * seq, D_MODEL), pool_p.reshape(nb * seq, POOL_DIM), 0,
                  attn_p.reshape(nb * seq, ATTN_OUT_DIM), 0, *mlp0, tm=512)
    ys = _mix_mlp(xs, pool_s, 0, attn_s, 0, *mlp0, tm=256)

    ow = _odd_weights(norm_mix[1], w_in_rnn[0], conv_w[0], conv_b[0], w_gate_a[0], b_gate_a[0], w_gate_x[0],
                      b_gate_x[0], lru_lambda[0])
    rnn_p, ctail_p, hlast_p = _odd_prompt(yp.reshape(nb, seq, D_MODEL), ow, tm=256)
    ys_tm = ys.reshape(db, n_tok, D_MODEL).transpose(1, 0, 2)
    rnn_s, conv_s_tm, lru_s = _odd_sample(ys_tm, state_conv[0].transpose(1, 0, 2), state_lru[0], ow, n_past)

    mlp1 = (w_out_rnn[0], norm_ffn[1], w_up[1], w_down[1])
    rnn_p2 = rnn_p.reshape(nb * seq, RNN_DIM)
    yp = _mix_mlp(yp, rnn_p2, 0, rnn_p2, 1, *mlp1, tm=512)
    rnn_s2 = rnn_s.reshape(n_tok * db, RNN_DIM)
    ys_out = _mix_mlp(ys_tm.reshape(n_tok * db, D_MODEL), rnn_s2, 0, rnn_s2, 1, *mlp1, tm=256)
    ys_out = ys_out.reshape(n_tok, db, D_MODEL).transpose(1, 0, 2)

    pool_state_s = jnp.concatenate([state_pool[0], u_s3], axis=1)[:, -POOL_HIST:]
    return (yp.reshape(nb, seq, D_MODEL), ys_out,
            c_p[None], krp_p[None, :, :, rope_sl], utail_p[None, :, 1:], ctail_p[None, :, SUBLANES - CONV_WIDTH + 1:],
            hlast_p[None, :, 0],
            c_s.reshape(1, db, n_tok, KV_RANK), krp_s[:, rope_sl].reshape(1, db, n_tok, QK_ROPE_DIM),
            pool_state_s[None], conv_s_tm.transpose(1, 0, 2)[None], lru_s[None])
```

```python
import functools

import numpy as np
import jax
import jax.numpy as jnp
from jax import lax
from jax.experimental import pallas as pl
from jax.experimental.pallas import tpu as pltpu

D_MODEL = 1024
PAGE_SIZE = 128
POOL_WINDOWS = (2, 4, 8, 16)
POOL_GROUP_DIM = 128
POOL_DIM = len(POOL_WINDOWS) * POOL_GROUP_DIM
POOL_HIST = max(POOL_WINDOWS) - 1
N_HEADS = 8
QK_NOPE_DIM = 64
QK_ROPE_DIM = 32
QK_HEAD_DIM = QK_NOPE_DIM + QK_ROPE_DIM
V_HEAD_DIM = 64
KV_RANK = 256
Q_DIM = N_HEADS * QK_HEAD_DIM
ATTN_OUT_DIM = N_HEADS * V_HEAD_DIM
ROPE_BASE = 10000.0
SOFTMAX_SCALE = QK_HEAD_DIM ** -0.5
LOG2_E = 1.4426950408889634
RNN_DIM = D_MODEL
RNN_BLOCKS = 8
RNN_BLOCK_DIM = RNN_DIM // RNN_BLOCKS
CONV_WIDTH = 4
LRU_C = 8.0
D_FF = 4 * D_MODEL
NORM_EPS = 1e-6

LANES = 128
SUBLANES = 8
VMEM_LIMIT_BYTES = 56 * 2 ** 20

HEAD_PAD = LANES
QP_DIM = N_HEADS * HEAD_PAD
W1_DIM = POOL_DIM + QP_DIM + KV_RANK + HEAD_PAD
ROPE_LANE0 = QK_NOPE_DIM
HALF = QK_ROPE_DIM // 2

F32 = jnp.float32
BF16 = jnp.bfloat16


def _dot(a, b):
    return jnp.dot(a, b, preferred_element_type=F32)


def _dot_nt(a, b):
    return lax.dot_general(a, b, (((1,), (1,)), ((), ())), preferred_element_type=F32)


def _dot_tn(a, b):
    return lax.dot_general(a, b, (((0,), (0,)), ((), ())), preferred_element_type=F32)


def _rms(x, g):
    ms = jnp.mean(x * x, axis=-1, keepdims=True)
    return x * lax.rsqrt(ms + NORM_EPS) * g


SEG_DUP = LANES // 2


def _expand(rs, e):
    hi = rs.astype(BF16)
    lo = (rs - hi.astype(F32)).astype(BF16)
    lane = lax.broadcasted_iota(jnp.int32, rs.shape, 1)
    return _dot(jnp.where(lane < SEG_DUP, hi, lo), e)


def _group_roll(x, shift):
    rows, cols = x.shape
    return pltpu.roll(x.reshape(rows // SUBLANES, SUBLANES, cols), shift, 1).reshape(rows, cols)


def _gelu_tanh(x):
    return 0.5 * x * (1.0 + jnp.tanh(0.7978845608028654 * (x + 0.044715 * (x * x * x))))


def _sigmoid(x):
    return 0.5 * jnp.tanh(0.5 * x) + 0.5


def _sqrt_one_minus_exp2(x):
    t = jnp.tanh(x)
    return jnp.sqrt(-2.0 * t / (1.0 - t))


def _const_spec(shape, grid_rank):
    zeros = (0,) * len(shape)
    if grid_rank == 1:
        return pl.BlockSpec(shape, lambda i: zeros, pipeline_mode=pl.Buffered(1))
    return pl.BlockSpec(shape, lambda i, j: zeros, pipeline_mode=pl.Buffered(1))


def _params(*sem):
    return pltpu.CompilerParams(dimension_semantics=sem, vmem_limit_bytes=VMEM_LIMIT_BYTES)


def _even_in_kernel(*refs, tm, prompt):
    if prompt:
        (x_ref, tab_ref, gmix_ref, w1_ref, gq_ref, segq_ref, eq_ref, gckv_ref, gkr_ref,
         wuk_ref, segk_ref, ek_ref, gk_ref, wuv_ref, wpool_ref, pscale_ref,
         q_ref, c_ref, krp_ref, k_ref, v_ref, pool_ref, utail_ref, uext_ref) = refs
    else:
        (x_ref, tab_ref, gmix_ref, w1_ref, gq_ref, segq_ref, eq_ref, gckv_ref, gkr_ref,
         q_ref, c_ref, krp_ref, u_ref) = refs

    xn = _rms(x_ref[...], gmix_ref[...]).astype(BF16)
    z = _dot(xn, w1_ref[...])
    u = z[:, 0:POOL_DIM]
    qz = z[:, POOL_DIM:POOL_DIM + QP_DIM]
    cz = z[:, POOL_DIM + QP_DIM:POOL_DIM + QP_DIM + KV_RANK]
    krz = z[:, POOL_DIM + QP_DIM + KV_RANK:]

    ta = tab_ref[:, 0:LANES]
    tb = tab_ref[:, LANES:2 * LANES]
    tc = tab_ref[:, 2 * LANES:3 * LANES]

    def rope(blk):
        return blk * ta + pltpu.roll(blk, HALF, 1) * tb + pltpu.roll(blk, LANES - HALF, 1) * tc

    msq = _dot((qz * qz).astype(BF16), segq_ref[...])
    qn = qz * _expand(lax.rsqrt(msq + NORM_EPS), eq_ref[...]) * gq_ref[...]
    for h in range(N_HEADS):
        lanes = slice(h * HEAD_PAD, (h + 1) * HEAD_PAD)
        q_ref[:, lanes] = rope(qn[:, lanes]).astype(q_ref.dtype)

    c = _rms(cz, gckv_ref[...])
    c_ref[...] = c
    mskr = jnp.sum(krz * krz, axis=-1, keepdims=True) * (1.0 / QK_ROPE_DIM)
    krr = rope(krz * lax.rsqrt(mskr + NORM_EPS) * gkr_ref[...])
    krp_ref[...] = krr

    if not prompt:
        u_ref[...] = u
        return

    cb = c.astype(BF16)
    kn = _dot(cb, wuk_ref[...])
    msk = _dot((kn * kn).astype(BF16), segk_ref[...])
    knn = kn * _expand(lax.rsqrt(msk + NORM_EPS), ek_ref[...]) * gk_ref[...]
    for h in range(N_HEADS):
        lanes = slice(h * HEAD_PAD, (h + 1) * HEAD_PAD)
        k_ref[:, lanes] = (knn[:, lanes] + krr).astype(BF16)
    ln = lax.broadcasted_iota(jnp.int32, (1, QP_DIM), 1)
    one_lane = (ln & (HEAD_PAD - 1)) + ((ln // HEAD_PAD) & 1) * V_HEAD_DIM == V_HEAD_DIM
    v_ref[...] = (_dot(cb, wuv_ref[...]) + jnp.where(one_lane, 1.0, 0.0)).astype(BF16)

    i = pl.program_id(1)
    hal = 2 * SUBLANES

    @pl.when(i == 0)
    def _():
        uext_ref[0:hal, :] = jnp.zeros((hal, POOL_DIM), F32)

    uext_ref[hal:hal + tm, :] = u
    pos = lax.broadcasted_iota(jnp.int32, (tm, POOL_GROUP_DIM), 0) + i * tm
    for g, w in enumerate(POOL_WINDOWS):
        cols = slice(g * POOL_GROUP_DIM, (g + 1) * POOL_GROUP_DIM)
        acc = uext_ref[hal:hal + tm, cols]
        for k in range(1, w):
            acc = acc + uext_ref[hal - k:hal - k + tm, cols]
        cnt = jnp.minimum(pos + 1, w).astype(F32)
        d = acc / cnt - u[:, cols]
        y = _dot(d.astype(BF16), wpool_ref[g]) * pscale_ref[:, cols]
        pool_ref[:, cols] = y.astype(BF16)
    uext_ref[0:hal, :] = uext_ref[tm:tm + hal, :]

    @pl.when(i == pl.num_programs(1) - 1)
    def _():
        utail_ref[...] = uext_ref[0:hal, :]


def _seg_mats():
    segq = np.zeros((QP_DIM, LANES), np.float32)
    eq = np.zeros((LANES, QP_DIM), np.float32)
    segk = np.zeros((QP_DIM, LANES), np.float32)
    ek = np.zeros((LANES, QP_DIM), np.float32)
    for h in range(N_HEADS):
        b = h * HEAD_PAD
        for dup in (0, SEG_DUP):
            segq[b:b + QK_NOPE_DIM, dup + 2 * h] = 1.0 / QK_NOPE_DIM
            segq[b + QK_NOPE_DIM:b + QK_HEAD_DIM, dup + 2 * h + 1] = 1.0 / QK_ROPE_DIM
            eq[dup + 2 * h, b:b + QK_NOPE_DIM] = 1.0
            eq[dup + 2 * h + 1, b + QK_NOPE_DIM:b + QK_HEAD_DIM] = 1.0
            segk[b:b + QK_NOPE_DIM, dup + h] = 1.0 / QK_NOPE_DIM
            ek[dup + h, b:b + QK_NOPE_DIM] = 1.0
    return [jnp.asarray(m, BF16) for m in (segq, eq, segk, ek)]


def _rope_table(pos):
    inv = ROPE_BASE ** (-jnp.arange(HALF, dtype=F32) / HALF)
    ang = pos.astype(F32)[:, None] * inv[None, :]
    cos, sin = jnp.cos(ang), jnp.sin(ang)
    n = pos.shape[0]
    one = jnp.ones((n, ROPE_LANE0), F32)
    zero = jnp.zeros((n, ROPE_LANE0), F32)
    zh = jnp.zeros((n, HALF), F32)
    tail1 = jnp.ones((n, LANES - ROPE_LANE0 - QK_ROPE_DIM), F32)
    tail0 = jnp.zeros((n, LANES - ROPE_LANE0 - QK_ROPE_DIM), F32)
    ta = jnp.concatenate([one, cos, cos, tail1], axis=1)
    tb = jnp.concatenate([zero, zh, sin, tail0], axis=1)
    tc = jnp.concatenate([zero, -sin, zh, tail0], axis=1)
    return jnp.concatenate([ta, tb, tc], axis=1)


def _head_pad(w, lead):
    d = w.shape[-1]
    return jnp.pad(w, ((0, 0), (0, 0), (0, HEAD_PAD - d))).reshape(lead, QP_DIM)


def _head_row(parts):
    row = jnp.concatenate(parts)
    row = jnp.pad(row, (0, HEAD_PAD - row.shape[0]))
    return jnp.tile(row, N_HEADS)[None, :]


def _even_weights(norm_mix, w_in, g_q_nope, g_q_rope, g_ckv, g_k_rope, g_k_nope, w_uk, w_uv, w_pool, pool_scale):
    wq = _head_pad(w_in[:, POOL_DIM:POOL_DIM + Q_DIM].reshape(D_MODEL, N_HEADS, QK_HEAD_DIM), D_MODEL)
    wc = w_in[:, POOL_DIM + Q_DIM:POOL_DIM + Q_DIM + KV_RANK]
    wkr = jnp.pad(w_in[:, POOL_DIM + Q_DIM + KV_RANK:], ((0, 0), (ROPE_LANE0, HEAD_PAD - QK_HEAD_DIM)))
    w1 = jnp.concatenate([w_in[:, :POOL_DIM], wq, wc, wkr], axis=1).astype(BF16)
    gq = _head_row([g_q_nope, g_q_rope]) * (SOFTMAX_SCALE * LOG2_E)
    gkr = _head_row([jnp.zeros((ROPE_LANE0,), F32), g_k_rope])[:, :HEAD_PAD]
    gk = _head_row([g_k_nope])
    wuk = _head_pad(w_uk, KV_RANK).astype(BF16)
    v_even = jnp.pad(w_uv, ((0, 0), (0, 0), (0, HEAD_PAD - V_HEAD_DIM)))
    v_odd = jnp.pad(w_uv, ((0, 0), (0, 0), (HEAD_PAD - V_HEAD_DIM, 0)))
    odd = (jnp.arange(N_HEADS) % 2 == 1)[None, :, None]
    wuv = jnp.where(odd, v_odd, v_even).reshape(KV_RANK, QP_DIM).astype(BF16)
    return dict(gmix=norm_mix[None, :], w1=w1, gq=gq, gckv=g_ckv[None, :], gkr=gkr, gk=gk, wuk=wuk, wuv=wuv,
                wpool=w_pool.astype(BF16), pscale=pool_scale[None, :])


def _even_in_prompt(x, tab, ew, tm):
    nb, seq, _ = x.shape
    nt = seq // tm
    segq, eq, segk, ek = _seg_mats()
    row = lambda width: pl.BlockSpec((None, tm, width), lambda b, i: (b, i, 0))
    cs = lambda a: _const_spec(a.shape, 2)
    consts = [ew['gmix'], ew['w1'], ew['gq'], segq, eq, ew['gckv'], ew['gkr'],
              ew['wuk'], segk, ek, ew['gk'], ew['wuv'], ew['wpool'], ew['pscale']]
    out_shape = (
        jax.ShapeDtypeStruct((nb, seq, QP_DIM), BF16),
        jax.ShapeDtypeStruct((nb, seq, KV_RANK), F32),
        jax.ShapeDtypeStruct((nb, seq, HEAD_PAD), F32),
        jax.ShapeDtypeStruct((nb, seq, QP_DIM), BF16),
        jax.ShapeDtypeStruct((nb, seq, QP_DIM), BF16),
        jax.ShapeDtypeStruct((nb, seq, POOL_DIM), BF16),
        jax.ShapeDtypeStruct((nb, 2 * SUBLANES, POOL_DIM), F32),
    )
    out_specs = (row(QP_DIM), row(KV_RANK), row(HEAD_PAD), row(QP_DIM), row(QP_DIM), row(POOL_DIM),
                 pl.BlockSpec((None, 2 * SUBLANES, POOL_DIM), lambda b, i: (b, 0, 0)))
    return pl.pallas_call(
        functools.partial(_even_in_kernel, tm=tm, prompt=True),
        grid=(nb, nt),
        in_specs=[row(D_MODEL), pl.BlockSpec((tm, 3 * LANES), lambda b, i: (i, 0))] + [cs(a) for a in consts],
        out_specs=out_specs,
        out_shape=out_shape,
        scratch_shapes=[pltpu.VMEM((2 * SUBLANES + tm, POOL_DIM), F32)],
        compiler_params=_params("arbitrary", "arbitrary"),
        name="even_in_prompt",
    )(x, tab, *consts)


def _even_in_sample(x, tab, ew, tm):
    rows = x.shape[0]
    segq, eq, _, _ = _seg_mats()
    row = lambda width: pl.BlockSpec((tm, width), lambda i: (i, 0))
    cs = lambda a: _const_spec(a.shape, 1)
    consts = [ew['gmix'], ew['w1'], ew['gq'], segq, eq, ew['gckv'], ew['gkr']]
    out_shape = (
        jax.ShapeDtypeStruct((rows, QP_DIM), F32),
        jax.ShapeDtypeStruct((rows, KV_RANK), F32),
        jax.ShapeDtypeStruct((rows, HEAD_PAD), F32),
        jax.ShapeDtypeStruct((rows, POOL_DIM), F32),
    )
    return pl.pallas_call(
        functools.partial(_even_in_kernel, tm=tm, prompt=False),
        grid=(rows // tm,),
        in_specs=[row(D_MODEL), row(3 * LANES)] + [cs(a) for a in consts],
        out_specs=(row(QP_DIM), row(KV_RANK), row(HEAD_PAD), row(POOL_DIM)),
        out_shape=out_shape,
        compiler_params=_params("arbitrary"),
        name="even_in_sample",
    )(x, tab, *consts)


def _prompt_attn_kernel(q_ref, k_ref, v_ref, o_ref, *, tq, tk):
    qi = pl.program_id(1)
    nsub = tq // tk
    causal = {n: lax.broadcasted_iota(jnp.int32, (n, tk), 0) >= lax.broadcasted_iota(jnp.int32, (n, tk), 1)
              for n in range(tk, tq + 1, tk)}
    lane = lax.broadcasted_iota(jnp.int32, (tq, HEAD_PAD), 1)
    for j in range(N_HEADS // 2):
        heads = (2 * j, 2 * j + 1)
        lanes = [slice(h * HEAD_PAD, (h + 1) * HEAD_PAD) for h in heads]
        qs = [q_ref[:, ln] for ln in lanes]

        def tile(kt, carry, r0, lanes=lanes, qs=qs):
            off = pl.multiple_of(kt * tk, tk)
            lo = 0 if r0 is None else r0
            new = []
            for (m, acc), ln, qh in zip(carry, lanes, qs):
                s = _dot_nt(qh[lo:, :], k_ref[pl.ds(off, tk), ln])
                if r0 is not None:
                    s = jnp.where(causal[tq - lo], s, -jnp.inf)
                m_new = jnp.maximum(m[lo:, :], jnp.max(s, axis=-1, keepdims=True))
                p = jnp.exp2(s - m_new)
                acc_new = (jnp.exp2(m[lo:, :] - m_new) * acc[lo:, :]
                           + _dot(p.astype(BF16), v_ref[pl.ds(off, tk), ln]))
                if lo:
                    m_new = jnp.concatenate([m[:lo, :], m_new], axis=0)
                    acc_new = jnp.concatenate([acc[:lo, :], acc_new], axis=0)
                new.append((m_new, acc_new))
            return tuple(new)

        def below(jj, carry):
            for u in range(nsub):
                carry = tile(jj * nsub + u, carry, None)
            return carry

        carry = tuple((jnp.full((tq, 1), -jnp.inf, F32), jnp.zeros((tq, HEAD_PAD), F32)) for _ in heads)
        carry = lax.fori_loop(0, qi, below, carry)
        for u in range(nsub):
            carry = tile(qi * nsub + u, carry, u * tk)
        (_, acc_e), (_, acc_o) = carry
        out_e = jnp.where(lane < V_HEAD_DIM, acc_e / acc_e[:, V_HEAD_DIM:V_HEAD_DIM + 1], 0.0)
        out_o = jnp.where(lane >= V_HEAD_DIM, acc_o / acc_o[:, 0:1], 0.0)
        o_ref[:, j * LANES:(j + 1) * LANES] = (out_e + out_o).astype(BF16)


def _prompt_attn(q, k, v, tq, tk):
    nb, seq, _ = q.shape
    assert tq % tk == 0 and seq % tq == 0
    full = pl.BlockSpec((None, seq, QP_DIM), lambda b, i: (b, 0, 0), pipeline_mode=pl.Buffered(1))
    return pl.pallas_call(
        functools.partial(_prompt_attn_kernel, tq=tq, tk=tk),
        grid=(nb, seq // tq),
        in_specs=[pl.BlockSpec((None, tq, QP_DIM), lambda b, i: (b, i, 0)), full, full],
        out_specs=pl.BlockSpec((None, tq, ATTN_OUT_DIM), lambda b, i: (b, i, 0)),
        out_shape=jax.ShapeDtypeStruct((nb, seq, ATTN_OUT_DIM), BF16),
        compiler_params=_params("arbitrary", "arbitrary"),
        name="prompt_attn",
    )(q, k, v)


def _sample_attn_kernel(pt_ref, q_ref, cn_ref, krn_ref, gk_ref, wukp_ref, wukd_ref, wuv_ref, segt_ref,
                        ckv_hbm, kr_hbm, o_ref, cbuf, krbuf, sem, qabs_ref, qr_ref, s_ref, cb_ref, acc_ref,
                        *, n_tok, n_pages, sbp):
    b = pl.program_id(0)
    slot = lax.rem(b, 2)
    sb = sbp * PAGE_SIZE
    n_sb = n_pages // sbp
    ncol = N_HEADS * n_tok

    def start_pages(row, i0, sl):
        for k in range(sbp):
            page = pt_ref[row * n_pages + i0 + k]
            pltpu.make_async_copy(ckv_hbm.at[0, page], cbuf.at[sl, i0 + k], sem.at[0, sl]).start()
            pltpu.make_async_copy(kr_hbm.at[0, page], krbuf.at[sl, i0 + k], sem.at[1, sl]).start()

    @pl.when(b == 0)
    def _():
        def first(i, carry):
            start_pages(0, i * sbp, 0)
            return carry
        lax.fori_loop(0, n_sb, first, 0)

    q = q_ref[...]
    qt = jnp.concatenate([q] * N_HEADS, axis=0)
    r = lax.broadcasted_iota(jnp.int32, (ncol, QP_DIM), 0)
    ln = lax.broadcasted_iota(jnp.int32, (ncol, QP_DIM), 1)
    keep = jnp.where((ln & (HEAD_PAD - 1)) < QK_NOPE_DIM, r // n_tok, -1) == ln // HEAD_PAD
    qg = jnp.where(keep, qt * gk_ref[...], 0.0).astype(BF16)
    qabs_ref[...] = _dot_nt(qg, wukp_ref[...]).astype(BF16)
    lane = lax.broadcasted_iota(jnp.int32, (n_tok, LANES), 1)
    blocks = [jnp.where(lane < QK_ROPE_DIM,
                        pltpu.roll(q[:, h * HEAD_PAD:(h + 1) * HEAD_PAD], LANES - ROPE_LANE0, 1), 0.0)
              for h in range(N_HEADS)]
    qr_ref[...] = jnp.concatenate(blocks, axis=0).astype(BF16)

    def sumsq(cb):
        k2 = _dot(cb, wukd_ref[...])
        k2 = k2 * k2
        return (k2[:, 0:LANES] + k2[:, LANES:2 * LANES]) + (k2[:, 2 * LANES:3 * LANES] + k2[:, 3 * LANES:])

    def scores(cb, krt):
        ss = _dot_nt(segt_ref[...], sumsq(cb).astype(BF16))
        sp = _dot_nt(qabs_ref[...], cb)
        rope = _dot(qr_ref[:, 0:QK_ROPE_DIM], krt.astype(BF16))
        return sp * lax.rsqrt(ss + NORM_EPS) + rope

    def lane_tiles(x, op):
        out = x[:, 0:LANES]
        for j in range(1, x.shape[1] // LANES):
            out = op(out, x[:, j * LANES:(j + 1) * LANES])
        return out

    pltpu.make_async_copy(ckv_hbm.at[0, pl.ds(0, n_pages)], cbuf.at[slot], sem.at[0, slot]).wait()
    pltpu.make_async_copy(kr_hbm.at[0, pl.ds(0, n_pages)], krbuf.at[slot], sem.at[1, slot]).wait()

    nxt = jnp.minimum(b + 1, pl.num_programs(0) - 1)

    def score_block(i, m):
        start_pages(nxt, i * sbp, 1 - slot)
        off = pl.multiple_of(i * sb, sb)
        cb = cbuf[slot, pl.ds(i * sbp, sbp)].reshape(sb, KV_RANK).astype(BF16)
        cb_ref[pl.ds(off, sb), :] = cb
        krt = jnp.concatenate([krbuf[slot, i * sbp + k] for k in range(sbp)], axis=1)
        s = scores(cb, krt)
        s_ref[i] = s
        return jnp.maximum(m, lane_tiles(s, jnp.maximum))

    m = lax.fori_loop(0, n_sb, score_block, jnp.full((ncol, LANES), -jnp.inf, F32), unroll=4)

    cn = jnp.concatenate([cn_ref[...], jnp.zeros((PAGE_SIZE - n_tok, KV_RANK), F32)], axis=0).astype(BF16)
    krn = jnp.concatenate([pltpu.roll(krn_ref[...], LANES - ROPE_LANE0, 1),
                           jnp.zeros((PAGE_SIZE - n_tok, LANES), F32)], axis=0)
    s = scores(cn, krn.T[0:QK_ROPE_DIM, :])
    key = lax.broadcasted_iota(jnp.int32, s.shape, 1)
    tq = lax.broadcasted_iota(jnp.int32, s.shape, 0) % n_tok
    s = jnp.where(key <= tq, s, -jnp.inf)
    m = jnp.max(jnp.maximum(m, s), axis=1, keepdims=True)

    p = jnp.exp2(s - m)
    acc_ref[...] = _dot(p.astype(BF16), cn)

    def value_block(i, lp):
        off = pl.multiple_of(i * sb, sb)
        p = jnp.exp2(s_ref[i] - m)
        acc_ref[...] += _dot(p.astype(BF16), cb_ref[pl.ds(off, sb), :])
        return lp + lane_tiles(p, jnp.add)

    l = jnp.sum(lax.fori_loop(0, n_sb, value_block, p, unroll=True), axis=1, keepdims=True)

    lat = acc_ref[...] * (1.0 / l)
    zz = _dot(lat.astype(BF16), wuv_ref[...])
    lane_h = lax.broadcasted_iota(jnp.int32, (n_tok, ATTN_OUT_DIM), 1) // V_HEAD_DIM
    out = jnp.zeros((n_tok, ATTN_OUT_DIM), F32)
    for h in range(N_HEADS):
        out = out + jnp.where(lane_h == h, zz[h * n_tok:(h + 1) * n_tok, :], 0.0)
    o_ref[...] = out

    @pl.when(b == pl.num_programs(0) - 1)
    def _():
        pltpu.make_async_copy(ckv_hbm.at[0, pl.ds(0, n_pages)], cbuf.at[1 - slot], sem.at[0, 1 - slot]).wait()
        pltpu.make_async_copy(kr_hbm.at[0, pl.ds(0, n_pages)], krbuf.at[1 - slot], sem.at[1, 1 - slot]).wait()


def _sample_attn(q, c, krp, cache_ckv, cache_krope, page_table, g_k_nope, w_uk, w_uv, n_tok, sbp):
    nb, n_pages = page_table.shape
    ncol = N_HEADS * n_tok
    gk = _head_row([g_k_nope])
    wukp = _head_pad(w_uk, KV_RANK).astype(BF16)
    wukd = w_uk.transpose(0, 2, 1).reshape(KV_RANK, N_HEADS * QK_NOPE_DIM).astype(BF16)
    wuv = w_uv.reshape(KV_RANK, ATTN_OUT_DIM).astype(BF16)
    segt = np.zeros((ncol, LANES), np.float32)
    for col in range(ncol):
        segt[col, (np.arange(LANES) % N_HEADS) == col // n_tok] = 1.0 / QK_NOPE_DIM
    segt = jnp.asarray(segt, BF16)
    pt = page_table.reshape(-1)
    kr_t = jnp.swapaxes(cache_krope, 2, 3)

    rows = lambda width: pl.BlockSpec((n_tok, width), lambda b, pt_ref: (b, 0))
    const = lambda a: pl.BlockSpec(a.shape, lambda b, pt_ref: (0,) * a.ndim)
    hbm = pl.BlockSpec(memory_space=pl.ANY)
    n_keys = n_pages * PAGE_SIZE
    grid_spec = pltpu.PrefetchScalarGridSpec(
        num_scalar_prefetch=1,
        grid=(nb,),
        in_specs=[rows(QP_DIM), rows(KV_RANK), rows(HEAD_PAD), const(gk), const(wukp), const(wukd), const(wuv),
                  const(segt), hbm, hbm],
        out_specs=rows(ATTN_OUT_DIM),
        scratch_shapes=[pltpu.VMEM((2, n_pages, PAGE_SIZE, KV_RANK), F32),
                        pltpu.VMEM((2, n_pages, QK_ROPE_DIM, PAGE_SIZE), F32),
                        pltpu.SemaphoreType.DMA((2, 2)),
                        pltpu.VMEM((ncol, KV_RANK), BF16), pltpu.VMEM((ncol, LANES), BF16),
                        pltpu.VMEM((n_pages // sbp, ncol, sbp * PAGE_SIZE), F32),
                        pltpu.VMEM((n_keys, KV_RANK), BF16),
                        pltpu.VMEM((ncol, KV_RANK), F32)],
    )
    return pl.pallas_call(
        functools.partial(_sample_attn_kernel, n_tok=n_tok, n_pages=n_pages, sbp=sbp),
        grid_spec=grid_spec,
        out_shape=jax.ShapeDtypeStruct((nb * n_tok, ATTN_OUT_DIM), F32),
        compiler_params=_params("arbitrary"),
        name="sample_attn",
    )(pt, q, c, krp, gk, wukp, wukd, wuv, segt, cache_ckv, kr_t)


def _pool_sample_kernel(u_ref, hist_ref, wpool_ref, pscale_ref, o_ref, *, n_tok, n_past):
    ext = [hist_ref[k] for k in range(POOL_HIST)] + [u_ref[t] for t in range(n_tok)]
    nb = u_ref.shape[1]
    for g, w in enumerate(POOL_WINDOWS):
        cols = slice(g * POOL_GROUP_DIM, (g + 1) * POOL_GROUP_DIM)
        ds = []
        for t in range(n_tok):
            acc = ext[POOL_HIST + t][:, cols]
            for k in range(1, w):
                acc = acc + ext[POOL_HIST + t - k][:, cols]
            ds.append(acc / float(min(n_past + t + 1, w)) - ext[POOL_HIST + t][:, cols])
        d = jnp.concatenate(ds, axis=0).astype(BF16)
        y = (_dot(d, wpool_ref[g]) * pscale_ref[:, cols]).astype(BF16)
        for t in range(n_tok):
            o_ref[t, :, cols] = y[t * nb:(t + 1) * nb, :]


def _pool_sample(u_tm, hist_tm, w_pool, pscale, n_past):
    n_tok, nb, _ = u_tm.shape
    return pl.pallas_call(
        functools.partial(_pool_sample_kernel, n_tok=n_tok, n_past=n_past),
        out_shape=jax.ShapeDtypeStruct((n_tok, nb, POOL_DIM), BF16),
        compiler_params=pltpu.CompilerParams(vmem_limit_bytes=VMEM_LIMIT_BYTES),
        name="pool_sample",
    )(u_tm, hist_tm, w_pool, pscale)


def _mix_mlp_kernel(x_ref, a1_ref, a2_ref, wo1_ref, wo2_ref, gffn_ref, wup_ref, wdn_ref, y_ref, *, ck):
    y_ref[...] = (x_ref[...] + _dot(a1_ref[...].astype(BF16), wo1_ref[...])
                  + _dot(a2_ref[...].astype(BF16), wo2_ref[...]))
    xn = _rms(y_ref[...], gffn_ref[...]).astype(BF16)
    for c in range(D_FF // ck):
        h = jnp.maximum(_dot(xn, wup_ref[:, c * ck:(c + 1) * ck]), 0.0)
        y_ref[...] += _dot((h * h).astype(BF16), wdn_ref[c * ck:(c + 1) * ck, :])


def _mix_mlp(x, a1, a1_col, a2, a2_col, w_out, g_ffn, w_up, w_down, tm, ck=1024):
    rows = x.shape[0]
    half = D_MODEL // 2
    wo = w_out.astype(BF16)
    consts = [wo[:half], wo[half:], g_ffn[None, :], w_up.astype(BF16), w_down.astype(BF16)]
    row = pl.BlockSpec((tm, D_MODEL), lambda i: (i, 0))
    return pl.pallas_call(
        functools.partial(_mix_mlp_kernel, ck=ck),
        grid=(rows // tm,),
        in_specs=[row, pl.BlockSpec((tm, half), lambda i: (i, a1_col)), pl.BlockSpec((tm, half), lambda i: (i, a2_col))]
                 + [_const_spec(a.shape, 1) for a in consts],
        out_specs=row,
        out_shape=jax.ShapeDtypeStruct((rows, D_MODEL), F32),
        compiler_params=_params("arbitrary"),
        name="mix_mlp",
    )(x, a1, a2, *consts)


def _gates(v, wg_ref, bga_ref, bgx_ref, lam_ref):
    vb = v.astype(BF16)
    pair = 2 * RNN_BLOCK_DIM
    ga, gx = [], []
    for p in range(RNN_BLOCKS // 2):
        g = _dot(vb[:, p * pair:(p + 1) * pair], wg_ref[p])
        ga.append(g[:, :pair])
        gx.append(g[:, pair:])
    r = _sigmoid(jnp.concatenate(ga, axis=1) + bga_ref[...])
    ig = _sigmoid(jnp.concatenate(gx, axis=1) + bgx_ref[...])
    nl = -lam_ref[...]
    softplus = jnp.maximum(nl, 0.0) + jnp.log1p(jnp.exp(-jnp.abs(nl)))
    log_a = (-LRU_C) * r * softplus
    return log_a, ig


def _odd_prompt_kernel(x_ref, gmix_ref, win_ref, cw_ref, cb_ref, wg_ref, bga_ref, bgx_ref, lam_ref,
                       y_ref, ctail_ref, hlast_ref, uext_ref, a_ref, b_ref, hcar_ref, *, tm):
    i = pl.program_id(1)
    sub = SUBLANES

    @pl.when(i == 0)
    def _():
        uext_ref[...] = jnp.zeros((sub, RNN_DIM), F32)
        hcar_ref[...] = jnp.zeros((sub, RNN_DIM), F32)

    xn = _rms(x_ref[...], gmix_ref[...]).astype(BF16)
    z = _dot(xn, win_ref[...])
    gate = z[:, :RNN_DIM]
    u = z[:, RNN_DIM:]
    rowmod = lax.broadcasted_iota(jnp.int32, (tm, RNN_DIM), 0) & (sub - 1)
    v = cb_ref[...]
    for k in range(CONV_WIDTH):
        back = CONV_WIDTH - 1 - k
        if back == 0:
            uk = u
        else:
            rot = _group_roll(u, back)
            prev = jnp.concatenate([pltpu.roll(uext_ref[...], back, 0), rot[:tm - sub, :]], axis=0)
            uk = jnp.where(rowmod < back, prev, rot)
        v = v + uk * cw_ref[k:k + 1, :]
    uext_ref[...] = u[tm - sub:, :]

    log_a, ig = _gates(v, wg_ref, bga_ref, bgx_ref, lam_ref)
    a = jnp.exp(log_a)
    row = lax.broadcasted_iota(jnp.int32, (tm, RNN_DIM), 0)
    mult = jnp.where(row + i * tm == 0, 1.0, _sqrt_one_minus_exp2(log_a))
    b = mult * ig * v

    for s in (1, 2, 4):
        ok = rowmod >= s
        b = jnp.where(ok, a * _group_roll(b, s), 0.0) + b
        a = jnp.where(ok, a * _group_roll(a, s), a)
    a_ref[...] = a
    b_ref[...] = b

    def group(g, hb):
        off = pl.multiple_of(g * sub, sub)
        hg = a_ref[pl.ds(off, sub), :] * hb + b_ref[pl.ds(off, sub), :]
        b_ref[pl.ds(off, sub), :] = hg
        return jnp.broadcast_to(hg[sub - 1:sub, :], (sub, RNN_DIM))

    hb = lax.fori_loop(0, tm // sub, group, hcar_ref[...])
    hcar_ref[...] = hb
    y_ref[...] = (_gelu_tanh(gate) * b_ref[...]).astype(BF16)

    @pl.when(i == pl.num_programs(1) - 1)
    def _():
        ctail_ref[...] = uext_ref[...]
        hlast_ref[...] = hb


def _odd_weights(norm_mix, w_in, conv_w, conv_b, w_ga, b_ga, w_gx, b_gx, lam):
    def pairs(w):
        z = jnp.zeros((RNN_BLOCKS // 2, RNN_BLOCK_DIM, RNN_BLOCK_DIM), F32)
        top = jnp.concatenate([w[0::2], z], axis=2)
        bot = jnp.concatenate([z, w[1::2]], axis=2)
        return jnp.concatenate([top, bot], axis=1)
    wg = jnp.concatenate([pairs(w_ga), pairs(w_gx)], axis=2).astype(BF16)
    return [norm_mix[None, :], w_in.astype(BF16), conv_w, conv_b[None, :], wg, b_ga[None, :], b_gx[None, :],
            lam[None, :]]


def _odd_prompt(x, ow, tm):
    nb, seq, _ = x.shape
    row = pl.BlockSpec((None, tm, D_MODEL), lambda b, i: (b, i, 0))
    tail = pl.BlockSpec((None, SUBLANES, RNN_DIM), lambda b, i: (b, 0, 0))
    return pl.pallas_call(
        functools.partial(_odd_prompt_kernel, tm=tm),
        grid=(nb, seq // tm),
        in_specs=[row] + [_const_spec(a.shape, 2) for a in ow],
        out_specs=(row, tail, tail),
        out_shape=(jax.ShapeDtypeStruct((nb, seq, RNN_DIM), BF16),
                   jax.ShapeDtypeStruct((nb, SUBLANES, RNN_DIM), F32),
                   jax.ShapeDtypeStruct((nb, SUBLANES, RNN_DIM), F32)),
        scratch_shapes=[pltpu.VMEM((SUBLANES, RNN_DIM), F32), pltpu.VMEM((tm, RNN_DIM), F32),
                        pltpu.VMEM((tm, RNN_DIM), F32), pltpu.VMEM((SUBLANES, RNN_DIM), F32)],
        compiler_params=_params("arbitrary", "arbitrary"),
        name="odd_prompt",
    )(x, *ow)


def _odd_sample_kernel(x_ref, ch_ref, h0_ref, gmix_ref, win_ref, cw_ref, cb_ref, wg_ref, bga_ref, bgx_ref, lam_ref,
                       y_ref, ctail_ref, hlast_ref, *, n_tok, n_past):
    nb = x_ref.shape[1]
    x = x_ref[...].reshape(n_tok * nb, D_MODEL)
    z = _dot(_rms(x, gmix_ref[...]).astype(BF16), win_ref[...])
    gate = z[:, :RNN_DIM]
    ext = [ch_ref[k] for k in range(CONV_WIDTH - 1)] + [z[t * nb:(t + 1) * nb, RNN_DIM:] for t in range(n_tok)]
    vs = []
    for t in range(n_tok):
        v = cb_ref[...]
        for k in range(CONV_WIDTH):
            v = v + ext[t + k] * cw_ref[k:k + 1, :]
        vs.append(v)
    v = jnp.concatenate(vs, axis=0)
    log_a, ig = _gates(v, wg_ref, bga_ref, bgx_ref, lam_ref)
    a = jnp.exp(log_a)
    mult = _sqrt_one_minus_exp2(log_a)
    if n_past == 0:
        first = lax.broadcasted_iota(jnp.int32, mult.shape, 0) < nb
        mult = jnp.where(first, 1.0, mult)
    b = mult * ig * v
    h = h0_ref[...]
    hs = []
    for t in range(n_tok):
        h = a[t * nb:(t + 1) * nb, :] * h + b[t * nb:(t + 1) * nb, :]
        hs.append(h)
    y = (_gelu_tanh(gate) * jnp.concatenate(hs, axis=0)).astype(BF16)
    y_ref[...] = y.reshape(n_tok, nb, RNN_DIM)
    for k in range(CONV_WIDTH - 1):
        ctail_ref[k] = ext[n_tok + k]
    hlast_ref[...] = h


def _odd_sample(x_tm, ch_tm, h0, ow, n_past, bb=32):
    n_tok, nb, _ = x_tm.shape
    blk = lambda t, w: pl.BlockSpec((t, bb, w), lambda i: (0, i, 0))
    return pl.pallas_call(
        functools.partial(_odd_sample_kernel, n_tok=n_tok, n_past=n_past),
        grid=(nb // bb,),
        in_specs=[blk(n_tok, D_MODEL), blk(CONV_WIDTH - 1, RNN_DIM), pl.BlockSpec((bb, RNN_DIM), lambda i: (i, 0))]
                 + [_const_spec(a.shape, 1) for a in ow],
        out_specs=(blk(n_tok, RNN_DIM), blk(CONV_WIDTH - 1, RNN_DIM), pl.BlockSpec((bb, RNN_DIM), lambda i: (i, 0))),
        out_shape=(jax.ShapeDtypeStruct((n_tok, nb, RNN_DIM), BF16),
                   jax.ShapeDtypeStruct((CONV_WIDTH - 1, nb, RNN_DIM), F32),
                   jax.ShapeDtypeStruct((nb, RNN_DIM), F32)),
        compiler_params=_params("arbitrary"),
        name="odd_sample",
    )(x_tm, ch_tm, h0, *ow)


def kernel(x_prompt, x_sample, cache_ckv, cache_krope, state_pool, state_conv, state_lru, page_table, norm_mix,
           w_in_even, g_q_nope, g_q_rope, g_ckv, g_k_rope, g_k_nope, w_uk, w_uv, w_pool, pool_scale, w_out_even,
           w_in_rnn, conv_w, conv_b, w_gate_a, b_gate_a, w_gate_x, b_gate_x, lru_lambda, w_out_rnn, norm_ffn,
           w_up, w_down):
    nb, seq, _ = x_prompt.shape
    db, n_tok, _ = x_sample.shape
    n_past = page_table.shape[1] * PAGE_SIZE
    depth = norm_mix.shape[0]
    assert depth == 2 and cache_ckv.shape[0] == 1, "one even (pool + MLA) layer followed by one odd (RG-LRU) layer"
    rope_sl = slice(ROPE_LANE0, ROPE_LANE0 + QK_ROPE_DIM)

    ew = _even_weights(norm_mix[0], w_in_even[0], g_q_nope[0], g_q_rope[0], g_ckv[0], g_k_rope[0], g_k_nope[0],
                       w_uk[0], w_uv[0], w_pool[0], pool_scale[0])
    tab_p = _rope_table(jnp.arange(seq, dtype=jnp.int32))
    tab_s = jnp.tile(_rope_table(n_past + jnp.arange(n_tok, dtype=jnp.int32)), (db, 1))

    q_p, c_p, krp_p, k_p, v_p, pool_p, utail_p = _even_in_prompt(x_prompt, tab_p, ew, tm=512)
    attn_p = _prompt_attn(q_p, k_p, v_p, tq=1024, tk=512)
    xs = x_sample.reshape(db * n_tok, D_MODEL)
    q_s, c_s, krp_s, u_s = _even_in_sample(xs, tab_s, ew, tm=256)
    attn_s = _sample_attn(q_s, c_s, krp_s, cache_ckv, cache_krope, page_table, g_k_nope[0], w_uk[0], w_uv[0],
                          n_tok=n_tok, sbp=8)
    u_s3 = u_s.reshape(db, n_tok, POOL_DIM)
    pool_s = _pool_sample(u_s3.transpose(1, 0, 2), state_pool[0].transpose(1, 0, 2), ew['wpool'], ew['pscale'],
                          n_past)
    pool_s = pool_s.transpose(1, 0, 2).reshape(db * n_tok, POOL_DIM)

    mlp0 = (w_out_even[0], norm_ffn[0], w_up[0], w_down[0])
    yp = _mix_mlp(x_prompt.reshape(nb * seq, D_MODEL), pool_p.reshape(nb * seq, POOL_DIM), 0,
                  attn_p.reshape(nb * seq, ATTN_OUT_DIM), 0, *mlp0, tm=512)
    ys = _mix_mlp(xs, pool_s, 0, attn_s, 0, *mlp0, tm=256)

    ow = _odd_weights(norm_mix[1], w_in_rnn[0], conv_w[0], conv_b[0], w_gate_a[0], b_gate_a[0], w_gate_x[0],
                      b_gate_x[0], lru_lambda[0])
    rnn_p, ctail_p, hlast_p = _odd_prompt(yp.reshape(nb, seq, D_MODEL), ow, tm=256)
    ys_tm = ys.reshape(db, n_tok, D_MODEL).transpose(1, 0, 2)
    rnn_s, conv_s_tm, lru_s = _odd_sample(ys_tm, state_conv[0].transpose(1, 0, 2), state_lru[0], ow, n_past)

    mlp1 = (w_out_rnn[0], norm_ffn[1], w_up[1], w_down[1])
    rnn_p2 = rnn_p.reshape(nb * seq, RNN_DIM)
    yp = _mix_mlp(yp, rnn_p2, 0, rnn_p2, 1, *mlp1, tm=512)
    rnn_s2 = rnn_s.reshape(n_tok * db, RNN_DIM)
    ys_out = _mix_mlp(ys_tm.reshape(n_tok * db, D_MODEL), rnn_s2, 0, rnn_s2, 1, *mlp1, tm=256)
    ys_out = ys_out.reshape(n_tok, db, D_MODEL).transpose(1, 0, 2)

    pool_state_s = jnp.concatenate([state_pool[0], u_s3], axis=1)[:, -POOL_HIST:]
    return (yp.reshape(nb, seq, D_MODEL), ys_out,
            c_p[None], krp_p[None, :, :, rope_sl], utail_p[None, :, 1:], ctail_p[None, :, SUBLANES - CONV_WIDTH + 1:],
            hlast_p[None, :, 0],
            c_s.reshape(1, db, n_tok, KV_RANK), krp_s[:, rope_sl].reshape(1, db, n_tok, QK_ROPE_DIM),
            pool_state_s[None], conv_s_tm.transpose(1, 0, 2)[None], lru_s[None])
```

```python
import functools

import numpy as np
import jax
import jax.numpy as jnp
from jax import lax
from jax.experimental import pallas as pl
from jax.experimental.pallas import tpu as pltpu

D_MODEL = 1024
PAGE_SIZE = 128
POOL_WINDOWS = (2, 4, 8, 16)
POOL_GROUP_DIM = 128
POOL_DIM = len(POOL_WINDOWS) * POOL_GROUP_DIM
POOL_HIST = max(POOL_WINDOWS) - 1
N_HEADS = 8
QK_NOPE_DIM = 64
QK_ROPE_DIM = 32
QK_HEAD_DIM = QK_NOPE_DIM + QK_ROPE_DIM
V_HEAD_DIM = 64
KV_RANK = 256
Q_DIM = N_HEADS * QK_HEAD_DIM
ATTN_OUT_DIM = N_HEADS * V_HEAD_DIM
ROPE_BASE = 10000.0
SOFTMAX_SCALE = QK_HEAD_DIM ** -0.5
LOG2_E = 1.4426950408889634
RNN_DIM = D_MODEL
RNN_BLOCKS = 8
RNN_BLOCK_DIM = RNN_DIM // RNN_BLOCKS
CONV_WIDTH = 4
LRU_C = 8.0
D_FF = 4 * D_MODEL
NORM_EPS = 1e-6

LANES = 128
SUBLANES = 8
VMEM_LIMIT_BYTES = 56 * 2 ** 20

HEAD_PAD = LANES
QP_DIM = N_HEADS * HEAD_PAD
W1_DIM = POOL_DIM + QP_DIM + KV_RANK + HEAD_PAD
ROPE_LANE0 = QK_NOPE_DIM
HALF = QK_ROPE_DIM // 2

F32 = jnp.float32
BF16 = jnp.bfloat16


def _dot(a, b):
    return jnp.dot(a, b, preferred_element_type=F32)


def _dot_nt(a, b):
    return lax.dot_general(a, b, (((1,), (1,)), ((), ())), preferred_element_type=F32)


def _dot_tn(a, b):
    return lax.dot_general(a, b, (((0,), (0,)), ((), ())), preferred_element_type=F32)


def _rms(x, g):
    ms = jnp.mean(x * x, axis=-1, keepdims=True)
    return x * lax.rsqrt(ms + NORM_EPS) * g


SEG_DUP = LANES // 2


def _expand(rs, e):
    hi = rs.astype(BF16)
    lo = (rs - hi.astype(F32)).astype(BF16)
    lane = lax.broadcasted_iota(jnp.int32, rs.shape, 1)
    return _dot(jnp.where(lane < SEG_DUP, hi, lo), e)


def _group_roll(x, shift):
    rows, cols = x.shape
    return pltpu.roll(x.reshape(rows // SUBLANES, SUBLANES, cols), shift, 1).reshape(rows, cols)


def _gelu_tanh(x):
    return 0.5 * x * (1.0 + jnp.tanh(0.7978845608028654 * (x + 0.044715 * (x * x * x))))


def _sigmoid(x):
    return 0.5 * jnp.tanh(0.5 * x) + 0.5


def _sqrt_one_minus_exp2(x):
    t = jnp.tanh(x)
    return jnp.sqrt(-2.0 * t / (1.0 - t))


def _const_spec(shape, grid_rank):
    zeros = (0,) * len(shape)
    if grid_rank == 1:
        return pl.BlockSpec(shape, lambda i: zeros, pipeline_mode=pl.Buffered(1))
    return pl.BlockSpec(shape, lambda i, j: zeros, pipeline_mode=pl.Buffered(1))


def _params(*sem):
    return pltpu.CompilerParams(dimension_semantics=sem, vmem_limit_bytes=VMEM_LIMIT_BYTES)


def _even_in_kernel(*refs, tm, prompt):
    if prompt:
        (x_ref, tab_ref, gmix_ref, w1_ref, gq_ref, segq_ref, eq_ref, gckv_ref, gkr_ref,
         wuk_ref, segk_ref, ek_ref, gk_ref, wuv_ref, wpool_ref, pscale_ref,
         q_ref, c_ref, krp_ref, k_ref, v_ref, pool_ref, utail_ref, uext_ref) = refs
    else:
        (x_ref, tab_ref, gmix_ref, w1_ref, gq_ref, segq_ref, eq_ref, gckv_ref, gkr_ref,
         q_ref, c_ref, krp_ref, u_ref) = refs

    xn = _rms(x_ref[...], gmix_ref[...]).astype(BF16)
    z = _dot(xn, w1_ref[...])
    u = z[:, 0:POOL_DIM]
    qz = z[:, POOL_DIM:POOL_DIM + QP_DIM]
    cz = z[:, POOL_DIM + QP_DIM:POOL_DIM + QP_DIM + KV_RANK]
    krz = z[:, POOL_DIM + QP_DIM + KV_RANK:]

    ta = tab_ref[:, 0:LANES]
    tb = tab_ref[:, LANES:2 * LANES]
    tc = tab_ref[:, 2 * LANES:3 * LANES]

    def rope(blk):
        return blk * ta + pltpu.roll(blk, HALF, 1) * tb + pltpu.roll(blk, LANES - HALF, 1) * tc

    msq = _dot((qz * qz).astype(BF16), segq_ref[...])
    qn = qz * _expand(lax.rsqrt(msq + NORM_EPS), eq_ref[...]) * gq_ref[...]
    for h in range(N_HEADS):
        lanes = slice(h * HEAD_PAD, (h + 1) * HEAD_PAD)
        q_ref[:, lanes] = rope(qn[:, lanes]).astype(q_ref.dtype)

    c = _rms(cz, gckv_ref[...])
    c_ref[...] = c
    mskr = jnp.sum(krz * krz, axis=-1, keepdims=True) * (1.0 / QK_ROPE_DIM)
    krr = rope(krz * lax.rsqrt(mskr + NORM_EPS) * gkr_ref[...])
    krp_ref[...] = krr

    if not prompt:
        u_ref[...] = u
        return

    cb = c.astype(BF16)
    kn = _dot(cb, wuk_ref[...])
    msk = _dot((kn * kn).astype(BF16), segk_ref[...])
    knn = kn * _expand(lax.rsqrt(msk + NORM_EPS), ek_ref[...]) * gk_ref[...]
    for h in range(N_HEADS):
        lanes = slice(h * HEAD_PAD, (h + 1) * HEAD_PAD)
        k_ref[:, lanes] = (knn[:, lanes] + krr).astype(BF16)
    ln = lax.broadcasted_iota(jnp.int32, (1, QP_DIM), 1)
    one_lane = (ln & (HEAD_PAD - 1)) + ((ln // HEAD_PAD) & 1) * V_HEAD_DIM == V_HEAD_DIM
    v_ref[...] = (_dot(cb, wuv_ref[...]) + jnp.where(one_lane, 1.0, 0.0)).astype(BF16)

    i = pl.program_id(1)
    hal = 2 * SUBLANES

    @pl.when(i == 0)
    def _():
        uext_ref[0:hal, :] = jnp.zeros((hal, POOL_DIM), F32)

    uext_ref[hal:hal + tm, :] = u
    pos = lax.broadcasted_iota(jnp.int32, (tm, POOL_GROUP_DIM), 0) + i * tm
    for g, w in enumerate(POOL_WINDOWS):
        cols = slice(g * POOL_GROUP_DIM, (g + 1) * POOL_GROUP_DIM)
        acc = uext_ref[hal:hal + tm, cols]
        for k in range(1, w):
            acc = acc + uext_ref[hal - k:hal - k + tm, cols]
        cnt = jnp.minimum(pos + 1, w).astype(F32)
        d = acc / cnt - u[:, cols]
        y = _dot(d.astype(BF16), wpool_ref[g]) * pscale_ref[:, cols]
        pool_ref[:, cols] = y.astype(BF16)
    uext_ref[0:hal, :] = uext_ref[tm:tm + hal, :]

    @pl.when(i == pl.num_programs(1) - 1)
    def _():
        utail_ref[...] = uext_ref[0:hal, :]


def _seg_mats():
    segq = np.zeros((QP_DIM, LANES), np.float32)
    eq = np.zeros((LANES, QP_DIM), np.float32)
    segk = np.zeros((QP_DIM, LANES), np.float32)
    ek = np.zeros((LANES, QP_DIM), np.float32)
    for h in range(N_HEADS):
        b = h * HEAD_PAD
        for dup in (0, SEG_DUP):
            segq[b:b + QK_NOPE_DIM, dup + 2 * h] = 1.0 / QK_NOPE_DIM
            segq[b + QK_NOPE_DIM:b + QK_HEAD_DIM, dup + 2 * h + 1] = 1.0 / QK_ROPE_DIM
            eq[dup + 2 * h, b:b + QK_NOPE_DIM] = 1.0
            eq[dup + 2 * h + 1, b + QK_NOPE_DIM:b + QK_HEAD_DIM] = 1.0
            segk[b:b + QK_NOPE_DIM, dup + h] = 1.0 / QK_NOPE_DIM
            ek[dup + h, b:b + QK_NOPE_DIM] = 1.0
    return [jnp.asarray(m, BF16) for m in (segq, eq, segk, ek)]


def _rope_table(pos):
    inv = ROPE_BASE ** (-jnp.arange(HALF, dtype=F32) / HALF)
    ang = pos.astype(F32)[:, None] * inv[None, :]
    cos, sin = jnp.cos(ang), jnp.sin(ang)
    n = pos.shape[0]
    one = jnp.ones((n, ROPE_LANE0), F32)
    zero = jnp.zeros((n, ROPE_LANE0), F32)
    zh = jnp.zeros((n, HALF), F32)
    tail1 = jnp.ones((n, LANES - ROPE_LANE0 - QK_ROPE_DIM), F32)
    tail0 = jnp.zeros((n, LANES - ROPE_LANE0 - QK_ROPE_DIM), F32)
    ta = jnp.concatenate([one, cos, cos, tail1], axis=1)
    tb = jnp.concatenate([zero, zh, sin, tail0], axis=1)
    tc = jnp.concatenate([zero, -sin, zh, tail0], axis=1)
    return jnp.concatenate([ta, tb, tc], axis=1)


def _head_pad(w, lead):
    d = w.shape[-1]
    return jnp.pad(w, ((0, 0), (0, 0), (0, HEAD_PAD - d))).reshape(lead, QP_DIM)


def _head_row(parts):
    row = jnp.concatenate(parts)
    row = jnp.pad(row, (0, HEAD_PAD - row.shape[0]))
    return jnp.tile(row, N_HEADS)[None, :]


def _even_weights(norm_mix, w_in, g_q_nope, g_q_rope, g_ckv, g_k_rope, g_k_nope, w_uk, w_uv, w_pool, pool_scale):
    w_in = w_in.astype(BF16)
    wq = _head_pad(w_in[:, POOL_DIM:POOL_DIM + Q_DIM].reshape(D_MODEL, N_HEADS, QK_HEAD_DIM), D_MODEL)
    wc = w_in[:, POOL_DIM + Q_DIM:POOL_DIM + Q_DIM + KV_RANK]
    wkr = jnp.pad(w_in[:, POOL_DIM + Q_DIM + KV_RANK:], ((0, 0), (ROPE_LANE0, HEAD_PAD - QK_HEAD_DIM)))
    w1 = jnp.concatenate([w_in[:, :POOL_DIM], wq, wc, wkr], axis=1)
    gq = _head_row([g_q_nope, g_q_rope]) * (SOFTMAX_SCALE * LOG2_E)
    gkr = _head_row([jnp.zeros((ROPE_LANE0,), F32), g_k_rope])[:, :HEAD_PAD]
    gk = _head_row([g_k_nope])
    wuk = _head_pad(w_uk, KV_RANK).astype(BF16)
    v_even = jnp.pad(w_uv, ((0, 0), (0, 0), (0, HEAD_PAD - V_HEAD_DIM)))
    v_odd = jnp.pad(w_uv, ((0, 0), (0, 0), (HEAD_PAD - V_HEAD_DIM, 0)))
    odd = (jnp.arange(N_HEADS) % 2 == 1)[None, :, None]
    wuv = jnp.where(odd, v_odd, v_even).reshape(KV_RANK, QP_DIM).astype(BF16)
    return dict(gmix=norm_mix[None, :], w1=w1, gq=gq, gckv=g_ckv[None, :], gkr=gkr, gk=gk, wuk=wuk, wuv=wuv,
                wpool=w_pool.astype(BF16), pscale=pool_scale[None, :])


def _even_in_prompt(x, tab, ew, tm):
    nb, seq, _ = x.shape
    nt = seq // tm
    segq, eq, segk, ek = _seg_mats()
    row = lambda width: pl.BlockSpec((None, tm, width), lambda b, i: (b, i, 0))
    cs = lambda a: _const_spec(a.shape, 2)
    consts = [ew['gmix'], ew['w1'], ew['gq'], segq, eq, ew['gckv'], ew['gkr'],
              ew['wuk'], segk, ek, ew['gk'], ew['wuv'], ew['wpool'], ew['pscale']]
    out_shape = (
        jax.ShapeDtypeStruct((nb, seq, QP_DIM), BF16),
        jax.ShapeDtypeStruct((nb, seq, KV_RANK), F32),
        jax.ShapeDtypeStruct((nb, seq, HEAD_PAD), F32),
        jax.ShapeDtypeStruct((nb, seq, QP_DIM), BF16),
        jax.ShapeDtypeStruct((nb, seq, QP_DIM), BF16),
        jax.ShapeDtypeStruct((nb, seq, POOL_DIM), BF16),
        jax.ShapeDtypeStruct((nb, 2 * SUBLANES, POOL_DIM), F32),
    )
    out_specs = (row(QP_DIM), row(KV_RANK), row(HEAD_PAD), row(QP_DIM), row(QP_DIM), row(POOL_DIM),
                 pl.BlockSpec((None, 2 * SUBLANES, POOL_DIM), lambda b, i: (b, 0, 0)))
    return pl.pallas_call(
        functools.partial(_even_in_kernel, tm=tm, prompt=True),
        grid=(nb, nt),
        in_specs=[row(D_MODEL), pl.BlockSpec((tm, 3 * LANES), lambda b, i: (i, 0))] + [cs(a) for a in consts],
        out_specs=out_specs,
        out_shape=out_shape,
        scratch_shapes=[pltpu.VMEM((2 * SUBLANES + tm, POOL_DIM), F32)],
        compiler_params=_params("arbitrary", "arbitrary"),
        name="even_in_prompt",
    )(x, tab, *consts)


def _even_in_sample(x, tab, ew, tm):
    rows = x.shape[0]
    segq, eq, _, _ = _seg_mats()
    row = lambda width: pl.BlockSpec((tm, width), lambda i: (i, 0))
    cs = lambda a: _const_spec(a.shape, 1)
    consts = [ew['gmix'], ew['w1'], ew['gq'], segq, eq, ew['gckv'], ew['gkr']]
    out_shape = (
        jax.ShapeDtypeStruct((rows, QP_DIM), F32),
        jax.ShapeDtypeStruct((rows, KV_RANK), F32),
        jax.ShapeDtypeStruct((rows, HEAD_PAD), F32),
        jax.ShapeDtypeStruct((rows, POOL_DIM), F32),
    )
    return pl.pallas_call(
        functools.partial(_even_in_kernel, tm=tm, prompt=False),
        grid=(rows // tm,),
        in_specs=[row(D_MODEL), row(3 * LANES)] + [cs(a) for a in consts],
        out_specs=(row(QP_DIM), row(KV_RANK), row(HEAD_PAD), row(POOL_DIM)),
        out_shape=out_shape,
        compiler_params=_params("arbitrary"),
        name="even_in_sample",
    )(x, tab, *consts)


def _prompt_attn_kernel(q_ref, k_ref, v_ref, o_ref, *, tq, tk):
    qi = pl.program_id(1)
    nsub = tq // tk
    causal = {n: lax.broadcasted_iota(jnp.int32, (n, tk), 0) >= lax.broadcasted_iota(jnp.int32, (n, tk), 1)
              for n in range(tk, tq + 1, tk)}
    lane = lax.broadcasted_iota(jnp.int32, (tq, HEAD_PAD), 1)
    for j in range(N_HEADS // 2):
        heads = (2 * j, 2 * j + 1)
        lanes = [slice(h * HEAD_PAD, (h + 1) * HEAD_PAD) for h in heads]
        qs = [q_ref[:, ln] for ln in lanes]

        def tile(kt, carry, r0, lanes=lanes, qs=qs):
            off = pl.multiple_of(kt * tk, tk)
            lo = 0 if r0 is None else r0
            new = []
            for (m, acc), ln, qh in zip(carry, lanes, qs):
                s = _dot_nt(qh[lo:, :], k_ref[pl.ds(off, tk), ln])
                if r0 is not None:
                    s = jnp.where(causal[tq - lo], s, -jnp.inf)
                m_new = jnp.maximum(m[lo:, :], jnp.max(s, axis=-1, keepdims=True))
                p = jnp.exp2(s - m_new)
                acc_new = (jnp.exp2(m[lo:, :] - m_new) * acc[lo:, :]
                           + _dot(p.astype(BF16), v_ref[pl.ds(off, tk), ln]))
                if lo:
                    m_new = jnp.concatenate([m[:lo, :], m_new], axis=0)
                    acc_new = jnp.concatenate([acc[:lo, :], acc_new], axis=0)
                new.append((m_new, acc_new))
            return tuple(new)

        def below(jj, carry):
            for u in range(nsub):
                carry = tile(jj * nsub + u, carry, None)
            return carry

        carry = tuple((jnp.full((tq, 1), -jnp.inf, F32), jnp.zeros((tq, HEAD_PAD), F32)) for _ in heads)
        carry = lax.fori_loop(0, qi, below, carry)
        for u in range(nsub):
            carry = tile(qi * nsub + u, carry, u * tk)
        (_, acc_e), (_, acc_o) = carry
        out_e = jnp.where(lane < V_HEAD_DIM, acc_e / acc_e[:, V_HEAD_DIM:V_HEAD_DIM + 1], 0.0)
        out_o = jnp.where(lane >= V_HEAD_DIM, acc_o / acc_o[:, 0:1], 0.0)
        o_ref[:, j * LANES:(j + 1) * LANES] = (out_e + out_o).astype(BF16)


def _prompt_attn(q, k, v, tq, tk):
    nb, seq, _ = q.shape
    assert tq % tk == 0 and seq % tq == 0
    full = pl.BlockSpec((None, seq, QP_DIM), lambda b, i: (b, 0, 0), pipeline_mode=pl.Buffered(1))
    return pl.pallas_call(
        functools.partial(_prompt_attn_kernel, tq=tq, tk=tk),
        grid=(nb, seq // tq),
        in_specs=[pl.BlockSpec((None, tq, QP_DIM), lambda b, i: (b, i, 0)), full, full],
        out_specs=pl.BlockSpec((None, tq, ATTN_OUT_DIM), lambda b, i: (b, i, 0)),
        out_shape=jax.ShapeDtypeStruct((nb, seq, ATTN_OUT_DIM), BF16),
        compiler_params=_params("arbitrary", "arbitrary"),
        name="prompt_attn",
    )(q, k, v)


def _sample_scores(cb, krt, qabs, qr, segt, wukd):
    k2 = _dot(cb, wukd)
    k2 = k2 * k2
    sumsq = (k2[:, 0:LANES] + k2[:, LANES:2 * LANES]) + (k2[:, 2 * LANES:3 * LANES] + k2[:, 3 * LANES:])
    ss = _dot_nt(segt, sumsq.astype(BF16))
    sp = _dot_nt(qabs, cb)
    rope = _dot(qr[:, 0:QK_ROPE_DIM], krt.astype(BF16))
    return sp * lax.rsqrt(ss + NORM_EPS) + rope


def _sample_prep_kernel(q_ref, cn_ref, krn_ref, gk_ref, wukp_ref, wukd_ref, segt_ref,
                        qabs_ref, qr_ref, snew_ref, *, n_tok, rows_per_step):
    ncol = N_HEADS * n_tok
    rows = rows_per_step * ncol
    lane = lax.broadcasted_iota(jnp.int32, (n_tok, LANES), 1)
    qts, qrs = [], []
    for j in range(rows_per_step):
        q = q_ref[j * n_tok:(j + 1) * n_tok, :]
        qts += [q] * N_HEADS
        qrs += [jnp.where(lane < QK_ROPE_DIM,
                          pltpu.roll(q[:, h * HEAD_PAD:(h + 1) * HEAD_PAD], LANES - ROPE_LANE0, 1), 0.0)
                for h in range(N_HEADS)]
    r = lax.broadcasted_iota(jnp.int32, (rows, QP_DIM), 0)
    ln = lax.broadcasted_iota(jnp.int32, (rows, QP_DIM), 1)
    keep = jnp.where((ln & (HEAD_PAD - 1)) < QK_NOPE_DIM, (r // n_tok) % N_HEADS, -1) == ln // HEAD_PAD
    qg = jnp.where(keep, jnp.concatenate(qts, axis=0) * gk_ref[...], 0.0).astype(BF16)
    qabs = _dot_nt(qg, wukp_ref[...]).astype(BF16)
    qr = jnp.concatenate(qrs, axis=0).astype(BF16)
    cn = cn_ref[...].astype(BF16)
    krt = pltpu.roll(krn_ref[...], LANES - ROPE_LANE0, 1).T[0:QK_ROPE_DIM, :]
    s = _sample_scores(cn, krt, qabs, qr, segt_ref[...], wukd_ref[...])
    row = lax.broadcasted_iota(jnp.int32, (rows, PAGE_SIZE), 0)
    key = lax.broadcasted_iota(jnp.int32, (rows, PAGE_SIZE), 1)
    same_row = jnp.where(key // n_tok == row // ncol, key % n_tok, n_tok)
    qabs_ref[...] = qabs
    qr_ref[...] = qr
    snew_ref[...] = jnp.where(same_row <= row % n_tok, s, -jnp.inf)


def _sample_attn_kernel(pt_ref, qabs_ref, qr_ref, snew_ref, cn_ref, wukd_ref, wuv_ref, segt_ref,
                        ckv_hbm, kr_hbm, o_ref, cbuf, krbuf, sem, s_ref, cb_ref, acc_ref,
                        *, n_tok, n_pages, sbp):
    b = pl.program_id(0)
    slot = lax.rem(b, 2)
    sb = sbp * PAGE_SIZE
    n_sb = n_pages // sbp
    ncol = N_HEADS * n_tok

    def start_pages(row, i0, sl):
        for k in range(sbp):
            page = pt_ref[row * n_pages + i0 + k]
            pltpu.make_async_copy(ckv_hbm.at[0, page], cbuf.at[sl, i0 + k], sem.at[0, sl]).start()
            pltpu.make_async_copy(kr_hbm.at[0, page], krbuf.at[sl, i0 + k], sem.at[1, sl]).start()

    @pl.when(b == 0)
    def _():
        def first(i, carry):
            start_pages(0, i * sbp, 0)
            return carry
        lax.fori_loop(0, n_sb, first, 0)

    def lane_tiles(x, op):
        out = x[:, 0:LANES]
        for j in range(1, x.shape[1] // LANES):
            out = op(out, x[:, j * LANES:(j + 1) * LANES])
        return out

    pltpu.make_async_copy(ckv_hbm.at[0, pl.ds(0, n_pages)], cbuf.at[slot], sem.at[0, slot]).wait()
    pltpu.make_async_copy(kr_hbm.at[0, pl.ds(0, n_pages)], krbuf.at[slot], sem.at[1, slot]).wait()

    nxt = jnp.minimum(b + 1, pl.num_programs(0) - 1)

    def score_block(i, m):
        start_pages(nxt, i * sbp, 1 - slot)
        off = pl.multiple_of(i * sb, sb)
        cb = cbuf[slot, pl.ds(i * sbp, sbp)].reshape(sb, KV_RANK).astype(BF16)
        cb_ref[pl.ds(off, sb), :] = cb
        krt = jnp.concatenate([krbuf[slot, i * sbp + k] for k in range(sbp)], axis=1)
        s = _sample_scores(cb, krt, qabs_ref[...], qr_ref[...], segt_ref[...], wukd_ref[...])
        s_ref[i] = s
        return jnp.maximum(m, lane_tiles(s, jnp.maximum))

    s = snew_ref[...]
    m = lax.fori_loop(0, n_sb, score_block, s, unroll=4)
    m = jnp.max(m, axis=1, keepdims=True)

    cn = cn_ref[...].astype(BF16)
    p = jnp.exp2(s - m)
    acc_ref[...] = _dot(p.astype(BF16), cn)

    def value_block(i, lp):
        off = pl.multiple_of(i * sb, sb)
        p = jnp.exp2(s_ref[i] - m)
        acc_ref[...] += _dot(p.astype(BF16), cb_ref[pl.ds(off, sb), :])
        return lp + lane_tiles(p, jnp.add)

    l = jnp.sum(lax.fori_loop(0, n_sb, value_block, p, unroll=True), axis=1, keepdims=True)

    lat = acc_ref[...] * (1.0 / l)
    zz = _dot(lat.astype(BF16), wuv_ref[...])
    lane_h = lax.broadcasted_iota(jnp.int32, (n_tok, ATTN_OUT_DIM), 1) // V_HEAD_DIM
    out = jnp.zeros((n_tok, ATTN_OUT_DIM), F32)
    for h in range(N_HEADS):
        out = out + jnp.where(lane_h == h, zz[h * n_tok:(h + 1) * n_tok, :], 0.0)
    o_ref[...] = out

    @pl.when(b == pl.num_programs(0) - 1)
    def _():
        pltpu.make_async_copy(ckv_hbm.at[0, pl.ds(0, n_pages)], cbuf.at[1 - slot], sem.at[0, 1 - slot]).wait()
        pltpu.make_async_copy(kr_hbm.at[0, pl.ds(0, n_pages)], krbuf.at[1 - slot], sem.at[1, 1 - slot]).wait()


def _sample_attn(q, c, krp, cache_ckv, cache_krope, page_table, g_k_nope, w_uk, w_uv, n_tok, sbp):
    nb, n_pages = page_table.shape
    ncol = N_HEADS * n_tok
    gk = _head_row([g_k_nope])
    wukp = _head_pad(w_uk, KV_RANK).astype(BF16)
    wukd = w_uk.transpose(0, 2, 1).reshape(KV_RANK, N_HEADS * QK_NOPE_DIM).astype(BF16)
    wuv = w_uv.reshape(KV_RANK, ATTN_OUT_DIM).astype(BF16)
    segt = np.zeros((ncol, LANES), np.float32)
    for col in range(ncol):
        segt[col, (np.arange(LANES) % N_HEADS) == col // n_tok] = 1.0 / QK_NOPE_DIM
    segt = jnp.asarray(segt, BF16)
    pt = page_table.reshape(-1)
    kr_t = jnp.swapaxes(cache_krope, 2, 3)

    rps = PAGE_SIZE // n_tok
    assert nb % rps == 0
    seg_all = jnp.tile(segt, (rps, 1))
    prow = lambda r, width: pl.BlockSpec((rps * r, width), lambda i: (i, 0))
    qabs, qr, snew = pl.pallas_call(
        functools.partial(_sample_prep_kernel, n_tok=n_tok, rows_per_step=rps),
        grid=(nb // rps,),
        in_specs=[prow(n_tok, QP_DIM), prow(n_tok, KV_RANK), prow(n_tok, HEAD_PAD)]
                 + [_const_spec(a.shape, 1) for a in (gk, wukp, wukd, seg_all)],
        out_specs=(prow(ncol, KV_RANK), prow(ncol, LANES), prow(ncol, PAGE_SIZE)),
        out_shape=(jax.ShapeDtypeStruct((nb * ncol, KV_RANK), BF16), jax.ShapeDtypeStruct((nb * ncol, LANES), BF16),
                   jax.ShapeDtypeStruct((nb * ncol, PAGE_SIZE), F32)),
        compiler_params=_params("arbitrary"),
        name="sample_prep",
    )(q, c, krp, gk, wukp, wukd, seg_all)

    rows = lambda r, width: pl.BlockSpec((r, width), lambda b, pt_ref: (b, 0))
    const = lambda a: pl.BlockSpec(a.shape, lambda b, pt_ref: (0,) * a.ndim)
    hbm = pl.BlockSpec(memory_space=pl.ANY)
    n_keys = n_pages * PAGE_SIZE
    grid_spec = pltpu.PrefetchScalarGridSpec(
        num_scalar_prefetch=1,
        grid=(nb,),
        in_specs=[rows(ncol, KV_RANK), rows(ncol, LANES), rows(ncol, PAGE_SIZE),
                  pl.BlockSpec((PAGE_SIZE, KV_RANK), lambda b, pt_ref: (b // rps, 0)),
                  const(wukd), const(wuv), const(segt), hbm, hbm],
        out_specs=rows(n_tok, ATTN_OUT_DIM),
        scratch_shapes=[pltpu.VMEM((2, n_pages, PAGE_SIZE, KV_RANK), F32),
                        pltpu.VMEM((2, n_pages, QK_ROPE_DIM, PAGE_SIZE), F32),
                        pltpu.SemaphoreType.DMA((2, 2)),
                        pltpu.VMEM((n_pages // sbp, ncol, sbp * PAGE_SIZE), F32),
                        pltpu.VMEM((n_keys, KV_RANK), BF16),
                        pltpu.VMEM((ncol, KV_RANK), F32)],
    )
    return pl.pallas_call(
        functools.partial(_sample_attn_kernel, n_tok=n_tok, n_pages=n_pages, sbp=sbp),
        grid_spec=grid_spec,
        out_shape=jax.ShapeDtypeStruct((nb * n_tok, ATTN_OUT_DIM), F32),
        compiler_params=_params("arbitrary"),
        name="sample_attn",
    )(pt, qabs, qr, snew, c, wukd, wuv, segt, cache_ckv, kr_t)


def _pool_sample_kernel(u_ref, hist_ref, wpool_ref, pscale_ref, o_ref, *, n_tok, n_past):
    ext = [hist_ref[k] for k in range(POOL_HIST)] + [u_ref[t] for t in range(n_tok)]
    nb = u_ref.shape[1]
    for g, w in enumerate(POOL_WINDOWS):
        cols = slice(g * POOL_GROUP_DIM, (g + 1) * POOL_GROUP_DIM)
        ds = []
        for t in range(n_tok):
            acc = ext[POOL_HIST + t][:, cols]
            for k in range(1, w):
                acc = acc + ext[POOL_HIST + t - k][:, cols]
            ds.append(acc / float(min(n_past + t + 1, w)) - ext[POOL_HIST + t][:, cols])
        d = jnp.concatenate(ds, axis=0).astype(BF16)
        y = (_dot(d, wpool_ref[g]) * pscale_ref[:, cols]).astype(BF16)
        for t in range(n_tok):
            o_ref[t, :, cols] = y[t * nb:(t + 1) * nb, :]


def _pool_sample(u_tm, hist_tm, w_pool, pscale, n_past):
    n_tok, nb, _ = u_tm.shape
    return pl.pallas_call(
        functools.partial(_pool_sample_kernel, n_tok=n_tok, n_past=n_past),
        out_shape=jax.ShapeDtypeStruct((n_tok, nb, POOL_DIM), BF16),
        compiler_params=pltpu.CompilerParams(vmem_limit_bytes=VMEM_LIMIT_BYTES),
        name="pool_sample",
    )(u_tm, hist_tm, w_pool, pscale)


def _mix_mlp_kernel(x_ref, a1_ref, a2_ref, wo1_ref, wo2_ref, gffn_ref, wup_ref, wdn_ref, y_ref, *, ck):
    y_ref[...] = (x_ref[...] + _dot(a1_ref[...].astype(BF16), wo1_ref[...])
                  + _dot(a2_ref[...].astype(BF16), wo2_ref[...]))
    xn = _rms(y_ref[...], gffn_ref[...]).astype(BF16)
    for c in range(D_FF // ck):
        h = jnp.maximum(_dot(xn, wup_ref[:, c * ck:(c + 1) * ck]), 0.0)
        y_ref[...] += _dot((h * h).astype(BF16), wdn_ref[c * ck:(c + 1) * ck, :])


def _mix_mlp(x, a1, a1_col, a2, a2_col, w_out, g_ffn, w_up, w_down, tm, ck=1024):
    rows = x.shape[0]
    half = D_MODEL // 2
    wo = w_out.astype(BF16)
    consts = [wo[:half], wo[half:], g_ffn[None, :], w_up.astype(BF16), w_down.astype(BF16)]
    row = pl.BlockSpec((tm, D_MODEL), lambda i: (i, 0))
    return pl.pallas_call(
        functools.partial(_mix_mlp_kernel, ck=ck),
        grid=(rows // tm,),
        in_specs=[row, pl.BlockSpec((tm, half), lambda i: (i, a1_col)), pl.BlockSpec((tm, half), lambda i: (i, a2_col))]
                 + [_const_spec(a.shape, 1) for a in consts],
        out_specs=row,
        out_shape=jax.ShapeDtypeStruct((rows, D_MODEL), F32),
        compiler_params=_params("arbitrary"),
        name="mix_mlp",
    )(x, a1, a2, *consts)


def _gates(v, wg_ref, bga_ref, bgx_ref, lam_ref):
    vb = v.astype(BF16)
    pair = 2 * RNN_BLOCK_DIM
    ga, gx = [], []
    for p in range(RNN_BLOCKS // 2):
        g = _dot(vb[:, p * pair:(p + 1) * pair], wg_ref[p])
        ga.append(g[:, :pair])
        gx.append(g[:, pair:])
    r = _sigmoid(jnp.concatenate(ga, axis=1) + bga_ref[...])
    ig = _sigmoid(jnp.concatenate(gx, axis=1) + bgx_ref[...])
    nl = -lam_ref[...]
    softplus = jnp.maximum(nl, 0.0) + jnp.log1p(jnp.exp(-jnp.abs(nl)))
    log_a = (-LRU_C) * r * softplus
    return log_a, ig


def _odd_prompt_kernel(x_ref, gmix_ref, win_ref, cw_ref, cb_ref, wg_ref, bga_ref, bgx_ref, lam_ref,
                       y_ref, ctail_ref, hlast_ref, uext_ref, a_ref, b_ref, hcar_ref, *, tm):
    i = pl.program_id(1)
    sub = SUBLANES

    @pl.when(i == 0)
    def _():
        uext_ref[...] = jnp.zeros((sub, RNN_DIM), F32)
        hcar_ref[...] = jnp.zeros((sub, RNN_DIM), F32)

    xn = _rms(x_ref[...], gmix_ref[...]).astype(BF16)
    z = _dot(xn, win_ref[...])
    gate = z[:, :RNN_DIM]
    u = z[:, RNN_DIM:]
    rowmod = lax.broadcasted_iota(jnp.int32, (tm, RNN_DIM), 0) & (sub - 1)
    v = cb_ref[...]
    for k in range(CONV_WIDTH):
        back = CONV_WIDTH - 1 - k
        if back == 0:
            uk = u
        else:
            rot = _group_roll(u, back)
            prev = jnp.concatenate([pltpu.roll(uext_ref[...], back, 0), rot[:tm - sub, :]], axis=0)
            uk = jnp.where(rowmod < back, prev, rot)
        v = v + uk * cw_ref[k:k + 1, :]
    uext_ref[...] = u[tm - sub:, :]

    log_a, ig = _gates(v, wg_ref, bga_ref, bgx_ref, lam_ref)
    a = jnp.exp(log_a)
    row = lax.broadcasted_iota(jnp.int32, (tm, RNN_DIM), 0)
    mult = jnp.where(row + i * tm == 0, 1.0, _sqrt_one_minus_exp2(log_a))
    b = mult * ig * v

    for s in (1, 2, 4):
        ok = rowmod >= s
        b = jnp.where(ok, a * _group_roll(b, s), 0.0) + b
        a = jnp.where(ok, a * _group_roll(a, s), a)
    a_ref[...] = a
    b_ref[...] = b

    def group(g, hb):
        off = pl.multiple_of(g * sub, sub)
        hg = a_ref[pl.ds(off, sub), :] * hb + b_ref[pl.ds(off, sub), :]
        b_ref[pl.ds(off, sub), :] = hg
        return jnp.broadcast_to(hg[sub - 1:sub, :], (sub, RNN_DIM))

    hb = lax.fori_loop(0, tm // sub, group, hcar_ref[...])
    hcar_ref[...] = hb
    y_ref[...] = (_gelu_tanh(gate) * b_ref[...]).astype(BF16)

    @pl.when(i == pl.num_programs(1) - 1)
    def _():
        ctail_ref[...] = uext_ref[...]
        hlast_ref[...] = hb


def _odd_weights(norm_mix, w_in, conv_w, conv_b, w_ga, b_ga, w_gx, b_gx, lam):
    def pairs(w):
        z = jnp.zeros((RNN_BLOCKS // 2, RNN_BLOCK_DIM, RNN_BLOCK_DIM), F32)
        top = jnp.concatenate([w[0::2], z], axis=2)
        bot = jnp.concatenate([z, w[1::2]], axis=2)
        return jnp.concatenate([top, bot], axis=1)
    wg = jnp.concatenate([pairs(w_ga), pairs(w_gx)], axis=2).astype(BF16)
    return [norm_mix[None, :], w_in.astype(BF16), conv_w, conv_b[None, :], wg, b_ga[None, :], b_gx[None, :],
            lam[None, :]]


def _odd_prompt(x, ow, tm):
    nb, seq, _ = x.shape
    row = pl.BlockSpec((None, tm, D_MODEL), lambda b, i: (b, i, 0))
    tail = pl.BlockSpec((None, SUBLANES, RNN_DIM), lambda b, i: (b, 0, 0))
    return pl.pallas_call(
        functools.partial(_odd_prompt_kernel, tm=tm),
        grid=(nb, seq // tm),
        in_specs=[row] + [_const_spec(a.shape, 2) for a in ow],
        out_specs=(row, tail, tail),
        out_shape=(jax.ShapeDtypeStruct((nb, seq, RNN_DIM), BF16),
                   jax.ShapeDtypeStruct((nb, SUBLANES, RNN_DIM), F32),
                   jax.ShapeDtypeStruct((nb, SUBLANES, RNN_DIM), F32)),
        scratch_shapes=[pltpu.VMEM((SUBLANES, RNN_DIM), F32), pltpu.VMEM((tm, RNN_DIM), F32),
                        pltpu.VMEM((tm, RNN_DIM), F32), pltpu.VMEM((SUBLANES, RNN_DIM), F32)],
        compiler_params=_params("arbitrary", "arbitrary"),
        name="odd_prompt",
    )(x, *ow)


def _odd_sample_kernel(x_ref, ch_ref, h0_ref, gmix_ref, win_ref, cw_ref, cb_ref, wg_ref, bga_ref, bgx_ref, lam_ref,
                       y_ref, ctail_ref, hlast_ref, *, n_tok, n_past):
    nb = x_ref.shape[1]
    x = x_ref[...].reshape(n_tok * nb, D_MODEL)
    z = _dot(_rms(x, gmix_ref[...]).astype(BF16), win_ref[...])
    gate = z[:, :RNN_DIM]
    ext = [ch_ref[k] for k in range(CONV_WIDTH - 1)] + [z[t * nb:(t + 1) * nb, RNN_DIM:] for t in range(n_tok)]
    vs = []
    for t in range(n_tok):
        v = cb_ref[...]
        for k in range(CONV_WIDTH):
            v = v + ext[t + k] * cw_ref[k:k + 1, :]
        vs.append(v)
    v = jnp.concatenate(vs, axis=0)
    log_a, ig = _gates(v, wg_ref, bga_ref, bgx_ref, lam_ref)
    a = jnp.exp(log_a)
    mult = _sqrt_one_minus_exp2(log_a)
    if n_past == 0:
        first = lax.broadcasted_iota(jnp.int32, mult.shape, 0) < nb
        mult = jnp.where(first, 1.0, mult)
    b = mult * ig * v
    h = h0_ref[...]
    hs = []
    for t in range(n_tok):
        h = a[t * nb:(t + 1) * nb, :] * h + b[t * nb:(t + 1) * nb, :]
        hs.append(h)
    y = (_gelu_tanh(gate) * jnp.concatenate(hs, axis=0)).astype(BF16)
    y_ref[...] = y.reshape(n_tok, nb, RNN_DIM)
    for k in range(CONV_WIDTH - 1):
        ctail_ref[k] = ext[n_tok + k]
    hlast_ref[...] = h


def _odd_sample(x_tm, ch_tm, h0, ow, n_past, bb=32):
    n_tok, nb, _ = x_tm.shape
    blk = lambda t, w: pl.BlockSpec((t, bb, w), lambda i: (0, i, 0))
    return pl.pallas_call(
        functools.partial(_odd_sample_kernel, n_tok=n_tok, n_past=n_past),
        grid=(nb // bb,),
        in_specs=[blk(n_tok, D_MODEL), blk(CONV_WIDTH - 1, RNN_DIM), pl.BlockSpec((bb, RNN_DIM), lambda i: (i, 0))]
                 + [_const_spec(a.shape, 1) for a in ow],
        out_specs=(blk(n_tok, RNN_DIM), blk(CONV_WIDTH - 1, RNN_DIM), pl.BlockSpec((bb, RNN_DIM), lambda i: (i, 0))),
        out_shape=(jax.ShapeDtypeStruct((n_tok, nb, RNN_DIM), BF16),
                   jax.ShapeDtypeStruct((CONV_WIDTH - 1, nb, RNN_DIM), F32),
                   jax.ShapeDtypeStruct((nb, RNN_DIM), F32)),
        compiler_params=_params("arbitrary"),
        name="odd_sample",
    )(x_tm, ch_tm, h0, *ow)


def kernel(x_prompt, x_sample, cache_ckv, cache_krope, state_pool, state_conv, state_lru, page_table, norm_mix,
           w_in_even, g_q_nope, g_q_rope, g_ckv, g_k_rope, g_k_nope, w_uk, w_uv, w_pool, pool_scale, w_out_even,
           w_in_rnn, conv_w, conv_b, w_gate_a, b_gate_a, w_gate_x, b_gate_x, lru_lambda, w_out_rnn, norm_ffn,
           w_up, w_down):
    nb, seq, _ = x_prompt.shape
    db, n_tok, _ = x_sample.shape
    n_past = page_table.shape[1] * PAGE_SIZE
    depth = norm_mix.shape[0]
    assert depth == 2 and cache_ckv.shape[0] == 1, "one even (pool + MLA) layer followed by one odd (RG-LRU) layer"
    rope_sl = slice(ROPE_LANE0, ROPE_LANE0 + QK_ROPE_DIM)

    ew = _even_weights(norm_mix[0], w_in_even[0], g_q_nope[0], g_q_rope[0], g_ckv[0], g_k_rope[0], g_k_nope[0],
                       w_uk[0], w_uv[0], w_pool[0], pool_scale[0])
    tab_p = _rope_table(jnp.arange(seq, dtype=jnp.int32))
    tab_s = jnp.tile(_rope_table(n_past + jnp.arange(n_tok, dtype=jnp.int32)), (db, 1))

    q_p, c_p, krp_p, k_p, v_p, pool_p, utail_p = _even_in_prompt(x_prompt, tab_p, ew, tm=512)
    attn_p = _prompt_attn(q_p, k_p, v_p, tq=1024, tk=512)
    xs = x_sample.reshape(db * n_tok, D_MODEL)
    q_s, c_s, krp_s, u_s = _even_in_sample(xs, tab_s, ew, tm=256)
    attn_s = _sample_attn(q_s, c_s, krp_s, cache_ckv, cache_krope, page_table, g_k_nope[0], w_uk[0], w_uv[0],
                          n_tok=n_tok, sbp=8)
    u_s3 = u_s.reshape(db, n_tok, POOL_DIM)
    pool_s = _pool_sample(u_s3.transpose(1, 0, 2), state_pool[0].transpose(1, 0, 2), ew['wpool'], ew['pscale'],
                          n_past)
    pool_s = pool_s.transpose(1, 0, 2).reshape(db * n_tok, POOL_DIM)

    mlp0 = (w_out_even[0], norm_ffn[0], w_up[0], w_down[0])
    yp = _mix_mlp(x_prompt.reshape(nb * seq, D_MODEL), pool_p.reshape(nb * seq, POOL_DIM), 0,
                  attn_p.reshape(nb * seq, ATTN_OUT_DIM), 0, *mlp0, tm=512)
    ys = _mix_mlp(xs, pool_s, 0, attn_s, 0, *mlp0, tm=256)

    ow = _odd_weights(norm_mix[1], w_in_rnn[0], conv_w[0], conv_b[0], w_gate_a[0], b_gate_a[0], w_gate_x[0],
                      b_gate_x[0], lru_lambda[0])
    rnn_p, ctail_p, hlast_p = _odd_prompt(yp.reshape(nb, seq, D_MODEL), ow, tm=256)
    ys_tm = ys.reshape(db, n_tok, D_MODEL).transpose(1, 0, 2)
    rnn_s, conv_s_tm, lru_s = _odd_sample(ys_tm, state_conv[0].transpose(1, 0, 2), state_lru[0], ow, n_past)

    mlp1 = (w_out_rnn[0], norm_ffn[1], w_up[1], w_down[1])
    rnn_p2 = rnn_p.reshape(nb * seq, RNN_DIM)
    yp = _mix_mlp(yp, rnn_p2, 0, rnn_p2, 1, *mlp1, tm=512)
    rnn_s2 = rnn_s.reshape(n_tok * db, RNN_DIM)
    ys_out = _mix_mlp(ys_tm.reshape(n_tok * db, D_MODEL), rnn_s2, 0, rnn_s2, 1, *mlp1, tm=256)
    ys_out = ys_out.reshape(n_tok, db, D_MODEL).transpose(1, 0, 2)

    pool_state_s = jnp.concatenate([state_pool[0], u_s3], axis=1)[:, -POOL_HIST:]
    return (yp.reshape(nb, seq, D_MODEL), ys_out,
            c_p[None], krp_p[None, :, :, rope_sl], utail_p[None, :, 1:], ctail_p[None, :, SUBLANES - CONV_WIDTH + 1:],
            hlast_p[None, :, 0],
            c_s.reshape(1, db, n_tok, KV_RANK), krp_s[:, rope_sl].reshape(1, db, n_tok, QK_ROPE_DIM),
            pool_state_s[None], conv_s_tm.transpose(1, 0, 2)[None], lru_s[None])
```

```python
import functools

import numpy as np
import jax
import jax.numpy as jnp
from jax import lax
from jax.experimental import pallas as pl
from jax.experimental.pallas import tpu as pltpu

D_MODEL = 1024
PAGE_SIZE = 128
POOL_WINDOWS = (2, 4, 8, 16)
POOL_GROUP_DIM = 128
POOL_DIM = len(POOL_WINDOWS) * POOL_GROUP_DIM
POOL_HIST = max(POOL_WINDOWS) - 1
N_HEADS = 8
QK_NOPE_DIM = 64
QK_ROPE_DIM = 32
QK_HEAD_DIM = QK_NOPE_DIM + QK_ROPE_DIM
V_HEAD_DIM = 64
KV_RANK = 256
Q_DIM = N_HEADS * QK_HEAD_DIM
ATTN_OUT_DIM = N_HEADS * V_HEAD_DIM
ROPE_BASE = 10000.0
SOFTMAX_SCALE = QK_HEAD_DIM ** -0.5
LOG2_E = 1.4426950408889634
RNN_DIM = D_MODEL
RNN_BLOCKS = 8
RNN_BLOCK_DIM = RNN_DIM // RNN_BLOCKS
CONV_WIDTH = 4
LRU_C = 8.0
D_FF = 4 * D_MODEL
NORM_EPS = 1e-6

LANES = 128
SUBLANES = 8
VMEM_LIMIT_BYTES = 56 * 2 ** 20

HEAD_PAD = LANES
QP_DIM = N_HEADS * HEAD_PAD
W1_DIM = POOL_DIM + QP_DIM + KV_RANK + HEAD_PAD
ROPE_LANE0 = QK_NOPE_DIM
HALF = QK_ROPE_DIM // 2
HEADS_PER_LOOP = 4

F32 = jnp.float32
BF16 = jnp.bfloat16


def _dot(a, b):
    return jnp.dot(a, b, preferred_element_type=F32)


def _dot_nt(a, b):
    return lax.dot_general(a, b, (((1,), (1,)), ((), ())), preferred_element_type=F32)


def _dot_tn(a, b):
    return lax.dot_general(a, b, (((0,), (0,)), ((), ())), preferred_element_type=F32)


def _rms(x, g):
    ms = jnp.mean(x * x, axis=-1, keepdims=True)
    return x * lax.rsqrt(ms + NORM_EPS) * g


SEG_DUP = LANES // 2


def _expand(rs, e):
    hi = rs.astype(BF16)
    lo = (rs - hi.astype(F32)).astype(BF16)
    lane = lax.broadcasted_iota(jnp.int32, rs.shape, 1)
    return _dot(jnp.where(lane < SEG_DUP, hi, lo), e)


def _group_roll(x, shift):
    rows, cols = x.shape
    return pltpu.roll(x.reshape(rows // SUBLANES, SUBLANES, cols), shift, 1).reshape(rows, cols)


def _gelu_tanh(x):
    half = 0.5 * x
    return half + half * jnp.tanh(x * (0.7978845608028654 + 0.035677408136300125 * (x * x)))


def _sigmoid(x):
    return 0.5 * jnp.tanh(0.5 * x) + 0.5


def _sqrt_one_minus_exp2(x):
    t = jnp.tanh(x)
    return jnp.sqrt(-2.0 * t / (1.0 - t))


def _const_spec(shape, grid_rank):
    zeros = (0,) * len(shape)
    if grid_rank == 1:
        return pl.BlockSpec(shape, lambda i: zeros, pipeline_mode=pl.Buffered(1))
    return pl.BlockSpec(shape, lambda i, j: zeros, pipeline_mode=pl.Buffered(1))


def _params(*sem):
    return pltpu.CompilerParams(dimension_semantics=sem, vmem_limit_bytes=VMEM_LIMIT_BYTES)


def _even_in_kernel(*refs, tm, prompt):
    if prompt:
        (x_ref, tab_ref, gmix_ref, w1_ref, gq_ref, segq_ref, eq_ref, gckv_ref, gkr_ref,
         wuk_ref, segk_ref, ek_ref, gk_ref, wuv_ref, wpool_ref, pscale_ref,
         q_ref, c_ref, krp_ref, k_ref, v_ref, pool_ref, utail_ref, halo_ref) = refs
    else:
        (x_ref, tab_ref, gmix_ref, w1_ref, gq_ref, segq_ref, eq_ref, gckv_ref, gkr_ref,
         q_ref, c_ref, krp_ref, u_ref) = refs

    xn = _rms(x_ref[...], gmix_ref[...]).astype(BF16)
    z = _dot(xn, w1_ref[...])
    u = z[:, 0:POOL_DIM]
    qz = z[:, POOL_DIM:POOL_DIM + QP_DIM]
    cz = z[:, POOL_DIM + QP_DIM:POOL_DIM + QP_DIM + KV_RANK]
    krz = z[:, POOL_DIM + QP_DIM + KV_RANK:]

    ta = tab_ref[:, 0:LANES]
    tb = tab_ref[:, LANES:2 * LANES]
    tc = tab_ref[:, 2 * LANES:3 * LANES]

    def rope(blk):
        return blk * ta + pltpu.roll(blk, HALF, 1) * tb + pltpu.roll(blk, LANES - HALF, 1) * tc

    msq = _dot((qz * qz).astype(BF16), segq_ref[...])
    qn = qz * _expand(lax.rsqrt(msq + NORM_EPS), eq_ref[...]) * gq_ref[...]
    for h in range(N_HEADS):
        lanes = slice(h * HEAD_PAD, (h + 1) * HEAD_PAD)
        q_ref[:, lanes] = rope(qn[:, lanes]).astype(q_ref.dtype)

    c = _rms(cz, gckv_ref[...])
    c_ref[...] = c
    mskr = jnp.sum(krz * krz, axis=-1, keepdims=True) * (1.0 / QK_ROPE_DIM)
    krr = rope(krz * lax.rsqrt(mskr + NORM_EPS) * gkr_ref[...])
    if prompt:
        krp_ref[...] = pltpu.roll(krr, LANES - ROPE_LANE0, 1)[:, 0:QK_ROPE_DIM]
    else:
        krp_ref[...] = krr

    if not prompt:
        u_ref[...] = u
        return

    cb = c.astype(BF16)
    kn = _dot(cb, wuk_ref[...])
    msk = _dot((kn * kn).astype(BF16), segk_ref[...])
    knn = kn * _expand(lax.rsqrt(msk + NORM_EPS), ek_ref[...]) * gk_ref[...]
    for h in range(N_HEADS):
        lanes = slice(h * HEAD_PAD, (h + 1) * HEAD_PAD)
        k_ref[:, lanes] = (knn[:, lanes] + krr).astype(BF16)
    ln = lax.broadcasted_iota(jnp.int32, (1, QP_DIM), 1)
    one_lane = (ln & (HEAD_PAD - 1)) + ((ln // HEAD_PAD) & 1) * V_HEAD_DIM == V_HEAD_DIM
    v_ref[...] = (_dot(cb, wuv_ref[...]) + jnp.where(one_lane, 1.0, 0.0)).astype(BF16)

    i = pl.program_id(1)
    sub = SUBLANES

    @pl.when(i == 0)
    def _():
        halo_ref[...] = jnp.zeros(halo_ref.shape, F32)

    rowmod = lax.broadcasted_iota(jnp.int32, (tm, POOL_GROUP_DIM), 0) & (sub - 1)
    pos = lax.broadcasted_iota(jnp.int32, (tm, POOL_GROUP_DIM), 0) + i * tm
    for g, w in enumerate(POOL_WINDOWS):
        cols = slice(g * POOL_GROUP_DIM, (g + 1) * POOL_GROUP_DIM)
        acc = u[:, cols]
        level, shift = 0, 1
        while shift < w:
            tail = halo_ref[level, :, cols]
            halo_ref[level, :, cols] = acc[tm - sub:, :]
            if shift < sub:
                rot = _group_roll(acc, shift)
                prev = jnp.concatenate([pltpu.roll(tail, shift, 0), rot[:tm - sub, :]], axis=0)
                acc = acc + jnp.where(rowmod < shift, prev, rot)
            else:
                acc = acc + jnp.concatenate([tail, acc[:tm - sub, :]], axis=0)
            level, shift = level + 1, 2 * shift
        cnt = jnp.minimum(pos + 1, w).astype(F32)
        d = acc / cnt - u[:, cols]
        y = _dot(d.astype(BF16), wpool_ref[g]) * pscale_ref[:, cols]
        pool_ref[:, cols] = y.astype(BF16)

    @pl.when(i == pl.num_programs(1) - 1)
    def _():
        utail_ref[...] = u[tm - 2 * sub:, :]


def _seg_mats():
    segq = np.zeros((QP_DIM, LANES), np.float32)
    eq = np.zeros((LANES, QP_DIM), np.float32)
    segk = np.zeros((QP_DIM, LANES), np.float32)
    ek = np.zeros((LANES, QP_DIM), np.float32)
    for h in range(N_HEADS):
        b = h * HEAD_PAD
        for dup in (0, SEG_DUP):
            segq[b:b + QK_NOPE_DIM, dup + 2 * h] = 1.0 / QK_NOPE_DIM
            segq[b + QK_NOPE_DIM:b + QK_HEAD_DIM, dup + 2 * h + 1] = 1.0 / QK_ROPE_DIM
            eq[dup + 2 * h, b:b + QK_NOPE_DIM] = 1.0
            eq[dup + 2 * h + 1, b + QK_NOPE_DIM:b + QK_HEAD_DIM] = 1.0
            segk[b:b + QK_NOPE_DIM, dup + h] = 1.0 / QK_NOPE_DIM
            ek[dup + h, b:b + QK_NOPE_DIM] = 1.0
    return [jnp.asarray(m, BF16) for m in (segq, eq, segk, ek)]


def _rope_table(pos):
    inv = ROPE_BASE ** (-jnp.arange(HALF, dtype=F32) / HALF)
    ang = pos.astype(F32)[:, None] * inv[None, :]
    cos, sin = jnp.cos(ang), jnp.sin(ang)
    n = pos.shape[0]
    one = jnp.ones((n, ROPE_LANE0), F32)
    zero = jnp.zeros((n, ROPE_LANE0), F32)
    zh = jnp.zeros((n, HALF), F32)
    tail1 = jnp.ones((n, LANES - ROPE_LANE0 - QK_ROPE_DIM), F32)
    tail0 = jnp.zeros((n, LANES - ROPE_LANE0 - QK_ROPE_DIM), F32)
    ta = jnp.concatenate([one, cos, cos, tail1], axis=1)
    tb = jnp.concatenate([zero, zh, sin, tail0], axis=1)
    tc = jnp.concatenate([zero, -sin, zh, tail0], axis=1)
    return jnp.concatenate([ta, tb, tc], axis=1)


def _head_pad(w, lead):
    d = w.shape[-1]
    return jnp.pad(w, ((0, 0), (0, 0), (0, HEAD_PAD - d))).reshape(lead, QP_DIM)


def _head_row(parts):
    row = jnp.concatenate(parts)
    row = jnp.pad(row, (0, HEAD_PAD - row.shape[0]))
    return jnp.tile(row, N_HEADS)[None, :]


def _even_weights(norm_mix, w_in, g_q_nope, g_q_rope, g_ckv, g_k_rope, g_k_nope, w_uk, w_uv, w_pool, pool_scale):
    w_in = w_in.astype(BF16)
    wq = _head_pad(w_in[:, POOL_DIM:POOL_DIM + Q_DIM].reshape(D_MODEL, N_HEADS, QK_HEAD_DIM), D_MODEL)
    wc = w_in[:, POOL_DIM + Q_DIM:POOL_DIM + Q_DIM + KV_RANK]
    wkr = jnp.pad(w_in[:, POOL_DIM + Q_DIM + KV_RANK:], ((0, 0), (ROPE_LANE0, HEAD_PAD - QK_HEAD_DIM)))
    w1 = jnp.concatenate([w_in[:, :POOL_DIM], wq, wc, wkr], axis=1)
    gq = _head_row([g_q_nope, g_q_rope]) * (SOFTMAX_SCALE * LOG2_E)
    gkr = _head_row([jnp.zeros((ROPE_LANE0,), F32), g_k_rope])[:, :HEAD_PAD]
    gk = _head_row([g_k_nope])
    wuk = _head_pad(w_uk, KV_RANK).astype(BF16)
    v_even = jnp.pad(w_uv, ((0, 0), (0, 0), (0, HEAD_PAD - V_HEAD_DIM)))
    v_odd = jnp.pad(w_uv, ((0, 0), (0, 0), (HEAD_PAD - V_HEAD_DIM, 0)))
    odd = (jnp.arange(N_HEADS) % 2 == 1)[None, :, None]
    wuv = jnp.where(odd, v_odd, v_even).reshape(KV_RANK, QP_DIM).astype(BF16)
    return dict(gmix=norm_mix[None, :], w1=w1, gq=gq, gckv=g_ckv[None, :], gkr=gkr, gk=gk, wuk=wuk, wuv=wuv,
                wpool=w_pool.astype(BF16), pscale=pool_scale[None, :])


def _even_in_prompt(x, tab, ew, tm):
    nb, seq, _ = x.shape
    nt = seq // tm
    segq, eq, segk, ek = _seg_mats()
    row = lambda width: pl.BlockSpec((None, tm, width), lambda b, i: (b, i, 0))
    cs = lambda a: _const_spec(a.shape, 2)
    consts = [ew['gmix'], ew['w1'], ew['gq'], segq, eq, ew['gckv'], ew['gkr'],
              ew['wuk'], segk, ek, ew['gk'], ew['wuv'], ew['wpool'], ew['pscale']]
    out_shape = (
        jax.ShapeDtypeStruct((nb, seq, QP_DIM), BF16),
        jax.ShapeDtypeStruct((nb, seq, KV_RANK), F32),
        jax.ShapeDtypeStruct((nb, seq, QK_ROPE_DIM), F32),
        jax.ShapeDtypeStruct((nb, seq, QP_DIM), BF16),
        jax.ShapeDtypeStruct((nb, seq, QP_DIM), BF16),
        jax.ShapeDtypeStruct((nb, seq, POOL_DIM), BF16),
        jax.ShapeDtypeStruct((nb, 2 * SUBLANES, POOL_DIM), F32),
    )
    out_specs = (row(QP_DIM), row(KV_RANK), row(QK_ROPE_DIM), row(QP_DIM), row(QP_DIM), row(POOL_DIM),
                 pl.BlockSpec((None, 2 * SUBLANES, POOL_DIM), lambda b, i: (b, 0, 0)))
    return pl.pallas_call(
        functools.partial(_even_in_kernel, tm=tm, prompt=True),
        grid=(nb, nt),
        in_specs=[row(D_MODEL), pl.BlockSpec((tm, 3 * LANES), lambda b, i: (i, 0))] + [cs(a) for a in consts],
        out_specs=out_specs,
        out_shape=out_shape,
        scratch_shapes=[pltpu.VMEM((max(POOL_WINDOWS).bit_length() - 1, SUBLANES, POOL_DIM), F32)],
        compiler_params=_params("arbitrary", "arbitrary"),
        name="even_in_prompt",
    )(x, tab, *consts)


def _even_in_sample(x, tab, ew, tm):
    rows = x.shape[0]
    segq, eq, _, _ = _seg_mats()
    row = lambda width: pl.BlockSpec((tm, width), lambda i: (i, 0))
    cs = lambda a: _const_spec(a.shape, 1)
    consts = [ew['gmix'], ew['w1'], ew['gq'], segq, eq, ew['gckv'], ew['gkr']]
    out_shape = (
        jax.ShapeDtypeStruct((rows, QP_DIM), F32),
        jax.ShapeDtypeStruct((rows, KV_RANK), F32),
        jax.ShapeDtypeStruct((rows, HEAD_PAD), F32),
        jax.ShapeDtypeStruct((rows, POOL_DIM), F32),
    )
    return pl.pallas_call(
        functools.partial(_even_in_kernel, tm=tm, prompt=False),
        grid=(rows // tm,),
        in_specs=[row(D_MODEL), row(3 * LANES)] + [cs(a) for a in consts],
        out_specs=(row(QP_DIM), row(KV_RANK), row(HEAD_PAD), row(POOL_DIM)),
        out_shape=out_shape,
        compiler_params=_params("arbitrary"),
        name="even_in_sample",
    )(x, tab, *consts)


def _prompt_attn_kernel(q_ref, k_ref, v_ref, o_ref, *, tq, tk, hpl):
    qi = pl.program_id(1)
    nsub = tq // tk
    causal = {n: lax.broadcasted_iota(jnp.int32, (n, tk), 0) >= lax.broadcasted_iota(jnp.int32, (n, tk), 1)
              for n in range(tk, tq + 1, tk)}
    lane = lax.broadcasted_iota(jnp.int32, (tq, HEAD_PAD), 1)
    for g in range(N_HEADS // hpl):
        heads = tuple(range(g * hpl, (g + 1) * hpl))
        lanes = [slice(h * HEAD_PAD, (h + 1) * HEAD_PAD) for h in heads]
        qs = [q_ref[:, ln] for ln in lanes]

        def tile(kt, carry, r0, lanes=lanes, qs=qs):
            off = pl.multiple_of(kt * tk, tk)
            lo = 0 if r0 is None else r0
            new = []
            for (m, acc), ln, qh in zip(carry, lanes, qs):
                s = _dot_nt(qh[lo:, :], k_ref[pl.ds(off, tk), ln])
                if r0 is not None:
                    s = jnp.where(causal[tq - lo], s, -jnp.inf)
                m_new = jnp.maximum(m[lo:, :], jnp.max(s, axis=-1, keepdims=True))
                p = jnp.exp2(s - m_new)
                acc_new = (jnp.exp2(m[lo:, :] - m_new) * acc[lo:, :]
                           + _dot(p.astype(BF16), v_ref[pl.ds(off, tk), ln]))
                if lo:
                    m_new = jnp.concatenate([m[:lo, :], m_new], axis=0)
                    acc_new = jnp.concatenate([acc[:lo, :], acc_new], axis=0)
                new.append((m_new, acc_new))
            return tuple(new)

        def below(jj, carry):
            for u in range(nsub):
                carry = tile(jj * nsub + u, carry, None)
            return carry

        carry = tuple((jnp.full((tq, 1), -jnp.inf, F32), jnp.zeros((tq, HEAD_PAD), F32)) for _ in heads)
        carry = lax.fori_loop(0, qi, below, carry)
        for u in range(nsub):
            carry = tile(qi * nsub + u, carry, u * tk)
        for k in range(0, hpl, 2):
            acc_e, acc_o = carry[k][1], carry[k + 1][1]
            out_e = jnp.where(lane < V_HEAD_DIM, acc_e / acc_e[:, V_HEAD_DIM:V_HEAD_DIM + 1], 0.0)
            out_o = jnp.where(lane >= V_HEAD_DIM, acc_o / acc_o[:, 0:1], 0.0)
            j = (heads[0] + k) // 2
            o_ref[:, j * LANES:(j + 1) * LANES] = (out_e + out_o).astype(BF16)


def _prompt_attn(q, k, v, tq, tk):
    nb, seq, _ = q.shape
    assert tq % tk == 0 and seq % tq == 0
    full = pl.BlockSpec((None, seq, QP_DIM), lambda b, i: (b, 0, 0), pipeline_mode=pl.Buffered(1))
    return pl.pallas_call(
        functools.partial(_prompt_attn_kernel, tq=tq, tk=tk, hpl=HEADS_PER_LOOP),
        grid=(nb, seq // tq),
        in_specs=[pl.BlockSpec((None, tq, QP_DIM), lambda b, i: (b, i, 0)), full, full],
        out_specs=pl.BlockSpec((None, tq, ATTN_OUT_DIM), lambda b, i: (b, i, 0)),
        out_shape=jax.ShapeDtypeStruct((nb, seq, ATTN_OUT_DIM), BF16),
        compiler_params=_params("arbitrary", "arbitrary"),
        name="prompt_attn",
    )(q, k, v)


def _sample_scores(cb, krt, qabs, qr, segt, wukd):
    k2 = _dot(cb, wukd)
    k2 = k2 * k2
    sumsq = (k2[:, 0:LANES] + k2[:, LANES:2 * LANES]) + (k2[:, 2 * LANES:3 * LANES] + k2[:, 3 * LANES:])
    ss = _dot_nt(segt, sumsq.astype(BF16))
    sp = _dot_nt(qabs, cb)
    rope = _dot(qr[:, 0:QK_ROPE_DIM], krt.astype(BF16))
    return sp * lax.rsqrt(ss + NORM_EPS) + rope


def _sample_prep_kernel(q_ref, cn_ref, krn_ref, gk_ref, wukp_ref, wukd_ref, segt_ref,
                        qabs_ref, qr_ref, snew_ref, *, n_tok, rows_per_step):
    ncol = N_HEADS * n_tok
    rows = rows_per_step * ncol
    lane = lax.broadcasted_iota(jnp.int32, (n_tok, LANES), 1)
    qts, qrs = [], []
    for j in range(rows_per_step):
        q = q_ref[j * n_tok:(j + 1) * n_tok, :]
        qts += [q] * N_HEADS
        qrs += [jnp.where(lane < QK_ROPE_DIM,
                          pltpu.roll(q[:, h * HEAD_PAD:(h + 1) * HEAD_PAD], LANES - ROPE_LANE0, 1), 0.0)
                for h in range(N_HEADS)]
    r = lax.broadcasted_iota(jnp.int32, (rows, QP_DIM), 0)
    ln = lax.broadcasted_iota(jnp.int32, (rows, QP_DIM), 1)
    keep = jnp.where((ln & (HEAD_PAD - 1)) < QK_NOPE_DIM, (r // n_tok) % N_HEADS, -1) == ln // HEAD_PAD
    qg = jnp.where(keep, jnp.concatenate(qts, axis=0) * gk_ref[...], 0.0).astype(BF16)
    qabs = _dot_nt(qg, wukp_ref[...]).astype(BF16)
    qr = jnp.concatenate(qrs, axis=0).astype(BF16)
    cn = cn_ref[...].astype(BF16)
    krt = pltpu.roll(krn_ref[...], LANES - ROPE_LANE0, 1).T[0:QK_ROPE_DIM, :]
    s = _sample_scores(cn, krt, qabs, qr, segt_ref[...], wukd_ref[...])
    row = lax.broadcasted_iota(jnp.int32, (rows, PAGE_SIZE), 0)
    key = lax.broadcasted_iota(jnp.int32, (rows, PAGE_SIZE), 1)
    same_row = jnp.where(key // n_tok == row // ncol, key % n_tok, n_tok)
    qabs_ref[...] = qabs
    qr_ref[...] = qr
    snew_ref[...] = jnp.where(same_row <= row % n_tok, s, -jnp.inf)


def _sample_attn_kernel(pt_ref, qabs_ref, qr_ref, snew_ref, cn_ref, wukd_ref, wuv_ref, segt_ref,
                        ckv_hbm, kr_hbm, o_ref, cbuf, krbuf, sem, s_ref, cb_ref, acc_ref,
                        *, n_tok, n_pages, sbp):
    b = pl.program_id(0)
    slot = lax.rem(b, 2)
    sb = sbp * PAGE_SIZE
    n_sb = n_pages // sbp
    ncol = N_HEADS * n_tok

    def start_pages(row, i0, sl):
        for k in range(sbp):
            page = pt_ref[row * n_pages + i0 + k]
            pltpu.make_async_copy(ckv_hbm.at[0, page], cbuf.at[sl, i0 + k], sem.at[0, sl]).start()
            pltpu.make_async_copy(kr_hbm.at[0, page], krbuf.at[sl, i0 + k], sem.at[1, sl]).start()

    @pl.when(b == 0)
    def _():
        def first(i, carry):
            start_pages(0, i * sbp, 0)
            return carry
        lax.fori_loop(0, n_sb, first, 0)

    pltpu.make_async_copy(ckv_hbm.at[0, pl.ds(0, n_pages)], cbuf.at[slot], sem.at[0, slot]).wait()
    pltpu.make_async_copy(kr_hbm.at[0, pl.ds(0, n_pages)], krbuf.at[slot], sem.at[1, slot]).wait()

    nxt = jnp.minimum(b + 1, pl.num_programs(0) - 1)

    def lane_tiles(x, op):
        out = x[:, 0:LANES]
        for j in range(1, x.shape[1] // LANES):
            out = op(out, x[:, j * LANES:(j + 1) * LANES])
        return out

    def score_block(i, m):
        start_pages(nxt, i * sbp, 1 - slot)
        off = pl.multiple_of(i * sb, sb)
        cb = cbuf[slot, pl.ds(i * sbp, sbp)].reshape(sb, KV_RANK).astype(BF16)
        cb_ref[pl.ds(off, sb), :] = cb
        krt = jnp.concatenate([krbuf[slot, i * sbp + k] for k in range(sbp)], axis=1)
        s = _sample_scores(cb, krt, qabs_ref[...], qr_ref[...], segt_ref[...], wukd_ref[...])
        s_ref[i] = s
        return jnp.maximum(m, lane_tiles(s, jnp.maximum))

    s = snew_ref[...]
    m = lax.fori_loop(0, n_sb, score_block, s, unroll=4)
    m = jnp.max(m, axis=1, keepdims=True)

    p = jnp.exp2(s - m)
    acc_ref[...] = _dot(p.astype(BF16), cn_ref[...].astype(BF16))

    def value_block(i, lp):
        off = pl.multiple_of(i * sb, sb)
        p = jnp.exp2(s_ref[i] - m)
        acc_ref[...] += _dot(p.astype(BF16), cb_ref[pl.ds(off, sb), :])
        return lp + lane_tiles(p, jnp.add)

    l = jnp.sum(lax.fori_loop(0, n_sb, value_block, p, unroll=4), axis=1, keepdims=True)

    lat = acc_ref[...] * (1.0 / l)
    zz = _dot(lat.astype(BF16), wuv_ref[...])
    lane_h = lax.broadcasted_iota(jnp.int32, (n_tok, ATTN_OUT_DIM), 1) // V_HEAD_DIM
    out = jnp.zeros((n_tok, ATTN_OUT_DIM), F32)
    for h in range(N_HEADS):
        out = out + jnp.where(lane_h == h, zz[h * n_tok:(h + 1) * n_tok, :], 0.0)
    o_ref[...] = out

    @pl.when(b == pl.num_programs(0) - 1)
    def _():
        pltpu.make_async_copy(ckv_hbm.at[0, pl.ds(0, n_pages)], cbuf.at[1 - slot], sem.at[0, 1 - slot]).wait()
        pltpu.make_async_copy(kr_hbm.at[0, pl.ds(0, n_pages)], krbuf.at[1 - slot], sem.at[1, 1 - slot]).wait()


def _sample_attn(q, c, krp, cache_ckv, cache_krope, page_table, g_k_nope, w_uk, w_uv, n_tok, sbp):
    nb, n_pages = page_table.shape
    ncol = N_HEADS * n_tok
    gk = _head_row([g_k_nope])
    wukp = _head_pad(w_uk, KV_RANK).astype(BF16)
    wukd = w_uk.transpose(0, 2, 1).reshape(KV_RANK, N_HEADS * QK_NOPE_DIM).astype(BF16)
    wuv = w_uv.reshape(KV_RANK, ATTN_OUT_DIM).astype(BF16)
    segt = np.zeros((ncol, LANES), np.float32)
    for col in range(ncol):
        segt[col, (np.arange(LANES) % N_HEADS) == col // n_tok] = 1.0 / QK_NOPE_DIM
    segt = jnp.asarray(segt, BF16)
    pt = page_table.reshape(-1)
    kr_t = jnp.swapaxes(cache_krope, 2, 3)

    rps = PAGE_SIZE // n_tok
    assert nb % rps == 0
    seg_all = jnp.tile(segt, (rps, 1))
    prow = lambda r, width: pl.BlockSpec((rps * r, width), lambda i: (i, 0))
    qabs, qr, snew = pl.pallas_call(
        functools.partial(_sample_prep_kernel, n_tok=n_tok, rows_per_step=rps),
        grid=(nb // rps,),
        in_specs=[prow(n_tok, QP_DIM), prow(n_tok, KV_RANK), prow(n_tok, HEAD_PAD)]
                 + [_const_spec(a.shape, 1) for a in (gk, wukp, wukd, seg_all)],
        out_specs=(prow(ncol, KV_RANK), prow(ncol, LANES), prow(ncol, PAGE_SIZE)),
        out_shape=(jax.ShapeDtypeStruct((nb * ncol, KV_RANK), BF16), jax.ShapeDtypeStruct((nb * ncol, LANES), BF16),
                   jax.ShapeDtypeStruct((nb * ncol, PAGE_SIZE), F32)),
        compiler_params=_params("arbitrary"),
        name="sample_prep",
    )(q, c, krp, gk, wukp, wukd, seg_all)

    rows = lambda r, width: pl.BlockSpec((r, width), lambda b, pt_ref: (b, 0))
    const = lambda a: pl.BlockSpec(a.shape, lambda b, pt_ref: (0,) * a.ndim)
    hbm = pl.BlockSpec(memory_space=pl.ANY)
    grid_spec = pltpu.PrefetchScalarGridSpec(
        num_scalar_prefetch=1,
        grid=(nb,),
        in_specs=[rows(ncol, KV_RANK), rows(ncol, LANES), rows(ncol, PAGE_SIZE),
                  pl.BlockSpec((PAGE_SIZE, KV_RANK), lambda b, pt_ref: (b // rps, 0)),
                  const(wukd), const(wuv), const(segt), hbm, hbm],
        out_specs=rows(n_tok, ATTN_OUT_DIM),
        scratch_shapes=[pltpu.VMEM((2, n_pages, PAGE_SIZE, KV_RANK), F32),
                        pltpu.VMEM((2, n_pages, QK_ROPE_DIM, PAGE_SIZE), F32),
                        pltpu.SemaphoreType.DMA((2, 2)),
                        pltpu.VMEM((n_pages // sbp, ncol, sbp * PAGE_SIZE), F32),
                        pltpu.VMEM((n_pages * PAGE_SIZE, KV_RANK), BF16),
                        pltpu.VMEM((ncol, KV_RANK), F32)],
    )
    return pl.pallas_call(
        functools.partial(_sample_attn_kernel, n_tok=n_tok, n_pages=n_pages, sbp=sbp),
        grid_spec=grid_spec,
        out_shape=jax.ShapeDtypeStruct((nb * n_tok, ATTN_OUT_DIM), F32),
        compiler_params=_params("arbitrary"),
        name="sample_attn",
    )(pt, qabs, qr, snew, c, wukd, wuv, segt, cache_ckv, kr_t)


def _pool_sample_kernel(u_ref, hist_ref, wpool_ref, pscale_ref, o_ref, *, n_tok, n_past):
    ext = [hist_ref[k] for k in range(POOL_HIST)] + [u_ref[t] for t in range(n_tok)]
    nb = u_ref.shape[1]
    for g, w in enumerate(POOL_WINDOWS):
        cols = slice(g * POOL_GROUP_DIM, (g + 1) * POOL_GROUP_DIM)
        ds = []
        for t in range(n_tok):
            acc = ext[POOL_HIST + t][:, cols]
            for k in range(1, w):
                acc = acc + ext[POOL_HIST + t - k][:, cols]
            ds.append(acc / float(min(n_past + t + 1, w)) - ext[POOL_HIST + t][:, cols])
        d = jnp.concatenate(ds, axis=0).astype(BF16)
        y = (_dot(d, wpool_ref[g]) * pscale_ref[:, cols]).astype(BF16)
        for t in range(n_tok):
            o_ref[t, :, cols] = y[t * nb:(t + 1) * nb, :]


def _pool_sample(u_tm, hist_tm, w_pool, pscale, n_past):
    n_tok, nb, _ = u_tm.shape
    return pl.pallas_call(
        functools.partial(_pool_sample_kernel, n_tok=n_tok, n_past=n_past),
        out_shape=jax.ShapeDtypeStruct((n_tok, nb, POOL_DIM), BF16),
        compiler_params=pltpu.CompilerParams(vmem_limit_bytes=VMEM_LIMIT_BYTES),
        name="pool_sample",
    )(u_tm, hist_tm, w_pool, pscale)


def _mix_mlp_kernel(xp_ref, a1p_ref, a2p_ref, xs_ref, a1s_ref, a2s_ref, wo1_ref, wo2_ref, gffn_ref, wup_ref, wdn_ref,
                    yp_ref, ys_ref, *, ck, n_p):
    def rows(x_ref, a1_ref, a2_ref, y_ref):
        y_ref[...] = (x_ref[...] + _dot(a1_ref[...].astype(BF16), wo1_ref[...])
                      + _dot(a2_ref[...].astype(BF16), wo2_ref[...]))
        xn = _rms(y_ref[...], gffn_ref[...]).astype(BF16)
        for c in range(D_FF // ck):
            h = jnp.maximum(_dot(xn, wup_ref[:, c * ck:(c + 1) * ck]), 0.0)
            y_ref[...] += _dot((h * h).astype(BF16), wdn_ref[c * ck:(c + 1) * ck, :])

    i = pl.program_id(0)

    @pl.when(i < n_p)
    def _():
        rows(xp_ref, a1p_ref, a2p_ref, yp_ref)

    @pl.when(i >= n_p)
    def _():
        rows(xs_ref, a1s_ref, a2s_ref, ys_ref)


def _mix_mlp(xp, a1p, a2p, xs, a1s, a2s, a1_col, a2_col, w_out, g_ffn, w_up, w_down, tm_p, tm_s, ck=1024):
    half = D_MODEL // 2
    n_p, n_s = xp.shape[0] // tm_p, xs.shape[0] // tm_s
    wo = w_out.astype(BF16)
    consts = [wo[:half], wo[half:], g_ffn[None, :], w_up.astype(BF16), w_down.astype(BF16)]
    p_spec = lambda width, col: pl.BlockSpec((tm_p, width), lambda i: (jnp.minimum(i, n_p - 1), col))
    s_spec = lambda width, col: pl.BlockSpec((tm_s, width), lambda i: (jnp.maximum(i - n_p, 0), col))
    return pl.pallas_call(
        functools.partial(_mix_mlp_kernel, ck=ck, n_p=n_p),
        grid=(n_p + n_s,),
        in_specs=[p_spec(D_MODEL, 0), p_spec(half, a1_col), p_spec(half, a2_col),
                  s_spec(D_MODEL, 0), s_spec(half, a1_col), s_spec(half, a2_col)]
                 + [_const_spec(a.shape, 1) for a in consts],
        out_specs=(p_spec(D_MODEL, 0), s_spec(D_MODEL, 0)),
        out_shape=(jax.ShapeDtypeStruct(xp.shape, F32), jax.ShapeDtypeStruct(xs.shape, F32)),
        compiler_params=_params("arbitrary"),
        name="mix_mlp",
    )(xp, a1p, a2p, xs, a1s, a2s, *consts)


def _gates(v, wg_ref, bga_ref, bgx_ref, lam_ref):
    vb = v.astype(BF16)
    pair = 2 * RNN_BLOCK_DIM
    ga, gx = [], []
    for p in range(RNN_BLOCKS // 2):
        g = _dot(vb[:, p * pair:(p + 1) * pair], wg_ref[p])
        ga.append(g[:, :pair])
        gx.append(g[:, pair:])
    r = _sigmoid(jnp.concatenate(ga, axis=1) + bga_ref[...])
    ig = _sigmoid(jnp.concatenate(gx, axis=1) + bgx_ref[...])
    nl = -lam_ref[...]
    softplus = jnp.maximum(nl, 0.0) + jnp.log1p(jnp.exp(-jnp.abs(nl)))
    log_a = (-LRU_C) * r * softplus
    return log_a, ig


def _odd_prompt_kernel(x_ref, gmix_ref, win_ref, cw_ref, cb_ref, wg_ref, bga_ref, bgx_ref, lam_ref,
                       y_ref, ctail_ref, hlast_ref, uext_ref, a_ref, b_ref, hcar_ref, *, tm):
    i = pl.program_id(1)
    sub = SUBLANES

    @pl.when(i == 0)
    def _():
        uext_ref[...] = jnp.zeros((sub, RNN_DIM), F32)
        hcar_ref[...] = jnp.zeros((sub, RNN_DIM), F32)

    xn = _rms(x_ref[...], gmix_ref[...]).astype(BF16)
    z = _dot(xn, win_ref[...])
    gate = z[:, :RNN_DIM]
    u = z[:, RNN_DIM:]
    rowmod = lax.broadcasted_iota(jnp.int32, (tm, RNN_DIM), 0) & (sub - 1)
    v = cb_ref[...]
    for k in range(CONV_WIDTH):
        back = CONV_WIDTH - 1 - k
        if back == 0:
            uk = u
        else:
            rot = _group_roll(u, back)
            prev = jnp.concatenate([pltpu.roll(uext_ref[...], back, 0), rot[:tm - sub, :]], axis=0)
            uk = jnp.where(rowmod < back, prev, rot)
        v = v + uk * cw_ref[k:k + 1, :]
    uext_ref[...] = u[tm - sub:, :]

    log_a, ig = _gates(v, wg_ref, bga_ref, bgx_ref, lam_ref)
    a = jnp.exp(log_a)
    row = lax.broadcasted_iota(jnp.int32, (tm, RNN_DIM), 0)
    mult = jnp.where(row + i * tm == 0, 1.0, _sqrt_one_minus_exp2(log_a))
    b = mult * ig * v

    for s in (1, 2, 4):
        ok = rowmod >= s
        b = jnp.where(ok, a * _group_roll(b, s), 0.0) + b
        a = jnp.where(ok, a * _group_roll(a, s), a)
    a_ref[...] = a
    b_ref[...] = b

    def group(g, hb):
        off = pl.multiple_of(g * sub, sub)
        hg = a_ref[pl.ds(off, sub), :] * hb + b_ref[pl.ds(off, sub), :]
        b_ref[pl.ds(off, sub), :] = hg
        return jnp.broadcast_to(hg[sub - 1:sub, :], (sub, RNN_DIM))

    hb = lax.fori_loop(0, tm // sub, group, hcar_ref[...])
    hcar_ref[...] = hb
    y_ref[...] = (_gelu_tanh(gate) * b_ref[...]).astype(BF16)

    @pl.when(i == pl.num_programs(1) - 1)
    def _():
        ctail_ref[...] = uext_ref[...]
        hlast_ref[...] = hb


def _odd_weights(norm_mix, w_in, conv_w, conv_b, w_ga, b_ga, w_gx, b_gx, lam):
    def pairs(w):
        z = jnp.zeros((RNN_BLOCKS // 2, RNN_BLOCK_DIM, RNN_BLOCK_DIM), F32)
        top = jnp.concatenate([w[0::2], z], axis=2)
        bot = jnp.concatenate([z, w[1::2]], axis=2)
        return jnp.concatenate([top, bot], axis=1)
    wg = jnp.concatenate([pairs(w_ga), pairs(w_gx)], axis=2).astype(BF16)
    return [norm_mix[None, :], w_in.astype(BF16), conv_w, conv_b[None, :], wg, b_ga[None, :], b_gx[None, :],
            lam[None, :]]


def _odd_prompt(x, ow, tm):
    nb, seq, _ = x.shape
    row = pl.BlockSpec((None, tm, D_MODEL), lambda b, i: (b, i, 0))
    tail = pl.BlockSpec((None, SUBLANES, RNN_DIM), lambda b, i: (b, 0, 0))
    return pl.pallas_call(
        functools.partial(_odd_prompt_kernel, tm=tm),
        grid=(nb, seq // tm),
        in_specs=[row] + [_const_spec(a.shape, 2) for a in ow],
        out_specs=(row, tail, tail),
        out_shape=(jax.ShapeDtypeStruct((nb, seq, RNN_DIM), BF16),
                   jax.ShapeDtypeStruct((nb, SUBLANES, RNN_DIM), F32),
                   jax.ShapeDtypeStruct((nb, SUBLANES, RNN_DIM), F32)),
        scratch_shapes=[pltpu.VMEM((SUBLANES, RNN_DIM), F32), pltpu.VMEM((tm, RNN_DIM), F32),
                        pltpu.VMEM((tm, RNN_DIM), F32), pltpu.VMEM((SUBLANES, RNN_DIM), F32)],
        compiler_params=_params("arbitrary", "arbitrary"),
        name="odd_prompt",
    )(x, *ow)


def _odd_sample_kernel(x_ref, ch_ref, h0_ref, gmix_ref, win_ref, cw_ref, cb_ref, wg_ref, bga_ref, bgx_ref, lam_ref,
                       y_ref, ctail_ref, hlast_ref, *, n_tok, n_past):
    nb = x_ref.shape[1]
    x = x_ref[...].reshape(n_tok * nb, D_MODEL)
    z = _dot(_rms(x, gmix_ref[...]).astype(BF16), win_ref[...])
    gate = z[:, :RNN_DIM]
    ext = [ch_ref[k] for k in range(CONV_WIDTH - 1)] + [z[t * nb:(t + 1) * nb, RNN_DIM:] for t in range(n_tok)]
    vs = []
    for t in range(n_tok):
        v = cb_ref[...]
        for k in range(CONV_WIDTH):
            v = v + ext[t + k] * cw_ref[k:k + 1, :]
        vs.append(v)
    v = jnp.concatenate(vs, axis=0)
    log_a, ig = _gates(v, wg_ref, bga_ref, bgx_ref, lam_ref)
    a = jnp.exp(log_a)
    mult = _sqrt_one_minus_exp2(log_a)
    if n_past == 0:
        first = lax.broadcasted_iota(jnp.int32, mult.shape, 0) < nb
        mult = jnp.where(first, 1.0, mult)
    b = mult * ig * v
    h = h0_ref[...]
    hs = []
    for t in range(n_tok):
        h = a[t * nb:(t + 1) * nb, :] * h + b[t * nb:(t + 1) * nb, :]
        hs.append(h)
    y = (_gelu_tanh(gate) * jnp.concatenate(hs, axis=0)).astype(BF16)
    y_ref[...] = y.reshape(n_tok, nb, RNN_DIM)
    for k in range(CONV_WIDTH - 1):
        ctail_ref[k] = ext[n_tok + k]
    hlast_ref[...] = h


def _odd_sample(x_tm, ch_tm, h0, ow, n_past, bb=32):
    n_tok, nb, _ = x_tm.shape
    blk = lambda t, w: pl.BlockSpec((t, bb, w), lambda i: (0, i, 0))
    return pl.pallas_call(
        functools.partial(_odd_sample_kernel, n_tok=n_tok, n_past=n_past),
        grid=(nb // bb,),
        in_specs=[blk(n_tok, D_MODEL), blk(CONV_WIDTH - 1, RNN_DIM), pl.BlockSpec((bb, RNN_DIM), lambda i: (i, 0))]
                 + [_const_spec(a.shape, 1) for a in ow],
        out_specs=(blk(n_tok, RNN_DIM), blk(CONV_WIDTH - 1, RNN_DIM), pl.BlockSpec((bb, RNN_DIM), lambda i: (i, 0))),
        out_shape=(jax.ShapeDtypeStruct((n_tok, nb, RNN_DIM), BF16),
                   jax.ShapeDtypeStruct((CONV_WIDTH - 1, nb, RNN_DIM), F32),
                   jax.ShapeDtypeStruct((nb, RNN_DIM), F32)),
        compiler_params=_params("arbitrary"),
        name="odd_sample",
    )(x_tm, ch_tm, h0, *ow)


def kernel(x_prompt, x_sample, cache_ckv, cache_krope, state_pool, state_conv, state_lru, page_table, norm_mix,
           w_in_even, g_q_nope, g_q_rope, g_ckv, g_k_rope, g_k_nope, w_uk, w_uv, w_pool, pool_scale, w_out_even,
           w_in_rnn, conv_w, conv_b, w_gate_a, b_gate_a, w_gate_x, b_gate_x, lru_lambda, w_out_rnn, norm_ffn,
           w_up, w_down):
    nb, seq, _ = x_prompt.shape
    db, n_tok, _ = x_sample.shape
    n_past = page_table.shape[1] * PAGE_SIZE
    depth = norm_mix.shape[0]
    assert depth == 2 and cache_ckv.shape[0] == 1, "one even (pool + MLA) layer followed by one odd (RG-LRU) layer"
    rope_sl = slice(ROPE_LANE0, ROPE_LANE0 + QK_ROPE_DIM)

    ew = _even_weights(norm_mix[0], w_in_even[0], g_q_nope[0], g_q_rope[0], g_ckv[0], g_k_rope[0], g_k_nope[0],
                       w_uk[0], w_uv[0], w_pool[0], pool_scale[0])
    tab_p = _rope_table(jnp.arange(seq, dtype=jnp.int32))
    tab_s = jnp.tile(_rope_table(n_past + jnp.arange(n_tok, dtype=jnp.int32)), (db, 1))

    q_p, c_p, kr_p, k_p, v_p, pool_p, utail_p = _even_in_prompt(x_prompt, tab_p, ew, tm=512)
    attn_p = _prompt_attn(q_p, k_p, v_p, tq=1024, tk=512)
    xs = x_sample.reshape(db * n_tok, D_MODEL)
    q_s, c_s, krp_s, u_s = _even_in_sample(xs, tab_s, ew, tm=256)
    attn_s = _sample_attn(q_s, c_s, krp_s, cache_ckv, cache_krope, page_table, g_k_nope[0], w_uk[0], w_uv[0],
                          n_tok=n_tok, sbp=8)
    u_s3 = u_s.reshape(db, n_tok, POOL_DIM)
    pool_s = _pool_sample(u_s3.transpose(1, 0, 2), state_pool[0].transpose(1, 0, 2), ew['wpool'], ew['pscale'],
                          n_past)
    pool_s = pool_s.transpose(1, 0, 2).reshape(db * n_tok, POOL_DIM)

    mlp0 = (w_out_even[0], norm_ffn[0], w_up[0], w_down[0])
    yp, ys = _mix_mlp(x_prompt.reshape(nb * seq, D_MODEL), pool_p.reshape(nb * seq, POOL_DIM),
                      attn_p.reshape(nb * seq, ATTN_OUT_DIM), xs, pool_s, attn_s, 0, 0, *mlp0, tm_p=512, tm_s=256)

    ow = _odd_weights(norm_mix[1], w_in_rnn[0], conv_w[0], conv_b[0], w_gate_a[0], b_gate_a[0], w_gate_x[0],
                      b_gate_x[0], lru_lambda[0])
    rnn_p, ctail_p, hlast_p = _odd_prompt(yp.reshape(nb, seq, D_MODEL), ow, tm=256)
    ys_tm = ys.reshape(db, n_tok, D_MODEL).transpose(1, 0, 2)
    rnn_s, conv_s_tm, lru_s = _odd_sample(ys_tm, state_conv[0].transpose(1, 0, 2), state_lru[0], ow, n_past)

    mlp1 = (w_out_rnn[0], norm_ffn[1], w_up[1], w_down[1])
    rnn_p2 = rnn_p.reshape(nb * seq, RNN_DIM)
    rnn_s2 = rnn_s.reshape(n_tok * db, RNN_DIM)
    yp, ys_out = _mix_mlp(yp, rnn_p2, rnn_p2, ys_tm.reshape(n_tok * db, D_MODEL), rnn_s2, rnn_s2, 0, 1, *mlp1,
                          tm_p=512, tm_s=256)
    ys_out = ys_out.reshape(n_tok, db, D_MODEL).transpose(1, 0, 2)

    pool_state_s = jnp.concatenate([state_pool[0], u_s3], axis=1)[:, -POOL_HIST:]
    return (yp.reshape(nb, seq, D_MODEL), ys_out,
            c_p[None], kr_p[None], utail_p[None, :, 1:], ctail_p[None, :, SUBLANES - CONV_WIDTH + 1:],
            hlast_p[None, :, 0],
            c_s.reshape(1, db, n_tok, KV_RANK), krp_s[:, rope_sl].reshape(1, db, n_tok, QK_ROPE_DIM),
            pool_state_s[None], conv_s_tm.transpose(1, 0, 2)[None], lru_s[None])
```

```python
import functools

import numpy as np
import jax
import jax.numpy as jnp
from jax import lax
from jax.experimental import pallas as pl
from jax.experimental.pallas import tpu as pltpu

D_MODEL = 1024
PAGE_SIZE = 128
POOL_WINDOWS = (2, 4, 8, 16)
POOL_GROUP_DIM = 128
POOL_DIM = len(POOL_WINDOWS) * POOL_GROUP_DIM
POOL_HIST = max(POOL_WINDOWS) - 1
N_HEADS = 8
QK_NOPE_DIM = 64
QK_ROPE_DIM = 32
QK_HEAD_DIM = QK_NOPE_DIM + QK_ROPE_DIM
V_HEAD_DIM = 64
KV_RANK = 256
Q_DIM = N_HEADS * QK_HEAD_DIM
ATTN_OUT_DIM = N_HEADS * V_HEAD_DIM
ROPE_BASE = 10000.0
SOFTMAX_SCALE = QK_HEAD_DIM ** -0.5
LOG2_E = 1.4426950408889634
RNN_DIM = D_MODEL
RNN_BLOCKS = 8
RNN_BLOCK_DIM = RNN_DIM // RNN_BLOCKS
CONV_WIDTH = 4
LRU_C = 8.0
D_FF = 4 * D_MODEL
NORM_EPS = 1e-6

LANES = 128
SUBLANES = 8
VMEM_LIMIT_BYTES = 56 * 2 ** 20

HEAD_PAD = LANES
QP_DIM = N_HEADS * HEAD_PAD
W1_DIM = POOL_DIM + QP_DIM + KV_RANK + HEAD_PAD
ROPE_LANE0 = QK_NOPE_DIM
HALF = QK_ROPE_DIM // 2
HEADS_PER_LOOP = 4

F32 = jnp.float32
BF16 = jnp.bfloat16


def _dot(a, b):
    return jnp.dot(a, b, preferred_element_type=F32)


def _dot_nt(a, b):
    return lax.dot_general(a, b, (((1,), (1,)), ((), ())), preferred_element_type=F32)


def _dot_tn(a, b):
    return lax.dot_general(a, b, (((0,), (0,)), ((), ())), preferred_element_type=F32)


def _rms(x, g):
    ms = jnp.mean(x * x, axis=-1, keepdims=True)
    return x * lax.rsqrt(ms + NORM_EPS) * g


SEG_DUP = LANES // 2


def _expand(rs, e):
    hi = rs.astype(BF16)
    lo = (rs - hi.astype(F32)).astype(BF16)
    lane = lax.broadcasted_iota(jnp.int32, rs.shape, 1)
    return _dot(jnp.where(lane < SEG_DUP, hi, lo), e)


def _group_roll(x, shift):
    rows, cols = x.shape
    return pltpu.roll(x.reshape(rows // SUBLANES, SUBLANES, cols), shift, 1).reshape(rows, cols)


def _gelu_tanh(x):
    half = 0.5 * x
    return half + half * jnp.tanh(x * (0.7978845608028654 + 0.035677408136300125 * (x * x)))


def _sigmoid(x):
    return 0.5 * jnp.tanh(0.5 * x) + 0.5


def _sqrt_one_minus_exp2(x):
    t = jnp.tanh(x)
    return jnp.sqrt(-2.0 * t / (1.0 - t))


def _const_spec(shape, grid_rank):
    zeros = (0,) * len(shape)
    if grid_rank == 1:
        return pl.BlockSpec(shape, lambda i: zeros, pipeline_mode=pl.Buffered(1))
    return pl.BlockSpec(shape, lambda i, j: zeros, pipeline_mode=pl.Buffered(1))


def _params(*sem):
    return pltpu.CompilerParams(dimension_semantics=sem, vmem_limit_bytes=VMEM_LIMIT_BYTES)


def _even_in_kernel(*refs, tm, prompt):
    if prompt:
        (x_ref, tab_ref, gmix_ref, w1_ref, gq_ref, segq_ref, eq_ref, gckv_ref, gkr_ref,
         wuk_ref, segk_ref, ek_ref, gk_ref, wuv_ref, wpool_ref, pscale_ref,
         q_ref, c_ref, krp_ref, k_ref, v_ref, pool_ref, utail_ref, halo_ref) = refs
    else:
        (x_ref, tab_ref, gmix_ref, w1_ref, gq_ref, segq_ref, eq_ref, gckv_ref, gkr_ref,
         q_ref, c_ref, krp_ref, u_ref) = refs

    xn = _rms(x_ref[...], gmix_ref[...]).astype(BF16)
    z = _dot(xn, w1_ref[...])
    u = z[:, 0:POOL_DIM]
    qz = z[:, POOL_DIM:POOL_DIM + QP_DIM]
    cz = z[:, POOL_DIM + QP_DIM:POOL_DIM + QP_DIM + KV_RANK]
    krz = z[:, POOL_DIM + QP_DIM + KV_RANK:]

    ta = tab_ref[:, 0:LANES]
    tb = tab_ref[:, LANES:2 * LANES]
    tc = tab_ref[:, 2 * LANES:3 * LANES]

    def rope(blk):
        return blk * ta + pltpu.roll(blk, HALF, 1) * tb + pltpu.roll(blk, LANES - HALF, 1) * tc

    msq = _dot((qz * qz).astype(BF16), segq_ref[...])
    qn = qz * _expand(lax.rsqrt(msq + NORM_EPS), eq_ref[...]) * gq_ref[...]
    for h in range(N_HEADS):
        lanes = slice(h * HEAD_PAD, (h + 1) * HEAD_PAD)
        q_ref[:, lanes] = rope(qn[:, lanes]).astype(q_ref.dtype)

    c = _rms(cz, gckv_ref[...])
    c_ref[...] = c
    mskr = jnp.sum(krz * krz, axis=-1, keepdims=True) * (1.0 / QK_ROPE_DIM)
    krr = rope(krz * lax.rsqrt(mskr + NORM_EPS) * gkr_ref[...])
    if prompt:
        krp_ref[...] = pltpu.roll(krr, LANES - ROPE_LANE0, 1)[:, 0:QK_ROPE_DIM]
    else:
        krp_ref[...] = krr

    if not prompt:
        u_ref[...] = u
        return

    cb = c.astype(BF16)
    kn = _dot(cb, wuk_ref[...])
    msk = _dot((kn * kn).astype(BF16), segk_ref[...])
    knn = kn * _expand(lax.rsqrt(msk + NORM_EPS), ek_ref[...]) * gk_ref[...]
    for h in range(N_HEADS):
        lanes = slice(h * HEAD_PAD, (h + 1) * HEAD_PAD)
        k_ref[:, lanes] = (knn[:, lanes] + krr).astype(BF16)
    ln = lax.broadcasted_iota(jnp.int32, (1, QP_DIM), 1)
    one_lane = (ln & (HEAD_PAD - 1)) + ((ln // HEAD_PAD) & 1) * V_HEAD_DIM == V_HEAD_DIM
    v_ref[...] = (_dot(cb, wuv_ref[...]) + jnp.where(one_lane, 1.0, 0.0)).astype(BF16)

    i = pl.program_id(1)
    sub = SUBLANES

    @pl.when(i == 0)
    def _():
        halo_ref[...] = jnp.zeros(halo_ref.shape, F32)

    rowmod = lax.broadcasted_iota(jnp.int32, (tm, POOL_GROUP_DIM), 0) & (sub - 1)
    pos = lax.broadcasted_iota(jnp.int32, (tm, POOL_GROUP_DIM), 0) + i * tm
    for g, w in enumerate(POOL_WINDOWS):
        cols = slice(g * POOL_GROUP_DIM, (g + 1) * POOL_GROUP_DIM)
        acc = u[:, cols]
        level, shift = 0, 1
        while shift < w:
            tail = halo_ref[level, :, cols]
            halo_ref[level, :, cols] = acc[tm - sub:, :]
            if shift < sub:
                rot = _group_roll(acc, shift)
                prev = jnp.concatenate([pltpu.roll(tail, shift, 0), rot[:tm - sub, :]], axis=0)
                acc = acc + jnp.where(rowmod < shift, prev, rot)
            else:
                acc = acc + jnp.concatenate([tail, acc[:tm - sub, :]], axis=0)
            level, shift = level + 1, 2 * shift
        cnt = jnp.minimum(pos + 1, w).astype(F32)
        d = acc / cnt - u[:, cols]
        y = _dot(d.astype(BF16), wpool_ref[g]) * pscale_ref[:, cols]
        pool_ref[:, cols] = y.astype(BF16)

    @pl.when(i == pl.num_programs(1) - 1)
    def _():
        utail_ref[...] = u[tm - 2 * sub:, :]


def _seg_mats():
    segq = np.zeros((QP_DIM, LANES), np.float32)
    eq = np.zeros((LANES, QP_DIM), np.float32)
    segk = np.zeros((QP_DIM, LANES), np.float32)
    ek = np.zeros((LANES, QP_DIM), np.float32)
    for h in range(N_HEADS):
        b = h * HEAD_PAD
        for dup in (0, SEG_DUP):
            segq[b:b + QK_NOPE_DIM, dup + 2 * h] = 1.0 / QK_NOPE_DIM
            segq[b + QK_NOPE_DIM:b + QK_HEAD_DIM, dup + 2 * h + 1] = 1.0 / QK_ROPE_DIM
            eq[dup + 2 * h, b:b + QK_NOPE_DIM] = 1.0
            eq[dup + 2 * h + 1, b + QK_NOPE_DIM:b + QK_HEAD_DIM] = 1.0
            segk[b:b + QK_NOPE_DIM, dup + h] = 1.0 / QK_NOPE_DIM
            ek[dup + h, b:b + QK_NOPE_DIM] = 1.0
    return [jnp.asarray(m, BF16) for m in (segq, eq, segk, ek)]


def _rope_table(pos):
    inv = ROPE_BASE ** (-jnp.arange(HALF, dtype=F32) / HALF)
    ang = pos.astype(F32)[:, None] * inv[None, :]
    cos, sin = jnp.cos(ang), jnp.sin(ang)
    n = pos.shape[0]
    one = jnp.ones((n, ROPE_LANE0), F32)
    zero = jnp.zeros((n, ROPE_LANE0), F32)
    zh = jnp.zeros((n, HALF), F32)
    tail1 = jnp.ones((n, LANES - ROPE_LANE0 - QK_ROPE_DIM), F32)
    tail0 = jnp.zeros((n, LANES - ROPE_LANE0 - QK_ROPE_DIM), F32)
    ta = jnp.concatenate([one, cos, cos, tail1], axis=1)
    tb = jnp.concatenate([zero, zh, sin, tail0], axis=1)
    tc = jnp.concatenate([zero, -sin, zh, tail0], axis=1)
    return jnp.concatenate([ta, tb, tc], axis=1)


def _head_pad(w, lead):
    d = w.shape[-1]
    return jnp.pad(w, ((0, 0), (0, 0), (0, HEAD_PAD - d))).reshape(lead, QP_DIM)


def _head_row(parts):
    row = jnp.concatenate(parts)
    row = jnp.pad(row, (0, HEAD_PAD - row.shape[0]))
    return jnp.tile(row, N_HEADS)[None, :]


def _even_weights(norm_mix, w_in, g_q_nope, g_q_rope, g_ckv, g_k_rope, g_k_nope, w_uk, w_uv, w_pool, pool_scale):
    w_in = w_in.astype(BF16)
    wq = _head_pad(w_in[:, POOL_DIM:POOL_DIM + Q_DIM].reshape(D_MODEL, N_HEADS, QK_HEAD_DIM), D_MODEL)
    wc = w_in[:, POOL_DIM + Q_DIM:POOL_DIM + Q_DIM + KV_RANK]
    wkr = jnp.pad(w_in[:, POOL_DIM + Q_DIM + KV_RANK:], ((0, 0), (ROPE_LANE0, HEAD_PAD - QK_HEAD_DIM)))
    w1 = jnp.concatenate([w_in[:, :POOL_DIM], wq, wc, wkr], axis=1)
    gq = _head_row([g_q_nope, g_q_rope]) * (SOFTMAX_SCALE * LOG2_E)
    gkr = _head_row([jnp.zeros((ROPE_LANE0,), F32), g_k_rope])[:, :HEAD_PAD]
    gk = _head_row([g_k_nope])
    wuk = _head_pad(w_uk, KV_RANK).astype(BF16)
    v_even = jnp.pad(w_uv, ((0, 0), (0, 0), (0, HEAD_PAD - V_HEAD_DIM)))
    v_odd = jnp.pad(w_uv, ((0, 0), (0, 0), (HEAD_PAD - V_HEAD_DIM, 0)))
    odd = (jnp.arange(N_HEADS) % 2 == 1)[None, :, None]
    wuv = jnp.where(odd, v_odd, v_even).reshape(KV_RANK, QP_DIM).astype(BF16)
    return dict(gmix=norm_mix[None, :], w1=w1, gq=gq, gckv=g_ckv[None, :], gkr=gkr, gk=gk, wuk=wuk, wuv=wuv,
                wpool=w_pool.astype(BF16), pscale=pool_scale[None, :])


def _even_in_prompt(x, tab, ew, tm):
    nb, seq, _ = x.shape
    nt = seq // tm
    segq, eq, segk, ek = _seg_mats()
    row = lambda width: pl.BlockSpec((None, tm, width), lambda b, i: (b, i, 0))
    cs = lambda a: _const_spec(a.shape, 2)
    consts = [ew['gmix'], ew['w1'], ew['gq'], segq, eq, ew['gckv'], ew['gkr'],
              ew['wuk'], segk, ek, ew['gk'], ew['wuv'], ew['wpool'], ew['pscale']]
    out_shape = (
        jax.ShapeDtypeStruct((nb, seq, QP_DIM), BF16),
        jax.ShapeDtypeStruct((nb, seq, KV_RANK), F32),
        jax.ShapeDtypeStruct((nb, seq, QK_ROPE_DIM), F32),
        jax.ShapeDtypeStruct((nb, seq, QP_DIM), BF16),
        jax.ShapeDtypeStruct((nb, seq, QP_DIM), BF16),
        jax.ShapeDtypeStruct((nb, seq, POOL_DIM), BF16),
        jax.ShapeDtypeStruct((nb, 2 * SUBLANES, POOL_DIM), F32),
    )
    out_specs = (row(QP_DIM), row(KV_RANK), row(QK_ROPE_DIM), row(QP_DIM), row(QP_DIM), row(POOL_DIM),
                 pl.BlockSpec((None, 2 * SUBLANES, POOL_DIM), lambda b, i: (b, 0, 0)))
    return pl.pallas_call(
        functools.partial(_even_in_kernel, tm=tm, prompt=True),
        grid=(nb, nt),
        in_specs=[row(D_MODEL), pl.BlockSpec((tm, 3 * LANES), lambda b, i: (i, 0))] + [cs(a) for a in consts],
        out_specs=out_specs,
        out_shape=out_shape,
        scratch_shapes=[pltpu.VMEM((max(POOL_WINDOWS).bit_length() - 1, SUBLANES, POOL_DIM), F32)],
        compiler_params=_params("arbitrary", "arbitrary"),
        name="even_in_prompt",
    )(x, tab, *consts)


def _even_in_sample(x, tab, ew, tm):
    rows = x.shape[0]
    segq, eq, _, _ = _seg_mats()
    row = lambda width: pl.BlockSpec((tm, width), lambda i: (i, 0))
    cs = lambda a: _const_spec(a.shape, 1)
    consts = [ew['gmix'], ew['w1'], ew['gq'], segq, eq, ew['gckv'], ew['gkr']]
    out_shape = (
        jax.ShapeDtypeStruct((rows, QP_DIM), F32),
        jax.ShapeDtypeStruct((rows, KV_RANK), F32),
        jax.ShapeDtypeStruct((rows, HEAD_PAD), F32),
        jax.ShapeDtypeStruct((rows, POOL_DIM), F32),
    )
    return pl.pallas_call(
        functools.partial(_even_in_kernel, tm=tm, prompt=False),
        grid=(rows // tm,),
        in_specs=[row(D_MODEL), row(3 * LANES)] + [cs(a) for a in consts],
        out_specs=(row(QP_DIM), row(KV_RANK), row(HEAD_PAD), row(POOL_DIM)),
        out_shape=out_shape,
        compiler_params=_params("arbitrary"),
        name="even_in_sample",
    )(x, tab, *consts)


def _prompt_attn_kernel(q_ref, k_ref, v_ref, o_ref, *, tq, tk, hpl):
    qi = pl.program_id(1)
    nsub = tq // tk
    causal = {n: lax.broadcasted_iota(jnp.int32, (n, tk), 0) >= lax.broadcasted_iota(jnp.int32, (n, tk), 1)
              for n in range(tk, tq + 1, tk)}
    lane = lax.broadcasted_iota(jnp.int32, (tq, HEAD_PAD), 1)
    for g in range(N_HEADS // hpl):
        heads = tuple(range(g * hpl, (g + 1) * hpl))
        lanes = [slice(h * HEAD_PAD, (h + 1) * HEAD_PAD) for h in heads]
        qs = [q_ref[:, ln] for ln in lanes]

        def tile(kt, carry, r0, lanes=lanes, qs=qs):
            off = pl.multiple_of(kt * tk, tk)
            lo = 0 if r0 is None else r0
            new = []
            for (m, acc), ln, qh in zip(carry, lanes, qs):
                s = _dot_nt(qh[lo:, :], k_ref[pl.ds(off, tk), ln])
                if r0 is not None:
                    s = jnp.where(causal[tq - lo], s, -jnp.inf)
                m_new = jnp.maximum(m[lo:, :], jnp.max(s, axis=-1, keepdims=True))
                p = jnp.exp2(s - m_new)
                acc_new = (jnp.exp2(m[lo:, :] - m_new) * acc[lo:, :]
                           + _dot(p.astype(BF16), v_ref[pl.ds(off, tk), ln]))
                if lo:
                    m_new = jnp.concatenate([m[:lo, :], m_new], axis=0)
                    acc_new = jnp.concatenate([acc[:lo, :], acc_new], axis=0)
                new.append((m_new, acc_new))
            return tuple(new)

        def below(jj, carry):
            for u in range(nsub):
                carry = tile(jj * nsub + u, carry, None)
            return carry

        carry = tuple((jnp.full((tq, 1), -jnp.inf, F32), jnp.zeros((tq, HEAD_PAD), F32)) for _ in heads)
        carry = lax.fori_loop(0, qi, below, carry)
        for u in range(nsub):
            carry = tile(qi * nsub + u, carry, u * tk)
        for k in range(0, hpl, 2):
            acc_e, acc_o = carry[k][1], carry[k + 1][1]
            out_e = jnp.where(lane < V_HEAD_DIM, acc_e / acc_e[:, V_HEAD_DIM:V_HEAD_DIM + 1], 0.0)
            out_o = jnp.where(lane >= V_HEAD_DIM, acc_o / acc_o[:, 0:1], 0.0)
            j = (heads[0] + k) // 2
            o_ref[:, j * LANES:(j + 1) * LANES] = (out_e + out_o).astype(BF16)


def _prompt_attn(q, k, v, tq, tk):
    nb, seq, _ = q.shape
    assert tq % tk == 0 and seq % tq == 0
    full = pl.BlockSpec((None, seq, QP_DIM), lambda b, i: (b, 0, 0), pipeline_mode=pl.Buffered(1))
    return pl.pallas_call(
        functools.partial(_prompt_attn_kernel, tq=tq, tk=tk, hpl=HEADS_PER_LOOP),
        grid=(nb, seq // tq),
        in_specs=[pl.BlockSpec((None, tq, QP_DIM), lambda b, i: (b, i, 0)), full, full],
        out_specs=pl.BlockSpec((None, tq, ATTN_OUT_DIM), lambda b, i: (b, i, 0)),
        out_shape=jax.ShapeDtypeStruct((nb, seq, ATTN_OUT_DIM), BF16),
        compiler_params=_params("arbitrary", "arbitrary"),
        name="prompt_attn",
    )(q, k, v)


def _sample_scores(cb, krt, qabs, qr, segt, wukd):
    k2 = _dot(cb, wukd)
    k2 = k2 * k2
    sumsq = (k2[:, 0:LANES] + k2[:, LANES:2 * LANES]) + (k2[:, 2 * LANES:3 * LANES] + k2[:, 3 * LANES:])
    ss = _dot_nt(segt, sumsq.astype(BF16))
    sp = _dot_nt(qabs, cb)
    rope = _dot(qr[:, 0:QK_ROPE_DIM], krt.astype(BF16))
    return sp * lax.rsqrt(ss + NORM_EPS) + rope


def _sample_prep_kernel(q_ref, cn_ref, krn_ref, gk_ref, wukp_ref, wukd_ref, segt_ref,
                        qabs_ref, qr_ref, snew_ref, *, n_tok, rows_per_step):
    ncol = N_HEADS * n_tok
    rows = rows_per_step * ncol
    lane = lax.broadcasted_iota(jnp.int32, (n_tok, LANES), 1)
    qts, qrs = [], []
    for j in range(rows_per_step):
        q = q_ref[j * n_tok:(j + 1) * n_tok, :]
        qts += [q] * N_HEADS
        qrs += [jnp.where(lane < QK_ROPE_DIM,
                          pltpu.roll(q[:, h * HEAD_PAD:(h + 1) * HEAD_PAD], LANES - ROPE_LANE0, 1), 0.0)
                for h in range(N_HEADS)]
    r = lax.broadcasted_iota(jnp.int32, (rows, QP_DIM), 0)
    ln = lax.broadcasted_iota(jnp.int32, (rows, QP_DIM), 1)
    keep = jnp.where((ln & (HEAD_PAD - 1)) < QK_NOPE_DIM, (r // n_tok) % N_HEADS, -1) == ln // HEAD_PAD
    qg = jnp.where(keep, jnp.concatenate(qts, axis=0) * gk_ref[...], 0.0).astype(BF16)
    qabs = _dot_nt(qg, wukp_ref[...]).astype(BF16)
    qr = jnp.concatenate(qrs, axis=0).astype(BF16)
    cn = cn_ref[...].astype(BF16)
    krt = pltpu.roll(krn_ref[...], LANES - ROPE_LANE0, 1).T[0:QK_ROPE_DIM, :]
    s = _sample_scores(cn, krt, qabs, qr, segt_ref[...], wukd_ref[...])
    row = lax.broadcasted_iota(jnp.int32, (rows, PAGE_SIZE), 0)
    key = lax.broadcasted_iota(jnp.int32, (rows, PAGE_SIZE), 1)
    same_row = jnp.where(key // n_tok == row // ncol, key % n_tok, n_tok)
    qabs_ref[...] = qabs
    qr_ref[...] = qr
    snew_ref[...] = jnp.where(same_row <= row % n_tok, s, -jnp.inf)


def _sample_attn_kernel(pt_ref, qabs_ref, qr_ref, snew_ref, cn_ref, wukd_ref, wuv_ref, segt_ref,
                        ckv_hbm, kr_hbm, o_ref, cbuf, krbuf, sem, s_ref, cb_ref, acc_ref,
                        *, n_tok, n_pages, sbp):
    b = pl.program_id(0)
    slot = lax.rem(b, 2)
    sb = sbp * PAGE_SIZE
    n_sb = n_pages // sbp
    ncol = N_HEADS * n_tok

    def start_pages(row, i0, sl):
        for k in range(sbp):
            page = pt_ref[row * n_pages + i0 + k]
            pltpu.make_async_copy(ckv_hbm.at[0, page], cbuf.at[sl, i0 + k], sem.at[0, sl]).start()
            pltpu.make_async_copy(kr_hbm.at[0, page], krbuf.at[sl, i0 + k], sem.at[1, sl]).start()

    @pl.when(b == 0)
    def _():
        def first(i, carry):
            start_pages(0, i * sbp, 0)
            return carry
        lax.fori_loop(0, n_sb, first, 0)

    pltpu.make_async_copy(ckv_hbm.at[0, pl.ds(0, n_pages)], cbuf.at[slot], sem.at[0, slot]).wait()
    pltpu.make_async_copy(kr_hbm.at[0, pl.ds(0, n_pages)], krbuf.at[slot], sem.at[1, slot]).wait()

    nxt = jnp.minimum(b + 1, pl.num_programs(0) - 1)

    def lane_tiles(x, op):
        out = x[:, 0:LANES]
        for j in range(1, x.shape[1] // LANES):
            out = op(out, x[:, j * LANES:(j + 1) * LANES])
        return out

    def score_block(i, m):
        start_pages(nxt, i * sbp, 1 - slot)
        off = pl.multiple_of(i * sb, sb)
        cb = cbuf[slot, pl.ds(i * sbp, sbp)].reshape(sb, KV_RANK).astype(BF16)
        cb_ref[pl.ds(off, sb), :] = cb
        krt = jnp.concatenate([krbuf[slot, i * sbp + k] for k in range(sbp)], axis=1)
        s = _sample_scores(cb, krt, qabs_ref[...], qr_ref[...], segt_ref[...], wukd_ref[...])
        s_ref[i] = s
        return jnp.maximum(m, lane_tiles(s, jnp.maximum))

    s = snew_ref[...]
    m = lax.fori_loop(0, n_sb, score_block, s, unroll=4)
    m = jnp.max(m, axis=1, keepdims=True)

    p = jnp.exp2(s - m)
    acc_ref[...] = _dot(p.astype(BF16), cn_ref[...].astype(BF16))

    def value_block(i, lp):
        off = pl.multiple_of(i * sb, sb)
        p = jnp.exp2(s_ref[i] - m)
        acc_ref[...] += _dot(p.astype(BF16), cb_ref[pl.ds(off, sb), :])
        return lp + lane_tiles(p, jnp.add)

    l = jnp.sum(lax.fori_loop(0, n_sb, value_block, p, unroll=True), axis=1, keepdims=True)

    lat = acc_ref[...] * (1.0 / l)
    zz = _dot(lat.astype(BF16), wuv_ref[...])
    lane_h = lax.broadcasted_iota(jnp.int32, (n_tok, ATTN_OUT_DIM), 1) // V_HEAD_DIM
    out = jnp.zeros((n_tok, ATTN_OUT_DIM), F32)
    for h in range(N_HEADS):
        out = out + jnp.where(lane_h == h, zz[h * n_tok:(h + 1) * n_tok, :], 0.0)
    o_ref[...] = out

    @pl.when(b == pl.num_programs(0) - 1)
    def _():
        pltpu.make_async_copy(ckv_hbm.at[0, pl.ds(0, n_pages)], cbuf.at[1 - slot], sem.at[0, 1 - slot]).wait()
        pltpu.make_async_copy(kr_hbm.at[0, pl.ds(0, n_pages)], krbuf.at[1 - slot], sem.at[1, 1 - slot]).wait()


def _sample_attn(q, c, krp, cache_ckv, cache_krope, page_table, g_k_nope, w_uk, w_uv, n_tok, sbp):
    nb, n_pages = page_table.shape
    ncol = N_HEADS * n_tok
    gk = _head_row([g_k_nope])
    wukp = _head_pad(w_uk, KV_RANK).astype(BF16)
    wukd = w_uk.transpose(0, 2, 1).reshape(KV_RANK, N_HEADS * QK_NOPE_DIM).astype(BF16)
    wuv = w_uv.reshape(KV_RANK, ATTN_OUT_DIM).astype(BF16)
    segt = np.zeros((ncol, LANES), np.float32)
    for col in range(ncol):
        segt[col, (np.arange(LANES) % N_HEADS) == col // n_tok] = 1.0 / QK_NOPE_DIM
    segt = jnp.asarray(segt, BF16)
    pt = page_table.reshape(-1)
    kr_t = jnp.swapaxes(cache_krope, 2, 3)

    rps = PAGE_SIZE // n_tok
    assert nb % rps == 0
    seg_all = jnp.tile(segt, (rps, 1))
    prow = lambda r, width: pl.BlockSpec((rps * r, width), lambda i: (i, 0))
    qabs, qr, snew = pl.pallas_call(
        functools.partial(_sample_prep_kernel, n_tok=n_tok, rows_per_step=rps),
        grid=(nb // rps,),
        in_specs=[prow(n_tok, QP_DIM), prow(n_tok, KV_RANK), prow(n_tok, HEAD_PAD)]
                 + [_const_spec(a.shape, 1) for a in (gk, wukp, wukd, seg_all)],
        out_specs=(prow(ncol, KV_RANK), prow(ncol, LANES), prow(ncol, PAGE_SIZE)),
        out_shape=(jax.ShapeDtypeStruct((nb * ncol, KV_RANK), BF16), jax.ShapeDtypeStruct((nb * ncol, LANES), BF16),
                   jax.ShapeDtypeStruct((nb * ncol, PAGE_SIZE), F32)),
        compiler_params=_params("arbitrary"),
        name="sample_prep",
    )(q, c, krp, gk, wukp, wukd, seg_all)

    rows = lambda r, width: pl.BlockSpec((r, width), lambda b, pt_ref: (b, 0))
    const = lambda a: pl.BlockSpec(a.shape, lambda b, pt_ref: (0,) * a.ndim)
    hbm = pl.BlockSpec(memory_space=pl.ANY)
    grid_spec = pltpu.PrefetchScalarGridSpec(
        num_scalar_prefetch=1,
        grid=(nb,),
        in_specs=[rows(ncol, KV_RANK), rows(ncol, LANES), rows(ncol, PAGE_SIZE),
                  pl.BlockSpec((PAGE_SIZE, KV_RANK), lambda b, pt_ref: (b // rps, 0)),
                  const(wukd), const(wuv), const(segt), hbm, hbm],
        out_specs=rows(n_tok, ATTN_OUT_DIM),
        scratch_shapes=[pltpu.VMEM((2, n_pages, PAGE_SIZE, KV_RANK), F32),
                        pltpu.VMEM((2, n_pages, QK_ROPE_DIM, PAGE_SIZE), F32),
                        pltpu.SemaphoreType.DMA((2, 2)),
                        pltpu.VMEM((n_pages // sbp, ncol, sbp * PAGE_SIZE), F32),
                        pltpu.VMEM((n_pages * PAGE_SIZE, KV_RANK), BF16),
                        pltpu.VMEM((ncol, KV_RANK), F32)],
    )
    return pl.pallas_call(
        functools.partial(_sample_attn_kernel, n_tok=n_tok, n_pages=n_pages, sbp=sbp),
        grid_spec=grid_spec,
        out_shape=jax.ShapeDtypeStruct((nb * n_tok, ATTN_OUT_DIM), F32),
        compiler_params=_params("arbitrary"),
        name="sample_attn",
    )(pt, qabs, qr, snew, c, wukd, wuv, segt, cache_ckv, kr_t)


def _pool_sample_kernel(u_ref, hist_ref, wpool_ref, pscale_ref, o_ref, *, n_tok, n_past):
    ext = [hist_ref[k] for k in range(POOL_HIST)] + [u_ref[t] for t in range(n_tok)]
    nb = u_ref.shape[1]
    for g, w in enumerate(POOL_WINDOWS):
        cols = slice(g * POOL_GROUP_DIM, (g + 1) * POOL_GROUP_DIM)
        ds = []
        for t in range(n_tok):
            acc = ext[POOL_HIST + t][:, cols]
            for k in range(1, w):
                acc = acc + ext[POOL_HIST + t - k][:, cols]
            ds.append(acc / float(min(n_past + t + 1, w)) - ext[POOL_HIST + t][:, cols])
        d = jnp.concatenate(ds, axis=0).astype(BF16)
        y = (_dot(d, wpool_ref[g]) * pscale_ref[:, cols]).astype(BF16)
        for t in range(n_tok):
            o_ref[t, :, cols] = y[t * nb:(t + 1) * nb, :]


def _pool_sample(u_tm, hist_tm, w_pool, pscale, n_past):
    n_tok, nb, _ = u_tm.shape
    return pl.pallas_call(
        functools.partial(_pool_sample_kernel, n_tok=n_tok, n_past=n_past),
        out_shape=jax.ShapeDtypeStruct((n_tok, nb, POOL_DIM), BF16),
        compiler_params=pltpu.CompilerParams(vmem_limit_bytes=VMEM_LIMIT_BYTES),
        name="pool_sample",
    )(u_tm, hist_tm, w_pool, pscale)


def _mix_mlp_kernel(xp_ref, a1p_ref, a2p_ref, xs_ref, a1s_ref, a2s_ref, wo1_ref, wo2_ref, gffn_ref, wup_ref, wdn_ref,
                    yp_ref, ys_ref, *, ck, n_p):
    def rows(x_ref, a1_ref, a2_ref, y_ref):
        y_ref[...] = (x_ref[...] + _dot(a1_ref[...].astype(BF16), wo1_ref[...])
                      + _dot(a2_ref[...].astype(BF16), wo2_ref[...]))
        xn = _rms(y_ref[...], gffn_ref[...]).astype(BF16)
        for c in range(D_FF // ck):
            h = jnp.maximum(_dot(xn, wup_ref[:, c * ck:(c + 1) * ck]), 0.0)
            y_ref[...] += _dot((h * h).astype(BF16), wdn_ref[c * ck:(c + 1) * ck, :])

    i = pl.program_id(0)

    @pl.when(i < n_p)
    def _():
        rows(xp_ref, a1p_ref, a2p_ref, yp_ref)

    @pl.when(i >= n_p)
    def _():
        rows(xs_ref, a1s_ref, a2s_ref, ys_ref)


def _mix_mlp(xp, a1p, a2p, xs, a1s, a2s, a1_col, a2_col, w_out, g_ffn, w_up, w_down, tm_p, tm_s, ck=1024):
    half = D_MODEL // 2
    n_p, n_s = xp.shape[0] // tm_p, xs.shape[0] // tm_s
    wo = w_out.astype(BF16)
    consts = [wo[:half], wo[half:], g_ffn[None, :], w_up.astype(BF16), w_down.astype(BF16)]
    p_spec = lambda width, col: pl.BlockSpec((tm_p, width), lambda i: (jnp.minimum(i, n_p - 1), col))
    s_spec = lambda width, col: pl.BlockSpec((tm_s, width), lambda i: (jnp.maximum(i - n_p, 0), col))
    return pl.pallas_call(
        functools.partial(_mix_mlp_kernel, ck=ck, n_p=n_p),
        grid=(n_p + n_s,),
        in_specs=[p_spec(D_MODEL, 0), p_spec(half, a1_col), p_spec(half, a2_col),
                  s_spec(D_MODEL, 0), s_spec(half, a1_col), s_spec(half, a2_col)]
                 + [_const_spec(a.shape, 1) for a in consts],
        out_specs=(p_spec(D_MODEL, 0), s_spec(D_MODEL, 0)),
        out_shape=(jax.ShapeDtypeStruct(xp.shape, F32), jax.ShapeDtypeStruct(xs.shape, F32)),
        compiler_params=_params("arbitrary"),
        name="mix_mlp",
    )(xp, a1p, a2p, xs, a1s, a2s, *consts)


def _gates(v, wg_ref, bga_ref, bgx_ref, lam_ref):
    vb = v.astype(BF16)
    pair = 2 * RNN_BLOCK_DIM
    ga, gx = [], []
    for p in range(RNN_BLOCKS // 2):
        g = _dot(vb[:, p * pair:(p + 1) * pair], wg_ref[p])
        ga.append(g[:, :pair])
        gx.append(g[:, pair:])
    r = _sigmoid(jnp.concatenate(ga, axis=1) + bga_ref[...])
    ig = _sigmoid(jnp.concatenate(gx, axis=1) + bgx_ref[...])
    nl = -lam_ref[...]
    softplus = jnp.maximum(nl, 0.0) + jnp.log1p(jnp.exp(-jnp.abs(nl)))
    log_a = (-LRU_C) * r * softplus
    return log_a, ig


def _odd_prompt_kernel(x_ref, gmix_ref, win_ref, cw_ref, cb_ref, wg_ref, bga_ref, bgx_ref, lam_ref,
                       y_ref, ctail_ref, hlast_ref, uext_ref, a_ref, b_ref, hcar_ref, *, tm):
    i = pl.program_id(1)
    sub = SUBLANES

    @pl.when(i == 0)
    def _():
        uext_ref[...] = jnp.zeros((sub, RNN_DIM), F32)
        hcar_ref[...] = jnp.zeros((sub, RNN_DIM), F32)

    xn = _rms(x_ref[...], gmix_ref[...]).astype(BF16)
    z = _dot(xn, win_ref[...])
    gate = z[:, :RNN_DIM]
    u = z[:, RNN_DIM:]
    rowmod = lax.broadcasted_iota(jnp.int32, (tm, RNN_DIM), 0) & (sub - 1)
    v = cb_ref[...]
    for k in range(CONV_WIDTH):
        back = CONV_WIDTH - 1 - k
        if back == 0:
            uk = u
        else:
            rot = _group_roll(u, back)
            prev = jnp.concatenate([pltpu.roll(uext_ref[...], back, 0), rot[:tm - sub, :]], axis=0)
            uk = jnp.where(rowmod < back, prev, rot)
        v = v + uk * cw_ref[k:k + 1, :]
    uext_ref[...] = u[tm - sub:, :]

    log_a, ig = _gates(v, wg_ref, bga_ref, bgx_ref, lam_ref)
    a = jnp.exp(log_a)
    row = lax.broadcasted_iota(jnp.int32, (tm, RNN_DIM), 0)
    mult = jnp.where(row + i * tm == 0, 1.0, _sqrt_one_minus_exp2(log_a))
    b = mult * ig * v

    for s in (1, 2, 4):
        ok = rowmod >= s
        b = jnp.where(ok, a * _group_roll(b, s), 0.0) + b
        a = jnp.where(ok, a * _group_roll(a, s), a)
    a_ref[...] = a
    b_ref[...] = b

    def group(g, hb):
        off = pl.multiple_of(g * sub, sub)
        hg = a_ref[pl.ds(off, sub), :] * hb + b_ref[pl.ds(off, sub), :]
        b_ref[pl.ds(off, sub), :] = hg
        return jnp.broadcast_to(hg[sub - 1:sub, :], (sub, RNN_DIM))

    hb = lax.fori_loop(0, tm // sub, group, hcar_ref[...])
    hcar_ref[...] = hb
    y_ref[...] = (_gelu_tanh(gate) * b_ref[...]).astype(BF16)

    @pl.when(i == pl.num_programs(1) - 1)
    def _():
        ctail_ref[...] = uext_ref[...]
        hlast_ref[...] = hb


def _odd_weights(norm_mix, w_in, conv_w, conv_b, w_ga, b_ga, w_gx, b_gx, lam):
    def pairs(w):
        z = jnp.zeros((RNN_BLOCKS // 2, RNN_BLOCK_DIM, RNN_BLOCK_DIM), F32)
        top = jnp.concatenate([w[0::2], z], axis=2)
        bot = jnp.concatenate([z, w[1::2]], axis=2)
        return jnp.concatenate([top, bot], axis=1)
    wg = jnp.concatenate([pairs(w_ga), pairs(w_gx)], axis=2).astype(BF16)
    return [norm_mix[None, :], w_in.astype(BF16), conv_w, conv_b[None, :], wg, b_ga[None, :], b_gx[None, :],
            lam[None, :]]


def _odd_prompt(x, ow, tm):
    nb, seq, _ = x.shape
    row = pl.BlockSpec((None, tm, D_MODEL), lambda b, i: (b, i, 0))
    tail = pl.BlockSpec((None, SUBLANES, RNN_DIM), lambda b, i: (b, 0, 0))
    return pl.pallas_call(
        functools.partial(_odd_prompt_kernel, tm=tm),
        grid=(nb, seq // tm),
        in_specs=[row] + [_const_spec(a.shape, 2) for a in ow],
        out_specs=(row, tail, tail),
        out_shape=(jax.ShapeDtypeStruct((nb, seq, RNN_DIM), BF16),
                   jax.ShapeDtypeStruct((nb, SUBLANES, RNN_DIM), F32),
                   jax.ShapeDtypeStruct((nb, SUBLANES, RNN_DIM), F32)),
        scratch_shapes=[pltpu.VMEM((SUBLANES, RNN_DIM), F32), pltpu.VMEM((tm, RNN_DIM), F32),
                        pltpu.VMEM((tm, RNN_DIM), F32), pltpu.VMEM((SUBLANES, RNN_DIM), F32)],
        compiler_params=_params("arbitrary", "arbitrary"),
        name="odd_prompt",
    )(x, *ow)


def _odd_sample_kernel(x_ref, ch_ref, h0_ref, gmix_ref, win_ref, cw_ref, cb_ref, wg_ref, bga_ref, bgx_ref, lam_ref,
                       y_ref, ctail_ref, hlast_ref, *, n_tok, n_past):
    nb = x_ref.shape[1]
    x = x_ref[...].reshape(n_tok * nb, D_MODEL)
    z = _dot(_rms(x, gmix_ref[...]).astype(BF16), win_ref[...])
    gate = z[:, :RNN_DIM]
    ext = [ch_ref[k] for k in range(CONV_WIDTH - 1)] + [z[t * nb:(t + 1) * nb, RNN_DIM:] for t in range(n_tok)]
    vs = []
    for t in range(n_tok):
        v = cb_ref[...]
        for k in range(CONV_WIDTH):
            v = v + ext[t + k] * cw_ref[k:k + 1, :]
        vs.append(v)
    v = jnp.concatenate(vs, axis=0)
    log_a, ig = _gates(v, wg_ref, bga_ref, bgx_ref, lam_ref)
    a = jnp.exp(log_a)
    mult = _sqrt_one_minus_exp2(log_a)
    if n_past == 0:
        first = lax.broadcasted_iota(jnp.int32, mult.shape, 0) < nb
        mult = jnp.where(first, 1.0, mult)
    b = mult * ig * v
    h = h0_ref[...]
    hs = []
    for t in range(n_tok):
        h = a[t * nb:(t + 1) * nb, :] * h + b[t * nb:(t + 1) * nb, :]
        hs.append(h)
    y = (_gelu_tanh(gate) * jnp.concatenate(hs, axis=0)).astype(BF16)
    y_ref[...] = y.reshape(n_tok, nb, RNN_DIM)
    for k in range(CONV_WIDTH - 1):
        ctail_ref[k] = ext[n_tok + k]
    hlast_ref[...] = h


def _odd_sample(x_tm, ch_tm, h0, ow, n_past, bb=32):
    n_tok, nb, _ = x_tm.shape
    blk = lambda t, w: pl.BlockSpec((t, bb, w), lambda i: (0, i, 0))
    return pl.pallas_call(
        functools.partial(_odd_sample_kernel, n_tok=n_tok, n_past=n_past),
        grid=(nb // bb,),
        in_specs=[blk(n_tok, D_MODEL), blk(CONV_WIDTH - 1, RNN_DIM), pl.BlockSpec((bb, RNN_DIM), lambda i: (i, 0))]
                 + [_const_spec(a.shape, 1) for a in ow],
        out_specs=(blk(n_tok, RNN_DIM), blk(CONV_WIDTH - 1, RNN_DIM), pl.BlockSpec((bb, RNN_DIM), lambda i: (i, 0))),
        out_shape=(jax.ShapeDtypeStruct((n_tok, nb, RNN_DIM), BF16),
                   jax.ShapeDtypeStruct((CONV_WIDTH - 1, nb, RNN_DIM), F32),
                   jax.ShapeDtypeStruct((nb, RNN_DIM), F32)),
        compiler_params=_params("arbitrary"),
        name="odd_sample",
    )(x_tm, ch_tm, h0, *ow)


def kernel(x_prompt, x_sample, cache_ckv, cache_krope, state_pool, state_conv, state_lru, page_table, norm_mix,
           w_in_even, g_q_nope, g_q_rope, g_ckv, g_k_rope, g_k_nope, w_uk, w_uv, w_pool, pool_scale, w_out_even,
           w_in_rnn, conv_w, conv_b, w_gate_a, b_gate_a, w_gate_x, b_gate_x, lru_lambda, w_out_rnn, norm_ffn,
           w_up, w_down):
    nb, seq, _ = x_prompt.shape
    db, n_tok, _ = x_sample.shape
    n_past = page_table.shape[1] * PAGE_SIZE
    depth = norm_mix.shape[0]
    assert depth == 2 and cache_ckv.shape[0] == 1, "one even (pool + MLA) layer followed by one odd (RG-LRU) layer"
    rope_sl = slice(ROPE_LANE0, ROPE_LANE0 + QK_ROPE_DIM)

    ew = _even_weights(norm_mix[0], w_in_even[0], g_q_nope[0], g_q_rope[0], g_ckv[0], g_k_rope[0], g_k_nope[0],
                       w_uk[0], w_uv[0], w_pool[0], pool_scale[0])
    tab_p = _rope_table(jnp.arange(seq, dtype=jnp.int32))
    tab_s = jnp.tile(_rope_table(n_past + jnp.arange(n_tok, dtype=jnp.int32)), (db, 1))

    q_p, c_p, kr_p, k_p, v_p, pool_p, utail_p = _even_in_prompt(x_prompt, tab_p, ew, tm=512)
    attn_p = _prompt_attn(q_p, k_p, v_p, tq=1024, tk=512)
    xs = x_sample.reshape(db * n_tok, D_MODEL)
    q_s, c_s, krp_s, u_s = _even_in_sample(xs, tab_s, ew, tm=256)
    attn_s = _sample_attn(q_s, c_s, krp_s, cache_ckv, cache_krope, page_table, g_k_nope[0], w_uk[0], w_uv[0],
                          n_tok=n_tok, sbp=8)
    u_s3 = u_s.reshape(db, n_tok, POOL_DIM)
    pool_s = _pool_sample(u_s3.transpose(1, 0, 2), state_pool[0].transpose(1, 0, 2), ew['wpool'], ew['pscale'],
                          n_past)
    pool_s = pool_s.transpose(1, 0, 2).reshape(db * n_tok, POOL_DIM)

    mlp0 = (w_out_even[0], norm_ffn[0], w_up[0], w_down[0])
    yp, ys = _mix_mlp(x_prompt.reshape(nb * seq, D_MODEL), pool_p.reshape(nb * seq, POOL_DIM),
                      attn_p.reshape(nb * seq, ATTN_OUT_DIM), xs, pool_s, attn_s, 0, 0, *mlp0, tm_p=512, tm_s=256)

    ow = _odd_weights(norm_mix[1], w_in_rnn[0], conv_w[0], conv_b[0], w_gate_a[0], b_gate_a[0], w_gate_x[0],
                      b_gate_x[0], lru_lambda[0])
    rnn_p, ctail_p, hlast_p = _odd_prompt(yp.reshape(nb, seq, D_MODEL), ow, tm=256)
    ys_tm = ys.reshape(db, n_tok, D_MODEL).transpose(1, 0, 2)
    rnn_s, conv_s_tm, lru_s = _odd_sample(ys_tm, state_conv[0].transpose(1, 0, 2), state_lru[0], ow, n_past)

    mlp1 = (w_out_rnn[0], norm_ffn[1], w_up[1], w_down[1])
    rnn_p2 = rnn_p.reshape(nb * seq, RNN_DIM)
    rnn_s2 = rnn_s.reshape(n_tok * db, RNN_DIM)
    yp, ys_out = _mix_mlp(yp, rnn_p2, rnn_p2, ys_tm.reshape(n_tok * db, D_MODEL), rnn_s2, rnn_s2, 0, 1, *mlp1,
                          tm_p=512, tm_s=256)
    ys_out = ys_out.reshape(n_tok, db, D_MODEL).transpose(1, 0, 2)

    pool_state_s = jnp.concatenate([state_pool[0], u_s3], axis=1)[:, -POOL_HIST:]
    return (yp.reshape(nb, seq, D_MODEL), ys_out,
            c_p[None], kr_p[None], utail_p[None, :, 1:], ctail_p[None, :, SUBLANES - CONV_WIDTH + 1:],
            hlast_p[None, :, 0],
            c_s.reshape(1, db, n_tok, KV_RANK), krp_s[:, rope_sl].reshape(1, db, n_tok, QK_ROPE_DIM),
            pool_state_s[None], conv_s_tm.transpose(1, 0, 2)[None], lru_s[None])
```

```python
import functools

import numpy as np
import jax
import jax.numpy as jnp
from jax import lax
from jax.experimental import pallas as pl
from jax.experimental.pallas import tpu as pltpu

D_MODEL = 1024
PAGE_SIZE = 128
POOL_WINDOWS = (2, 4, 8, 16)
POOL_GROUP_DIM = 128
POOL_DIM = len(POOL_WINDOWS) * POOL_GROUP_DIM
POOL_HIST = max(POOL_WINDOWS) - 1
N_HEADS = 8
QK_NOPE_DIM = 64
QK_ROPE_DIM = 32
QK_HEAD_DIM = QK_NOPE_DIM + QK_ROPE_DIM
V_HEAD_DIM = 64
KV_RANK = 256
Q_DIM = N_HEADS * QK_HEAD_DIM
ATTN_OUT_DIM = N_HEADS * V_HEAD_DIM
ROPE_BASE = 10000.0
SOFTMAX_SCALE = QK_HEAD_DIM ** -0.5
LOG2_E = 1.4426950408889634
RNN_DIM = D_MODEL
RNN_BLOCKS = 8
RNN_BLOCK_DIM = RNN_DIM // RNN_BLOCKS
CONV_WIDTH = 4
LRU_C = 8.0
D_FF = 4 * D_MODEL
NORM_EPS = 1e-6

LANES = 128
SUBLANES = 8
VMEM_LIMIT_BYTES = 56 * 2 ** 20

HEAD_PAD = LANES
QP_DIM = N_HEADS * HEAD_PAD
W1_DIM = POOL_DIM + QP_DIM + KV_RANK + HEAD_PAD
ROPE_LANE0 = QK_NOPE_DIM
HALF = QK_ROPE_DIM // 2
HEADS_PER_LOOP = 4

F32 = jnp.float32
BF16 = jnp.bfloat16


def _dot(a, b):
    return jnp.dot(a, b, preferred_element_type=F32)


def _dot_nt(a, b):
    return lax.dot_general(a, b, (((1,), (1,)), ((), ())), preferred_element_type=F32)


def _dot_tn(a, b):
    return lax.dot_general(a, b, (((0,), (0,)), ((), ())), preferred_element_type=F32)


def _rms(x, g):
    ms = jnp.mean(x * x, axis=-1, keepdims=True)
    return x * lax.rsqrt(ms + NORM_EPS) * g


SEG_DUP = LANES // 2


def _expand(rs, e):
    hi = rs.astype(BF16)
    lo = (rs - hi.astype(F32)).astype(BF16)
    lane = lax.broadcasted_iota(jnp.int32, rs.shape, 1)
    return _dot(jnp.where(lane < SEG_DUP, hi, lo), e)


def _group_roll(x, shift):
    rows, cols = x.shape
    return pltpu.roll(x.reshape(rows // SUBLANES, SUBLANES, cols), shift, 1).reshape(rows, cols)


def _gelu_tanh(x):
    half = 0.5 * x
    return half + half * jnp.tanh(x * (0.7978845608028654 + 0.035677408136300125 * (x * x)))


def _sigmoid(x):
    return 0.5 * jnp.tanh(0.5 * x) + 0.5


def _sqrt_one_minus_exp2(x):
    t = jnp.tanh(x)
    return jnp.sqrt(-2.0 * t / (1.0 - t))


def _const_spec(shape, grid_rank):
    zeros = (0,) * len(shape)
    if grid_rank == 1:
        return pl.BlockSpec(shape, lambda i: zeros, pipeline_mode=pl.Buffered(1))
    return pl.BlockSpec(shape, lambda i, j: zeros, pipeline_mode=pl.Buffered(1))


def _params(*sem):
    return pltpu.CompilerParams(dimension_semantics=sem, vmem_limit_bytes=VMEM_LIMIT_BYTES)


def _even_in_kernel(*refs, tm, prompt):
    if prompt:
        (x_ref, tab_ref, gmix_ref, w1_ref, gq_ref, segq_ref, eq_ref, gckv_ref, gkr_ref,
         wuk_ref, segk_ref, ek_ref, gk_ref, wuv_ref, wpool_ref, pscale_ref,
         q_ref, c_ref, krp_ref, k_ref, v_ref, pool_ref, utail_ref, halo_ref) = refs
    else:
        (x_ref, tab_ref, gmix_ref, w1_ref, gq_ref, segq_ref, eq_ref, gckv_ref, gkr_ref,
         q_ref, c_ref, krp_ref, u_ref) = refs

    xn = _rms(x_ref[...], gmix_ref[...]).astype(BF16)
    z = _dot(xn, w1_ref[...])
    u = z[:, 0:POOL_DIM]
    qz = z[:, POOL_DIM:POOL_DIM + QP_DIM]
    cz = z[:, POOL_DIM + QP_DIM:POOL_DIM + QP_DIM + KV_RANK]
    krz = z[:, POOL_DIM + QP_DIM + KV_RANK:]

    ta = tab_ref[:, 0:LANES]
    tb = tab_ref[:, LANES:2 * LANES]
    tc = tab_ref[:, 2 * LANES:3 * LANES]

    def rope(blk):
        return blk * ta + pltpu.roll(blk, HALF, 1) * tb + pltpu.roll(blk, LANES - HALF, 1) * tc

    msq = _dot((qz * qz).astype(BF16), segq_ref[...])
    qn = qz * _expand(lax.rsqrt(msq + NORM_EPS), eq_ref[...]) * gq_ref[...]
    for h in range(N_HEADS):
        lanes = slice(h * HEAD_PAD, (h + 1) * HEAD_PAD)
        q_ref[:, lanes] = rope(qn[:, lanes]).astype(q_ref.dtype)

    c = _rms(cz, gckv_ref[...])
    c_ref[...] = c
    mskr = jnp.sum(krz * krz, axis=-1, keepdims=True) * (1.0 / QK_ROPE_DIM)
    krr = rope(krz * lax.rsqrt(mskr + NORM_EPS) * gkr_ref[...])
    if prompt:
        krp_ref[...] = pltpu.roll(krr, LANES - ROPE_LANE0, 1)[:, 0:QK_ROPE_DIM]
    else:
        krp_ref[...] = krr

    if not prompt:
        u_ref[...] = u
        return

    cb = c.astype(BF16)
    kn = _dot(cb, wuk_ref[...])
    msk = _dot((kn * kn).astype(BF16), segk_ref[...])
    knn = kn * _expand(lax.rsqrt(msk + NORM_EPS), ek_ref[...]) * gk_ref[...]
    for h in range(N_HEADS):
        lanes = slice(h * HEAD_PAD, (h + 1) * HEAD_PAD)
        k_ref[:, lanes] = (knn[:, lanes] + krr).astype(BF16)
    ln = lax.broadcasted_iota(jnp.int32, (1, QP_DIM), 1)
    one_lane = (ln & (HEAD_PAD - 1)) + ((ln // HEAD_PAD) & 1) * V_HEAD_DIM == V_HEAD_DIM
    v_ref[...] = (_dot(cb, wuv_ref[...]) + jnp.where(one_lane, 1.0, 0.0)).astype(BF16)

    i = pl.program_id(1)
    sub = SUBLANES

    @pl.when(i == 0)
    def _():
        halo_ref[...] = jnp.zeros(halo_ref.shape, F32)

    rowmod = lax.broadcasted_iota(jnp.int32, (tm, POOL_GROUP_DIM), 0) & (sub - 1)
    pos = lax.broadcasted_iota(jnp.int32, (tm, POOL_GROUP_DIM), 0) + i * tm
    for g, w in enumerate(POOL_WINDOWS):
        cols = slice(g * POOL_GROUP_DIM, (g + 1) * POOL_GROUP_DIM)
        acc = u[:, cols]
        level, shift = 0, 1
        while shift < w:
            tail = halo_ref[level, :, cols]
            halo_ref[level, :, cols] = acc[tm - sub:, :]
            if shift < sub:
                rot = _group_roll(acc, shift)
                prev = jnp.concatenate([pltpu.roll(tail, shift, 0), rot[:tm - sub, :]], axis=0)
                acc = acc + jnp.where(rowmod < shift, prev, rot)
            else:
                acc = acc + jnp.concatenate([tail, acc[:tm - sub, :]], axis=0)
            level, shift = level + 1, 2 * shift
        cnt = jnp.minimum(pos + 1, w).astype(F32)
        d = acc / cnt - u[:, cols]
        y = _dot(d.astype(BF16), wpool_ref[g]) * pscale_ref[:, cols]
        pool_ref[:, cols] = y.astype(BF16)

    @pl.when(i == pl.num_programs(1) - 1)
    def _():
        utail_ref[...] = u[tm - 2 * sub:, :]


def _seg_mats():
    segq = np.zeros((QP_DIM, LANES), np.float32)
    eq = np.zeros((LANES, QP_DIM), np.float32)
    segk = np.zeros((QP_DIM, LANES), np.float32)
    ek = np.zeros((LANES, QP_DIM), np.float32)
    for h in range(N_HEADS):
        b = h * HEAD_PAD
        for dup in (0, SEG_DUP):
            segq[b:b + QK_NOPE_DIM, dup + 2 * h] = 1.0 / QK_NOPE_DIM
            segq[b + QK_NOPE_DIM:b + QK_HEAD_DIM, dup + 2 * h + 1] = 1.0 / QK_ROPE_DIM
            eq[dup + 2 * h, b:b + QK_NOPE_DIM] = 1.0
            eq[dup + 2 * h + 1, b + QK_NOPE_DIM:b + QK_HEAD_DIM] = 1.0
            segk[b:b + QK_NOPE_DIM, dup + h] = 1.0 / QK_NOPE_DIM
            ek[dup + h, b:b + QK_NOPE_DIM] = 1.0
    return [jnp.asarray(m, BF16) for m in (segq, eq, segk, ek)]


def _rope_table(pos):
    inv = ROPE_BASE ** (-jnp.arange(HALF, dtype=F32) / HALF)
    ang = pos.astype(F32)[:, None] * inv[None, :]
    cos, sin = jnp.cos(ang), jnp.sin(ang)
    n = pos.shape[0]
    one = jnp.ones((n, ROPE_LANE0), F32)
    zero = jnp.zeros((n, ROPE_LANE0), F32)
    zh = jnp.zeros((n, HALF), F32)
    tail1 = jnp.ones((n, LANES - ROPE_LANE0 - QK_ROPE_DIM), F32)
    tail0 = jnp.zeros((n, LANES - ROPE_LANE0 - QK_ROPE_DIM), F32)
    ta = jnp.concatenate([one, cos, cos, tail1], axis=1)
    tb = jnp.concatenate([zero, zh, sin, tail0], axis=1)
    tc = jnp.concatenate([zero, -sin, zh, tail0], axis=1)
    return jnp.concatenate([ta, tb, tc], axis=1)


def _head_pad(w, lead):
    d = w.shape[-1]
    return jnp.pad(w, ((0, 0), (0, 0), (0, HEAD_PAD - d))).reshape(lead, QP_DIM)


def _head_row(parts):
    row = jnp.concatenate(parts)
    row = jnp.pad(row, (0, HEAD_PAD - row.shape[0]))
    return jnp.tile(row, N_HEADS)[None, :]


def _even_weights(norm_mix, w_in, g_q_nope, g_q_rope, g_ckv, g_k_rope, g_k_nope, w_uk, w_uv, w_pool, pool_scale):
    w_in = w_in.astype(BF16)
    wq = _head_pad(w_in[:, POOL_DIM:POOL_DIM + Q_DIM].reshape(D_MODEL, N_HEADS, QK_HEAD_DIM), D_MODEL)
    wc = w_in[:, POOL_DIM + Q_DIM:POOL_DIM + Q_DIM + KV_RANK]
    wkr = jnp.pad(w_in[:, POOL_DIM + Q_DIM + KV_RANK:], ((0, 0), (ROPE_LANE0, HEAD_PAD - QK_HEAD_DIM)))
    w1 = jnp.concatenate([w_in[:, :POOL_DIM], wq, wc, wkr], axis=1)
    gq = _head_row([g_q_nope, g_q_rope]) * (SOFTMAX_SCALE * LOG2_E)
    gkr = _head_row([jnp.zeros((ROPE_LANE0,), F32), g_k_rope])[:, :HEAD_PAD]
    gk = _head_row([g_k_nope])
    wuk = _head_pad(w_uk, KV_RANK).astype(BF16)
    v_even = jnp.pad(w_uv, ((0, 0), (0, 0), (0, HEAD_PAD - V_HEAD_DIM)))
    v_odd = jnp.pad(w_uv, ((0, 0), (0, 0), (HEAD_PAD - V_HEAD_DIM, 0)))
    odd = (jnp.arange(N_HEADS) % 2 == 1)[None, :, None]
    wuv = jnp.where(odd, v_odd, v_even).reshape(KV_RANK, QP_DIM).astype(BF16)
    return dict(gmix=norm_mix[None, :], w1=w1, gq=gq, gckv=g_ckv[None, :], gkr=gkr, gk=gk, wuk=wuk, wuv=wuv,
                wpool=w_pool.astype(BF16), pscale=pool_scale[None, :])


def _even_in_prompt(x, tab, ew, tm):
    nb, seq, _ = x.shape
    nt = seq // tm
    segq, eq, segk, ek = _seg_mats()
    row = lambda width: pl.BlockSpec((None, tm, width), lambda b, i: (b, i, 0))
    cs = lambda a: _const_spec(a.shape, 2)
    consts = [ew['gmix'], ew['w1'], ew['gq'], segq, eq, ew['gckv'], ew['gkr'],
              ew['wuk'], segk, ek, ew['gk'], ew['wuv'], ew['wpool'], ew['pscale']]
    out_shape = (
        jax.ShapeDtypeStruct((nb, seq, QP_DIM), BF16),
        jax.ShapeDtypeStruct((nb, seq, KV_RANK), F32),
        jax.ShapeDtypeStruct((nb, seq, QK_ROPE_DIM), F32),
        jax.ShapeDtypeStruct((nb, seq, QP_DIM), BF16),
        jax.ShapeDtypeStruct((nb, seq, QP_DIM), BF16),
        jax.ShapeDtypeStruct((nb, seq, POOL_DIM), BF16),
        jax.ShapeDtypeStruct((nb, 2 * SUBLANES, POOL_DIM), F32),
    )
    out_specs = (row(QP_DIM), row(KV_RANK), row(QK_ROPE_DIM), row(QP_DIM), row(QP_DIM), row(POOL_DIM),
                 pl.BlockSpec((None, 2 * SUBLANES, POOL_DIM), lambda b, i: (b, 0, 0)))
    return pl.pallas_call(
        functools.partial(_even_in_kernel, tm=tm, prompt=True),
        grid=(nb, nt),
        in_specs=[row(D_MODEL), pl.BlockSpec((tm, 3 * LANES), lambda b, i: (i, 0))] + [cs(a) for a in consts],
        out_specs=out_specs,
        out_shape=out_shape,
        scratch_shapes=[pltpu.VMEM((max(POOL_WINDOWS).bit_length() - 1, SUBLANES, POOL_DIM), F32)],
        compiler_params=_params("arbitrary", "arbitrary"),
        name="even_in_prompt",
    )(x, tab, *consts)


def _even_in_sample(x, tab, ew, tm):
    rows = x.shape[0]
    segq, eq, _, _ = _seg_mats()
    row = lambda width: pl.BlockSpec((tm, width), lambda i: (i, 0))
    cs = lambda a: _const_spec(a.shape, 1)
    consts = [ew['gmix'], ew['w1'], ew['gq'], segq, eq, ew['gckv'], ew['gkr']]
    out_shape = (
        jax.ShapeDtypeStruct((rows, QP_DIM), F32),
        jax.ShapeDtypeStruct((rows, KV_RANK), F32),
        jax.ShapeDtypeStruct((rows, HEAD_PAD), F32),
        jax.ShapeDtypeStruct((rows, POOL_DIM), F32),
    )
    return pl.pallas_call(
        functools.partial(_even_in_kernel, tm=tm, prompt=False),
        grid=(rows // tm,),
        in_specs=[row(D_MODEL), row(3 * LANES)] + [cs(a) for a in consts],
        out_specs=(row(QP_DIM), row(KV_RANK), row(HEAD_PAD), row(POOL_DIM)),
        out_shape=out_shape,
        compiler_params=_params("arbitrary"),
        name="even_in_sample",
    )(x, tab, *consts)


def _prompt_attn_kernel(q_ref, k_ref, v_ref, o_ref, *, tq, tk, hpl):
    qi = pl.program_id(1)
    nsub = tq // tk
    causal = {n: lax.broadcasted_iota(jnp.int32, (n, tk), 0) >= lax.broadcasted_iota(jnp.int32, (n, tk), 1)
              for n in range(tk, tq + 1, tk)}
    lane = lax.broadcasted_iota(jnp.int32, (tq, HEAD_PAD), 1)
    for g in range(N_HEADS // hpl):
        heads = tuple(range(g * hpl, (g + 1) * hpl))
        lanes = [slice(h * HEAD_PAD, (h + 1) * HEAD_PAD) for h in heads]
        qs = [q_ref[:, ln] for ln in lanes]

        def tile(kt, carry, r0, lanes=lanes, qs=qs):
            off = pl.multiple_of(kt * tk, tk)
            lo = 0 if r0 is None else r0
            new = []
            for (m, acc), ln, qh in zip(carry, lanes, qs):
                s = _dot_nt(qh[lo:, :], k_ref[pl.ds(off, tk), ln])
                if r0 is not None:
                    s = jnp.where(causal[tq - lo], s, -jnp.inf)
                m_new = jnp.maximum(m[lo:, :], jnp.max(s, axis=-1, keepdims=True))
                p = jnp.exp2(s - m_new)
                acc_new = (jnp.exp2(m[lo:, :] - m_new) * acc[lo:, :]
                           + _dot(p.astype(BF16), v_ref[pl.ds(off, tk), ln]))
                if lo:
                    m_new = jnp.concatenate([m[:lo, :], m_new], axis=0)
                    acc_new = jnp.concatenate([acc[:lo, :], acc_new], axis=0)
                new.append((m_new, acc_new))
            return tuple(new)

        def below(jj, carry):
            for u in range(nsub):
                carry = tile(jj * nsub + u, carry, None)
            return carry

        carry = tuple((jnp.full((tq, 1), -jnp.inf, F32), jnp.zeros((tq, HEAD_PAD), F32)) for _ in heads)
        carry = lax.fori_loop(0, qi, below, carry)
        for u in range(nsub):
            carry = tile(qi * nsub + u, carry, u * tk)
        for k in range(0, hpl, 2):
            acc_e, acc_o = carry[k][1], carry[k + 1][1]
            out_e = jnp.where(lane < V_HEAD_DIM, acc_e / acc_e[:, V_HEAD_DIM:V_HEAD_DIM + 1], 0.0)
            out_o = jnp.where(lane >= V_HEAD_DIM, acc_o / acc_o[:, 0:1], 0.0)
            j = (heads[0] + k) // 2
            o_ref[:, j * LANES:(j + 1) * LANES] = (out_e + out_o).astype(BF16)


def _prompt_attn(q, k, v, tq, tk):
    nb, seq, _ = q.shape
    assert tq % tk == 0 and seq % tq == 0
    full = pl.BlockSpec((None, seq, QP_DIM), lambda b, i: (b, 0, 0), pipeline_mode=pl.Buffered(1))
    return pl.pallas_call(
        functools.partial(_prompt_attn_kernel, tq=tq, tk=tk, hpl=HEADS_PER_LOOP),
        grid=(nb, seq // tq),
        in_specs=[pl.BlockSpec((None, tq, QP_DIM), lambda b, i: (b, i, 0)), full, full],
        out_specs=pl.BlockSpec((None, tq, ATTN_OUT_DIM), lambda b, i: (b, i, 0)),
        out_shape=jax.ShapeDtypeStruct((nb, seq, ATTN_OUT_DIM), BF16),
        compiler_params=_params("arbitrary", "arbitrary"),
        name="prompt_attn",
    )(q, k, v)


def _sample_scores(cb, krt, qabs, qr, segt, wukd):
    k2 = _dot(cb, wukd)
    k2 = k2 * k2
    sumsq = (k2[:, 0:LANES] + k2[:, LANES:2 * LANES]) + (k2[:, 2 * LANES:3 * LANES] + k2[:, 3 * LANES:])
    ss = _dot_nt(segt, sumsq.astype(BF16))
    sp = _dot_nt(qabs, cb)
    rope = _dot(qr[:, 0:QK_ROPE_DIM], krt.astype(BF16))
    return sp * lax.rsqrt(ss + NORM_EPS) + rope


def _sample_prep_kernel(q_ref, cn_ref, krn_ref, gk_ref, wukp_ref, wukd_ref, segt_ref,
                        qabs_ref, qr_ref, snew_ref, *, n_tok, rows_per_step):
    ncol = N_HEADS * n_tok
    rows = rows_per_step * ncol
    lane = lax.broadcasted_iota(jnp.int32, (n_tok, LANES), 1)
    qts, qrs = [], []
    for j in range(rows_per_step):
        q = q_ref[j * n_tok:(j + 1) * n_tok, :]
        qts += [q] * N_HEADS
        qrs += [jnp.where(lane < QK_ROPE_DIM,
                          pltpu.roll(q[:, h * HEAD_PAD:(h + 1) * HEAD_PAD], LANES - ROPE_LANE0, 1), 0.0)
                for h in range(N_HEADS)]
    r = lax.broadcasted_iota(jnp.int32, (rows, QP_DIM), 0)
    ln = lax.broadcasted_iota(jnp.int32, (rows, QP_DIM), 1)
    keep = jnp.where((ln & (HEAD_PAD - 1)) < QK_NOPE_DIM, (r // n_tok) % N_HEADS, -1) == ln // HEAD_PAD
    qg = jnp.where(keep, jnp.concatenate(qts, axis=0) * gk_ref[...], 0.0).astype(BF16)
    qabs = _dot_nt(qg, wukp_ref[...]).astype(BF16)
    qr = jnp.concatenate(qrs, axis=0).astype(BF16)
    cn = cn_ref[...].astype(BF16)
    krt = pltpu.roll(krn_ref[...], LANES - ROPE_LANE0, 1).T[0:QK_ROPE_DIM, :]
    s = _sample_scores(cn, krt, qabs, qr, segt_ref[...], wukd_ref[...])
    row = lax.broadcasted_iota(jnp.int32, (rows, PAGE_SIZE), 0)
    key = lax.broadcasted_iota(jnp.int32, (rows, PAGE_SIZE), 1)
    same_row = jnp.where(key // n_tok == row // ncol, key % n_tok, n_tok)
    qabs_ref[...] = qabs
    qr_ref[...] = qr
    snew_ref[...] = jnp.where(same_row <= row % n_tok, s, -jnp.inf)


def _sample_attn_kernel(pt_ref, qabs_ref, qr_ref, snew_ref, cn_ref, wukd_ref, wuv_ref, segt_ref,
                        ckv_hbm, kr_hbm, o_ref, cbuf, krbuf, sem, s_ref, cb_ref, acc_ref,
                        *, n_tok, n_pages, sbp):
    b = pl.program_id(0)
    slot = lax.rem(b, 2)
    sb = sbp * PAGE_SIZE
    n_sb = n_pages // sbp
    ncol = N_HEADS * n_tok

    def start_pages(row, i0, sl):
        for k in range(sbp):
            page = pt_ref[row * n_pages + i0 + k]
            pltpu.make_async_copy(ckv_hbm.at[0, page], cbuf.at[sl, i0 + k], sem.at[0, sl]).start()
            pltpu.make_async_copy(kr_hbm.at[0, page], krbuf.at[sl, i0 + k], sem.at[1, sl]).start()

    @pl.when(b == 0)
    def _():
        def first(i, carry):
            start_pages(0, i * sbp, 0)
            return carry
        lax.fori_loop(0, n_sb, first, 0)

    pltpu.make_async_copy(ckv_hbm.at[0, pl.ds(0, n_pages)], cbuf.at[slot], sem.at[0, slot]).wait()
    pltpu.make_async_copy(kr_hbm.at[0, pl.ds(0, n_pages)], krbuf.at[slot], sem.at[1, slot]).wait()

    nxt = jnp.minimum(b + 1, pl.num_programs(0) - 1)

    def lane_tiles(x, op):
        out = x[:, 0:LANES]
        for j in range(1, x.shape[1] // LANES):
            out = op(out, x[:, j * LANES:(j + 1) * LANES])
        return out

    def score_block(i, m):
        start_pages(nxt, i * sbp, 1 - slot)
        off = pl.multiple_of(i * sb, sb)
        cb = cbuf[slot, pl.ds(i * sbp, sbp)].reshape(sb, KV_RANK).astype(BF16)
        cb_ref[pl.ds(off, sb), :] = cb
        krt = jnp.concatenate([krbuf[slot, i * sbp + k] for k in range(sbp)], axis=1)
        s = _sample_scores(cb, krt, qabs_ref[...], qr_ref[...], segt_ref[...], wukd_ref[...])
        s_ref[i] = s
        return jnp.maximum(m, lane_tiles(s, jnp.maximum))

    s = snew_ref[...]
    m = lax.fori_loop(0, n_sb, score_block, s, unroll=4)
    m = jnp.max(m, axis=1, keepdims=True)

    p = jnp.exp2(s - m)
    acc_ref[...] = _dot(p.astype(BF16), cn_ref[...].astype(BF16))

    def value_block(i, lp):
        off = pl.multiple_of(i * sb, sb)
        p = jnp.exp2(s_ref[i] - m)
        acc_ref[...] += _dot(p.astype(BF16), cb_ref[pl.ds(off, sb), :])
        return lp + lane_tiles(p, jnp.add)

    l = jnp.sum(lax.fori_loop(0, n_sb, value_block, p, unroll=True), axis=1, keepdims=True)

    lat = acc_ref[...] * (1.0 / l)
    zz = _dot(lat.astype(BF16), wuv_ref[...])
    lane_h = lax.broadcasted_iota(jnp.int32, (n_tok, ATTN_OUT_DIM), 1) // V_HEAD_DIM
    out = jnp.zeros((n_tok, ATTN_OUT_DIM), F32)
    for h in range(N_HEADS):
        out = out + jnp.where(lane_h == h, zz[h * n_tok:(h + 1) * n_tok, :], 0.0)
    o_ref[...] = out

    @pl.when(b == pl.num_programs(0) - 1)
    def _():
        pltpu.make_async_copy(ckv_hbm.at[0, pl.ds(0, n_pages)], cbuf.at[1 - slot], sem.at[0, 1 - slot]).wait()
        pltpu.make_async_copy(kr_hbm.at[0, pl.ds(0, n_pages)], krbuf.at[1 - slot], sem.at[1, 1 - slot]).wait()


def _sample_attn(q, c, krp, cache_ckv, cache_krope, page_table, g_k_nope, w_uk, w_uv, n_tok, sbp):
    nb, n_pages = page_table.shape
    ncol = N_HEADS * n_tok
    gk = _head_row([g_k_nope])
    wukp = _head_pad(w_uk, KV_RANK).astype(BF16)
    wukd = w_uk.transpose(0, 2, 1).reshape(KV_RANK, N_HEADS * QK_NOPE_DIM).astype(BF16)
    wuv = w_uv.reshape(KV_RANK, ATTN_OUT_DIM).astype(BF16)
    segt = np.zeros((ncol, LANES), np.float32)
    for col in range(ncol):
        segt[col, (np.arange(LANES) % N_HEADS) == col // n_tok] = 1.0 / QK_NOPE_DIM
    segt = jnp.asarray(segt, BF16)
    pt = page_table.reshape(-1)
    kr_t = jnp.swapaxes(cache_krope, 2, 3)

    rps = PAGE_SIZE // n_tok
    assert nb % rps == 0
    seg_all = jnp.tile(segt, (rps, 1))
    prow = lambda r, width: pl.BlockSpec((rps * r, width), lambda i: (i, 0))
    qabs, qr, snew = pl.pallas_call(
        functools.partial(_sample_prep_kernel, n_tok=n_tok, rows_per_step=rps),
        grid=(nb // rps,),
        in_specs=[prow(n_tok, QP_DIM), prow(n_tok, KV_RANK), prow(n_tok, HEAD_PAD)]
                 + [_const_spec(a.shape, 1) for a in (gk, wukp, wukd, seg_all)],
        out_specs=(prow(ncol, KV_RANK), prow(ncol, LANES), prow(ncol, PAGE_SIZE)),
        out_shape=(jax.ShapeDtypeStruct((nb * ncol, KV_RANK), BF16), jax.ShapeDtypeStruct((nb * ncol, LANES), BF16),
                   jax.ShapeDtypeStruct((nb * ncol, PAGE_SIZE), F32)),
        compiler_params=_params("arbitrary"),
        name="sample_prep",
    )(q, c, krp, gk, wukp, wukd, seg_all)

    rows = lambda r, width: pl.BlockSpec((r, width), lambda b, pt_ref: (b, 0))
    const = lambda a: pl.BlockSpec(a.shape, lambda b, pt_ref: (0,) * a.ndim)
    hbm = pl.BlockSpec(memory_space=pl.ANY)
    grid_spec = pltpu.PrefetchScalarGridSpec(
        num_scalar_prefetch=1,
        grid=(nb,),
        in_specs=[rows(ncol, KV_RANK), rows(ncol, LANES), rows(ncol, PAGE_SIZE),
                  pl.BlockSpec((PAGE_SIZE, KV_RANK), lambda b, pt_ref: (b // rps, 0)),
                  const(wukd), const(wuv), const(segt), hbm, hbm],
        out_specs=rows(n_tok, ATTN_OUT_DIM),
        scratch_shapes=[pltpu.VMEM((2, n_pages, PAGE_SIZE, KV_RANK), F32),
                        pltpu.VMEM((2, n_pages, QK_ROPE_DIM, PAGE_SIZE), F32),
                        pltpu.SemaphoreType.DMA((2, 2)),
                        pltpu.VMEM((n_pages // sbp, ncol, sbp * PAGE_SIZE), F32),
                        pltpu.VMEM((n_pages * PAGE_SIZE, KV_RANK), BF16),
                        pltpu.VMEM((ncol, KV_RANK), F32)],
    )
    return pl.pallas_call(
        functools.partial(_sample_attn_kernel, n_tok=n_tok, n_pages=n_pages, sbp=sbp),
        grid_spec=grid_spec,
        out_shape=jax.ShapeDtypeStruct((nb * n_tok, ATTN_OUT_DIM), F32),
        compiler_params=_params("arbitrary"),
        name="sample_attn",
    )(pt, qabs, qr, snew, c, wukd, wuv, segt, cache_ckv, kr_t)


def _pool_sample_kernel(u_ref, hist_ref, wpool_ref, pscale_ref, o_ref, *, n_tok, n_past):
    ext = [hist_ref[k] for k in range(POOL_HIST)] + [u_ref[t] for t in range(n_tok)]
    nb = u_ref.shape[1]
    for g, w in enumerate(POOL_WINDOWS):
        cols = slice(g * POOL_GROUP_DIM, (g + 1) * POOL_GROUP_DIM)
        ds = []
        for t in range(n_tok):
            acc = ext[POOL_HIST + t][:, cols]
            for k in range(1, w):
                acc = acc + ext[POOL_HIST + t - k][:, cols]
            ds.append(acc / float(min(n_past + t + 1, w)) - ext[POOL_HIST + t][:, cols])
        d = jnp.concatenate(ds, axis=0).astype(BF16)
        y = (_dot(d, wpool_ref[g]) * pscale_ref[:, cols]).astype(BF16)
        for t in range(n_tok):
            o_ref[t, :, cols] = y[t * nb:(t + 1) * nb, :]


def _pool_sample(u_tm, hist_tm, w_pool, pscale, n_past):
    n_tok, nb, _ = u_tm.shape
    return pl.pallas_call(
        functools.partial(_pool_sample_kernel, n_tok=n_tok, n_past=n_past),
        out_shape=jax.ShapeDtypeStruct((n_tok, nb, POOL_DIM), BF16),
        compiler_params=pltpu.CompilerParams(vmem_limit_bytes=VMEM_LIMIT_BYTES),
        name="pool_sample",
    )(u_tm, hist_tm, w_pool, pscale)


def _mix_mlp_kernel(xp_ref, a1p_ref, a2p_ref, xs_ref, a1s_ref, a2s_ref, wo1_ref, wo2_ref, gffn_ref, wup_ref, wdn_ref,
                    yp_ref, ys_ref, *, ck, n_p):
    def rows(x_ref, a1_ref, a2_ref, y_ref):
        y_ref[...] = (x_ref[...] + _dot(a1_ref[...].astype(BF16), wo1_ref[...])
                      + _dot(a2_ref[...].astype(BF16), wo2_ref[...]))
        xn = _rms(y_ref[...], gffn_ref[...]).astype(BF16)
        for c in range(D_FF // ck):
            h = jnp.maximum(_dot(xn, wup_ref[:, c * ck:(c + 1) * ck]), 0.0)
            y_ref[...] += _dot((h * h).astype(BF16), wdn_ref[c * ck:(c + 1) * ck, :])

    i = pl.program_id(0)

    @pl.when(i < n_p)
    def _():
        rows(xp_ref, a1p_ref, a2p_ref, yp_ref)

    @pl.when(i >= n_p)
    def _():
        rows(xs_ref, a1s_ref, a2s_ref, ys_ref)


def _mix_mlp(xp, a1p, a2p, xs, a1s, a2s, a1_col, a2_col, w_out, g_ffn, w_up, w_down, tm_p, tm_s, ck=1024):
    half = D_MODEL // 2
    n_p, n_s = xp.shape[0] // tm_p, xs.shape[0] // tm_s
    wo = w_out.astype(BF16)
    consts = [wo[:half], wo[half:], g_ffn[None, :], w_up.astype(BF16), w_down.astype(BF16)]
    p_spec = lambda width, col: pl.BlockSpec((tm_p, width), lambda i: (jnp.minimum(i, n_p - 1), col))
    s_spec = lambda width, col: pl.BlockSpec((tm_s, width), lambda i: (jnp.maximum(i - n_p, 0), col))
    return pl.pallas_call(
        functools.partial(_mix_mlp_kernel, ck=ck, n_p=n_p),
        grid=(n_p + n_s,),
        in_specs=[p_spec(D_MODEL, 0), p_spec(half, a1_col), p_spec(half, a2_col),
                  s_spec(D_MODEL, 0), s_spec(half, a1_col), s_spec(half, a2_col)]
                 + [_const_spec(a.shape, 1) for a in consts],
        out_specs=(p_spec(D_MODEL, 0), s_spec(D_MODEL, 0)),
        out_shape=(jax.ShapeDtypeStruct(xp.shape, F32), jax.ShapeDtypeStruct(xs.shape, F32)),
        compiler_params=_params("arbitrary"),
        name="mix_mlp",
    )(xp, a1p, a2p, xs, a1s, a2s, *consts)


def _gates(v, wg_ref, bga_ref, bgx_ref, lam_ref):
    vb = v.astype(BF16)
    pair = 2 * RNN_BLOCK_DIM
    ga, gx = [], []
    for p in range(RNN_BLOCKS // 2):
        g = _dot(vb[:, p * pair:(p + 1) * pair], wg_ref[p])
        ga.append(g[:, :pair])
        gx.append(g[:, pair:])
    r = _sigmoid(jnp.concatenate(ga, axis=1) + bga_ref[...])
    ig = _sigmoid(jnp.concatenate(gx, axis=1) + bgx_ref[...])
    nl = -lam_ref[...]
    softplus = jnp.maximum(nl, 0.0) + jnp.log1p(jnp.exp(-jnp.abs(nl)))
    log_a = (-LRU_C) * r * softplus
    return log_a, ig


def _odd_prompt_kernel(x_ref, gmix_ref, win_ref, cw_ref, cb_ref, wg_ref, bga_ref, bgx_ref, lam_ref,
                       y_ref, ctail_ref, hlast_ref, uext_ref, a_ref, b_ref, hcar_ref, *, tm):
    i = pl.program_id(1)
    sub = SUBLANES

    @pl.when(i == 0)
    def _():
        uext_ref[...] = jnp.zeros((sub, RNN_DIM), F32)
        hcar_ref[...] = jnp.zeros((sub, RNN_DIM), F32)

    xn = _rms(x_ref[...], gmix_ref[...]).astype(BF16)
    z = _dot(xn, win_ref[...])
    gate = z[:, :RNN_DIM]
    u = z[:, RNN_DIM:]
    rowmod = lax.broadcasted_iota(jnp.int32, (tm, RNN_DIM), 0) & (sub - 1)
    v = cb_ref[...]
    for k in range(CONV_WIDTH):
        back = CONV_WIDTH - 1 - k
        if back == 0:
            uk = u
        else:
            rot = _group_roll(u, back)
            prev = jnp.concatenate([pltpu.roll(uext_ref[...], back, 0), rot[:tm - sub, :]], axis=0)
            uk = jnp.where(rowmod < back, prev, rot)
        v = v + uk * cw_ref[k:k + 1, :]
    uext_ref[...] = u[tm - sub:, :]

    log_a, ig = _gates(v, wg_ref, bga_ref, bgx_ref, lam_ref)
    a = jnp.exp(log_a)
    row = lax.broadcasted_iota(jnp.int32, (tm, RNN_DIM), 0)
    mult = jnp.where(row + i * tm == 0, 1.0, _sqrt_one_minus_exp2(log_a))
    b = mult * ig * v

    for s in (1, 2, 4):
        ok = rowmod >= s
        b = jnp.where(ok, a * _group_roll(b, s), 0.0) + b
        a = jnp.where(ok, a * _group_roll(a, s), a)
    a_ref[...] = a
    b_ref[...] = b

    def group(g, hb):
        off = pl.multiple_of(g * sub, sub)
        hg = a_ref[pl.ds(off, sub), :] * hb + b_ref[pl.ds(off, sub), :]
        b_ref[pl.ds(off, sub), :] = hg
        return jnp.broadcast_to(hg[sub - 1:sub, :], (sub, RNN_DIM))

    hb = lax.fori_loop(0, tm // sub, group, hcar_ref[...])
    hcar_ref[...] = hb
    y_ref[...] = (_gelu_tanh(gate) * b_ref[...]).astype(BF16)

    @pl.when(i == pl.num_programs(1) - 1)
    def _():
        ctail_ref[...] = uext_ref[...]
        hlast_ref[...] = hb


def _odd_weights(norm_mix, w_in, conv_w, conv_b, w_ga, b_ga, w_gx, b_gx, lam):
    def pairs(w):
        z = jnp.zeros((RNN_BLOCKS // 2, RNN_BLOCK_DIM, RNN_BLOCK_DIM), F32)
        top = jnp.concatenate([w[0::2], z], axis=2)
        bot = jnp.concatenate([z, w[1::2]], axis=2)
        return jnp.concatenate([top, bot], axis=1)
    wg = jnp.concatenate([pairs(w_ga), pairs(w_gx)], axis=2).astype(BF16)
    return [norm_mix[None, :], w_in.astype(BF16), conv_w, conv_b[None, :], wg, b_ga[None, :], b_gx[None, :],
            lam[None, :]]


def _odd_prompt(x, ow, tm):
    nb, seq, _ = x.shape
    row = pl.BlockSpec((None, tm, D_MODEL), lambda b, i: (b, i, 0))
    tail = pl.BlockSpec((None, SUBLANES, RNN_DIM), lambda b, i: (b, 0, 0))
    return pl.pallas_call(
        functools.partial(_odd_prompt_kernel, tm=tm),
        grid=(nb, seq // tm),
        in_specs=[row] + [_const_spec(a.shape, 2) for a in ow],
        out_specs=(row, tail, tail),
        out_shape=(jax.ShapeDtypeStruct((nb, seq, RNN_DIM), BF16),
                   jax.ShapeDtypeStruct((nb, SUBLANES, RNN_DIM), F32),
                   jax.ShapeDtypeStruct((nb, SUBLANES, RNN_DIM), F32)),
        scratch_shapes=[pltpu.VMEM((SUBLANES, RNN_DIM), F32), pltpu.VMEM((tm, RNN_DIM), F32),
                        pltpu.VMEM((tm, RNN_DIM), F32), pltpu.VMEM((SUBLANES, RNN_DIM), F32)],
        compiler_params=_params("arbitrary", "arbitrary"),
        name="odd_prompt",
    )(x, *ow)


def _odd_sample_kernel(x_ref, ch_ref, h0_ref, gmix_ref, win_ref, cw_ref, cb_ref, wg_ref, bga_ref, bgx_ref, lam_ref,
                       y_ref, ctail_ref, hlast_ref, *, n_tok, n_past):
    nb = x_ref.shape[1]
    x = x_ref[...].reshape(n_tok * nb, D_MODEL)
    z = _dot(_rms(x, gmix_ref[...]).astype(BF16), win_ref[...])
    gate = z[:, :RNN_DIM]
    ext = [ch_ref[k] for k in range(CONV_WIDTH - 1)] + [z[t * nb:(t + 1) * nb, RNN_DIM:] for t in range(n_tok)]
    vs = []
    for t in range(n_tok):
        v = cb_ref[...]
        for k in range(CONV_WIDTH):
            v = v + ext[t + k] * cw_ref[k:k + 1, :]
        vs.append(v)
    v = jnp.concatenate(vs, axis=0)
    log_a, ig = _gates(v, wg_ref, bga_ref, bgx_ref, lam_ref)
    a = jnp.exp(log_a)
    mult = _sqrt_one_minus_exp2(log_a)
    if n_past == 0:
        first = lax.broadcasted_iota(jnp.int32, mult.shape, 0) < nb
        mult = jnp.where(first, 1.0, mult)
    b = mult * ig * v
    h = h0_ref[...]
    hs = []
    for t in range(n_tok):
        h = a[t * nb:(t + 1) * nb, :] * h + b[t * nb:(t + 1) * nb, :]
        hs.append(h)
    y = (_gelu_tanh(gate) * jnp.concatenate(hs, axis=0)).astype(BF16)
    y_ref[...] = y.reshape(n_tok, nb, RNN_DIM)
    for k in range(CONV_WIDTH - 1):
        ctail_ref[k] = ext[n_tok + k]
    hlast_ref[...] = h


def _odd_sample(x_tm, ch_tm, h0, ow, n_past, bb=32):
    n_tok, nb, _ = x_tm.shape
    blk = lambda t, w: pl.BlockSpec((t, bb, w), lambda i: (0, i, 0))
    return pl.pallas_call(
        functools.partial(_odd_sample_kernel, n_tok=n_tok, n_past=n_past),
        grid=(nb // bb,),
        in_specs=[blk(n_tok, D_MODEL), blk(CONV_WIDTH - 1, RNN_DIM), pl.BlockSpec((bb, RNN_DIM), lambda i: (i, 0))]
                 + [_const_spec(a.shape, 1) for a in ow],
        out_specs=(blk(n_tok, RNN_DIM), blk(CONV_WIDTH - 1, RNN_DIM), pl.BlockSpec((bb, RNN_DIM), lambda i: (i, 0))),
        out_shape=(jax.ShapeDtypeStruct((n_tok, nb, RNN_DIM), BF16),
                   jax.ShapeDtypeStruct((CONV_WIDTH - 1, nb, RNN_DIM), F32),
                   jax.ShapeDtypeStruct((nb, RNN_DIM), F32)),
        compiler_params=_params("arbitrary"),
        name="odd_sample",
    )(x_tm, ch_tm, h0, *ow)


def kernel(x_prompt, x_sample, cache_ckv, cache_krope, state_pool, state_conv, state_lru, page_table, norm_mix,
           w_in_even, g_q_nope, g_q_rope, g_ckv, g_k_rope, g_k_nope, w_uk, w_uv, w_pool, pool_scale, w_out_even,
           w_in_rnn, conv_w, conv_b, w_gate_a, b_gate_a, w_gate_x, b_gate_x, lru_lambda, w_out_rnn, norm_ffn,
           w_up, w_down):
    nb, seq, _ = x_prompt.shape
    db, n_tok, _ = x_sample.shape
    n_past = page_table.shape[1] * PAGE_SIZE
    depth = norm_mix.shape[0]
    assert depth == 2 and cache_ckv.shape[0] == 1, "one even (pool + MLA) layer followed by one odd (RG-LRU) layer"
    rope_sl = slice(ROPE_LANE0, ROPE_LANE0 + QK_ROPE_DIM)

    ew = _even_weights(norm_mix[0], w_in_even[0], g_q_nope[0], g_q_rope[0], g_ckv[0], g_k_rope[0], g_k_nope[0],
                       w_uk[0], w_uv[0], w_pool[0], pool_scale[0])
    tab_p = _rope_table(jnp.arange(seq, dtype=jnp.int32))
    tab_s = jnp.tile(_rope_table(n_past + jnp.arange(n_tok, dtype=jnp.int32)), (db, 1))

    q_p, c_p, kr_p, k_p, v_p, pool_p, utail_p = _even_in_prompt(x_prompt, tab_p, ew, tm=512)
    attn_p = _prompt_attn(q_p, k_p, v_p, tq=1024, tk=512)
    xs = x_sample.reshape(db * n_tok, D_MODEL)
    q_s, c_s, krp_s, u_s = _even_in_sample(xs, tab_s, ew, tm=256)
    attn_s = _sample_attn(q_s, c_s, krp_s, cache_ckv, cache_krope, page_table, g_k_nope[0], w_uk[0], w_uv[0],
                          n_tok=n_tok, sbp=8)
    u_s3 = u_s.reshape(db, n_tok, POOL_DIM)
    pool_s = _pool_sample(u_s3.transpose(1, 0, 2), state_pool[0].transpose(1, 0, 2), ew['wpool'], ew['pscale'],
                          n_past)
    pool_s = pool_s.transpose(1, 0, 2).reshape(db * n_tok, POOL_DIM)

    mlp0 = (w_out_even[0], norm_ffn[0], w_up[0], w_down[0])
    yp, ys = _mix_mlp(x_prompt.reshape(nb * seq, D_MODEL), pool_p.reshape(nb * seq, POOL_DIM),
                      attn_p.reshape(nb * seq, ATTN_OUT_DIM), xs, pool_s, attn_s, 0, 0, *mlp0, tm_p=512, tm_s=256)

    ow = _odd_weights(norm_mix[1], w_in_rnn[0], conv_w[0], conv_b[0], w_gate_a[0], b_gate_a[0], w_gate_x[0],
                      b_gate_x[0], lru_lambda[0])
    rnn_p, ctail_p, hlast_p = _odd_prompt(yp.reshape(nb, seq, D_MODEL), ow, tm=512)
    ys_tm = ys.reshape(db, n_tok, D_MODEL).transpose(1, 0, 2)
    rnn_s, conv_s_tm, lru_s = _odd_sample(ys_tm, state_conv[0].transpose(1, 0, 2), state_lru[0], ow, n_past)

    mlp1 = (w_out_rnn[0], norm_ffn[1], w_up[1], w_down[1])
    rnn_p2 = rnn_p.reshape(nb * seq, RNN_DIM)
    rnn_s2 = rnn_s.reshape(n_tok * db, RNN_DIM)
    yp, ys_out = _mix_mlp(yp, rnn_p2, rnn_p2, ys_tm.reshape(n_tok * db, D_MODEL), rnn_s2, rnn_s2, 0, 1, *mlp1,
                          tm_p=512, tm_s=256)
    ys_out = ys_out.reshape(n_tok, db, D_MODEL).transpose(1, 0, 2)

    pool_state_s = jnp.concatenate([state_pool[0], u_s3], axis=1)[:, -POOL_HIST:]
    return (yp.reshape(nb, seq, D_MODEL), ys_out,
            c_p[None], kr_p[None], utail_p[None, :, 1:], ctail_p[None, :, SUBLANES - CONV_WIDTH + 1:],
            hlast_p[None, :, 0],
            c_s.reshape(1, db, n_tok, KV_RANK), krp_s[:, rope_sl].reshape(1, db, n_tok, QK_ROPE_DIM),
            pool_state_s[None], conv_s_tm.transpose(1, 0, 2)[None], lru_s[None])
```

```python
import functools

import numpy as np
import jax
import jax.numpy as jnp
from jax import lax
from jax.experimental import pallas as pl
from jax.experimental.pallas import tpu as pltpu

D_MODEL = 1024
PAGE_SIZE = 128
POOL_WINDOWS = (2, 4, 8, 16)
POOL_GROUP_DIM = 128
POOL_DIM = len(POOL_WINDOWS) * POOL_GROUP_DIM
POOL_HIST = max(POOL_WINDOWS) - 1
N_HEADS = 8
QK_NOPE_DIM = 64
QK_ROPE_DIM = 32
QK_HEAD_DIM = QK_NOPE_DIM + QK_ROPE_DIM
V_HEAD_DIM = 64
KV_RANK = 256
Q_DIM = N_HEADS * QK_HEAD_DIM
ATTN_OUT_DIM = N_HEADS * V_HEAD_DIM
ROPE_BASE = 10000.0
SOFTMAX_SCALE = QK_HEAD_DIM ** -0.5
LOG2_E = 1.4426950408889634
RNN_DIM = D_MODEL
RNN_BLOCKS = 8
RNN_BLOCK_DIM = RNN_DIM // RNN_BLOCKS
CONV_WIDTH = 4
LRU_C = 8.0
D_FF = 4 * D_MODEL
NORM_EPS = 1e-6

LANES = 128
SUBLANES = 8
VMEM_LIMIT_BYTES = 56 * 2 ** 20

HEAD_PAD = LANES
QP_DIM = N_HEADS * HEAD_PAD
W1_DIM = POOL_DIM + QP_DIM + KV_RANK + HEAD_PAD
ROPE_LANE0 = QK_NOPE_DIM
HALF = QK_ROPE_DIM // 2
HEADS_PER_LOOP = 4

F32 = jnp.float32
BF16 = jnp.bfloat16


def _dot(a, b):
    return jnp.dot(a, b, preferred_element_type=F32)


def _dot_nt(a, b):
    return lax.dot_general(a, b, (((1,), (1,)), ((), ())), preferred_element_type=F32)


def _dot_tn(a, b):
    return lax.dot_general(a, b, (((0,), (0,)), ((), ())), preferred_element_type=F32)


def _rms(x, g):
    ms = jnp.mean(x * x, axis=-1, keepdims=True)
    return x * lax.rsqrt(ms + NORM_EPS) * g


SEG_DUP = LANES // 2


def _expand(rs, e):
    hi = rs.astype(BF16)
    lo = (rs - hi.astype(F32)).astype(BF16)
    lane = lax.broadcasted_iota(jnp.int32, rs.shape, 1)
    return _dot(jnp.where(lane < SEG_DUP, hi, lo), e)


def _group_roll(x, shift):
    rows, cols = x.shape
    return pltpu.roll(x.reshape(rows // SUBLANES, SUBLANES, cols), shift, 1).reshape(rows, cols)


def _gelu_tanh(x):
    half = 0.5 * x
    return half + half * jnp.tanh(x * (0.7978845608028654 + 0.035677408136300125 * (x * x)))


def _sigmoid(x):
    return 0.5 * jnp.tanh(0.5 * x) + 0.5


def _sqrt_one_minus_exp2(x):
    t = jnp.tanh(x)
    return jnp.sqrt(-2.0 * t / (1.0 - t))


def _const_spec(shape, grid_rank):
    zeros = (0,) * len(shape)
    if grid_rank == 1:
        return pl.BlockSpec(shape, lambda i: zeros, pipeline_mode=pl.Buffered(1))
    return pl.BlockSpec(shape, lambda i, j: zeros, pipeline_mode=pl.Buffered(1))


def _params(*sem):
    return pltpu.CompilerParams(dimension_semantics=sem, vmem_limit_bytes=VMEM_LIMIT_BYTES)


def _even_in_kernel(*refs, tm, prompt):
    if prompt:
        (x_ref, tab_ref, gmix_ref, w1_ref, gq_ref, segq_ref, eq_ref, gckv_ref, gkr_ref,
         wuk_ref, segk_ref, ek_ref, gk_ref, wuv_ref, wpool_ref, pscale_ref,
         q_ref, c_ref, krp_ref, k_ref, v_ref, pool_ref, utail_ref, halo_ref) = refs
    else:
        (x_ref, tab_ref, gmix_ref, w1_ref, gq_ref, segq_ref, eq_ref, gckv_ref, gkr_ref,
         q_ref, c_ref, krp_ref, u_ref) = refs

    xn = _rms(x_ref[...], gmix_ref[...]).astype(BF16)
    z = _dot(xn, w1_ref[...])
    u = z[:, 0:POOL_DIM]
    qz = z[:, POOL_DIM:POOL_DIM + QP_DIM]
    cz = z[:, POOL_DIM + QP_DIM:POOL_DIM + QP_DIM + KV_RANK]
    krz = z[:, POOL_DIM + QP_DIM + KV_RANK:]

    ta = tab_ref[:, 0:LANES]
    tb = tab_ref[:, LANES:2 * LANES]
    tc = tab_ref[:, 2 * LANES:3 * LANES]

    def rope(blk):
        return blk * ta + pltpu.roll(blk, HALF, 1) * tb + pltpu.roll(blk, LANES - HALF, 1) * tc

    msq = _dot((qz * qz).astype(BF16), segq_ref[...])
    qn = qz * _expand(lax.rsqrt(msq + NORM_EPS), eq_ref[...]) * gq_ref[...]
    for h in range(N_HEADS):
        lanes = slice(h * HEAD_PAD, (h + 1) * HEAD_PAD)
        q_ref[:, lanes] = rope(qn[:, lanes]).astype(q_ref.dtype)

    c = _rms(cz, gckv_ref[...])
    c_ref[...] = c
    mskr = jnp.sum(krz * krz, axis=-1, keepdims=True) * (1.0 / QK_ROPE_DIM)
    krr = rope(krz * lax.rsqrt(mskr + NORM_EPS) * gkr_ref[...])
    if prompt:
        krp_ref[...] = pltpu.roll(krr, LANES - ROPE_LANE0, 1)[:, 0:QK_ROPE_DIM]
    else:
        krp_ref[...] = krr

    if not prompt:
        u_ref[...] = u
        return

    cb = c.astype(BF16)
    kn = _dot(cb, wuk_ref[...])
    msk = _dot((kn * kn).astype(BF16), segk_ref[...])
    knn = kn * _expand(lax.rsqrt(msk + NORM_EPS), ek_ref[...]) * gk_ref[...]
    for h in range(N_HEADS):
        lanes = slice(h * HEAD_PAD, (h + 1) * HEAD_PAD)
        k_ref[:, lanes] = (knn[:, lanes] + krr).astype(BF16)
    ln = lax.broadcasted_iota(jnp.int32, (1, QP_DIM), 1)
    one_lane = (ln & (HEAD_PAD - 1)) + ((ln // HEAD_PAD) & 1) * V_HEAD_DIM == V_HEAD_DIM
    v_ref[...] = (_dot(cb, wuv_ref[...]) + jnp.where(one_lane, 1.0, 0.0)).astype(BF16)

    i = pl.program_id(1)
    sub = SUBLANES

    @pl.when(i == 0)
    def _():
        halo_ref[...] = jnp.zeros(halo_ref.shape, F32)

    rowmod = lax.broadcasted_iota(jnp.int32, (tm, POOL_GROUP_DIM), 0) & (sub - 1)
    pos = lax.broadcasted_iota(jnp.int32, (tm, POOL_GROUP_DIM), 0) + i * tm
    for g, w in enumerate(POOL_WINDOWS):
        cols = slice(g * POOL_GROUP_DIM, (g + 1) * POOL_GROUP_DIM)
        acc = u[:, cols]
        level, shift = 0, 1
        while shift < w:
            tail = halo_ref[level, :, cols]
            halo_ref[level, :, cols] = acc[tm - sub:, :]
            if shift < sub:
                rot = _group_roll(acc, shift)
                prev = jnp.concatenate([pltpu.roll(tail, shift, 0), rot[:tm - sub, :]], axis=0)
                acc = acc + jnp.where(rowmod < shift, prev, rot)
            else:
                acc = acc + jnp.concatenate([tail, acc[:tm - sub, :]], axis=0)
            level, shift = level + 1, 2 * shift
        cnt = jnp.minimum(pos + 1, w).astype(F32)
        d = acc / cnt - u[:, cols]
        y = _dot(d.astype(BF16), wpool_ref[g]) * pscale_ref[:, cols]
        pool_ref[:, cols] = y.astype(BF16)

    @pl.when(i == pl.num_programs(1) - 1)
    def _():
        utail_ref[...] = u[tm - 2 * sub:, :]


def _seg_mats():
    segq = np.zeros((QP_DIM, LANES), np.float32)
    eq = np.zeros((LANES, QP_DIM), np.float32)
    segk = np.zeros((QP_DIM, LANES), np.float32)
    ek = np.zeros((LANES, QP_DIM), np.float32)
    for h in range(N_HEADS):
        b = h * HEAD_PAD
        for dup in (0, SEG_DUP):
            segq[b:b + QK_NOPE_DIM, dup + 2 * h] = 1.0 / QK_NOPE_DIM
            segq[b + QK_NOPE_DIM:b + QK_HEAD_DIM, dup + 2 * h + 1] = 1.0 / QK_ROPE_DIM
            eq[dup + 2 * h, b:b + QK_NOPE_DIM] = 1.0
            eq[dup + 2 * h + 1, b + QK_NOPE_DIM:b + QK_HEAD_DIM] = 1.0
            segk[b:b + QK_NOPE_DIM, dup + h] = 1.0 / QK_NOPE_DIM
            ek[dup + h, b:b + QK_NOPE_DIM] = 1.0
    return [jnp.asarray(m, BF16) for m in (segq, eq, segk, ek)]


def _rope_table(pos):
    inv = ROPE_BASE ** (-jnp.arange(HALF, dtype=F32) / HALF)
    ang = pos.astype(F32)[:, None] * inv[None, :]
    cos, sin = jnp.cos(ang), jnp.sin(ang)
    n = pos.shape[0]
    one = jnp.ones((n, ROPE_LANE0), F32)
    zero = jnp.zeros((n, ROPE_LANE0), F32)
    zh = jnp.zeros((n, HALF), F32)
    tail1 = jnp.ones((n, LANES - ROPE_LANE0 - QK_ROPE_DIM), F32)
    tail0 = jnp.zeros((n, LANES - ROPE_LANE0 - QK_ROPE_DIM), F32)
    ta = jnp.concatenate([one, cos, cos, tail1], axis=1)
    tb = jnp.concatenate([zero, zh, sin, tail0], axis=1)
    tc = jnp.concatenate([zero, -sin, zh, tail0], axis=1)
    return jnp.concatenate([ta, tb, tc], axis=1)


def _head_pad(w, lead):
    d = w.shape[-1]
    return jnp.pad(w, ((0, 0), (0, 0), (0, HEAD_PAD - d))).reshape(lead, QP_DIM)


def _head_row(parts):
    row = jnp.concatenate(parts)
    row = jnp.pad(row, (0, HEAD_PAD - row.shape[0]))
    return jnp.tile(row, N_HEADS)[None, :]


def _even_weights(norm_mix, w_in, g_q_nope, g_q_rope, g_ckv, g_k_rope, g_k_nope, w_uk, w_uv, w_pool, pool_scale):
    w_in = w_in.astype(BF16)
    wq = _head_pad(w_in[:, POOL_DIM:POOL_DIM + Q_DIM].reshape(D_MODEL, N_HEADS, QK_HEAD_DIM), D_MODEL)
    wc = w_in[:, POOL_DIM + Q_DIM:POOL_DIM + Q_DIM + KV_RANK]
    wkr = jnp.pad(w_in[:, POOL_DIM + Q_DIM + KV_RANK:], ((0, 0), (ROPE_LANE0, HEAD_PAD - QK_HEAD_DIM)))
    w1 = jnp.concatenate([w_in[:, :POOL_DIM], wq, wc, wkr], axis=1)
    gq = _head_row([g_q_nope, g_q_rope]) * (SOFTMAX_SCALE * LOG2_E)
    gkr = _head_row([jnp.zeros((ROPE_LANE0,), F32), g_k_rope])[:, :HEAD_PAD]
    gk = _head_row([g_k_nope])
    wuk = _head_pad(w_uk, KV_RANK).astype(BF16)
    v_even = jnp.pad(w_uv, ((0, 0), (0, 0), (0, HEAD_PAD - V_HEAD_DIM)))
    v_odd = jnp.pad(w_uv, ((0, 0), (0, 0), (HEAD_PAD - V_HEAD_DIM, 0)))
    odd = (jnp.arange(N_HEADS) % 2 == 1)[None, :, None]
    wuv = jnp.where(odd, v_odd, v_even).reshape(KV_RANK, QP_DIM).astype(BF16)
    return dict(gmix=norm_mix[None, :], w1=w1, gq=gq, gckv=g_ckv[None, :], gkr=gkr, gk=gk, wuk=wuk, wuv=wuv,
                wpool=w_pool.astype(BF16), pscale=pool_scale[None, :])


def _even_in_prompt(x, tab, ew, tm):
    nb, seq, _ = x.shape
    nt = seq // tm
    segq, eq, segk, ek = _seg_mats()
    row = lambda width: pl.BlockSpec((None, tm, width), lambda b, i: (b, i, 0))
    cs = lambda a: _const_spec(a.shape, 2)
    consts = [ew['gmix'], ew['w1'], ew['gq'], segq, eq, ew['gckv'], ew['gkr'],
              ew['wuk'], segk, ek, ew['gk'], ew['wuv'], ew['wpool'], ew['pscale']]
    out_shape = (
        jax.ShapeDtypeStruct((nb, seq, QP_DIM), BF16),
        jax.ShapeDtypeStruct((nb, seq, KV_RANK), F32),
        jax.ShapeDtypeStruct((nb, seq, QK_ROPE_DIM), F32),
        jax.ShapeDtypeStruct((nb, seq, QP_DIM), BF16),
        jax.ShapeDtypeStruct((nb, seq, QP_DIM), BF16),
        jax.ShapeDtypeStruct((nb, seq, POOL_DIM), BF16),
        jax.ShapeDtypeStruct((nb, 2 * SUBLANES, POOL_DIM), F32),
    )
    out_specs = (row(QP_DIM), row(KV_RANK), row(QK_ROPE_DIM), row(QP_DIM), row(QP_DIM), row(POOL_DIM),
                 pl.BlockSpec((None, 2 * SUBLANES, POOL_DIM), lambda b, i: (b, 0, 0)))
    return pl.pallas_call(
        functools.partial(_even_in_kernel, tm=tm, prompt=True),
        grid=(nb, nt),
        in_specs=[row(D_MODEL), pl.BlockSpec((tm, 3 * LANES), lambda b, i: (i, 0))] + [cs(a) for a in consts],
        out_specs=out_specs,
        out_shape=out_shape,
        scratch_shapes=[pltpu.VMEM((max(POOL_WINDOWS).bit_length() - 1, SUBLANES, POOL_DIM), F32)],
        compiler_params=_params("arbitrary", "arbitrary"),
        name="even_in_prompt",
    )(x, tab, *consts)


def _even_in_sample(x, tab, ew, tm):
    rows = x.shape[0]
    segq, eq, _, _ = _seg_mats()
    row = lambda width: pl.BlockSpec((tm, width), lambda i: (i, 0))
    cs = lambda a: _const_spec(a.shape, 1)
    consts = [ew['gmix'], ew['w1'], ew['gq'], segq, eq, ew['gckv'], ew['gkr']]
    out_shape = (
        jax.ShapeDtypeStruct((rows, QP_DIM), F32),
        jax.ShapeDtypeStruct((rows, KV_RANK), F32),
        jax.ShapeDtypeStruct((rows, HEAD_PAD), F32),
        jax.ShapeDtypeStruct((rows, POOL_DIM), F32),
    )
    return pl.pallas_call(
        functools.partial(_even_in_kernel, tm=tm, prompt=False),
        grid=(rows // tm,),
        in_specs=[row(D_MODEL), row(3 * LANES)] + [cs(a) for a in consts],
        out_specs=(row(QP_DIM), row(KV_RANK), row(HEAD_PAD), row(POOL_DIM)),
        out_shape=out_shape,
        compiler_params=_params("arbitrary"),
        name="even_in_sample",
    )(x, tab, *consts)


def _prompt_attn_kernel(q_ref, k_ref, v_ref, o_ref, *, tq, tk, hpl):
    qi = pl.program_id(1)
    nsub = tq // tk
    causal = {n: lax.broadcasted_iota(jnp.int32, (n, tk), 0) >= lax.broadcasted_iota(jnp.int32, (n, tk), 1)
              for n in range(tk, tq + 1, tk)}
    lane = lax.broadcasted_iota(jnp.int32, (tq, HEAD_PAD), 1)
    for g in range(N_HEADS // hpl):
        heads = tuple(range(g * hpl, (g + 1) * hpl))
        lanes = [slice(h * HEAD_PAD, (h + 1) * HEAD_PAD) for h in heads]
        qs = [q_ref[:, ln] for ln in lanes]

        def tile(kt, carry, r0, lanes=lanes, qs=qs):
            off = pl.multiple_of(kt * tk, tk)
            lo = 0 if r0 is None else r0
            new = []
            for (m, acc), ln, qh in zip(carry, lanes, qs):
                s = _dot_nt(qh[lo:, :], k_ref[pl.ds(off, tk), ln])
                if r0 is not None:
                    s = jnp.where(causal[tq - lo], s, -jnp.inf)
                m_new = jnp.maximum(m[lo:, :], jnp.max(s, axis=-1, keepdims=True))
                p = jnp.exp2(s - m_new)
                acc_new = (jnp.exp2(m[lo:, :] - m_new) * acc[lo:, :]
                           + _dot(p.astype(BF16), v_ref[pl.ds(off, tk), ln]))
                if lo:
                    m_new = jnp.concatenate([m[:lo, :], m_new], axis=0)
                    acc_new = jnp.concatenate([acc[:lo, :], acc_new], axis=0)
                new.append((m_new, acc_new))
            return tuple(new)

        def below(jj, carry):
            for u in range(nsub):
                carry = tile(jj * nsub + u, carry, None)
            return carry

        carry = tuple((jnp.full((tq, 1), -jnp.inf, F32), jnp.zeros((tq, HEAD_PAD), F32)) for _ in heads)
        carry = lax.fori_loop(0, qi, below, carry)
        for u in range(nsub):
            carry = tile(qi * nsub + u, carry, u * tk)
        for k in range(0, hpl, 2):
            acc_e, acc_o = carry[k][1], carry[k + 1][1]
            out_e = jnp.where(lane < V_HEAD_DIM, acc_e / acc_e[:, V_HEAD_DIM:V_HEAD_DIM + 1], 0.0)
            out_o = jnp.where(lane >= V_HEAD_DIM, acc_o / acc_o[:, 0:1], 0.0)
            j = (heads[0] + k) // 2
            o_ref[:, j * LANES:(j + 1) * LANES] = (out_e + out_o).astype(BF16)


def _prompt_attn(q, k, v, tq, tk):
    nb, seq, _ = q.shape
    assert tq % tk == 0 and seq % tq == 0
    full = pl.BlockSpec((None, seq, QP_DIM), lambda b, i: (b, 0, 0), pipeline_mode=pl.Buffered(1))
    return pl.pallas_call(
        functools.partial(_prompt_attn_kernel, tq=tq, tk=tk, hpl=HEADS_PER_LOOP),
        grid=(nb, seq // tq),
        in_specs=[pl.BlockSpec((None, tq, QP_DIM), lambda b, i: (b, i, 0)), full, full],
        out_specs=pl.BlockSpec((None, tq, ATTN_OUT_DIM), lambda b, i: (b, i, 0)),
        out_shape=jax.ShapeDtypeStruct((nb, seq, ATTN_OUT_DIM), BF16),
        compiler_params=_params("arbitrary", "arbitrary"),
        name="prompt_attn",
    )(q, k, v)


def _sample_scores(cb, krt, qabs, qr, segt, wukd):
    k2 = _dot(cb, wukd)
    k2 = k2 * k2
    sumsq = (k2[:, 0:LANES] + k2[:, LANES:2 * LANES]) + (k2[:, 2 * LANES:3 * LANES] + k2[:, 3 * LANES:])
    ss = _dot_nt(segt, sumsq.astype(BF16))
    sp = _dot_nt(qabs, cb)
    rope = _dot(qr[:, 0:QK_ROPE_DIM], krt.astype(BF16))
    return sp * lax.rsqrt(ss + NORM_EPS) + rope


def _sample_prep_kernel(q_ref, cn_ref, krn_ref, gk_ref, wukp_ref, wukd_ref, segt_ref,
                        qabs_ref, qr_ref, snew_ref, *, n_tok, rows_per_step):
    ncol = N_HEADS * n_tok
    rows = rows_per_step * ncol
    lane = lax.broadcasted_iota(jnp.int32, (n_tok, LANES), 1)
    qts, qrs = [], []
    for j in range(rows_per_step):
        q = q_ref[j * n_tok:(j + 1) * n_tok, :]
        qts += [q] * N_HEADS
        qrs += [jnp.where(lane < QK_ROPE_DIM,
                          pltpu.roll(q[:, h * HEAD_PAD:(h + 1) * HEAD_PAD], LANES - ROPE_LANE0, 1), 0.0)
                for h in range(N_HEADS)]
    r = lax.broadcasted_iota(jnp.int32, (rows, QP_DIM), 0)
    ln = lax.broadcasted_iota(jnp.int32, (rows, QP_DIM), 1)
    keep = jnp.where((ln & (HEAD_PAD - 1)) < QK_NOPE_DIM, (r // n_tok) % N_HEADS, -1) == ln // HEAD_PAD
    qg = jnp.where(keep, jnp.concatenate(qts, axis=0) * gk_ref[...], 0.0).astype(BF16)
    qabs = _dot_nt(qg, wukp_ref[...]).astype(BF16)
    qr = jnp.concatenate(qrs, axis=0).astype(BF16)
    cn = cn_ref[...].astype(BF16)
    krt = pltpu.roll(krn_ref[...], LANES - ROPE_LANE0, 1).T[0:QK_ROPE_DIM, :]
    s = _sample_scores(cn, krt, qabs, qr, segt_ref[...], wukd_ref[...])
    row = lax.broadcasted_iota(jnp.int32, (rows, PAGE_SIZE), 0)
    key = lax.broadcasted_iota(jnp.int32, (rows, PAGE_SIZE), 1)
    same_row = jnp.where(key // n_tok == row // ncol, key % n_tok, n_tok)
    qabs_ref[...] = qabs
    qr_ref[...] = qr
    snew_ref[...] = jnp.where(same_row <= row % n_tok, s, -jnp.inf)


def _sample_attn_kernel(pt_ref, qabs_ref, qr_ref, snew_ref, cn_ref, wukd_ref, wuv_ref, segt_ref,
                        ckv_hbm, kr_hbm, o_ref, cbuf, krbuf, sem, s_ref, cb_ref, acc_ref,
                        *, n_tok, n_pages, sbp):
    b = pl.program_id(0)
    slot = lax.rem(b, 2)
    sb = sbp * PAGE_SIZE
    n_sb = n_pages // sbp
    ncol = N_HEADS * n_tok

    def start_pages(row, i0, sl):
        for k in range(sbp):
            page = pt_ref[row * n_pages + i0 + k]
            pltpu.make_async_copy(ckv_hbm.at[0, page], cbuf.at[sl, i0 + k], sem.at[0, sl]).start()
            pltpu.make_async_copy(kr_hbm.at[0, page], krbuf.at[sl, i0 + k], sem.at[1, sl]).start()

    @pl.when(b == 0)
    def _():
        def first(i, carry):
            start_pages(0, i * sbp, 0)
            return carry
        lax.fori_loop(0, n_sb, first, 0)

    pltpu.make_async_copy(ckv_hbm.at[0, pl.ds(0, n_pages)], cbuf.at[slot], sem.at[0, slot]).wait()
    pltpu.make_async_copy(kr_hbm.at[0, pl.ds(0, n_pages)], krbuf.at[slot], sem.at[1, slot]).wait()

    nxt = jnp.minimum(b + 1, pl.num_programs(0) - 1)

    def lane_tiles(x, op):
        out = x[:, 0:LANES]
        for j in range(1, x.shape[1] // LANES):
            out = op(out, x[:, j * LANES:(j + 1) * LANES])
        return out

    def score_block(i, m):
        start_pages(nxt, i * sbp, 1 - slot)
        off = pl.multiple_of(i * sb, sb)
        cb = cbuf[slot, pl.ds(i * sbp, sbp)].reshape(sb, KV_RANK).astype(BF16)
        cb_ref[pl.ds(off, sb), :] = cb
        krt = jnp.concatenate([krbuf[slot, i * sbp + k] for k in range(sbp)], axis=1)
        s = _sample_scores(cb, krt, qabs_ref[...], qr_ref[...], segt_ref[...], wukd_ref[...])
        s_ref[i] = s
        return jnp.maximum(m, lane_tiles(s, jnp.maximum))

    s = snew_ref[...]
    m = lax.fori_loop(0, n_sb, score_block, s, unroll=4)
    m = jnp.max(m, axis=1, keepdims=True)

    p = jnp.exp2(s - m)
    acc_ref[...] = _dot(p.astype(BF16), cn_ref[...].astype(BF16))

    def value_block(i, lp):
        off = pl.multiple_of(i * sb, sb)
        p = jnp.exp2(s_ref[i] - m)
        acc_ref[...] += _dot(p.astype(BF16), cb_ref[pl.ds(off, sb), :])
        return lp + lane_tiles(p, jnp.add)

    l = jnp.sum(lax.fori_loop(0, n_sb, value_block, p, unroll=True), axis=1, keepdims=True)

    lat = acc_ref[...] * (1.0 / l)
    zz = _dot(lat.astype(BF16), wuv_ref[...])
    lane_h = lax.broadcasted_iota(jnp.int32, (n_tok, ATTN_OUT_DIM), 1) // V_HEAD_DIM
    out = jnp.zeros((n_tok, ATTN_OUT_DIM), F32)
    for h in range(N_HEADS):
        out = out + jnp.where(lane_h == h, zz[h * n_tok:(h + 1) * n_tok, :], 0.0)
    o_ref[...] = out

    @pl.when(b == pl.num_programs(0) - 1)
    def _():
        pltpu.make_async_copy(ckv_hbm.at[0, pl.ds(0, n_pages)], cbuf.at[1 - slot], sem.at[0, 1 - slot]).wait()
        pltpu.make_async_copy(kr_hbm.at[0, pl.ds(0, n_pages)], krbuf.at[1 - slot], sem.at[1, 1 - slot]).wait()


def _sample_attn(q, c, krp, cache_ckv, cache_krope, page_table, g_k_nope, w_uk, w_uv, n_tok, sbp):
    nb, n_pages = page_table.shape
    ncol = N_HEADS * n_tok
    gk = _head_row([g_k_nope])
    wukp = _head_pad(w_uk, KV_RANK).astype(BF16)
    wukd = w_uk.transpose(0, 2, 1).reshape(KV_RANK, N_HEADS * QK_NOPE_DIM).astype(BF16)
    wuv = w_uv.reshape(KV_RANK, ATTN_OUT_DIM).astype(BF16)
    segt = np.zeros((ncol, LANES), np.float32)
    for col in range(ncol):
        segt[col, (np.arange(LANES) % N_HEADS) == col // n_tok] = 1.0 / QK_NOPE_DIM
    segt = jnp.asarray(segt, BF16)
    pt = page_table.reshape(-1)
    kr_t = jnp.swapaxes(cache_krope, 2, 3)

    rps = PAGE_SIZE // n_tok
    assert nb % rps == 0
    seg_all = jnp.tile(segt, (rps, 1))
    prow = lambda r, width: pl.BlockSpec((rps * r, width), lambda i: (i, 0))
    qabs, qr, snew = pl.pallas_call(
        functools.partial(_sample_prep_kernel, n_tok=n_tok, rows_per_step=rps),
        grid=(nb // rps,),
        in_specs=[prow(n_tok, QP_DIM), prow(n_tok, KV_RANK), prow(n_tok, HEAD_PAD)]
                 + [_const_spec(a.shape, 1) for a in (gk, wukp, wukd, seg_all)],
        out_specs=(prow(ncol, KV_RANK), prow(ncol, LANES), prow(ncol, PAGE_SIZE)),
        out_shape=(jax.ShapeDtypeStruct((nb * ncol, KV_RANK), BF16), jax.ShapeDtypeStruct((nb * ncol, LANES), BF16),
                   jax.ShapeDtypeStruct((nb * ncol, PAGE_SIZE), F32)),
        compiler_params=_params("arbitrary"),
        name="sample_prep",
    )(q, c, krp, gk, wukp, wukd, seg_all)

    rows = lambda r, width: pl.BlockSpec((r, width), lambda b, pt_ref: (b, 0))
    const = lambda a: pl.BlockSpec(a.shape, lambda b, pt_ref: (0,) * a.ndim)
    hbm = pl.BlockSpec(memory_space=pl.ANY)
    grid_spec = pltpu.PrefetchScalarGridSpec(
        num_scalar_prefetch=1,
        grid=(nb,),
        in_specs=[rows(ncol, KV_RANK), rows(ncol, LANES), rows(ncol, PAGE_SIZE),
                  pl.BlockSpec((PAGE_SIZE, KV_RANK), lambda b, pt_ref: (b // rps, 0)),
                  const(wukd), const(wuv), const(segt), hbm, hbm],
        out_specs=rows(n_tok, ATTN_OUT_DIM),
        scratch_shapes=[pltpu.VMEM((2, n_pages, PAGE_SIZE, KV_RANK), F32),
                        pltpu.VMEM((2, n_pages, QK_ROPE_DIM, PAGE_SIZE), F32),
                        pltpu.SemaphoreType.DMA((2, 2)),
                        pltpu.VMEM((n_pages // sbp, ncol, sbp * PAGE_SIZE), F32),
                        pltpu.VMEM((n_pages * PAGE_SIZE, KV_RANK), BF16),
                        pltpu.VMEM((ncol, KV_RANK), F32)],
    )
    return pl.pallas_call(
        functools.partial(_sample_attn_kernel, n_tok=n_tok, n_pages=n_pages, sbp=sbp),
        grid_spec=grid_spec,
        out_shape=jax.ShapeDtypeStruct((nb * n_tok, ATTN_OUT_DIM), F32),
        compiler_params=_params("arbitrary"),
        name="sample_attn",
    )(pt, qabs, qr, snew, c, wukd, wuv, segt, cache_ckv, kr_t)


def _pool_sample_kernel(u_ref, hist_ref, wpool_ref, pscale_ref, o_ref, *, n_tok, n_past):
    ext = [hist_ref[k] for k in range(POOL_HIST)] + [u_ref[t] for t in range(n_tok)]
    nb = u_ref.shape[1]
    for g, w in enumerate(POOL_WINDOWS):
        cols = slice(g * POOL_GROUP_DIM, (g + 1) * POOL_GROUP_DIM)
        ds = []
        for t in range(n_tok):
            acc = ext[POOL_HIST + t][:, cols]
            for k in range(1, w):
                acc = acc + ext[POOL_HIST + t - k][:, cols]
            ds.append(acc / float(min(n_past + t + 1, w)) - ext[POOL_HIST + t][:, cols])
        d = jnp.concatenate(ds, axis=0).astype(BF16)
        y = (_dot(d, wpool_ref[g]) * pscale_ref[:, cols]).astype(BF16)
        for t in range(n_tok):
            o_ref[t, :, cols] = y[t * nb:(t + 1) * nb, :]


def _pool_sample(u_tm, hist_tm, w_pool, pscale, n_past):
    n_tok, nb, _ = u_tm.shape
    return pl.pallas_call(
        functools.partial(_pool_sample_kernel, n_tok=n_tok, n_past=n_past),
        out_shape=jax.ShapeDtypeStruct((n_tok, nb, POOL_DIM), BF16),
        compiler_params=pltpu.CompilerParams(vmem_limit_bytes=VMEM_LIMIT_BYTES),
        name="pool_sample",
    )(u_tm, hist_tm, w_pool, pscale)


def _mix_mlp_kernel(xp_ref, a1p_ref, a2p_ref, xs_ref, a1s_ref, a2s_ref, wo1_ref, wo2_ref, gffn_ref, wup_ref, wdn_ref,
                    yp_ref, ys_ref, *, ck, n_p):
    def rows(x_ref, a1_ref, a2_ref, y_ref):
        y_ref[...] = (x_ref[...] + _dot(a1_ref[...].astype(BF16), wo1_ref[...])
                      + _dot(a2_ref[...].astype(BF16), wo2_ref[...]))
        xn = _rms(y_ref[...], gffn_ref[...]).astype(BF16)
        for c in range(D_FF // ck):
            h = jnp.maximum(_dot(xn, wup_ref[:, c * ck:(c + 1) * ck]), 0.0)
            y_ref[...] += _dot((h * h).astype(BF16), wdn_ref[c * ck:(c + 1) * ck, :])

    i = pl.program_id(0)

    @pl.when(i < n_p)
    def _():
        rows(xp_ref, a1p_ref, a2p_ref, yp_ref)

    @pl.when(i >= n_p)
    def _():
        rows(xs_ref, a1s_ref, a2s_ref, ys_ref)


def _mix_mlp(xp, a1p, a2p, xs, a1s, a2s, a1_col, a2_col, w_out, g_ffn, w_up, w_down, tm_p, tm_s, ck=1024):
    half = D_MODEL // 2
    n_p, n_s = xp.shape[0] // tm_p, xs.shape[0] // tm_s
    wo = w_out.astype(BF16)
    consts = [wo[:half], wo[half:], g_ffn[None, :], w_up.astype(BF16), w_down.astype(BF16)]
    p_spec = lambda width, col: pl.BlockSpec((tm_p, width), lambda i: (jnp.minimum(i, n_p - 1), col))
    s_spec = lambda width, col: pl.BlockSpec((tm_s, width), lambda i: (jnp.maximum(i - n_p, 0), col))
    return pl.pallas_call(
        functools.partial(_mix_mlp_kernel, ck=ck, n_p=n_p),
        grid=(n_p + n_s,),
        in_specs=[p_spec(D_MODEL, 0), p_spec(half, a1_col), p_spec(half, a2_col),
                  s_spec(D_MODEL, 0), s_spec(half, a1_col), s_spec(half, a2_col)]
                 + [_const_spec(a.shape, 1) for a in consts],
        out_specs=(p_spec(D_MODEL, 0), s_spec(D_MODEL, 0)),
        out_shape=(jax.ShapeDtypeStruct(xp.shape, F32), jax.ShapeDtypeStruct(xs.shape, F32)),
        compiler_params=_params("arbitrary"),
        name="mix_mlp",
    )(xp, a1p, a2p, xs, a1s, a2s, *consts)


def _gates(v, wg_ref, bga_ref, bgx_ref, lam_ref):
    vb = v.astype(BF16)
    pair = 2 * RNN_BLOCK_DIM
    ga, gx = [], []
    for p in range(RNN_BLOCKS // 2):
        g = _dot(vb[:, p * pair:(p + 1) * pair], wg_ref[p])
        ga.append(g[:, :pair])
        gx.append(g[:, pair:])
    r = _sigmoid(jnp.concatenate(ga, axis=1) + bga_ref[...])
    ig = _sigmoid(jnp.concatenate(gx, axis=1) + bgx_ref[...])
    nl = -lam_ref[...]
    softplus = jnp.maximum(nl, 0.0) + jnp.log1p(jnp.exp(-jnp.abs(nl)))
    log_a = (-LRU_C) * r * softplus
    return log_a, ig


def _odd_prompt_kernel(x_ref, gmix_ref, win_ref, cw_ref, cb_ref, wg_ref, bga_ref, bgx_ref, lam_ref,
                       y_ref, ctail_ref, hlast_ref, uext_ref, a_ref, b_ref, hcar_ref, *, tm):
    i = pl.program_id(1)
    sub = SUBLANES

    @pl.when(i == 0)
    def _():
        uext_ref[...] = jnp.zeros((sub, RNN_DIM), F32)
        hcar_ref[...] = jnp.zeros((sub, RNN_DIM), F32)

    xn = _rms(x_ref[...], gmix_ref[...]).astype(BF16)
    z = _dot(xn, win_ref[...])
    gate = z[:, :RNN_DIM]
    u = z[:, RNN_DIM:]
    rowmod = lax.broadcasted_iota(jnp.int32, (tm, RNN_DIM), 0) & (sub - 1)
    v = cb_ref[...]
    for k in range(CONV_WIDTH):
        back = CONV_WIDTH - 1 - k
        if back == 0:
            uk = u
        else:
            rot = _group_roll(u, back)
            prev = jnp.concatenate([pltpu.roll(uext_ref[...], back, 0), rot[:tm - sub, :]], axis=0)
            uk = jnp.where(rowmod < back, prev, rot)
        v = v + uk * cw_ref[k:k + 1, :]
    uext_ref[...] = u[tm - sub:, :]

    log_a, ig = _gates(v, wg_ref, bga_ref, bgx_ref, lam_ref)
    a = jnp.exp(log_a)
    row = lax.broadcasted_iota(jnp.int32, (tm, RNN_DIM), 0)
    mult = jnp.where(row + i * tm == 0, 1.0, _sqrt_one_minus_exp2(log_a))
    b = mult * ig * v

    for s in (1, 2, 4):
        ok = rowmod >= s
        b = jnp.where(ok, a * _group_roll(b, s), 0.0) + b
        a = jnp.where(ok, a * _group_roll(a, s), a)
    a_ref[...] = a
    b_ref[...] = b

    def group(g, hb):
        off = pl.multiple_of(g * sub, sub)
        hg = a_ref[pl.ds(off, sub), :] * hb + b_ref[pl.ds(off, sub), :]
        b_ref[pl.ds(off, sub), :] = hg
        return jnp.broadcast_to(hg[sub - 1:sub, :], (sub, RNN_DIM))

    hb = lax.fori_loop(0, tm // sub, group, hcar_ref[...])
    hcar_ref[...] = hb
    y_ref[...] = (_gelu_tanh(gate) * b_ref[...]).astype(BF16)

    @pl.when(i == pl.num_programs(1) - 1)
    def _():
        ctail_ref[...] = uext_ref[...]
        hlast_ref[...] = hb


def _odd_weights(norm_mix, w_in, conv_w, conv_b, w_ga, b_ga, w_gx, b_gx, lam):
    def pairs(w):
        z = jnp.zeros((RNN_BLOCKS // 2, RNN_BLOCK_DIM, RNN_BLOCK_DIM), F32)
        top = jnp.concatenate([w[0::2], z], axis=2)
        bot = jnp.concatenate([z, w[1::2]], axis=2)
        return jnp.concatenate([top, bot], axis=1)
    wg = jnp.concatenate([pairs(w_ga), pairs(w_gx)], axis=2).astype(BF16)
    return [norm_mix[None, :], w_in.astype(BF16), conv_w, conv_b[None, :], wg, b_ga[None, :], b_gx[None, :],
            lam[None, :]]


def _odd_prompt(x, ow, tm):
    nb, seq, _ = x.shape
    row = pl.BlockSpec((None, tm, D_MODEL), lambda b, i: (b, i, 0))
    tail = pl.BlockSpec((None, SUBLANES, RNN_DIM), lambda b, i: (b, 0, 0))
    return pl.pallas_call(
        functools.partial(_odd_prompt_kernel, tm=tm),
        grid=(nb, seq // tm),
        in_specs=[row] + [_const_spec(a.shape, 2) for a in ow],
        out_specs=(row, tail, tail),
        out_shape=(jax.ShapeDtypeStruct((nb, seq, RNN_DIM), BF16),
                   jax.ShapeDtypeStruct((nb, SUBLANES, RNN_DIM), F32),
                   jax.ShapeDtypeStruct((nb, SUBLANES, RNN_DIM), F32)),
        scratch_shapes=[pltpu.VMEM((SUBLANES, RNN_DIM), F32), pltpu.VMEM((tm, RNN_DIM), F32),
                        pltpu.VMEM((tm, RNN_DIM), F32), pltpu.VMEM((SUBLANES, RNN_DIM), F32)],
        compiler_params=_params("arbitrary", "arbitrary"),
        name="odd_prompt",
    )(x, *ow)


def _odd_sample_kernel(x_ref, ch_ref, h0_ref, gmix_ref, win_ref, cw_ref, cb_ref, wg_ref, bga_ref, bgx_ref, lam_ref,
                       y_ref, ctail_ref, hlast_ref, *, n_tok, n_past):
    nb = x_ref.shape[1]
    x = x_ref[...].reshape(n_tok * nb, D_MODEL)
    z = _dot(_rms(x, gmix_ref[...]).astype(BF16), win_ref[...])
    gate = z[:, :RNN_DIM]
    ext = [ch_ref[k] for k in range(CONV_WIDTH - 1)] + [z[t * nb:(t + 1) * nb, RNN_DIM:] for t in range(n_tok)]
    vs = []
    for t in range(n_tok):
        v = cb_ref[...]
        for k in range(CONV_WIDTH):
            v = v + ext[t + k] * cw_ref[k:k + 1, :]
        vs.append(v)
    v = jnp.concatenate(vs, axis=0)
    log_a, ig = _gates(v, wg_ref, bga_ref, bgx_ref, lam_ref)
    a = jnp.exp(log_a)
    mult = _sqrt_one_minus_exp2(log_a)
    if n_past == 0:
        first = lax.broadcasted_iota(jnp.int32, mult.shape, 0) < nb
        mult = jnp.where(first, 1.0, mult)
    b = mult * ig * v
    h = h0_ref[...]
    hs = []
    for t in range(n_tok):
        h = a[t * nb:(t + 1) * nb, :] * h + b[t * nb:(t + 1) * nb, :]
        hs.append(h)
    y = (_gelu_tanh(gate) * jnp.concatenate(hs, axis=0)).astype(BF16)
    y_ref[...] = y.reshape(n_tok, nb, RNN_DIM)
    for k in range(CONV_WIDTH - 1):
        ctail_ref[k] = ext[n_tok + k]
    hlast_ref[...] = h


def _odd_sample(x_tm, ch_tm, h0, ow, n_past, bb=32):
    n_tok, nb, _ = x_tm.shape
    blk = lambda t, w: pl.BlockSpec((t, bb, w), lambda i: (0, i, 0))
    return pl.pallas_call(
        functools.partial(_odd_sample_kernel, n_tok=n_tok, n_past=n_past),
        grid=(nb // bb,),
        in_specs=[blk(n_tok, D_MODEL), blk(CONV_WIDTH - 1, RNN_DIM), pl.BlockSpec((bb, RNN_DIM), lambda i: (i, 0))]
                 + [_const_spec(a.shape, 1) for a in ow],
        out_specs=(blk(n_tok, RNN_DIM), blk(CONV_WIDTH - 1, RNN_DIM), pl.BlockSpec((bb, RNN_DIM), lambda i: (i, 0))),
        out_shape=(jax.ShapeDtypeStruct((n_tok, nb, RNN_DIM), BF16),
                   jax.ShapeDtypeStruct((CONV_WIDTH - 1, nb, RNN_DIM), F32),
                   jax.ShapeDtypeStruct((nb, RNN_DIM), F32)),
        compiler_params=_params("arbitrary"),
        name="odd_sample",
    )(x_tm, ch_tm, h0, *ow)


def kernel(x_prompt, x_sample, cache_ckv, cache_krope, state_pool, state_conv, state_lru, page_table, norm_mix,
           w_in_even, g_q_nope, g_q_rope, g_ckv, g_k_rope, g_k_nope, w_uk, w_uv, w_pool, pool_scale, w_out_even,
           w_in_rnn, conv_w, conv_b, w_gate_a, b_gate_a, w_gate_x, b_gate_x, lru_lambda, w_out_rnn, norm_ffn,
           w_up, w_down):
    nb, seq, _ = x_prompt.shape
    db, n_tok, _ = x_sample.shape
    n_past = page_table.shape[1] * PAGE_SIZE
    depth = norm_mix.shape[0]
    assert depth == 2 and cache_ckv.shape[0] == 1, "one even (pool + MLA) layer followed by one odd (RG-LRU) layer"
    rope_sl = slice(ROPE_LANE0, ROPE_LANE0 + QK_ROPE_DIM)

    ew = _even_weights(norm_mix[0], w_in_even[0], g_q_nope[0], g_q_rope[0], g_ckv[0], g_k_rope[0], g_k_nope[0],
                       w_uk[0], w_uv[0], w_pool[0], pool_scale[0])
    tab_p = _rope_table(jnp.arange(seq, dtype=jnp.int32))
    tab_s = jnp.tile(_rope_table(n_past + jnp.arange(n_tok, dtype=jnp.int32)), (db, 1))

    q_p, c_p, kr_p, k_p, v_p, pool_p, utail_p = _even_in_prompt(x_prompt, tab_p, ew, tm=512)
    attn_p = _prompt_attn(q_p, k_p, v_p, tq=1024, tk=512)
    xs = x_sample.reshape(db * n_tok, D_MODEL)
    q_s, c_s, krp_s, u_s = _even_in_sample(xs, tab_s, ew, tm=256)
    attn_s = _sample_attn(q_s, c_s, krp_s, cache_ckv, cache_krope, page_table, g_k_nope[0], w_uk[0], w_uv[0],
                          n_tok=n_tok, sbp=8)
    u_s3 = u_s.reshape(db, n_tok, POOL_DIM)
    pool_s = _pool_sample(u_s3.transpose(1, 0, 2), state_pool[0].transpose(1, 0, 2), ew['wpool'], ew['pscale'],
                          n_past)
    pool_s = pool_s.transpose(1, 0, 2).reshape(db * n_tok, POOL_DIM)

    mlp0 = (w_out_even[0], norm_ffn[0], w_up[0], w_down[0])
    yp, ys = _mix_mlp(x_prompt.reshape(nb * seq, D_MODEL), pool_p.reshape(nb * seq, POOL_DIM),
                      attn_p.reshape(nb * seq, ATTN_OUT_DIM), xs, pool_s, attn_s, 0, 0, *mlp0, tm_p=1024, tm_s=256)

    ow = _odd_weights(norm_mix[1], w_in_rnn[0], conv_w[0], conv_b[0], w_gate_a[0], b_gate_a[0], w_gate_x[0],
                      b_gate_x[0], lru_lambda[0])
    rnn_p, ctail_p, hlast_p = _odd_prompt(yp.reshape(nb, seq, D_MODEL), ow, tm=512)
    ys_tm = ys.reshape(db, n_tok, D_MODEL).transpose(1, 0, 2)
    rnn_s, conv_s_tm, lru_s = _odd_sample(ys_tm, state_conv[0].transpose(1, 0, 2), state_lru[0], ow, n_past)

    mlp1 = (w_out_rnn[0], norm_ffn[1], w_up[1], w_down[1])
    rnn_p2 = rnn_p.reshape(nb * seq, RNN_DIM)
    rnn_s2 = rnn_s.reshape(n_tok * db, RNN_DIM)
    yp, ys_out = _mix_mlp(yp, rnn_p2, rnn_p2, ys_tm.reshape(n_tok * db, D_MODEL), rnn_s2, rnn_s2, 0, 1, *mlp1,
                          tm_p=1024, tm_s=256)
    ys_out = ys_out.reshape(n_tok, db, D_MODEL).transpose(1, 0, 2)

    pool_state_s = jnp.concatenate([state_pool[0], u_s3], axis=1)[:, -POOL_HIST:]
    return (yp.reshape(nb, seq, D_MODEL), ys_out,
            c_p[None], kr_p[None], utail_p[None, :, 1:], ctail_p[None, :, SUBLANES - CONV_WIDTH + 1:],
            hlast_p[None, :, 0],
            c_s.reshape(1, db, n_tok, KV_RANK), krp_s[:, rope_sl].reshape(1, db, n_tok, QK_ROPE_DIM),
            pool_state_s[None], conv_s_tm.transpose(1, 0, 2)[None], lru_s[None])
```

```python
import functools

import numpy as np
import jax
import jax.numpy as jnp
from jax import lax
from jax.experimental import pallas as pl
from jax.experimental.pallas import tpu as pltpu

D_MODEL = 1024
PAGE_SIZE = 128
POOL_WINDOWS = (2, 4, 8, 16)
POOL_GROUP_DIM = 128
POOL_DIM = len(POOL_WINDOWS) * POOL_GROUP_DIM
POOL_HIST = max(POOL_WINDOWS) - 1
N_HEADS = 8
QK_NOPE_DIM = 64
QK_ROPE_DIM = 32
QK_HEAD_DIM = QK_NOPE_DIM + QK_ROPE_DIM
V_HEAD_DIM = 64
KV_RANK = 256
Q_DIM = N_HEADS * QK_HEAD_DIM
ATTN_OUT_DIM = N_HEADS * V_HEAD_DIM
ROPE_BASE = 10000.0
SOFTMAX_SCALE = QK_HEAD_DIM ** -0.5
LOG2_E = 1.4426950408889634
RNN_DIM = D_MODEL
RNN_BLOCKS = 8
RNN_BLOCK_DIM = RNN_DIM // RNN_BLOCKS
CONV_WIDTH = 4
LRU_C = 8.0
D_FF = 4 * D_MODEL
NORM_EPS = 1e-6

LANES = 128
SUBLANES = 8
VMEM_LIMIT_BYTES = 56 * 2 ** 20

HEAD_PAD = LANES
QP_DIM = N_HEADS * HEAD_PAD
W1_DIM = POOL_DIM + QP_DIM + KV_RANK + HEAD_PAD
ROPE_LANE0 = QK_NOPE_DIM
HALF = QK_ROPE_DIM // 2
HEADS_PER_LOOP = 4

F32 = jnp.float32
BF16 = jnp.bfloat16


def _dot(a, b):
    return jnp.dot(a, b, preferred_element_type=F32)


def _dot_nt(a, b):
    return lax.dot_general(a, b, (((1,), (1,)), ((), ())), preferred_element_type=F32)


def _dot_tn(a, b):
    return lax.dot_general(a, b, (((0,), (0,)), ((), ())), preferred_element_type=F32)


def _rms(x, g):
    ms = jnp.mean(x * x, axis=-1, keepdims=True)
    return x * lax.rsqrt(ms + NORM_EPS) * g


SEG_DUP = LANES // 2


def _expand(rs, e):
    hi = rs.astype(BF16)
    lo = (rs - hi.astype(F32)).astype(BF16)
    lane = lax.broadcasted_iota(jnp.int32, rs.shape, 1)
    return _dot(jnp.where(lane < SEG_DUP, hi, lo), e)


def _group_roll(x, shift):
    rows, cols = x.shape
    return pltpu.roll(x.reshape(rows // SUBLANES, SUBLANES, cols), shift, 1).reshape(rows, cols)


def _gelu_tanh(x):
    half = 0.5 * x
    return half + half * jnp.tanh(x * (0.7978845608028654 + 0.035677408136300125 * (x * x)))


def _sigmoid(x):
    return 0.5 * jnp.tanh(0.5 * x) + 0.5


def _sqrt_one_minus_exp2(x):
    t = jnp.tanh(x)
    return jnp.sqrt(-2.0 * t / (1.0 - t))


def _const_spec(shape, grid_rank):
    zeros = (0,) * len(shape)
    if grid_rank == 1:
        return pl.BlockSpec(shape, lambda i: zeros, pipeline_mode=pl.Buffered(1))
    return pl.BlockSpec(shape, lambda i, j: zeros, pipeline_mode=pl.Buffered(1))


def _params(*sem):
    return pltpu.CompilerParams(dimension_semantics=sem, vmem_limit_bytes=VMEM_LIMIT_BYTES)


def _even_in_kernel(*refs, tm, prompt):
    if prompt:
        (x_ref, tab_ref, gmix_ref, w1_ref, gq_ref, segq_ref, eq_ref, gckv_ref, gkr_ref,
         wuk_ref, segk_ref, ek_ref, gk_ref, wuv_ref, wpool_ref, pscale_ref,
         q_ref, c_ref, krp_ref, k_ref, v_ref, pool_ref, utail_ref, halo_ref) = refs
    else:
        (x_ref, tab_ref, gmix_ref, w1_ref, gq_ref, segq_ref, eq_ref, gckv_ref, gkr_ref,
         q_ref, c_ref, krp_ref, u_ref) = refs

    xn = _rms(x_ref[...], gmix_ref[...]).astype(BF16)
    z = _dot(xn, w1_ref[...])
    u = z[:, 0:POOL_DIM]
    qz = z[:, POOL_DIM:POOL_DIM + QP_DIM]
    cz = z[:, POOL_DIM + QP_DIM:POOL_DIM + QP_DIM + KV_RANK]
    krz = z[:, POOL_DIM + QP_DIM + KV_RANK:]

    ta = tab_ref[:, 0:LANES]
    tb = tab_ref[:, LANES:2 * LANES]
    tc = tab_ref[:, 2 * LANES:3 * LANES]

    def rope(blk):
        return blk * ta + pltpu.roll(blk, HALF, 1) * tb + pltpu.roll(blk, LANES - HALF, 1) * tc

    msq = _dot((qz * qz).astype(BF16), segq_ref[...])
    qn = qz * _expand(lax.rsqrt(msq + NORM_EPS), eq_ref[...]) * gq_ref[...]
    for h in range(N_HEADS):
        lanes = slice(h * HEAD_PAD, (h + 1) * HEAD_PAD)
        q_ref[:, lanes] = rope(qn[:, lanes]).astype(q_ref.dtype)

    c = _rms(cz, gckv_ref[...])
    c_ref[...] = c
    mskr = jnp.sum(krz * krz, axis=-1, keepdims=True) * (1.0 / QK_ROPE_DIM)
    krr = rope(krz * lax.rsqrt(mskr + NORM_EPS) * gkr_ref[...])
    if prompt:
        krp_ref[...] = pltpu.roll(krr, LANES - ROPE_LANE0, 1)[:, 0:QK_ROPE_DIM]
    else:
        krp_ref[...] = krr

    if not prompt:
        u_ref[...] = u
        return

    cb = c.astype(BF16)
    kn = _dot(cb, wuk_ref[...])
    msk = _dot((kn * kn).astype(BF16), segk_ref[...])
    knn = kn * _expand(lax.rsqrt(msk + NORM_EPS), ek_ref[...]) * gk_ref[...]
    for h in range(N_HEADS):
        lanes = slice(h * HEAD_PAD, (h + 1) * HEAD_PAD)
        k_ref[:, lanes] = (knn[:, lanes] + krr).astype(BF16)
    ln = lax.broadcasted_iota(jnp.int32, (1, QP_DIM), 1)
    one_lane = (ln & (HEAD_PAD - 1)) + ((ln // HEAD_PAD) & 1) * V_HEAD_DIM == V_HEAD_DIM
    v_ref[...] = (_dot(cb, wuv_ref[...]) + jnp.where(one_lane, 1.0, 0.0)).astype(BF16)

    i = pl.program_id(1)
    sub = SUBLANES

    @pl.when(i == 0)
    def _():
        halo_ref[...] = jnp.zeros(halo_ref.shape, F32)

    rowmod = lax.broadcasted_iota(jnp.int32, (tm, POOL_GROUP_DIM), 0) & (sub - 1)
    pos = lax.broadcasted_iota(jnp.int32, (tm, POOL_GROUP_DIM), 0) + i * tm
    for g, w in enumerate(POOL_WINDOWS):
        cols = slice(g * POOL_GROUP_DIM, (g + 1) * POOL_GROUP_DIM)
        acc = u[:, cols]
        level, shift = 0, 1
        while shift < w:
            tail = halo_ref[level, :, cols]
            halo_ref[level, :, cols] = acc[tm - sub:, :]
            if shift < sub:
                rot = _group_roll(acc, shift)
                prev = jnp.concatenate([pltpu.roll(tail, shift, 0), rot[:tm - sub, :]], axis=0)
                acc = acc + jnp.where(rowmod < shift, prev, rot)
            else:
                acc = acc + jnp.concatenate([tail, acc[:tm - sub, :]], axis=0)
            level, shift = level + 1, 2 * shift
        cnt = jnp.minimum(pos + 1, w).astype(F32)
        d = acc / cnt - u[:, cols]
        y = _dot(d.astype(BF16), wpool_ref[g]) * pscale_ref[:, cols]
        pool_ref[:, cols] = y.astype(BF16)

    @pl.when(i == pl.num_programs(1) - 1)
    def _():
        utail_ref[...] = u[tm - 2 * sub:, :]


def _seg_mats():
    segq = np.zeros((QP_DIM, LANES), np.float32)
    eq = np.zeros((LANES, QP_DIM), np.float32)
    segk = np.zeros((QP_DIM, LANES), np.float32)
    ek = np.zeros((LANES, QP_DIM), np.float32)
    for h in range(N_HEADS):
        b = h * HEAD_PAD
        for dup in (0, SEG_DUP):
            segq[b:b + QK_NOPE_DIM, dup + 2 * h] = 1.0 / QK_NOPE_DIM
            segq[b + QK_NOPE_DIM:b + QK_HEAD_DIM, dup + 2 * h + 1] = 1.0 / QK_ROPE_DIM
            eq[dup + 2 * h, b:b + QK_NOPE_DIM] = 1.0
            eq[dup + 2 * h + 1, b + QK_NOPE_DIM:b + QK_HEAD_DIM] = 1.0
            segk[b:b + QK_NOPE_DIM, dup + h] = 1.0 / QK_NOPE_DIM
            ek[dup + h, b:b + QK_NOPE_DIM] = 1.0
    return [jnp.asarray(m, BF16) for m in (segq, eq, segk, ek)]


def _rope_table(pos):
    inv = ROPE_BASE ** (-jnp.arange(HALF, dtype=F32) / HALF)
    inv_lane = jnp.concatenate([jnp.zeros((ROPE_LANE0,), F32), inv, inv,
                                jnp.zeros((LANES - ROPE_LANE0 - QK_ROPE_DIM,), F32)])
    ang = pos.astype(F32)[:, None] * inv_lane[None, :]
    cos, sin = jnp.cos(ang), jnp.sin(ang)
    lane = np.arange(LANES)
    upper = jnp.asarray((lane >= ROPE_LANE0 + HALF) & (lane < ROPE_LANE0 + QK_ROPE_DIM), F32)
    lower = jnp.asarray((lane >= ROPE_LANE0) & (lane < ROPE_LANE0 + HALF), F32)
    return jnp.concatenate([cos, sin * upper, -sin * lower], axis=1)


def _head_pad(w, lead):
    d = w.shape[-1]
    return jnp.pad(w, ((0, 0), (0, 0), (0, HEAD_PAD - d))).reshape(lead, QP_DIM)


def _head_row(parts):
    row = jnp.concatenate(parts)
    row = jnp.pad(row, (0, HEAD_PAD - row.shape[0]))
    return jnp.tile(row, N_HEADS)[None, :]


def _even_weights(norm_mix, w_in, g_q_nope, g_q_rope, g_ckv, g_k_rope, g_k_nope, w_uk, w_uv, w_pool, pool_scale):
    w_in = w_in.astype(BF16)
    wq = _head_pad(w_in[:, POOL_DIM:POOL_DIM + Q_DIM].reshape(D_MODEL, N_HEADS, QK_HEAD_DIM), D_MODEL)
    wc = w_in[:, POOL_DIM + Q_DIM:POOL_DIM + Q_DIM + KV_RANK]
    wkr = jnp.pad(w_in[:, POOL_DIM + Q_DIM + KV_RANK:], ((0, 0), (ROPE_LANE0, HEAD_PAD - QK_HEAD_DIM)))
    w1 = jnp.concatenate([w_in[:, :POOL_DIM], wq, wc, wkr], axis=1)
    gq = _head_row([g_q_nope, g_q_rope]) * (SOFTMAX_SCALE * LOG2_E)
    gkr = _head_row([jnp.zeros((ROPE_LANE0,), F32), g_k_rope])[:, :HEAD_PAD]
    gk = _head_row([g_k_nope])
    wuk = _head_pad(w_uk, KV_RANK).astype(BF16)
    v_even = jnp.pad(w_uv, ((0, 0), (0, 0), (0, HEAD_PAD - V_HEAD_DIM)))
    v_odd = jnp.pad(w_uv, ((0, 0), (0, 0), (HEAD_PAD - V_HEAD_DIM, 0)))
    odd = (jnp.arange(N_HEADS) % 2 == 1)[None, :, None]
    wuv = jnp.where(odd, v_odd, v_even).reshape(KV_RANK, QP_DIM).astype(BF16)
    return dict(gmix=norm_mix[None, :], w1=w1, gq=gq, gckv=g_ckv[None, :], gkr=gkr, gk=gk, wuk=wuk, wuv=wuv,
                wpool=w_pool.astype(BF16), pscale=pool_scale[None, :])


def _even_in_prompt(x, tab, ew, tm):
    nb, seq, _ = x.shape
    nt = seq // tm
    segq, eq, segk, ek = _seg_mats()
    row = lambda width: pl.BlockSpec((None, tm, width), lambda b, i: (b, i, 0))
    cs = lambda a: _const_spec(a.shape, 2)
    consts = [ew['gmix'], ew['w1'], ew['gq'], segq, eq, ew['gckv'], ew['gkr'],
              ew['wuk'], segk, ek, ew['gk'], ew['wuv'], ew['wpool'], ew['pscale']]
    out_shape = (
        jax.ShapeDtypeStruct((nb, seq, QP_DIM), BF16),
        jax.ShapeDtypeStruct((nb, seq, KV_RANK), F32),
        jax.ShapeDtypeStruct((nb, seq, QK_ROPE_DIM), F32),
        jax.ShapeDtypeStruct((nb, seq, QP_DIM), BF16),
        jax.ShapeDtypeStruct((nb, seq, QP_DIM), BF16),
        jax.ShapeDtypeStruct((nb, seq, POOL_DIM), BF16),
        jax.ShapeDtypeStruct((nb, 2 * SUBLANES, POOL_DIM), F32),
    )
    out_specs = (row(QP_DIM), row(KV_RANK), row(QK_ROPE_DIM), row(QP_DIM), row(QP_DIM), row(POOL_DIM),
                 pl.BlockSpec((None, 2 * SUBLANES, POOL_DIM), lambda b, i: (b, 0, 0)))
    return pl.pallas_call(
        functools.partial(_even_in_kernel, tm=tm, prompt=True),
        grid=(nb, nt),
        in_specs=[row(D_MODEL), pl.BlockSpec((tm, 3 * LANES), lambda b, i: (i, 0))] + [cs(a) for a in consts],
        out_specs=out_specs,
        out_shape=out_shape,
        scratch_shapes=[pltpu.VMEM((max(POOL_WINDOWS).bit_length() - 1, SUBLANES, POOL_DIM), F32)],
        compiler_params=_params("arbitrary", "arbitrary"),
        name="even_in_prompt",
    )(x, tab, *consts)


def _even_in_sample(x, tab, ew, tm):
    rows = x.shape[0]
    segq, eq, _, _ = _seg_mats()
    row = lambda width: pl.BlockSpec((tm, width), lambda i: (i, 0))
    cs = lambda a: _const_spec(a.shape, 1)
    consts = [ew['gmix'], ew['w1'], ew['gq'], segq, eq, ew['gckv'], ew['gkr']]
    out_shape = (
        jax.ShapeDtypeStruct((rows, QP_DIM), F32),
        jax.ShapeDtypeStruct((rows, KV_RANK), F32),
        jax.ShapeDtypeStruct((rows, HEAD_PAD), F32),
        jax.ShapeDtypeStruct((rows, POOL_DIM), F32),
    )
    return pl.pallas_call(
        functools.partial(_even_in_kernel, tm=tm, prompt=False),
        grid=(rows // tm,),
        in_specs=[row(D_MODEL), row(3 * LANES)] + [cs(a) for a in consts],
        out_specs=(row(QP_DIM), row(KV_RANK), row(HEAD_PAD), row(POOL_DIM)),
        out_shape=out_shape,
        compiler_params=_params("arbitrary"),
        name="even_in_sample",
    )(x, tab, *consts)


def _prompt_attn_kernel(q_ref, k_ref, v_ref, o_ref, *, tq, tk, hpl):
    qi = pl.program_id(1)
    nsub = tq // tk
    causal = {n: lax.broadcasted_iota(jnp.int32, (n, tk), 0) >= lax.broadcasted_iota(jnp.int32, (n, tk), 1)
              for n in range(tk, tq + 1, tk)}
    lane = lax.broadcasted_iota(jnp.int32, (tq, HEAD_PAD), 1)
    for g in range(N_HEADS // hpl):
        heads = tuple(range(g * hpl, (g + 1) * hpl))
        lanes = [slice(h * HEAD_PAD, (h + 1) * HEAD_PAD) for h in heads]
        qs = [q_ref[:, ln] for ln in lanes]

        def tile(kt, carry, r0, lanes=lanes, qs=qs):
            off = pl.multiple_of(kt * tk, tk)
            lo = 0 if r0 is None else r0
            new = []
            for (m, acc), ln, qh in zip(carry, lanes, qs):
                s = _dot_nt(qh[lo:, :], k_ref[pl.ds(off, tk), ln])
                if r0 is not None:
                    s = jnp.where(causal[tq - lo], s, -jnp.inf)
                m_new = jnp.maximum(m[lo:, :], jnp.max(s, axis=-1, keepdims=True))
                p = jnp.exp2(s - m_new)
                acc_new = (jnp.exp2(m[lo:, :] - m_new) * acc[lo:, :]
                           + _dot(p.astype(BF16), v_ref[pl.ds(off, tk), ln]))
                if lo:
                    m_new = jnp.concatenate([m[:lo, :], m_new], axis=0)
                    acc_new = jnp.concatenate([acc[:lo, :], acc_new], axis=0)
                new.append((m_new, acc_new))
            return tuple(new)

        def below(jj, carry):
            for u in range(nsub):
                carry = tile(jj * nsub + u, carry, None)
            return carry

        carry = tuple((jnp.full((tq, 1), -jnp.inf, F32), jnp.zeros((tq, HEAD_PAD), F32)) for _ in heads)
        carry = lax.fori_loop(0, qi, below, carry)
        for u in range(nsub):
            carry = tile(qi * nsub + u, carry, u * tk)
        for k in range(0, hpl, 2):
            acc_e, acc_o = carry[k][1], carry[k + 1][1]
            out_e = jnp.where(lane < V_HEAD_DIM, acc_e / acc_e[:, V_HEAD_DIM:V_HEAD_DIM + 1], 0.0)
            out_o = jnp.where(lane >= V_HEAD_DIM, acc_o / acc_o[:, 0:1], 0.0)
            j = (heads[0] + k) // 2
            o_ref[:, j * LANES:(j + 1) * LANES] = (out_e + out_o).astype(BF16)


def _prompt_attn(q, k, v, tq, tk):
    nb, seq, _ = q.shape
    assert tq % tk == 0 and seq % tq == 0
    full = pl.BlockSpec((None, seq, QP_DIM), lambda b, i: (b, 0, 0), pipeline_mode=pl.Buffered(1))
    return pl.pallas_call(
        functools.partial(_prompt_attn_kernel, tq=tq, tk=tk, hpl=HEADS_PER_LOOP),
        grid=(nb, seq // tq),
        in_specs=[pl.BlockSpec((None, tq, QP_DIM), lambda b, i: (b, i, 0)), full, full],
        out_specs=pl.BlockSpec((None, tq, ATTN_OUT_DIM), lambda b, i: (b, i, 0)),
        out_shape=jax.ShapeDtypeStruct((nb, seq, ATTN_OUT_DIM), BF16),
        compiler_params=_params("arbitrary", "arbitrary"),
        name="prompt_attn",
    )(q, k, v)


def _sample_scores(cb, krt, qabs, qr, segt, wukd):
    k2 = _dot(cb, wukd)
    k2 = k2 * k2
    sumsq = (k2[:, 0:LANES] + k2[:, LANES:2 * LANES]) + (k2[:, 2 * LANES:3 * LANES] + k2[:, 3 * LANES:])
    ss = _dot_nt(segt, sumsq.astype(BF16))
    sp = _dot_nt(qabs, cb)
    rope = _dot(qr[:, 0:QK_ROPE_DIM], krt.astype(BF16))
    return sp * lax.rsqrt(ss + NORM_EPS) + rope


def _sample_prep_kernel(q_ref, cn_ref, krn_ref, gk_ref, wukp_ref, wukd_ref, segt_ref,
                        qabs_ref, qr_ref, snew_ref, *, n_tok, rows_per_step):
    ncol = N_HEADS * n_tok
    rows = rows_per_step * ncol
    lane = lax.broadcasted_iota(jnp.int32, (n_tok, LANES), 1)
    qts, qrs = [], []
    for j in range(rows_per_step):
        q = q_ref[j * n_tok:(j + 1) * n_tok, :]
        qts += [q] * N_HEADS
        qrs += [jnp.where(lane < QK_ROPE_DIM,
                          pltpu.roll(q[:, h * HEAD_PAD:(h + 1) * HEAD_PAD], LANES - ROPE_LANE0, 1), 0.0)
                for h in range(N_HEADS)]
    r = lax.broadcasted_iota(jnp.int32, (rows, QP_DIM), 0)
    ln = lax.broadcasted_iota(jnp.int32, (rows, QP_DIM), 1)
    keep = jnp.where((ln & (HEAD_PAD - 1)) < QK_NOPE_DIM, (r // n_tok) % N_HEADS, -1) == ln // HEAD_PAD
    qg = jnp.where(keep, jnp.concatenate(qts, axis=0) * gk_ref[...], 0.0).astype(BF16)
    qabs = _dot_nt(qg, wukp_ref[...]).astype(BF16)
    qr = jnp.concatenate(qrs, axis=0).astype(BF16)
    cn = cn_ref[...].astype(BF16)
    krt = pltpu.roll(krn_ref[...], LANES - ROPE_LANE0, 1).T[0:QK_ROPE_DIM, :]
    s = _sample_scores(cn, krt, qabs, qr, segt_ref[...], wukd_ref[...])
    row = lax.broadcasted_iota(jnp.int32, (rows, PAGE_SIZE), 0)
    key = lax.broadcasted_iota(jnp.int32, (rows, PAGE_SIZE), 1)
    same_row = jnp.where(key // n_tok == row // ncol, key % n_tok, n_tok)
    qabs_ref[...] = qabs
    qr_ref[...] = qr
    snew_ref[...] = jnp.where(same_row <= row % n_tok, s, -jnp.inf)


def _sample_attn_kernel(pt_ref, qabs_ref, qr_ref, snew_ref, cn_ref, wukd_ref, wuv_ref, segt_ref,
                        ckv_hbm, kr_hbm, o_ref, cbuf, krbuf, sem, s_ref, cb_ref, acc_ref,
                        *, n_tok, n_pages, sbp):
    b = pl.program_id(0)
    slot = lax.rem(b, 2)
    sb = sbp * PAGE_SIZE
    n_sb = n_pages // sbp
    ncol = N_HEADS * n_tok

    def start_pages(row, i0, sl):
        for k in range(sbp):
            page = pt_ref[row * n_pages + i0 + k]
            pltpu.make_async_copy(ckv_hbm.at[0, page], cbuf.at[sl, i0 + k], sem.at[0, sl]).start()
            pltpu.make_async_copy(kr_hbm.at[0, page], krbuf.at[sl, i0 + k], sem.at[1, sl]).start()

    @pl.when(b == 0)
    def _():
        def first(i, carry):
            start_pages(0, i * sbp, 0)
            return carry
        lax.fori_loop(0, n_sb, first, 0)

    pltpu.make_async_copy(ckv_hbm.at[0, pl.ds(0, n_pages)], cbuf.at[slot], sem.at[0, slot]).wait()
    pltpu.make_async_copy(kr_hbm.at[0, pl.ds(0, n_pages)], krbuf.at[slot], sem.at[1, slot]).wait()

    nxt = jnp.minimum(b + 1, pl.num_programs(0) - 1)

    def lane_tiles(x, op):
        out = x[:, 0:LANES]
        for j in range(1, x.shape[1] // LANES):
            out = op(out, x[:, j * LANES:(j + 1) * LANES])
        return out

    def score_block(i, m):
        start_pages(nxt, i * sbp, 1 - slot)
        off = pl.multiple_of(i * sb, sb)
        cb = cbuf[slot, pl.ds(i * sbp, sbp)].reshape(sb, KV_RANK).astype(BF16)
        cb_ref[pl.ds(off, sb), :] = cb
        krt = jnp.concatenate([krbuf[slot, i * sbp + k] for k in range(sbp)], axis=1)
        s = _sample_scores(cb, krt, qabs_ref[...], qr_ref[...], segt_ref[...], wukd_ref[...])
        s_ref[i] = s
        return jnp.maximum(m, lane_tiles(s, jnp.maximum))

    s = snew_ref[...]
    m = lax.fori_loop(0, n_sb, score_block, s, unroll=4)
    m = jnp.max(m, axis=1, keepdims=True)

    p = jnp.exp2(s - m)
    acc_ref[...] = _dot(p.astype(BF16), cn_ref[...].astype(BF16))

    def value_block(i, lp):
        off = pl.multiple_of(i * sb, sb)
        p = jnp.exp2(s_ref[i] - m)
        acc_ref[...] += _dot(p.astype(BF16), cb_ref[pl.ds(off, sb), :])
        return lp + lane_tiles(p, jnp.add)

    l = jnp.sum(lax.fori_loop(0, n_sb, value_block, p, unroll=True), axis=1, keepdims=True)

    lat = acc_ref[...] * (1.0 / l)
    zz = _dot(lat.astype(BF16), wuv_ref[...])
    lane_h = lax.broadcasted_iota(jnp.int32, (n_tok, ATTN_OUT_DIM), 1) // V_HEAD_DIM
    out = jnp.zeros((n_tok, ATTN_OUT_DIM), F32)
    for h in range(N_HEADS):
        out = out + jnp.where(lane_h == h, zz[h * n_tok:(h + 1) * n_tok, :], 0.0)
    o_ref[...] = out

    @pl.when(b == pl.num_programs(0) - 1)
    def _():
        pltpu.make_async_copy(ckv_hbm.at[0, pl.ds(0, n_pages)], cbuf.at[1 - slot], sem.at[0, 1 - slot]).wait()
        pltpu.make_async_copy(kr_hbm.at[0, pl.ds(0, n_pages)], krbuf.at[1 - slot], sem.at[1, 1 - slot]).wait()


def _sample_attn(q, c, krp, cache_ckv, cache_krope, page_table, g_k_nope, w_uk, w_uv, n_tok, sbp):
    nb, n_pages = page_table.shape
    ncol = N_HEADS * n_tok
    gk = _head_row([g_k_nope])
    wukp = _head_pad(w_uk, KV_RANK).astype(BF16)
    wukd = w_uk.transpose(0, 2, 1).reshape(KV_RANK, N_HEADS * QK_NOPE_DIM).astype(BF16)
    wuv = w_uv.reshape(KV_RANK, ATTN_OUT_DIM).astype(BF16)
    segt = np.zeros((ncol, LANES), np.float32)
    for col in range(ncol):
        segt[col, (np.arange(LANES) % N_HEADS) == col // n_tok] = 1.0 / QK_NOPE_DIM
    segt = jnp.asarray(segt, BF16)
    pt = page_table.reshape(-1)
    kr_t = jnp.swapaxes(cache_krope, 2, 3)

    rps = PAGE_SIZE // n_tok
    assert nb % rps == 0
    seg_all = jnp.tile(segt, (rps, 1))
    prow = lambda r, width: pl.BlockSpec((rps * r, width), lambda i: (i, 0))
    qabs, qr, snew = pl.pallas_call(
        functools.partial(_sample_prep_kernel, n_tok=n_tok, rows_per_step=rps),
        grid=(nb // rps,),
        in_specs=[prow(n_tok, QP_DIM), prow(n_tok, KV_RANK), prow(n_tok, HEAD_PAD)]
                 + [_const_spec(a.shape, 1) for a in (gk, wukp, wukd, seg_all)],
        out_specs=(prow(ncol, KV_RANK), prow(ncol, LANES), prow(ncol, PAGE_SIZE)),
        out_shape=(jax.ShapeDtypeStruct((nb * ncol, KV_RANK), BF16), jax.ShapeDtypeStruct((nb * ncol, LANES), BF16),
                   jax.ShapeDtypeStruct((nb * ncol, PAGE_SIZE), F32)),
        compiler_params=_params("arbitrary"),
        name="sample_prep",
    )(q, c, krp, gk, wukp, wukd, seg_all)

    rows = lambda r, width: pl.BlockSpec((r, width), lambda b, pt_ref: (b, 0))
    const = lambda a: pl.BlockSpec(a.shape, lambda b, pt_ref: (0,) * a.ndim)
    hbm = pl.BlockSpec(memory_space=pl.ANY)
    grid_spec = pltpu.PrefetchScalarGridSpec(
        num_scalar_prefetch=1,
        grid=(nb,),
        in_specs=[rows(ncol, KV_RANK), rows(ncol, LANES), rows(ncol, PAGE_SIZE),
                  pl.BlockSpec((PAGE_SIZE, KV_RANK), lambda b, pt_ref: (b // rps, 0)),
                  const(wukd), const(wuv), const(segt), hbm, hbm],
        out_specs=rows(n_tok, ATTN_OUT_DIM),
        scratch_shapes=[pltpu.VMEM((2, n_pages, PAGE_SIZE, KV_RANK), F32),
                        pltpu.VMEM((2, n_pages, QK_ROPE_DIM, PAGE_SIZE), F32),
                        pltpu.SemaphoreType.DMA((2, 2)),
                        pltpu.VMEM((n_pages // sbp, ncol, sbp * PAGE_SIZE), F32),
                        pltpu.VMEM((n_pages * PAGE_SIZE, KV_RANK), BF16),
                        pltpu.VMEM((ncol, KV_RANK), F32)],
    )
    return pl.pallas_call(
        functools.partial(_sample_attn_kernel, n_tok=n_tok, n_pages=n_pages, sbp=sbp),
        grid_spec=grid_spec,
        out_shape=jax.ShapeDtypeStruct((nb * n_tok, ATTN_OUT_DIM), F32),
        compiler_params=_params("arbitrary"),
        name="sample_attn",
    )(pt, qabs, qr, snew, c, wukd, wuv, segt, cache_ckv, kr_t)


def _pool_sample_kernel(u_ref, hist_ref, wpool_ref, pscale_ref, o_ref, *, n_tok, n_past):
    ext = [hist_ref[k] for k in range(POOL_HIST)] + [u_ref[t] for t in range(n_tok)]
    nb = u_ref.shape[1]
    for g, w in enumerate(POOL_WINDOWS):
        cols = slice(g * POOL_GROUP_DIM, (g + 1) * POOL_GROUP_DIM)
        ds = []
        for t in range(n_tok):
            acc = ext[POOL_HIST + t][:, cols]
            for k in range(1, w):
                acc = acc + ext[POOL_HIST + t - k][:, cols]
            ds.append(acc / float(min(n_past + t + 1, w)) - ext[POOL_HIST + t][:, cols])
        d = jnp.concatenate(ds, axis=0).astype(BF16)
        y = (_dot(d, wpool_ref[g]) * pscale_ref[:, cols]).astype(BF16)
        for t in range(n_tok):
            o_ref[t, :, cols] = y[t * nb:(t + 1) * nb, :]


def _pool_sample(u_tm, hist_tm, w_pool, pscale, n_past):
    n_tok, nb, _ = u_tm.shape
    return pl.pallas_call(
        functools.partial(_pool_sample_kernel, n_tok=n_tok, n_past=n_past),
        out_shape=jax.ShapeDtypeStruct((n_tok, nb, POOL_DIM), BF16),
        compiler_params=pltpu.CompilerParams(vmem_limit_bytes=VMEM_LIMIT_BYTES),
        name="pool_sample",
    )(u_tm, hist_tm, w_pool, pscale)


def _mix_mlp_kernel(xp_ref, a1p_ref, a2p_ref, xs_ref, a1s_ref, a2s_ref, wo1_ref, wo2_ref, gffn_ref, wup_ref, wdn_ref,
                    yp_ref, ys_ref, *, ck, n_p):
    def rows(x_ref, a1_ref, a2_ref, y_ref):
        y_ref[...] = (x_ref[...] + _dot(a1_ref[...].astype(BF16), wo1_ref[...])
                      + _dot(a2_ref[...].astype(BF16), wo2_ref[...]))
        xn = _rms(y_ref[...], gffn_ref[...]).astype(BF16)
        for c in range(D_FF // ck):
            h = jnp.maximum(_dot(xn, wup_ref[:, c * ck:(c + 1) * ck]), 0.0)
            y_ref[...] += _dot((h * h).astype(BF16), wdn_ref[c * ck:(c + 1) * ck, :])

    i = pl.program_id(0)

    @pl.when(i < n_p)
    def _():
        rows(xp_ref, a1p_ref, a2p_ref, yp_ref)

    @pl.when(i >= n_p)
    def _():
        rows(xs_ref, a1s_ref, a2s_ref, ys_ref)


def _mix_mlp(xp, a1p, a2p, xs, a1s, a2s, a1_col, a2_col, w_out, g_ffn, w_up, w_down, tm_p, tm_s, ck=1024):
    half = D_MODEL // 2
    n_p, n_s = xp.shape[0] // tm_p, xs.shape[0] // tm_s
    wo = w_out.astype(BF16)
    consts = [wo[:half], wo[half:], g_ffn[None, :], w_up.astype(BF16), w_down.astype(BF16)]
    p_spec = lambda width, col: pl.BlockSpec((tm_p, width), lambda i: (jnp.minimum(i, n_p - 1), col))
    s_spec = lambda width, col: pl.BlockSpec((tm_s, width), lambda i: (jnp.maximum(i - n_p, 0), col))
    return pl.pallas_call(
        functools.partial(_mix_mlp_kernel, ck=ck, n_p=n_p),
        grid=(n_p + n_s,),
        in_specs=[p_spec(D_MODEL, 0), p_spec(half, a1_col), p_spec(half, a2_col),
                  s_spec(D_MODEL, 0), s_spec(half, a1_col), s_spec(half, a2_col)]
                 + [_const_spec(a.shape, 1) for a in consts],
        out_specs=(p_spec(D_MODEL, 0), s_spec(D_MODEL, 0)),
        out_shape=(jax.ShapeDtypeStruct(xp.shape, F32), jax.ShapeDtypeStruct(xs.shape, F32)),
        compiler_params=_params("arbitrary"),
        name="mix_mlp",
    )(xp, a1p, a2p, xs, a1s, a2s, *consts)


def _gates(v, wg_ref, bga_ref, bgx_ref, lam_ref):
    vb = v.astype(BF16)
    pair = 2 * RNN_BLOCK_DIM
    ga, gx = [], []
    for p in range(RNN_BLOCKS // 2):
        g = _dot(vb[:, p * pair:(p + 1) * pair], wg_ref[p])
        ga.append(g[:, :pair])
        gx.append(g[:, pair:])
    r = _sigmoid(jnp.concatenate(ga, axis=1) + bga_ref[...])
    ig = _sigmoid(jnp.concatenate(gx, axis=1) + bgx_ref[...])
    nl = -lam_ref[...]
    softplus = jnp.maximum(nl, 0.0) + jnp.log1p(jnp.exp(-jnp.abs(nl)))
    log_a = (-LRU_C) * r * softplus
    return log_a, ig


def _odd_prompt_kernel(x_ref, gmix_ref, win_ref, cw_ref, cb_ref, wg_ref, bga_ref, bgx_ref, lam_ref,
                       y_ref, ctail_ref, hlast_ref, uext_ref, a_ref, b_ref, hcar_ref, *, tm):
    i = pl.program_id(1)
    sub = SUBLANES

    @pl.when(i == 0)
    def _():
        uext_ref[...] = jnp.zeros((sub, RNN_DIM), F32)
        hcar_ref[...] = jnp.zeros((sub, RNN_DIM), F32)

    xn = _rms(x_ref[...], gmix_ref[...]).astype(BF16)
    z = _dot(xn, win_ref[...])
    gate = z[:, :RNN_DIM]
    u = z[:, RNN_DIM:]
    rowmod = lax.broadcasted_iota(jnp.int32, (tm, RNN_DIM), 0) & (sub - 1)
    v = cb_ref[...]
    for k in range(CONV_WIDTH):
        back = CONV_WIDTH - 1 - k
        if back == 0:
            uk = u
        else:
            rot = _group_roll(u, back)
            prev = jnp.concatenate([pltpu.roll(uext_ref[...], back, 0), rot[:tm - sub, :]], axis=0)
            uk = jnp.where(rowmod < back, prev, rot)
        v = v + uk * cw_ref[k:k + 1, :]
    uext_ref[...] = u[tm - sub:, :]

    log_a, ig = _gates(v, wg_ref, bga_ref, bgx_ref, lam_ref)
    a = jnp.exp(log_a)
    row = lax.broadcasted_iota(jnp.int32, (tm, RNN_DIM), 0)
    mult = jnp.where(row + i * tm == 0, 1.0, _sqrt_one_minus_exp2(log_a))
    b = mult * ig * v

    for s in (1, 2, 4):
        ok = rowmod >= s
        b = jnp.where(ok, a * _group_roll(b, s), 0.0) + b
        a = jnp.where(ok, a * _group_roll(a, s), a)
    a_ref[...] = a
    b_ref[...] = b

    def group(g, hb):
        off = pl.multiple_of(g * sub, sub)
        hg = a_ref[pl.ds(off, sub), :] * hb + b_ref[pl.ds(off, sub), :]
        b_ref[pl.ds(off, sub), :] = hg
        return jnp.broadcast_to(hg[sub - 1:sub, :], (sub, RNN_DIM))

    hb = lax.fori_loop(0, tm // sub, group, hcar_ref[...])
    hcar_ref[...] = hb
    y_ref[...] = (_gelu_tanh(gate) * b_ref[...]).astype(BF16)

    @pl.when(i == pl.num_programs(1) - 1)
    def _():
        ctail_ref[...] = uext_ref[...]
        hlast_ref[...] = hb


def _odd_weights(norm_mix, w_in, conv_w, conv_b, w_ga, b_ga, w_gx, b_gx, lam):
    def pairs(w):
        z = jnp.zeros((RNN_BLOCKS // 2, RNN_BLOCK_DIM, RNN_BLOCK_DIM), F32)
        top = jnp.concatenate([w[0::2], z], axis=2)
        bot = jnp.concatenate([z, w[1::2]], axis=2)
        return jnp.concatenate([top, bot], axis=1)
    wg = jnp.concatenate([pairs(w_ga), pairs(w_gx)], axis=2).astype(BF16)
    return [norm_mix[None, :], w_in.astype(BF16), conv_w, conv_b[None, :], wg, b_ga[None, :], b_gx[None, :],
            lam[None, :]]


def _odd_prompt(x, ow, tm):
    nb, seq, _ = x.shape
    row = pl.BlockSpec((None, tm, D_MODEL), lambda b, i: (b, i, 0))
    tail = pl.BlockSpec((None, SUBLANES, RNN_DIM), lambda b, i: (b, 0, 0))
    return pl.pallas_call(
        functools.partial(_odd_prompt_kernel, tm=tm),
        grid=(nb, seq // tm),
        in_specs=[row] + [_const_spec(a.shape, 2) for a in ow],
        out_specs=(row, tail, tail),
        out_shape=(jax.ShapeDtypeStruct((nb, seq, RNN_DIM), BF16),
                   jax.ShapeDtypeStruct((nb, SUBLANES, RNN_DIM), F32),
                   jax.ShapeDtypeStruct((nb, SUBLANES, RNN_DIM), F32)),
        scratch_shapes=[pltpu.VMEM((SUBLANES, RNN_DIM), F32), pltpu.VMEM((tm, RNN_DIM), F32),
                        pltpu.VMEM((tm, RNN_DIM), F32), pltpu.VMEM((SUBLANES, RNN_DIM), F32)],
        compiler_params=_params("arbitrary", "arbitrary"),
        name="odd_prompt",
    )(x, *ow)


def _odd_sample_kernel(x_ref, ch_ref, h0_ref, gmix_ref, win_ref, cw_ref, cb_ref, wg_ref, bga_ref, bgx_ref, lam_ref,
                       y_ref, ctail_ref, hlast_ref, *, n_tok, n_past):
    nb = x_ref.shape[1]
    x = x_ref[...].reshape(n_tok * nb, D_MODEL)
    z = _dot(_rms(x, gmix_ref[...]).astype(BF16), win_ref[...])
    gate = z[:, :RNN_DIM]
    ext = [ch_ref[k] for k in range(CONV_WIDTH - 1)] + [z[t * nb:(t + 1) * nb, RNN_DIM:] for t in range(n_tok)]
    vs = []
    for t in range(n_tok):
        v = cb_ref[...]
        for k in range(CONV_WIDTH):
            v = v + ext[t + k] * cw_ref[k:k + 1, :]
        vs.append(v)
    v = jnp.concatenate(vs, axis=0)
    log_a, ig = _gates(v, wg_ref, bga_ref, bgx_ref, lam_ref)
    a = jnp.exp(log_a)
    mult = _sqrt_one_minus_exp2(log_a)
    if n_past == 0:
        first = lax.broadcasted_iota(jnp.int32, mult.shape, 0) < nb
        mult = jnp.where(first, 1.0, mult)
    b = mult * ig * v
    h = h0_ref[...]
    hs = []
    for t in range(n_tok):
        h = a[t * nb:(t + 1) * nb, :] * h + b[t * nb:(t + 1) * nb, :]
        hs.append(h)
    y = (_gelu_tanh(gate) * jnp.concatenate(hs, axis=0)).astype(BF16)
    y_ref[...] = y.reshape(n_tok, nb, RNN_DIM)
    for k in range(CONV_WIDTH - 1):
        ctail_ref[k] = ext[n_tok + k]
    hlast_ref[...] = h


def _odd_sample(x_tm, ch_tm, h0, ow, n_past, bb=32):
    n_tok, nb, _ = x_tm.shape
    blk = lambda t, w: pl.BlockSpec((t, bb, w), lambda i: (0, i, 0))
    return pl.pallas_call(
        functools.partial(_odd_sample_kernel, n_tok=n_tok, n_past=n_past),
        grid=(nb // bb,),
        in_specs=[blk(n_tok, D_MODEL), blk(CONV_WIDTH - 1, RNN_DIM), pl.BlockSpec((bb, RNN_DIM), lambda i: (i, 0))]
                 + [_const_spec(a.shape, 1) for a in ow],
        out_specs=(blk(n_tok, RNN_DIM), blk(CONV_WIDTH - 1, RNN_DIM), pl.BlockSpec((bb, RNN_DIM), lambda i: (i, 0))),
        out_shape=(jax.ShapeDtypeStruct((n_tok, nb, RNN_DIM), BF16),
                   jax.ShapeDtypeStruct((CONV_WIDTH - 1, nb, RNN_DIM), F32),
                   jax.ShapeDtypeStruct((nb, RNN_DIM), F32)),
        compiler_params=_params("arbitrary"),
        name="odd_sample",
    )(x_tm, ch_tm, h0, *ow)


def kernel(x_prompt, x_sample, cache_ckv, cache_krope, state_pool, state_conv, state_lru, page_table, norm_mix,
           w_in_even, g_q_nope, g_q_rope, g_ckv, g_k_rope, g_k_nope, w_uk, w_uv, w_pool, pool_scale, w_out_even,
           w_in_rnn, conv_w, conv_b, w_gate_a, b_gate_a, w_gate_x, b_gate_x, lru_lambda, w_out_rnn, norm_ffn,
           w_up, w_down):
    nb, seq, _ = x_prompt.shape
    db, n_tok, _ = x_sample.shape
    n_past = page_table.shape[1] * PAGE_SIZE
    depth = norm_mix.shape[0]
    assert depth == 2 and cache_ckv.shape[0] == 1, "one even (pool + MLA) layer followed by one odd (RG-LRU) layer"
    rope_sl = slice(ROPE_LANE0, ROPE_LANE0 + QK_ROPE_DIM)

    ew = _even_weights(norm_mix[0], w_in_even[0], g_q_nope[0], g_q_rope[0], g_ckv[0], g_k_rope[0], g_k_nope[0],
                       w_uk[0], w_uv[0], w_pool[0], pool_scale[0])
    tab_p = _rope_table(jnp.arange(seq, dtype=jnp.int32))
    tab_s = jnp.tile(_rope_table(n_past + jnp.arange(n_tok, dtype=jnp.int32)), (db, 1))

    q_p, c_p, kr_p, k_p, v_p, pool_p, utail_p = _even_in_prompt(x_prompt, tab_p, ew, tm=512)
    attn_p = _prompt_attn(q_p, k_p, v_p, tq=1024, tk=512)
    xs = x_sample.reshape(db * n_tok, D_MODEL)
    q_s, c_s, krp_s, u_s = _even_in_sample(xs, tab_s, ew, tm=256)
    attn_s = _sample_attn(q_s, c_s, krp_s, cache_ckv, cache_krope, page_table, g_k_nope[0], w_uk[0], w_uv[0],
                          n_tok=n_tok, sbp=8)
    u_s3 = u_s.reshape(db, n_tok, POOL_DIM)
    pool_s = _pool_sample(u_s3.transpose(1, 0, 2), state_pool[0].transpose(1, 0, 2), ew['wpool'], ew['pscale'],
                          n_past)
    pool_s = pool_s.transpose(1, 0, 2).reshape(db * n_tok, POOL_DIM)

    mlp0 = (w_out_even[0], norm_ffn[0], w_up[0], w_down[0])
    yp, ys = _mix_mlp(x_prompt.reshape(nb * seq, D_MODEL), pool_p.reshape(nb * seq, POOL_DIM),
                      attn_p.reshape(nb * seq, ATTN_OUT_DIM), xs, pool_s, attn_s, 0, 0, *mlp0, tm_p=512, tm_s=256)

    ow = _odd_weights(norm_mix[1], w_in_rnn[0], conv_w[0], conv_b[0], w_gate_a[0], b_gate_a[0], w_gate_x[0],
                      b_gate_x[0], lru_lambda[0])
    rnn_p, ctail_p, hlast_p = _odd_prompt(yp.reshape(nb, seq, D_MODEL), ow, tm=512)
    ys_tm = ys.reshape(db, n_tok, D_MODEL).transpose(1, 0, 2)
    rnn_s, conv_s_tm, lru_s = _odd_sample(ys_tm, state_conv[0].transpose(1, 0, 2), state_lru[0], ow, n_past)

    mlp1 = (w_out_rnn[0], norm_ffn[1], w_up[1], w_down[1])
    rnn_p2 = rnn_p.reshape(nb * seq, RNN_DIM)
    rnn_s2 = rnn_s.reshape(n_tok * db, RNN_DIM)
    yp, ys_out = _mix_mlp(yp, rnn_p2, rnn_p2, ys_tm.reshape(n_tok * db, D_MODEL), rnn_s2, rnn_s2, 0, 1, *mlp1,
                          tm_p=512, tm_s=256)
    ys_out = ys_out.reshape(n_tok, db, D_MODEL).transpose(1, 0, 2)

    pool_state_s = jnp.concatenate([state_pool[0], u_s3], axis=1)[:, -POOL_HIST:]
    return (yp.reshape(nb, seq, D_MODEL), ys_out,
            c_p[None], kr_p[None], utail_p[None, :, 1:], ctail_p[None, :, SUBLANES - CONV_WIDTH + 1:],
            hlast_p[None, :, 0],
            c_s.reshape(1, db, n_tok, KV_RANK), krp_s[:, rope_sl].reshape(1, db, n_tok, QK_ROPE_DIM),
            pool_state_s[None], conv_s_tm.transpose(1, 0, 2)[None], lru_s[None])
```
